```python
import math
import jax, jax.numpy as jnp
from jax import lax
import numpy as np

D_MODEL = 1024
BATCH = 16
SEQ = 2048
DEPTH = 2

CHUNK = 64
Q_BLOCK = 128
D_MIX = D_MODEL
RMS_EPS = 1e-6
MLA_NOPE = 64
MLA_ROPE = 32
MLA_V = 64
MLA_HEADS = D_MIX // (2 * MLA_V)
MLA_Q_RANK = D_MODEL // 4
MLA_KV_RANK = D_MODEL // 8
ROPE_THETA = 10000.0
SB_DIM = 64
SB_HEADS = D_MIX // (4 * SB_DIM)
SB_WIDTH = SB_HEADS * SB_DIM
CA_DIM = 64
CA_HEADS = D_MIX // (4 * CA_DIM)
CA_WIDTH = CA_HEADS * CA_DIM
CA_LEFT_CHUNKS = 8
REL_CLIP = 256
D_IN = MLA_Q_RANK + MLA_KV_RANK + MLA_ROPE + 3 * SB_WIDTH + 3 * CA_WIDTH
N_EXPERTS = 32
TOP_K = 4
D_FF = D_MODEL
SWIGLU_LIMIT = 7.0
SWIGLU_ALPHA = 1.702
EXPERT_BLOCK = 128

kernel_name = "hybrid_mla_stickbreak_chunkbias_moe"


def _rmsnorm(x, g):
    xf = x.astype(jnp.float32)
    y = xf * lax.rsqrt(jnp.mean(xf * xf, axis=-1, keepdims=True) + RMS_EPS)
    return (y * g.astype(jnp.float32)).astype(x.dtype)


def _rope(x, pos):
    d = x.shape[-1]
    inv = ROPE_THETA ** (-jnp.arange(0, d, 2, dtype=jnp.float32) / d)
    ang = pos.astype(jnp.float32)[..., None] * inv
    cos = jnp.cos(ang)[:, :, None, :]
    sin = jnp.sin(ang)[:, :, None, :]
    xf = x.astype(jnp.float32)
    x1, x2 = jnp.split(xf, 2, axis=-1)
    return jnp.concatenate([x1 * cos - x2 * sin, x2 * cos + x1 * sin], axis=-1).astype(x.dtype)


def _to_qblocks(a):
    b, s = a.shape[:2]
    return jnp.moveaxis(a.reshape((b, s // Q_BLOCK, Q_BLOCK) + a.shape[2:]), 1, 0)


def _from_qblocks(o):
    nb, b, qb = o.shape[:3]
    return jnp.moveaxis(o, 0, 1).reshape((b, nb * qb) + o.shape[3:])


def _split_points():
    sizes = [MLA_Q_RANK, MLA_KV_RANK, MLA_ROPE, SB_WIDTH, SB_WIDTH, SB_WIDTH, CA_WIDTH, CA_WIDTH, CA_WIDTH]
    pts, acc = [], 0
    for s in sizes[:-1]:
        acc += s
        pts.append(acc)
    return pts


def _mla(cq, ckv, kpe, pos, q_norm, w_uq, kv_norm, w_ukv):
    b, s, _ = cq.shape
    q = (_rmsnorm(cq, q_norm) @ w_uq).reshape(b, s, MLA_HEADS, MLA_NOPE + MLA_ROPE)
    q_nope, q_pe = q[..., :MLA_NOPE], _rope(q[..., MLA_NOPE:], pos)
    kv = (_rmsnorm(ckv, kv_norm) @ w_ukv).reshape(b, s, MLA_HEADS, MLA_NOPE + MLA_V)
    k_nope, v = kv[..., :MLA_NOPE], kv[..., MLA_NOPE:]
    k_pe = _rope(kpe[:, :, None, :], pos)[:, :, 0]
    scale = 1.0 / math.sqrt(MLA_NOPE + MLA_ROPE)
    key_chunk = jnp.arange(s) // CHUNK

    def block(args):
        blk, qn, qp = args
        sc = (jnp.einsum('bqhd,bkhd->bhqk', qn, k_nope)
              + jnp.einsum('bqhd,bkd->bhqk', qp, k_pe)).astype(jnp.float32) * scale
        q_chunk = (blk * Q_BLOCK + jnp.arange(Q_BLOCK)) // CHUNK
        allowed = key_chunk[None, :] <= q_chunk[:, None]
        p = jax.nn.softmax(jnp.where(allowed, sc, -jnp.inf), axis=-1)
        return jnp.einsum('bhqk,bkhd->bqhd', p.astype(v.dtype), v)

    nb = s // Q_BLOCK
    o = lax.map(block, (jnp.arange(nb), _to_qblocks(q_nope), _to_qblocks(q_pe)))
    return _from_qblocks(o).reshape(b, s, MLA_HEADS * MLA_V)


def _stick_breaking(q, k, v):
    b, s, _ = q.shape
    q = q.reshape(b, s, SB_HEADS, SB_DIM)
    k = k.reshape(b, s, SB_HEADS, SB_DIM)
    v = v.reshape(b, s, SB_HEADS, SB_DIM)
    scale = 1.0 / math.sqrt(SB_DIM)
    key_idx = jnp.arange(s)

    def block(args):
        blk, qb = args
        z = jnp.einsum('bqhd,bkhd->bhqk', qb, k).astype(jnp.float32) * scale
        t = blk * Q_BLOCK + jnp.arange(Q_BLOCK)
        strict = key_idx[None, :] < t[:, None]
        log_keep = jnp.where(strict, jax.nn.log_sigmoid(-z), 0.0)
        between = lax.cumsum(log_keep, axis=3, reverse=True) - log_keep
        a = jnp.where(strict, jnp.exp(jax.nn.log_sigmoid(z) + between), 0.0)
        return jnp.einsum('bhqk,bkhd->bqhd', a.astype(v.dtype), v)

    nb = s // Q_BLOCK
    o = lax.map(block, (jnp.arange(nb), _to_qblocks(q)))
    return _from_qblocks(o).reshape(b, s, SB_WIDTH)


def _chunk_band_attention(q, k, v, rel_bias):
    b, s, _ = q.shape
    nc = s // CHUNK
    w = CA_LEFT_CHUNKS + 1
    qc = q.reshape(b, nc, CHUNK, CA_HEADS, CA_DIM)
    pad = ((0, 0), (CA_LEFT_CHUNKS, 0), (0, 0), (0, 0), (0, 0))
    kc = jnp.pad(k.reshape(b, nc, CHUNK, CA_HEADS, CA_DIM), pad)
    vc = jnp.pad(v.reshape(b, nc, CHUNK, CA_HEADS, CA_DIM), pad)
    idx = jnp.arange(nc)[:, None] + jnp.arange(w)[None, :]
    kb = kc[:, idx].reshape(b, nc, w * CHUNK, CA_HEADS, CA_DIM)
    vb = vc[:, idx].reshape(b, nc, w * CHUNK, CA_HEADS, CA_DIM)
    valid = jnp.repeat(idx >= CA_LEFT_CHUNKS, CHUNK, axis=1)
    rel = (CA_LEFT_CHUNKS * CHUNK + jnp.arange(CHUNK))[:, None] - jnp.arange(w * CHUNK)[None, :]
    bias = rel_bias[:, jnp.clip(rel, -REL_CLIP, REL_CLIP) + REL_CLIP]
    sc = jnp.einsum('bcqhd,bckhd->bhcqk', qc, kb).astype(jnp.float32) / math.sqrt(CA_DIM)
    sc = sc + bias[:, None].astype(jnp.float32)
    sc = jnp.where(valid[None, None, :, None, :], sc, -jnp.inf)
    p = jax.nn.softmax(sc, axis=-1)
    o = jnp.einsum('bhcqk,bckhd->bcqhd', p.astype(vb.dtype), vb)
    return o.reshape(b, s, CA_WIDTH)


def _moe(h, w_router, b_router, w_gu, b_gu, w_dn, b_dn):
    bsz, seq, d = h.shape
    n_tok = bsz * seq
    xt = h.reshape(n_tok, d)
    logits = (xt @ w_router + b_router).astype(jnp.float32)
    top_val, top_idx = lax.top_k(logits, TOP_K)
    gate = jax.nn.softmax(top_val, axis=-1)
    n_assign = n_tok * TOP_K
    e_flat = top_idx.reshape(n_assign)
    tok_flat = jnp.arange(n_assign, dtype=jnp.int32) // TOP_K
    g_flat = gate.reshape(n_assign)
    order = jnp.argsort(e_flat)
    e_sorted = e_flat[order]
    sizes = jnp.bincount(e_flat, length=N_EXPERTS)
    padded = (sizes + EXPERT_BLOCK - 1) // EXPERT_BLOCK * EXPERT_BLOCK
    starts = jnp.cumsum(sizes) - sizes
    p_ends = jnp.cumsum(padded)
    p_starts = p_ends - padded
    dest = p_starts[e_sorted] + jnp.arange(n_assign) - starts[e_sorted]
    n_rows = -(-(n_assign + N_EXPERTS * (EXPERT_BLOCK - 1)) // EXPERT_BLOCK) * EXPERT_BLOCK
    n_blk = n_rows // EXPERT_BLOCK
    row_tok = jnp.full((n_rows,), n_tok, jnp.int32).at[dest].set(tok_flat[order])
    row_gate = jnp.zeros((n_rows,), jnp.float32).at[dest].set(g_flat[order])
    blk_exp = jnp.minimum(jnp.searchsorted(p_ends, jnp.arange(n_blk) * EXPERT_BLOCK, side='right'),
                          N_EXPERTS - 1)
    x_pad = jnp.concatenate([xt, jnp.zeros((1, d), xt.dtype)], axis=0)
    xs = x_pad[row_tok].reshape(n_blk, EXPERT_BLOCK, d)

    def expert_rows(args):
        xb, e = args
        gu = xb @ w_gu[e] + b_gu[e]
        g, u = jnp.split(gu, 2, axis=-1)
        g = jnp.minimum(g, SWIGLU_LIMIT)
        u = jnp.clip(u, -SWIGLU_LIMIT, SWIGLU_LIMIT)
        act = (u + 1.0) * (g * jax.nn.sigmoid(SWIGLU_ALPHA * g))
        return act @ w_dn[e] + b_dn[e]

    ys = lax.map(expert_rows, (xs, blk_exp)).reshape(n_rows, d)
    ys = ys * row_gate[:, None].astype(ys.dtype)
    out = jnp.zeros((n_tok + 1, d), ys.dtype).at[row_tok].add(ys)
    return out[:n_tok].reshape(bsz, seq, d)


def setup_inputs(seed: int = 0) -> dict:
    key = jax.random.key(seed)
    ks = jax.random.split(key, 24)
    f32 = jnp.float32

    def nrm(k, shape, scale):
        return jax.random.normal(k, shape, f32) * scale

    def gain(k, shape):
        return 1.0 + 0.05 * jax.random.normal(k, shape, f32)

    L = DEPTH
    x = nrm(ks[0], (BATCH, SEQ, D_MODEL), 1.0)
    offs = jax.random.randint(ks[1], (BATCH, 1), 0, 256, dtype=jnp.int32) * CHUNK
    positions = offs + jnp.arange(SEQ, dtype=jnp.int32)[None, :]
    return {
        "x": x,
        "positions": positions,
        "attn_norm": gain(ks[2], (L, D_MODEL)),
        "w_in": nrm(ks[3], (L, D_MODEL, D_IN), D_MODEL ** -0.5),
        "q_norm": gain(ks[4], (L, MLA_Q_RANK)),
        "w_uq": nrm(ks[5], (L, MLA_Q_RANK, MLA_HEADS * (MLA_NOPE + MLA_ROPE)), MLA_Q_RANK ** -0.5),
        "kv_norm": gain(ks[6], (L, MLA_KV_RANK)),
        "w_ukv": nrm(ks[7], (L, MLA_KV_RANK, MLA_HEADS * (MLA_NOPE + MLA_V)), MLA_KV_RANK ** -0.5),
        "rel_bias": nrm(ks[8], (L, CA_HEADS, 2 * REL_CLIP + 1), 0.2),
        "mix_norm": gain(ks[9], (L, D_MIX)),
        "w_o": nrm(ks[10], (L, D_MIX, D_MODEL), D_MIX ** -0.5),
        "ffn_norm": gain(ks[11], (L, D_MODEL)),
        "w_router": nrm(ks[12], (L, D_MODEL, N_EXPERTS), D_MODEL ** -0.5),
        "b_router": nrm(ks[13], (L, N_EXPERTS), 0.01),
        "w_gate_up": nrm(ks[14], (L, N_EXPERTS, D_MODEL, 2 * D_FF), D_MODEL ** -0.5),
        "b_gate_up": nrm(ks[15], (L, N_EXPERTS, 2 * D_FF), 0.02),
        "w_down": nrm(ks[16], (L, N_EXPERTS, D_FF, D_MODEL), D_FF ** -0.5),
        "b_down": nrm(ks[17], (L, N_EXPERTS, D_MODEL), 0.02),
        "final_norm": gain(ks[18], (D_MODEL,)),
    }


def reference(x, positions, attn_norm, w_in, q_norm, w_uq, kv_norm, w_ukv, rel_bias, mix_norm,
              w_o, ffn_norm, w_router, b_router, w_gate_up, b_gate_up, w_down, b_down, final_norm):
    a_w = MLA_HEADS * MLA_V
    for l in range(DEPTH):
        h = _rmsnorm(x, attn_norm[l])
        proj = h @ w_in[l]
        cq, ckv, kpe, sb_q, sb_k, sb_v, ca_q, ca_k, ca_v = jnp.split(proj, _split_points(), axis=-1)
        o_a = _mla(cq, ckv, kpe, positions, q_norm[l], w_uq[l], kv_norm[l], w_ukv[l])
        o_b = _stick_breaking(sb_q, sb_k, sb_v)
        o_c = _chunk_band_attention(ca_q, ca_k, ca_v, rel_bias[l])
        g = mix_norm[l]
        mixed = jnp.concatenate([
            _rmsnorm(o_a, g[:a_w]),
            _rmsnorm(o_b, g[a_w:a_w + SB_WIDTH]),
            _rmsnorm(o_c, g[a_w + SB_WIDTH:]),
        ], axis=-1)
        x = x + mixed @ w_o[l]
        h = _rmsnorm(x, ffn_norm[l])
        x = x + _moe(h, w_router[l], b_router[l], w_gate_up[l], b_gate_up[l], w_down[l], b_down[l])
    return _rmsnorm(x, final_norm)
```

```python
import functools
import math

import jax
import jax.numpy as jnp
from jax import lax
from jax.experimental import pallas as pl
from jax.experimental.pallas import tpu as pltpu

D_MODEL = 1024
RMS_EPS = 1e-6
MLA_NOPE, MLA_ROPE, MLA_V, MLA_HEADS = 64, 32, 64, 8
MLA_Q_RANK, MLA_KV_RANK = 256, 128
ROPE_THETA = 10000.0
SB_DIM, SB_HEADS = 64, 4
CA_DIM, CA_HEADS = 64, 4
CHUNK = 64
CA_LEFT_CHUNKS = 8
REL_CLIP = 256
N_EXPERTS, TOP_K = 32, 4
D_FF = 1024
SWIGLU_LIMIT, SWIGLU_ALPHA = 7.0, 1.702

LANES = 128
HEAD_PAD = 128
A_W = MLA_HEADS * MLA_V
SB_W = SB_HEADS * SB_DIM
CA_W = CA_HEADS * CA_DIM
CA_PAD = CA_LEFT_CHUNKS * CHUNK
CA_WIN = CA_PAD + 2 * CHUNK
REL_TAB = 1024

TM_TOK = 256
TQ_MLA = 256
TQ_SB = 128
TQ_CA = 2 * CHUNK
TM_EXP = 256

_C_CQ, _C_CKV, _C_KPE, _C_KROT = 0, 256, 384, 512
_C_SB, _C_CA, D_IN2 = 640, 1408, 2176

NEG = -1e30
VMEM_LIMIT = 56 * 1024 * 1024


def _rms(v, g):
    return v * lax.rsqrt(jnp.mean(v * v, axis=-1, keepdims=True) + RMS_EPS) * g


def _dot(a, b):
    return jnp.dot(a, b, preferred_element_type=jnp.float32)


def _dot_nt(a, b):
    return lax.dot_general(a, b, (((1,), (1,)), ((), ())), preferred_element_type=jnp.float32)


def _bf(v):
    return v.astype(jnp.bfloat16)


def _cparams(sem):
    return pltpu.CompilerParams(dimension_semantics=sem, vmem_limit_bytes=VMEM_LIMIT)


def _rope_kernel(pos_ref, inv_ref, cos_ref, sin_ref):
    ang = pos_ref[...].astype(jnp.float32) * inv_ref[...]
    cos_ref[...] = jnp.cos(ang)
    sin_ref[...] = jnp.sin(ang)


def _rope_tables(pos_col, inv_row):
    n = pos_col.shape[0]
    return pl.pallas_call(
        _rope_kernel,
        grid=(n // TM_TOK,),
        in_specs=[pl.BlockSpec((TM_TOK, 1), lambda i: (i, 0)),
                  pl.BlockSpec((1, LANES), lambda i: (0, 0))],
        out_specs=[pl.BlockSpec((TM_TOK, LANES), lambda i: (i, 0))] * 2,
        out_shape=[jax.ShapeDtypeStruct((n, LANES), jnp.float32)] * 2,
        compiler_params=_cparams(("parallel",)),
        name="rope_tables",
    )(pos_col, inv_row)


def _inproj_kernel(x_ref, g_ref, win_ref, cos_ref, sin_ref, qn_ref, wq_ref, kvn_ref, wkv_ref,
                   q_ref, k_ref, v_ref, sbq_ref, sbk_ref, sbv_ref, caq_ref, cak_ref, cav_ref):
    h = _rms(x_ref[...], g_ref[...])
    proj = _dot(_bf(h), win_ref[...])
    cos128, sin128 = cos_ref[...], sin_ref[...]
    cos_h = jnp.concatenate([cos128] * MLA_HEADS, axis=1)
    sin_h = jnp.concatenate([sin128] * MLA_HEADS, axis=1)
    w = MLA_HEADS * HEAD_PAD

    cqn = _rms(proj[:, _C_CQ:_C_CKV], qn_ref[...])
    q2 = _dot(_bf(cqn), wq_ref[...])
    q = (q2[:, :w] * cos_h + q2[:, w:] * sin_h) * (1.0 / math.sqrt(MLA_NOPE + MLA_ROPE))
    q_ref[...] = _bf(q)

    ckvn = _rms(proj[:, _C_CKV:_C_KPE], kvn_ref[...])
    kv2 = _dot(_bf(ckvn), wkv_ref[...])
    kpe = proj[:, _C_KPE:_C_KROT] * cos128 + proj[:, _C_KROT:_C_SB] * sin128
    k_ref[...] = _bf(kv2[:, :w] + jnp.concatenate([kpe] * MLA_HEADS, axis=1))
    v_ref[...] = _bf(kv2[:, w:])

    sb_scale = 1.0 / math.sqrt(SB_DIM)
    sbq_ref[...] = _bf(proj[:, _C_SB:_C_SB + SB_W] * sb_scale)
    sbk_ref[...] = _bf(proj[:, _C_SB + SB_W:_C_SB + 2 * SB_W])
    sbv_ref[...] = _bf(proj[:, _C_SB + 2 * SB_W:_C_CA])
    ca_scale = 1.0 / math.sqrt(CA_DIM)
    caq_ref[...] = _bf(proj[:, _C_CA:_C_CA + CA_W] * ca_scale)
    cak_ref[...] = _bf(proj[:, _C_CA + CA_W:_C_CA + 2 * CA_W])
    cav_ref[...] = _bf(proj[:, _C_CA + 2 * CA_W:D_IN2])


def _inproj(x, g, win2, cos_t, sin_t, qn, wq2, kvn, wkv2):
    n = x.shape[0]
    tm = TM_TOK
    row = lambda width: pl.BlockSpec((tm, width), lambda i: (i, 0))
    full = lambda a: pl.BlockSpec(a.shape, lambda i: (0,) * a.ndim)
    widths = [MLA_HEADS * HEAD_PAD, MLA_HEADS * HEAD_PAD, A_W] + [SB_W] * 3 + [CA_W] * 3
    return pl.pallas_call(
        _inproj_kernel,
        grid=(n // tm,),
        in_specs=[row(D_MODEL), full(g), full(win2), row(LANES), row(LANES),
                  full(qn), full(wq2), full(kvn), full(wkv2)],
        out_specs=[row(wd) for wd in widths],
        out_shape=[jax.ShapeDtypeStruct((n, wd), jnp.bfloat16) for wd in widths],
        compiler_params=_cparams(("parallel",)),
        name="inproj",
    )(x, g, win2, cos_t, sin_t, qn, wq2, kvn, wkv2)


def _mla_kernel(q_ref, k_ref, v_ref, g_ref, o_ref):
    i = pl.program_id(1)
    tq = TQ_MLA
    rows = lax.broadcasted_iota(jnp.int32, (tq, tq), 0) // CHUNK
    cols = lax.broadcasted_iota(jnp.int32, (tq, tq), 1) // CHUNK
    diag_ok = cols <= rows
    lane = lax.broadcasted_iota(jnp.int32, (tq, LANES), 1)

    def step(qh, h, kb, carry, mask):
        m, l, acc = carry
        start = pl.multiple_of(kb * tq, tq)
        kblk = k_ref[pl.ds(start, tq), h * HEAD_PAD:(h + 1) * HEAD_PAD]
        vblk = v_ref[pl.ds(start, tq), (h // 2) * LANES:(h // 2 + 1) * LANES]
        s = _dot_nt(qh, kblk)
        if mask is not None:
            s = jnp.where(mask, s, -jnp.inf)
        m_new = jnp.maximum(m, jnp.max(s, axis=-1, keepdims=True))
        p = jnp.exp(s - m_new)
        alpha = jnp.exp(m - m_new)
        l = alpha * l + jnp.sum(p, axis=-1, keepdims=True)
        acc = alpha * acc + _dot(_bf(p), vblk)
        return m_new, l, acc

    outs = []
    for h in range(MLA_HEADS):
        qh = q_ref[:, h * HEAD_PAD:(h + 1) * HEAD_PAD]
        init = (jnp.full((tq, 1), -jnp.inf, jnp.float32), jnp.zeros((tq, 1), jnp.float32),
                jnp.zeros((tq, LANES), jnp.float32))
        carry = lax.fori_loop(0, i, lambda kb, c: step(qh, h, kb, c, None), init)
        _, l, acc = step(qh, h, i, carry, diag_ok)
        outs.append(acc / l)
    pairs = [jnp.where(lane < MLA_V, outs[2 * p], outs[2 * p + 1]) for p in range(MLA_HEADS // 2)]
    o = jnp.concatenate(pairs, axis=1)
    o_ref[...] = _bf(_rms(o, g_ref[...]))


def _mla_attention(q, k, v, g, batch, seq):
    nq = seq // TQ_MLA
    w = MLA_HEADS * HEAD_PAD
    return pl.pallas_call(
        _mla_kernel,
        grid=(batch, nq),
        in_specs=[pl.BlockSpec((TQ_MLA, w), lambda b, i: (b * nq + i, 0)),
                  pl.BlockSpec((seq, w), lambda b, i: (b, 0)),
                  pl.BlockSpec((seq, A_W), lambda b, i: (b, 0)),
                  pl.BlockSpec((1, A_W), lambda b, i: (0, 0))],
        out_specs=pl.BlockSpec((TQ_MLA, A_W), lambda b, i: (b * nq + i, 0)),
        out_shape=jax.ShapeDtypeStruct((batch * seq, A_W), jnp.bfloat16),
        compiler_params=_cparams(("parallel", "parallel")),
        name="mla_attention",
    )(q, k, v, g)


def _sb_kernel(q_ref, k_ref, v_ref, g_ref, o_ref):
    i = pl.program_id(1)
    t = TQ_SB
    r = lax.broadcasted_iota(jnp.int32, (t, t), 0)
    c = lax.broadcasted_iota(jnp.int32, (t, t), 1)
    strict = c < r
    r2 = lax.broadcasted_iota(jnp.int32, (t, 2 * t), 0)
    c2 = lax.broadcasted_iota(jnp.int32, (t, 2 * t), 1)
    sum_mat = jnp.where((c2 >= t) | (r2 > c2), 1.0, 0.0).astype(jnp.bfloat16)
    lane = lax.broadcasted_iota(jnp.int32, (t, LANES), 1)

    def step(qm, pair, kb, carry, mask):
        run, acc = carry
        start = pl.multiple_of(kb * t, t)
        kblk = k_ref[pl.ds(start, t), pair * LANES:(pair + 1) * LANES]
        vblk = v_ref[pl.ds(start, t), pair * LANES:(pair + 1) * LANES]
        z = _dot_nt(qm, kblk)
        log_keep = -(jnp.maximum(z, 0.0) + jnp.log1p(jnp.exp(-jnp.abs(z))))
        if mask is not None:
            log_keep = jnp.where(mask, log_keep, 0.0)
        hi = _bf(log_keep)
        lo = _bf(log_keep - hi.astype(jnp.float32))
        sums = _dot(hi, sum_mat) + _dot(lo, sum_mat)
        a = jnp.exp(z + log_keep + run + sums[:, :t])
        if mask is not None:
            a = jnp.where(mask, a, 0.0)
        acc = acc + _dot(_bf(a), vblk)
        return run + sums[:, t:], acc

    outs = []
    for h in range(SB_HEADS):
        pair, half = h // 2, h % 2
        qp = q_ref[:, pair * LANES:(pair + 1) * LANES]
        qm = jnp.where((lane >= half * SB_DIM) & (lane < (half + 1) * SB_DIM), qp, jnp.zeros_like(qp))
        init = (jnp.zeros((t, t), jnp.float32), jnp.zeros((t, LANES), jnp.float32))
        carry = step(qm, pair, i, init, strict)
        _, acc = lax.fori_loop(0, i, lambda n, cr: step(qm, pair, i - 1 - n, cr, None), carry)
        outs.append(acc)
    pairs = [jnp.where(lane < SB_DIM, outs[2 * p], outs[2 * p + 1]) for p in range(SB_HEADS // 2)]
    o = jnp.concatenate(pairs, axis=1)
    o_ref[...] = _bf(_rms(o, g_ref[...]))


def _sb_attention(q, k, v, g, batch, seq):
    nq = seq // TQ_SB
    return pl.pallas_call(
        _sb_kernel,
        grid=(batch, nq),
        in_specs=[pl.BlockSpec((TQ_SB, SB_W), lambda b, i: (b * nq + i, 0)),
                  pl.BlockSpec((seq, SB_W), lambda b, i: (b, 0)),
                  pl.BlockSpec((seq, SB_W), lambda b, i: (b, 0)),
                  pl.BlockSpec((1, SB_W), lambda b, i: (0, 0))],
        out_specs=pl.BlockSpec((TQ_SB, SB_W), lambda b, i: (b * nq + i, 0)),
        out_shape=jax.ShapeDtypeStruct((batch * seq, SB_W), jnp.bfloat16),
        compiler_params=_cparams(("parallel", "parallel")),
        name="sb_attention",
    )(q, k, v, g)


def _ca_kernel(q_ref, k_ref, v_ref, tab_ref, g_ref, o_ref):
    i = pl.program_id(1)
    t = TQ_CA
    start = pl.multiple_of(i * t, t)
    r = lax.broadcasted_iota(jnp.int32, (t, CA_WIN), 0)
    c = lax.broadcasted_iota(jnp.int32, (t, CA_WIN), 1)
    lo = (r // CHUNK) * CHUNK
    valid = (c >= lo) & (c < lo + CA_PAD + CHUNK) & (c + i * t >= CA_PAD)
    lane = lax.broadcasted_iota(jnp.int32, (t, LANES), 1)

    outs = []
    for h in range(CA_HEADS):
        pair, half = h // 2, h % 2
        qp = q_ref[:, pair * LANES:(pair + 1) * LANES]
        qm = jnp.where((lane >= half * CA_DIM) & (lane < (half + 1) * CA_DIM), qp, jnp.zeros_like(qp))
        kwin = k_ref[pl.ds(start, CA_WIN), pair * LANES:(pair + 1) * LANES]
        vwin = v_ref[pl.ds(start, CA_WIN), pair * LANES:(pair + 1) * LANES]
        tab = jnp.broadcast_to(tab_ref[h:h + 1, :], (t, REL_TAB))
        bias = pltpu.roll(tab, REL_TAB - (REL_CLIP - 1), 1, stride=1, stride_axis=0)[:, :CA_WIN]
        s = jnp.where(valid, _dot_nt(qm, kwin) + bias, -jnp.inf)
        m = jnp.max(s, axis=-1, keepdims=True)
        p = jnp.exp(s - m)
        l = jnp.sum(p, axis=-1, keepdims=True)
        outs.append(_dot(_bf(p), vwin) / l)
    pairs = [jnp.where(lane < CA_DIM, outs[2 * p], outs[2 * p + 1]) for p in range(CA_HEADS // 2)]
    o = jnp.concatenate(pairs, axis=1)
    o_ref[...] = _bf(_rms(o, g_ref[...]))


def _ca_attention(q, kpad, vpad, tab, g, batch, seq):
    nq = seq // TQ_CA
    return pl.pallas_call(
        _ca_kernel,
        grid=(batch, nq),
        in_specs=[pl.BlockSpec((TQ_CA, CA_W), lambda b, i: (b * nq + i, 0)),
                  pl.BlockSpec((seq + CA_PAD, CA_W), lambda b, i: (b, 0)),
                  pl.BlockSpec((seq + CA_PAD, CA_W), lambda b, i: (b, 0)),
                  pl.BlockSpec((CA_HEADS, REL_TAB), lambda b, i: (0, 0)),
                  pl.BlockSpec((1, CA_W), lambda b, i: (0, 0))],
        out_specs=pl.BlockSpec((TQ_CA, CA_W), lambda b, i: (b * nq + i, 0)),
        out_shape=jax.ShapeDtypeStruct((batch * seq, CA_W), jnp.bfloat16),
        compiler_params=_cparams(("parallel", "parallel")),
        name="ca_attention",
    )(q, kpad, vpad, tab, g)


def _outproj_router_kernel(ma_ref, mb_ref, mc_ref, wo_ref, x_ref, g_ref, wrh_ref, wrl_ref, br_ref,
                           xn_ref, h_ref, idx_ref, gate_ref, cnt_ref, base_ref):
    tm = TM_TOK

    @pl.when(pl.program_id(0) == 0)
    def _():
        base_ref[...] = jnp.zeros_like(base_ref)

    attn = (_dot(ma_ref[...], wo_ref[0:A_W, :]) + _dot(mb_ref[...], wo_ref[A_W:A_W + SB_W, :])
            + _dot(mc_ref[...], wo_ref[A_W + SB_W:, :]))
    xn = x_ref[...] + attn
    xn_ref[...] = xn
    h = _rms(xn, g_ref[...])
    h_ref[...] = h

    h_hi = _bf(h)
    h_lo = _bf(h - h_hi.astype(jnp.float32))
    logits = (_dot(h_hi, wrh_ref[...]) + _dot(h_hi, wrl_ref[...]) + _dot(h_lo, wrh_ref[...])
              + br_ref[...])
    lane = lax.broadcasted_iota(jnp.int32, (tm, LANES), 1)
    lane_f = lane.astype(jnp.float32)

    work = logits
    vals, idxs, hots = [], [], []
    for _ in range(TOP_K):
        mx = jnp.max(work, axis=-1, keepdims=True)
        ix = jnp.min(jnp.where(work == mx, lane_f, float(LANES)), axis=-1, keepdims=True)
        hot = lane_f == ix
        work = jnp.where(hot, -jnp.inf, work)
        vals.append(mx)
        idxs.append(ix)
        hots.append(hot)
    exps = [jnp.exp(v - vals[0]) for v in vals]
    denom = exps[0] + exps[1] + exps[2] + exps[3]
    gates = [e / denom for e in exps]

    sel = jnp.zeros((tm, LANES), jnp.float32)
    for hot in hots:
        sel = sel + jnp.where(hot, 1.0, 0.0)
    r = lax.broadcasted_iota(jnp.int32, (tm, tm), 0)
    c = lax.broadcasted_iota(jnp.int32, (tm, tm), 1)
    before = jnp.where(c < r, 1.0, 0.0).astype(jnp.bfloat16)
    rank_dense = base_ref[...] + _dot(before, _bf(sel))
    base_ref[...] = base_ref[...] + jnp.sum(sel, axis=0, keepdims=True)
    cnt_ref[...] = base_ref[...].astype(jnp.int32)

    packed = jnp.zeros((tm, LANES), jnp.float32)
    gate_lanes = jnp.zeros((tm, LANES), jnp.float32)
    for kk in range(TOP_K):
        rank = jnp.sum(jnp.where(hots[kk], rank_dense, 0.0), axis=-1, keepdims=True)
        packed = packed + jnp.where(lane == kk, idxs[kk], 0.0) + jnp.where(lane == TOP_K + kk, rank, 0.0)
        gate_lanes = gate_lanes + jnp.where(lane == kk, gates[kk], 0.0)
    gate_ref[...] = gate_lanes
    idx_ref[...] = jnp.transpose(packed)[0:2 * TOP_K, :].astype(jnp.int32)


def _outproj_router(ma, mb, mc, wo, x, g, wrh, wrl, br):
    n = x.shape[0]
    tm = TM_TOK
    row = lambda width: pl.BlockSpec((tm, width), lambda i: (i, 0))
    full = lambda a: pl.BlockSpec(a.shape, lambda i: (0,) * a.ndim)
    return pl.pallas_call(
        _outproj_router_kernel,
        grid=(n // tm,),
        in_specs=[row(A_W), row(SB_W), row(CA_W), full(wo), row(D_MODEL), full(g),
                  full(wrh), full(wrl), full(br)],
        out_specs=[row(D_MODEL), row(D_MODEL), pl.BlockSpec((2 * TOP_K, tm), lambda i: (0, i)),
                   row(LANES), pl.BlockSpec((1, LANES), lambda i: (0, 0))],
        out_shape=[jax.ShapeDtypeStruct((n, D_MODEL), jnp.float32),
                   jax.ShapeDtypeStruct((n, D_MODEL), jnp.float32),
                   jax.ShapeDtypeStruct((2 * TOP_K, n), jnp.int32),
                   jax.ShapeDtypeStruct((n, LANES), jnp.float32),
                   jax.ShapeDtypeStruct((1, LANES), jnp.int32)],
        scratch_shapes=[pltpu.VMEM((1, LANES), jnp.float32)],
        compiler_params=_cparams(("arbitrary",)),
        name="outproj_router",
    )(ma, mb, mc, wo, x, g, wrh, wrl, br)


def _row_copy(src_ref, src_row, dst_ref, dst_row, sem):
    return pltpu.make_async_copy(src_ref.at[pl.ds(src_row, 1)], dst_ref.at[pl.ds(dst_row, 1)], sem)


def _scatter_kernel(pstart_ref, idx_ref, h_ref, xs_in_ref, xs_ref, sem):
    del xs_in_ref
    tm = TM_TOK

    def issue(t, _):
        for kk in range(TOP_K):
            dst = pstart_ref[idx_ref[kk, t]] + idx_ref[TOP_K + kk, t]
            _row_copy(h_ref, t, xs_ref, dst, sem).start()
        return 0

    lax.fori_loop(0, tm, issue, 0)

    def drain(t, _):
        for kk in range(TOP_K):
            _row_copy(h_ref, 0, xs_ref, 0, sem).wait()
        return 0

    lax.fori_loop(0, tm, drain, 0)


def _scatter_rows(pstart, idx8, h, xs_zero):
    n = h.shape[0]
    tm = TM_TOK
    grid_spec = pltpu.PrefetchScalarGridSpec(
        num_scalar_prefetch=1,
        grid=(n // tm,),
        in_specs=[pl.BlockSpec((2 * TOP_K, tm), lambda i, ps: (0, i), memory_space=pltpu.SMEM),
                  pl.BlockSpec((tm, D_MODEL), lambda i, ps: (i, 0)),
                  pl.BlockSpec(memory_space=pl.ANY)],
        out_specs=pl.BlockSpec(memory_space=pl.ANY),
        scratch_shapes=[pltpu.SemaphoreType.DMA],
    )
    return pl.pallas_call(
        _scatter_kernel,
        grid_spec=grid_spec,
        out_shape=jax.ShapeDtypeStruct(xs_zero.shape, xs_zero.dtype),
        input_output_aliases={3: 0},
        compiler_params=_cparams(("arbitrary",)),
        name="scatter_rows",
    )(pstart, idx8, h, xs_zero)


def _expert_kernel(be_ref, bfirst_ref, bvalid_ref, xs_ref, wgu_ref, bgu_ref, wdn_ref, bdn_ref,
                   ys_ref, wgu_bf, wdn_bf):
    b = pl.program_id(0)

    @pl.when(bfirst_ref[b] == 1)
    def _():
        wgu_bf[...] = _bf(wgu_ref[0])
        wdn_bf[...] = _bf(wdn_ref[0])

    @pl.when(bvalid_ref[b] == 1)
    def _():
        gu = _dot(_bf(xs_ref[...]), wgu_bf[...]) + bgu_ref[0]
        gte = jnp.minimum(gu[:, :D_FF], SWIGLU_LIMIT)
        up = jnp.clip(gu[:, D_FF:], -SWIGLU_LIMIT, SWIGLU_LIMIT)
        act = (up + 1.0) * (gte * (1.0 / (1.0 + jnp.exp(-SWIGLU_ALPHA * gte))))
        ys_ref[...] = _dot(_bf(act), wdn_bf[...]) + bdn_ref[0]

    @pl.when(bvalid_ref[b] == 0)
    def _():
        ys_ref[...] = jnp.zeros_like(ys_ref)


def _expert_ffn(blk_e, blk_first, blk_valid, xs, wgu, bgu, wdn, bdn):
    n_rows = xs.shape[0]
    tm = TM_EXP
    grid_spec = pltpu.PrefetchScalarGridSpec(
        num_scalar_prefetch=3,
        grid=(n_rows // tm,),
        in_specs=[pl.BlockSpec((tm, D_MODEL), lambda b, e, f, v: (b, 0)),
                  pl.BlockSpec((1, D_MODEL, 2 * D_FF), lambda b, e, f, v: (e[b], 0, 0)),
                  pl.BlockSpec((1, 1, 2 * D_FF), lambda b, e, f, v: (e[b], 0, 0)),
                  pl.BlockSpec((1, D_FF, D_MODEL), lambda b, e, f, v: (e[b], 0, 0)),
                  pl.BlockSpec((1, 1, D_MODEL), lambda b, e, f, v: (e[b], 0, 0))],
        out_specs=pl.BlockSpec((tm, D_MODEL), lambda b, e, f, v: (b, 0)),
        scratch_shapes=[pltpu.VMEM((D_MODEL, 2 * D_FF), jnp.bfloat16),
                        pltpu.VMEM((D_FF, D_MODEL), jnp.bfloat16)],
    )
    return pl.pallas_call(
        _expert_kernel,
        grid_spec=grid_spec,
        out_shape=jax.ShapeDtypeStruct((n_rows, D_MODEL), jnp.float32),
        compiler_params=_cparams(("arbitrary",)),
        name="expert_ffn",
    )(blk_e, blk_first, blk_valid, xs, wgu, bgu, wdn, bdn)


def _combine_kernel(final, pstart_ref, idx_ref, gate_ref, x_ref, ys_ref, gfin_ref, o_ref, buf, sem):
    tm = TM_TOK

    def issue(t, _):
        for kk in range(TOP_K):
            src = pstart_ref[idx_ref[kk, t]] + idx_ref[TOP_K + kk, t]
            pltpu.make_async_copy(ys_ref.at[pl.ds(src, 1)], buf.at[kk, pl.ds(t, 1)], sem).start()
        return 0

    lax.fori_loop(0, tm, issue, 0)

    def drain(t, _):
        for kk in range(TOP_K):
            pltpu.make_async_copy(ys_ref.at[pl.ds(0, 1)], buf.at[kk, pl.ds(0, 1)], sem).wait()
        return 0

    lax.fori_loop(0, tm, drain, 0)

    gates = gate_ref[...]
    out = x_ref[...]
    for kk in range(TOP_K):
        out = out + buf[kk] * gates[:, kk:kk + 1]
    if final:
        out = _rms(out, gfin_ref[...])
    o_ref[...] = out


def _combine(pstart, idx8, gates, x, ys, gfin, final):
    n = x.shape[0]
    tm = TM_TOK
    grid_spec = pltpu.PrefetchScalarGridSpec(
        num_scalar_prefetch=1,
        grid=(n // tm,),
        in_specs=[pl.BlockSpec((2 * TOP_K, tm), lambda i, ps: (0, i), memory_space=pltpu.SMEM),
                  pl.BlockSpec((tm, LANES), lambda i, ps: (i, 0)),
                  pl.BlockSpec((tm, D_MODEL), lambda i, ps: (i, 0)),
                  pl.BlockSpec(memory_space=pl.ANY),
                  pl.BlockSpec((1, D_MODEL), lambda i, ps: (0, 0))],
        out_specs=pl.BlockSpec((tm, D_MODEL), lambda i, ps: (i, 0)),
        scratch_shapes=[pltpu.VMEM((TOP_K, tm, D_MODEL), jnp.float32), pltpu.SemaphoreType.DMA],
    )
    return pl.pallas_call(
        functools.partial(_combine_kernel, final),
        grid_spec=grid_spec,
        out_shape=jax.ShapeDtypeStruct((n, D_MODEL), jnp.float32),
        compiler_params=_cparams(("arbitrary",)),
        name="combine_final" if final else "combine",
    )(pstart, idx8, gates, x, ys, gfin)


def _pad_heads(w, parts):
    rows = w.shape[0]
    per = w.shape[1] // MLA_HEADS
    w3 = w.reshape(rows, MLA_HEADS, per)
    cols = [jnp.zeros((rows, MLA_HEADS, b - a), w.dtype) if sign == 0 else sign * w3[:, :, a:b]
            for a, b, sign in parts]
    used = sum(b - a for a, b, _ in parts)
    cols.append(jnp.zeros((rows, MLA_HEADS, HEAD_PAD - used), w.dtype))
    return jnp.concatenate(cols, axis=2).reshape(rows, MLA_HEADS * HEAD_PAD)


def _layer_weights(w_in, w_uq, w_ukv, w_router, b_router, rel_bias):
    half = MLA_ROPE // 2
    cq, ckv, kpe, sb, ca = (w_in[:, 0:256], w_in[:, 256:384], w_in[:, 384:416],
                            w_in[:, 416:1184], w_in[:, 1184:1952])
    z = lambda width: jnp.zeros((D_MODEL, width), w_in.dtype)
    kpe_pad = jnp.concatenate([z(MLA_NOPE), kpe, z(HEAD_PAD - MLA_NOPE - MLA_ROPE)], axis=1)
    kpe_rot = jnp.concatenate([z(MLA_NOPE), -kpe[:, half:], kpe[:, :half],
                               z(HEAD_PAD - MLA_NOPE - MLA_ROPE)], axis=1)
    win2 = jnp.concatenate([cq, ckv, kpe_pad, kpe_rot, sb, ca], axis=1).astype(jnp.bfloat16)

    d = MLA_NOPE + MLA_ROPE
    wq_full = _pad_heads(w_uq, [(0, d, 1)])
    wq_rot = _pad_heads(w_uq, [(0, MLA_NOPE, 0), (MLA_NOPE + half, d, -1), (MLA_NOPE, MLA_NOPE + half, 1)])
    wq2 = jnp.concatenate([wq_full, wq_rot], axis=1).astype(jnp.bfloat16)

    wk = _pad_heads(w_ukv, [(0, MLA_NOPE, 1)])
    wv = w_ukv.reshape(MLA_KV_RANK, MLA_HEADS, MLA_NOPE + MLA_V)[:, :, MLA_NOPE:].reshape(MLA_KV_RANK, A_W)
    wkv2 = jnp.concatenate([wk, wv], axis=1).astype(jnp.bfloat16)

    wr = jnp.pad(w_router, ((0, 0), (0, LANES - N_EXPERTS)))
    wr_hi = wr.astype(jnp.bfloat16)
    wr_lo = (wr - wr_hi.astype(jnp.float32)).astype(jnp.bfloat16)
    br = jnp.pad(b_router, (0, LANES - N_EXPERTS), constant_values=NEG).reshape(1, LANES)

    ext = jnp.concatenate([rel_bias, jnp.broadcast_to(rel_bias[:, -1:], (CA_HEADS, REL_TAB - 2 * REL_CLIP - 1))], axis=1)
    tab = ext[:, ::-1]
    return win2, wq2, wkv2, wr_hi, wr_lo, br, tab


def kernel(x, positions, attn_norm, w_in, q_norm, w_uq, kv_norm, w_ukv, rel_bias, mix_norm,
           w_o, ffn_norm, w_router, b_router, w_gate_up, b_gate_up, w_down, b_down, final_norm):
    batch, seq, _ = x.shape
    n = batch * seq
    depth = w_in.shape[0]
    xf = x.reshape(n, D_MODEL)

    inv = ROPE_THETA ** (-jnp.arange(0, MLA_ROPE, 2, dtype=jnp.float32) / MLA_ROPE)
    inv_row = jnp.concatenate([jnp.zeros((MLA_NOPE,), jnp.float32), inv, inv,
                               jnp.zeros((HEAD_PAD - MLA_NOPE - MLA_ROPE,), jnp.float32)]).reshape(1, LANES)
    cos_t, sin_t = _rope_tables(positions.reshape(n, 1), inv_row)

    n_rows = n * TOP_K + N_EXPERTS * TM_EXP
    n_blk = n_rows // TM_EXP
    row2 = lambda v: v.reshape(1, -1)

    for l in range(depth):
        win2, wq2, wkv2, wr_hi, wr_lo, br, tab = _layer_weights(
            w_in[l], w_uq[l], w_ukv[l], w_router[l], b_router[l], rel_bias[l])
        q, k, v, sbq, sbk, sbv, caq, cak, cav = _inproj(
            xf, row2(attn_norm[l]), win2, cos_t, sin_t, row2(q_norm[l]), wq2, row2(kv_norm[l]), wkv2)

        g = mix_norm[l]
        ma = _mla_attention(q, k, v, row2(g[:A_W]), batch, seq)
        mb = _sb_attention(sbq, sbk, sbv, row2(g[A_W:A_W + SB_W]), batch, seq)
        pad = lambda a: jnp.pad(a.reshape(batch, seq, CA_W), ((0, 0), (CA_PAD, 0), (0, 0))).reshape(-1, CA_W)
        mc = _ca_attention(caq, pad(cak), pad(cav), tab, row2(g[A_W + SB_W:]), batch, seq)

        xn, h, idx8, gates, counts = _outproj_router(
            ma, mb, mc, w_o[l].astype(jnp.bfloat16), xf, row2(ffn_norm[l]), wr_hi, wr_lo, br)

        sizes = counts[0, :N_EXPERTS]
        padded = (sizes + TM_EXP - 1) // TM_EXP * TM_EXP
        p_ends = jnp.cumsum(padded)
        p_starts = (p_ends - padded).astype(jnp.int32)
        blk_start = jnp.arange(n_blk, dtype=jnp.int32) * TM_EXP
        blk_valid = (blk_start < p_ends[-1]).astype(jnp.int32)
        blk_e = jnp.minimum(jnp.searchsorted(p_ends, jnp.minimum(blk_start, p_ends[-1] - 1), side='right'),
                            N_EXPERTS - 1).astype(jnp.int32)
        blk_first = jnp.concatenate([jnp.ones((1,), jnp.int32), (blk_e[1:] != blk_e[:-1]).astype(jnp.int32)])

        xs = _scatter_rows(p_starts, idx8, h, jnp.zeros((n_rows, D_MODEL), jnp.float32))
        ys = _expert_ffn(blk_e, blk_first, blk_valid, xs, w_gate_up[l],
                         b_gate_up[l].reshape(N_EXPERTS, 1, 2 * D_FF), w_down[l],
                         b_down[l].reshape(N_EXPERTS, 1, D_MODEL))
        xf = _combine(p_starts, idx8, gates, xn, ys, row2(final_norm), final=(l == depth - 1))

    return xf.reshape(batch, seq, D_MODEL)
```

```python
import functools
import math

import jax
import jax.numpy as jnp
from jax import lax
from jax.experimental import pallas as pl
from jax.experimental.pallas import tpu as pltpu

D_MODEL = 1024
RMS_EPS = 1e-6
MLA_NOPE, MLA_ROPE, MLA_V, MLA_HEADS = 64, 32, 64, 8
MLA_Q_RANK, MLA_KV_RANK = 256, 128
ROPE_THETA = 10000.0
SB_DIM, SB_HEADS = 64, 4
CA_DIM, CA_HEADS = 64, 4
CHUNK = 64
CA_LEFT_CHUNKS = 8
REL_CLIP = 256
N_EXPERTS, TOP_K = 32, 4
D_FF = 1024
SWIGLU_LIMIT, SWIGLU_ALPHA = 7.0, 1.702

LANES = 128
HEAD_PAD = 128
A_W = MLA_HEADS * MLA_V
SB_W = SB_HEADS * SB_DIM
CA_W = CA_HEADS * CA_DIM
CA_PAD = CA_LEFT_CHUNKS * CHUNK
CA_WIN = CA_PAD + 2 * CHUNK
REL_TAB = 1024

TM_TOK = 256
TQ_MLA = 256
TQ_SB = 128
TQ_CA = 2 * CHUNK
TM_EXP = 256

_C_CQ, _C_CKV, _C_KPE, _C_KROT = 0, 256, 384, 512
_C_SB, _C_CA, D_IN2 = 640, 1408, 2176

NEG = -1e30
VMEM_LIMIT = 56 * 1024 * 1024


def _rms(v, g):
    return v * lax.rsqrt(jnp.mean(v * v, axis=-1, keepdims=True) + RMS_EPS) * g


def _dot(a, b):
    return jnp.dot(a, b, preferred_element_type=jnp.float32)


def _dot_nt(a, b):
    return lax.dot_general(a, b, (((1,), (1,)), ((), ())), preferred_element_type=jnp.float32)


def _bf(v):
    return v.astype(jnp.bfloat16)


def _cparams(sem):
    return pltpu.CompilerParams(dimension_semantics=sem, vmem_limit_bytes=VMEM_LIMIT)


def _rope_kernel(pos_ref, inv_ref, cos_ref, sin_ref):
    ang = pos_ref[...].astype(jnp.float32) * inv_ref[...]
    cos_ref[...] = jnp.cos(ang)
    sin_ref[...] = jnp.sin(ang)


def _rope_tables(pos_col, inv_row):
    n = pos_col.shape[0]
    return pl.pallas_call(
        _rope_kernel,
        grid=(n // TM_TOK,),
        in_specs=[pl.BlockSpec((TM_TOK, 1), lambda i: (i, 0)),
                  pl.BlockSpec((1, LANES), lambda i: (0, 0))],
        out_specs=[pl.BlockSpec((TM_TOK, LANES), lambda i: (i, 0))] * 2,
        out_shape=[jax.ShapeDtypeStruct((n, LANES), jnp.float32)] * 2,
        compiler_params=_cparams(("parallel",)),
        name="rope_tables",
    )(pos_col, inv_row)


def _inproj_kernel(x_ref, g_ref, win_ref, cos_ref, sin_ref, qn_ref, wq_ref, kvn_ref, wkv_ref,
                   q_ref, k_ref, v_ref, sbq_ref, sbk_ref, sbv_ref, caq_ref, cak_ref, cav_ref):
    h = _rms(x_ref[...], g_ref[...])
    proj = _dot(_bf(h), win_ref[...])
    cos128, sin128 = cos_ref[...], sin_ref[...]
    cos_h = jnp.concatenate([cos128] * MLA_HEADS, axis=1)
    sin_h = jnp.concatenate([sin128] * MLA_HEADS, axis=1)
    w = MLA_HEADS * HEAD_PAD

    cqn = _rms(proj[:, _C_CQ:_C_CKV], qn_ref[...])
    q2 = _dot(_bf(cqn), wq_ref[...])
    q = (q2[:, :w] * cos_h + q2[:, w:] * sin_h) * (1.0 / math.sqrt(MLA_NOPE + MLA_ROPE))
    q_ref[...] = _bf(q)

    ckvn = _rms(proj[:, _C_CKV:_C_KPE], kvn_ref[...])
    kv2 = _dot(_bf(ckvn), wkv_ref[...])
    kpe = proj[:, _C_KPE:_C_KROT] * cos128 + proj[:, _C_KROT:_C_SB] * sin128
    k_ref[...] = _bf(kv2[:, :w] + jnp.concatenate([kpe] * MLA_HEADS, axis=1))
    v_ref[...] = _bf(kv2[:, w:])

    sb_scale = 1.0 / math.sqrt(SB_DIM)
    sbq_ref[...] = _bf(proj[:, _C_SB:_C_SB + SB_W] * sb_scale)
    sbk_ref[...] = _bf(proj[:, _C_SB + SB_W:_C_SB + 2 * SB_W])
    sbv_ref[...] = _bf(proj[:, _C_SB + 2 * SB_W:_C_CA])
    ca_scale = 1.0 / math.sqrt(CA_DIM)
    caq_ref[...] = _bf(proj[:, _C_CA:_C_CA + CA_W] * ca_scale)
    cak_ref[...] = _bf(proj[:, _C_CA + CA_W:_C_CA + 2 * CA_W])
    cav_ref[...] = _bf(proj[:, _C_CA + 2 * CA_W:D_IN2])


def _inproj(x, g, win2, cos_t, sin_t, qn, wq2, kvn, wkv2):
    n = x.shape[0]
    tm = TM_TOK
    row = lambda width: pl.BlockSpec((tm, width), lambda i: (i, 0))
    full = lambda a: pl.BlockSpec(a.shape, lambda i: (0,) * a.ndim)
    widths = [MLA_HEADS * HEAD_PAD, MLA_HEADS * HEAD_PAD, A_W] + [SB_W] * 3 + [CA_W] * 3
    return pl.pallas_call(
        _inproj_kernel,
        grid=(n // tm,),
        in_specs=[row(D_MODEL), full(g), full(win2), row(LANES), row(LANES),
                  full(qn), full(wq2), full(kvn), full(wkv2)],
        out_specs=[row(wd) for wd in widths],
        out_shape=[jax.ShapeDtypeStruct((n, wd), jnp.bfloat16) for wd in widths],
        compiler_params=_cparams(("parallel",)),
        name="inproj",
    )(x, g, win2, cos_t, sin_t, qn, wq2, kvn, wkv2)


def _mla_kernel(q_ref, k_ref, v_ref, g_ref, o_ref, m_ref, l_ref, acc_ref):
    i = pl.program_id(1)
    tq = TQ_MLA
    n_pairs = MLA_HEADS // 2
    rows = lax.broadcasted_iota(jnp.int32, (tq, tq), 0) // CHUNK
    cols = lax.broadcasted_iota(jnp.int32, (tq, tq), 1) // CHUNK
    diag_ok = cols <= rows
    low = lax.broadcasted_iota(jnp.int32, (tq, LANES), 1) < MLA_V

    m_ref[...] = jnp.full(m_ref.shape, -jnp.inf, jnp.float32)
    l_ref[...] = jnp.zeros(l_ref.shape, jnp.float32)
    acc_ref[...] = jnp.zeros(acc_ref.shape, jnp.float32)

    def block(kb, pairs, mask):
        start = pl.multiple_of(kb * tq, tq)
        for pr in pairs:
            vpair = v_ref[pl.ds(start, tq), pr * LANES:(pr + 1) * LANES]
            zero = jnp.zeros_like(vpair)
            v_bd = jnp.concatenate([jnp.where(low, vpair, zero), jnp.where(low, zero, vpair)], axis=0)
            ps, alphas = [], []
            for h in (2 * pr, 2 * pr + 1):
                qh = q_ref[:, h * HEAD_PAD:(h + 1) * HEAD_PAD]
                kblk = k_ref[pl.ds(start, tq), h * HEAD_PAD:(h + 1) * HEAD_PAD]
                s = _dot_nt(qh, kblk)
                if mask is not None:
                    s = jnp.where(mask, s, -jnp.inf)
                m_old = m_ref[h]
                m_new = jnp.maximum(m_old, jnp.max(s, axis=-1, keepdims=True))
                p = jnp.exp(s - m_new)
                alpha = jnp.exp(m_old - m_new)
                l_ref[h] = alpha * l_ref[h] + jnp.sum(p, axis=-1, keepdims=True)
                m_ref[h] = m_new
                ps.append(_bf(p))
                alphas.append(alpha)
            alpha_pair = jnp.where(low, alphas[0], alphas[1])
            acc_ref[pr] = alpha_pair * acc_ref[pr] + _dot(jnp.concatenate(ps, axis=1), v_bd)

    group = 2
    for g0 in range(0, n_pairs, group):
        pairs = tuple(range(g0, g0 + group))

        def body(kb, carry, pairs=pairs):
            block(kb, pairs, None)
            return carry

        lax.fori_loop(0, i, body, 0)
        block(i, pairs, diag_ok)

    outs = [acc_ref[pr] / jnp.where(low, l_ref[2 * pr], l_ref[2 * pr + 1]) for pr in range(n_pairs)]
    o = jnp.concatenate(outs, axis=1)
    o_ref[...] = _bf(_rms(o, g_ref[...]))


def _mla_attention(q, k, v, g, batch, seq):
    nq = seq // TQ_MLA
    w = MLA_HEADS * HEAD_PAD
    return pl.pallas_call(
        _mla_kernel,
        grid=(batch, nq),
        in_specs=[pl.BlockSpec((TQ_MLA, w), lambda b, i: (b * nq + i, 0)),
                  pl.BlockSpec((seq, w), lambda b, i: (b, 0)),
                  pl.BlockSpec((seq, A_W), lambda b, i: (b, 0)),
                  pl.BlockSpec((1, A_W), lambda b, i: (0, 0))],
        out_specs=pl.BlockSpec((TQ_MLA, A_W), lambda b, i: (b * nq + i, 0)),
        out_shape=jax.ShapeDtypeStruct((batch * seq, A_W), jnp.bfloat16),
        scratch_shapes=[pltpu.VMEM((MLA_HEADS, TQ_MLA, 1), jnp.float32),
                        pltpu.VMEM((MLA_HEADS, TQ_MLA, 1), jnp.float32),
                        pltpu.VMEM((MLA_HEADS // 2, TQ_MLA, LANES), jnp.float32)],
        compiler_params=_cparams(("parallel", "parallel")),
        name="mla_attention",
    )(q, k, v, g)


def _sb_kernel(q_ref, k_ref, v_ref, g_ref, o_ref, run_ref, acc_ref):
    i = pl.program_id(1)
    t = TQ_SB
    n_pairs = SB_HEADS // 2
    r = lax.broadcasted_iota(jnp.int32, (2 * t, t), 0)
    c = lax.broadcasted_iota(jnp.int32, (2 * t, t), 1)
    strict = c < jnp.where(r >= t, r - t, r)
    top_low = (r < t) == (c < SB_DIM)
    r2 = lax.broadcasted_iota(jnp.int32, (t, 2 * t), 0)
    c2 = lax.broadcasted_iota(jnp.int32, (t, 2 * t), 1)
    sum_mat = jnp.where((c2 >= t) | (r2 > c2), 1.0, 0.0).astype(jnp.bfloat16)

    run_ref[...] = jnp.zeros(run_ref.shape, jnp.float32)
    acc_ref[...] = jnp.zeros(acc_ref.shape, jnp.float32)

    def block(kb, mask):
        start = pl.multiple_of(kb * t, t)
        slowest = None
        for pr in range(n_pairs):
            qp = q_ref[:, pr * LANES:(pr + 1) * LANES]
            q2 = jnp.concatenate([qp, qp], axis=0)
            qm = jnp.where(top_low, q2, jnp.zeros_like(q2))
            kblk = k_ref[pl.ds(start, t), pr * LANES:(pr + 1) * LANES]
            vblk = v_ref[pl.ds(start, t), pr * LANES:(pr + 1) * LANES]
            z = _dot_nt(qm, kblk)
            log_keep = -(jnp.maximum(z, 0.0) + jnp.log1p(jnp.exp(-jnp.abs(z))))
            if mask is not None:
                log_keep = jnp.where(mask, log_keep, 0.0)
            hi = _bf(log_keep)
            lo = _bf(log_keep - hi.astype(jnp.float32))
            sums = _dot(hi, sum_mat) + _dot(lo, sum_mat)
            run = run_ref[pr]
            a = jnp.exp(z + log_keep + run + sums[:, :t])
            if mask is not None:
                a = jnp.where(mask, a, 0.0)
            acc_ref[pr] = acc_ref[pr] + _dot(_bf(a), vblk)
            run = run + sums[:, t:]
            run_ref[pr] = run
            top = jnp.max(run)
            slowest = top if slowest is None else jnp.maximum(slowest, top)
        return slowest

    underflow = -104.0
    first = block(i, strict)
    lax.while_loop(lambda cr: (cr[0] >= 0) & (cr[1] > underflow),
                   lambda cr: (cr[0] - 1, block(cr[0], None)),
                   (i - 1, first))

    lane = lax.broadcasted_iota(jnp.int32, (t, LANES), 1)
    outs = [jnp.where(lane < SB_DIM, acc_ref[pr, 0:t, :], acc_ref[pr, t:2 * t, :]) for pr in range(n_pairs)]
    o = jnp.concatenate(outs, axis=1)
    o_ref[...] = _bf(_rms(o, g_ref[...]))


def _sb_attention(q, k, v, g, batch, seq):
    nq = seq // TQ_SB
    return pl.pallas_call(
        _sb_kernel,
        grid=(batch, nq),
        in_specs=[pl.BlockSpec((TQ_SB, SB_W), lambda b, i: (b * nq + i, 0)),
                  pl.BlockSpec((seq, SB_W), lambda b, i: (b, 0)),
                  pl.BlockSpec((seq, SB_W), lambda b, i: (b, 0)),
                  pl.BlockSpec((1, SB_W), lambda b, i: (0, 0))],
        out_specs=pl.BlockSpec((TQ_SB, SB_W), lambda b, i: (b * nq + i, 0)),
        out_shape=jax.ShapeDtypeStruct((batch * seq, SB_W), jnp.bfloat16),
        scratch_shapes=[pltpu.VMEM((SB_HEADS // 2, 2 * TQ_SB, TQ_SB), jnp.float32),
                        pltpu.VMEM((SB_HEADS // 2, 2 * TQ_SB, LANES), jnp.float32)],
        compiler_params=_cparams(("parallel", "parallel")),
        name="sb_attention",
    )(q, k, v, g)


def _ca_kernel(q_ref, k_ref, v_ref, tab_ref, g_ref, o_ref):
    i = pl.program_id(1)
    t = TQ_CA
    start = pl.multiple_of(i * t, t)
    r = lax.broadcasted_iota(jnp.int32, (t, CA_WIN), 0)
    c = lax.broadcasted_iota(jnp.int32, (t, CA_WIN), 1)
    lo = (r // CHUNK) * CHUNK
    valid = (c >= lo) & (c < lo + CA_PAD + CHUNK) & (c + i * t >= CA_PAD)
    lane = lax.broadcasted_iota(jnp.int32, (t, LANES), 1)

    outs = []
    for h in range(CA_HEADS):
        pair, half = h // 2, h % 2
        qp = q_ref[:, pair * LANES:(pair + 1) * LANES]
        qm = jnp.where((lane >= half * CA_DIM) & (lane < (half + 1) * CA_DIM), qp, jnp.zeros_like(qp))
        kwin = k_ref[pl.ds(start, CA_WIN), pair * LANES:(pair + 1) * LANES]
        vwin = v_ref[pl.ds(start, CA_WIN), pair * LANES:(pair + 1) * LANES]
        tab = jnp.broadcast_to(tab_ref[h:h + 1, :], (t, REL_TAB))
        bias = pltpu.roll(tab, REL_TAB - (REL_CLIP - 1), 1, stride=1, stride_axis=0)[:, :CA_WIN]
        s = jnp.where(valid, _dot_nt(qm, kwin) + bias, -jnp.inf)
        m = jnp.max(s, axis=-1, keepdims=True)
        p = jnp.exp(s - m)
        l = jnp.sum(p, axis=-1, keepdims=True)
        outs.append(_dot(_bf(p), vwin) / l)
    pairs = [jnp.where(lane < CA_DIM, outs[2 * p], outs[2 * p + 1]) for p in range(CA_HEADS // 2)]
    o = jnp.concatenate(pairs, axis=1)
    o_ref[...] = _bf(_rms(o, g_ref[...]))


def _ca_attention(q, kpad, vpad, tab, g, batch, seq):
    nq = seq // TQ_CA
    return pl.pallas_call(
        _ca_kernel,
        grid=(batch, nq),
        in_specs=[pl.BlockSpec((TQ_CA, CA_W), lambda b, i: (b * nq + i, 0)),
                  pl.BlockSpec((seq + CA_PAD, CA_W), lambda b, i: (b, 0)),
                  pl.BlockSpec((seq + CA_PAD, CA_W), lambda b, i: (b, 0)),
                  pl.BlockSpec((CA_HEADS, REL_TAB), lambda b, i: (0, 0)),
                  pl.BlockSpec((1, CA_W), lambda b, i: (0, 0))],
        out_specs=pl.BlockSpec((TQ_CA, CA_W), lambda b, i: (b * nq + i, 0)),
        out_shape=jax.ShapeDtypeStruct((batch * seq, CA_W), jnp.bfloat16),
        compiler_params=_cparams(("parallel", "parallel")),
        name="ca_attention",
    )(q, kpad, vpad, tab, g)


def _outproj_router_kernel(ma_ref, mb_ref, mc_ref, wo_ref, x_ref, g_ref, wrh_ref, wrl_ref, br_ref,
                           xn_ref, h_ref, idx_ref, gate_ref, cnt_ref, base_ref):
    tm = TM_TOK

    @pl.when(pl.program_id(0) == 0)
    def _():
        base_ref[...] = jnp.zeros_like(base_ref)

    attn = (_dot(ma_ref[...], wo_ref[0:A_W, :]) + _dot(mb_ref[...], wo_ref[A_W:A_W + SB_W, :])
            + _dot(mc_ref[...], wo_ref[A_W + SB_W:, :]))
    xn = x_ref[...] + attn
    xn_ref[...] = xn
    h = _rms(xn, g_ref[...])
    h_ref[...] = h

    h_hi = _bf(h)
    h_lo = _bf(h - h_hi.astype(jnp.float32))
    logits = (_dot(h_hi, wrh_ref[...]) + _dot(h_hi, wrl_ref[...]) + _dot(h_lo, wrh_ref[...])
              + br_ref[...])
    lane = lax.broadcasted_iota(jnp.int32, (tm, LANES), 1)
    lane_f = lane.astype(jnp.float32)

    work = logits
    vals, idxs, hots = [], [], []
    for _ in range(TOP_K):
        mx = jnp.max(work, axis=-1, keepdims=True)
        ix = jnp.min(jnp.where(work == mx, lane_f, float(LANES)), axis=-1, keepdims=True)
        hot = lane_f == ix
        work = jnp.where(hot, -jnp.inf, work)
        vals.append(mx)
        idxs.append(ix)
        hots.append(hot)
    exps = [jnp.exp(v - vals[0]) for v in vals]
    denom = exps[0] + exps[1] + exps[2] + exps[3]
    gates = [e / denom for e in exps]

    sel = jnp.zeros((tm, LANES), jnp.float32)
    for hot in hots:
        sel = sel + jnp.where(hot, 1.0, 0.0)
    r = lax.broadcasted_iota(jnp.int32, (tm, tm), 0)
    c = lax.broadcasted_iota(jnp.int32, (tm, tm), 1)
    before = jnp.where(c < r, 1.0, 0.0).astype(jnp.bfloat16)
    rank_dense = base_ref[...] + _dot(before, _bf(sel))
    base_ref[...] = base_ref[...] + jnp.sum(sel, axis=0, keepdims=True)
    cnt_ref[...] = base_ref[...].astype(jnp.int32)

    packed = jnp.zeros((tm, LANES), jnp.float32)
    gate_lanes = jnp.zeros((tm, LANES), jnp.float32)
    for kk in range(TOP_K):
        rank = jnp.sum(jnp.where(hots[kk], rank_dense, 0.0), axis=-1, keepdims=True)
        packed = packed + jnp.where(lane == kk, idxs[kk], 0.0) + jnp.where(lane == TOP_K + kk, rank, 0.0)
        gate_lanes = gate_lanes + jnp.where(lane == kk, gates[kk], 0.0)
    gate_ref[...] = gate_lanes
    idx_ref[...] = jnp.transpose(packed)[0:2 * TOP_K, :].astype(jnp.int32)


def _outproj_router(ma, mb, mc, wo, x, g, wrh, wrl, br):
    n = x.shape[0]
    tm = TM_TOK
    row = lambda width: pl.BlockSpec((tm, width), lambda i: (i, 0))
    full = lambda a: pl.BlockSpec(a.shape, lambda i: (0,) * a.ndim)
    return pl.pallas_call(
        _outproj_router_kernel,
        grid=(n // tm,),
        in_specs=[row(A_W), row(SB_W), row(CA_W), full(wo), row(D_MODEL), full(g),
                  full(wrh), full(wrl), full(br)],
        out_specs=[row(D_MODEL), row(D_MODEL), pl.BlockSpec((2 * TOP_K, tm), lambda i: (0, i)),
                   row(LANES), pl.BlockSpec((1, LANES), lambda i: (0, 0))],
        out_shape=[jax.ShapeDtypeStruct((n, D_MODEL), jnp.float32),
                   jax.ShapeDtypeStruct((n, D_MODEL), jnp.float32),
                   jax.ShapeDtypeStruct((2 * TOP_K, n), jnp.int32),
                   jax.ShapeDtypeStruct((n, LANES), jnp.float32),
                   jax.ShapeDtypeStruct((1, LANES), jnp.int32)],
        scratch_shapes=[pltpu.VMEM((1, LANES), jnp.float32)],
        compiler_params=_cparams(("arbitrary",)),
        name="outproj_router",
    )(ma, mb, mc, wo, x, g, wrh, wrl, br)


def _row_copy(src_ref, src_row, dst_ref, dst_row, sem):
    return pltpu.make_async_copy(src_ref.at[pl.ds(src_row, 1)], dst_ref.at[pl.ds(dst_row, 1)], sem)


def _scatter_kernel(pstart_ref, idx_ref, h_ref, xs_in_ref, xs_ref, sem):
    del xs_in_ref
    tm = TM_TOK

    def issue(t, _):
        for kk in range(TOP_K):
            dst = pstart_ref[idx_ref[kk, t]] + idx_ref[TOP_K + kk, t]
            _row_copy(h_ref, t, xs_ref, dst, sem).start()
        return 0

    lax.fori_loop(0, tm, issue, 0, unroll=8)

    def drain(t, _):
        for kk in range(TOP_K):
            _row_copy(h_ref, 0, xs_ref, 0, sem).wait()
        return 0

    lax.fori_loop(0, tm, drain, 0, unroll=8)


def _scatter_rows(pstart, idx8, h, xs_zero):
    n = h.shape[0]
    tm = TM_TOK
    grid_spec = pltpu.PrefetchScalarGridSpec(
        num_scalar_prefetch=1,
        grid=(n // tm,),
        in_specs=[pl.BlockSpec((2 * TOP_K, tm), lambda i, ps: (0, i), memory_space=pltpu.SMEM),
                  pl.BlockSpec((tm, D_MODEL), lambda i, ps: (i, 0)),
                  pl.BlockSpec(memory_space=pl.ANY)],
        out_specs=pl.BlockSpec(memory_space=pl.ANY),
        scratch_shapes=[pltpu.SemaphoreType.DMA],
    )
    return pl.pallas_call(
        _scatter_kernel,
        grid_spec=grid_spec,
        out_shape=jax.ShapeDtypeStruct(xs_zero.shape, xs_zero.dtype),
        input_output_aliases={3: 0},
        compiler_params=_cparams(("arbitrary",)),
        name="scatter_rows",
    )(pstart, idx8, h, xs_zero)


def _expert_kernel(be_ref, bfirst_ref, bvalid_ref, xs_ref, wgu_ref, bgu_ref, wdn_ref, bdn_ref,
                   ys_ref, wgu_bf, wdn_bf):
    b = pl.program_id(0)

    @pl.when(bfirst_ref[b] == 1)
    def _():
        wgu_bf[...] = _bf(wgu_ref[0])
        wdn_bf[...] = _bf(wdn_ref[0])

    @pl.when(bvalid_ref[b] == 1)
    def _():
        gu = _dot(_bf(xs_ref[...]), wgu_bf[...]) + bgu_ref[0]
        gte = jnp.minimum(gu[:, :D_FF], SWIGLU_LIMIT)
        up = jnp.clip(gu[:, D_FF:], -SWIGLU_LIMIT, SWIGLU_LIMIT)
        act = (up + 1.0) * (gte * (1.0 / (1.0 + jnp.exp(-SWIGLU_ALPHA * gte))))
        ys_ref[...] = _dot(_bf(act), wdn_bf[...]) + bdn_ref[0]

    @pl.when(bvalid_ref[b] == 0)
    def _():
        ys_ref[...] = jnp.zeros_like(ys_ref)


def _expert_ffn(blk_e, blk_first, blk_valid, xs, wgu, bgu, wdn, bdn):
    n_rows = xs.shape[0]
    tm = TM_EXP
    grid_spec = pltpu.PrefetchScalarGridSpec(
        num_scalar_prefetch=3,
        grid=(n_rows // tm,),
        in_specs=[pl.BlockSpec((tm, D_MODEL), lambda b, e, f, v: (b, 0)),
                  pl.BlockSpec((1, D_MODEL, 2 * D_FF), lambda b, e, f, v: (e[b], 0, 0)),
                  pl.BlockSpec((1, 1, 2 * D_FF), lambda b, e, f, v: (e[b], 0, 0)),
                  pl.BlockSpec((1, D_FF, D_MODEL), lambda b, e, f, v: (e[b], 0, 0)),
                  pl.BlockSpec((1, 1, D_MODEL), lambda b, e, f, v: (e[b], 0, 0))],
        out_specs=pl.BlockSpec((tm, D_MODEL), lambda b, e, f, v: (b, 0)),
        scratch_shapes=[pltpu.VMEM((D_MODEL, 2 * D_FF), jnp.bfloat16),
                        pltpu.VMEM((D_FF, D_MODEL), jnp.bfloat16)],
    )
    return pl.pallas_call(
        _expert_kernel,
        grid_spec=grid_spec,
        out_shape=jax.ShapeDtypeStruct((n_rows, D_MODEL), jnp.float32),
        compiler_params=_cparams(("arbitrary",)),
        name="expert_ffn",
    )(blk_e, blk_first, blk_valid, xs, wgu, bgu, wdn, bdn)


def _combine_kernel(final, pstart_ref, idx_ref, gate_ref, x_ref, ys_ref, gfin_ref, o_ref, buf, sem):
    tm = TM_TOK

    def issue(t, _):
        for kk in range(TOP_K):
            src = pstart_ref[idx_ref[kk, t]] + idx_ref[TOP_K + kk, t]
            pltpu.make_async_copy(ys_ref.at[pl.ds(src, 1)], buf.at[kk, pl.ds(t, 1)], sem).start()
        return 0

    lax.fori_loop(0, tm, issue, 0, unroll=8)

    def drain(t, _):
        for kk in range(TOP_K):
            pltpu.make_async_copy(ys_ref.at[pl.ds(0, 1)], buf.at[kk, pl.ds(0, 1)], sem).wait()
        return 0

    lax.fori_loop(0, tm, drain, 0, unroll=8)

    gates = gate_ref[...]
    out = x_ref[...]
    for kk in range(TOP_K):
        out = out + buf[kk] * gates[:, kk:kk + 1]
    if final:
        out = _rms(out, gfin_ref[...])
    o_ref[...] = out


def _combine(pstart, idx8, gates, x, ys, gfin, final):
    n = x.shape[0]
    tm = TM_TOK
    grid_spec = pltpu.PrefetchScalarGridSpec(
        num_scalar_prefetch=1,
        grid=(n // tm,),
        in_specs=[pl.BlockSpec((2 * TOP_K, tm), lambda i, ps: (0, i), memory_space=pltpu.SMEM),
                  pl.BlockSpec((tm, LANES), lambda i, ps: (i, 0)),
                  pl.BlockSpec((tm, D_MODEL), lambda i, ps: (i, 0)),
                  pl.BlockSpec(memory_space=pl.ANY),
                  pl.BlockSpec((1, D_MODEL), lambda i, ps: (0, 0))],
        out_specs=pl.BlockSpec((tm, D_MODEL), lambda i, ps: (i, 0)),
        scratch_shapes=[pltpu.VMEM((TOP_K, tm, D_MODEL), jnp.float32), pltpu.SemaphoreType.DMA],
    )
    return pl.pallas_call(
        functools.partial(_combine_kernel, final),
        grid_spec=grid_spec,
        out_shape=jax.ShapeDtypeStruct((n, D_MODEL), jnp.float32),
        compiler_params=_cparams(("arbitrary",)),
        name="combine_final" if final else "combine",
    )(pstart, idx8, gates, x, ys, gfin)


def _pad_heads(w, parts):
    rows = w.shape[0]
    per = w.shape[1] // MLA_HEADS
    w3 = w.reshape(rows, MLA_HEADS, per)
    cols = [jnp.zeros((rows, MLA_HEADS, b - a), w.dtype) if sign == 0 else sign * w3[:, :, a:b]
            for a, b, sign in parts]
    used = sum(b - a for a, b, _ in parts)
    cols.append(jnp.zeros((rows, MLA_HEADS, HEAD_PAD - used), w.dtype))
    return jnp.concatenate(cols, axis=2).reshape(rows, MLA_HEADS * HEAD_PAD)


def _layer_weights(w_in, w_uq, w_ukv, w_router, b_router, rel_bias):
    half = MLA_ROPE // 2
    cq, ckv, kpe, sb, ca = (w_in[:, 0:256], w_in[:, 256:384], w_in[:, 384:416],
                            w_in[:, 416:1184], w_in[:, 1184:1952])
    z = lambda width: jnp.zeros((D_MODEL, width), w_in.dtype)
    kpe_pad = jnp.concatenate([z(MLA_NOPE), kpe, z(HEAD_PAD - MLA_NOPE - MLA_ROPE)], axis=1)
    kpe_rot = jnp.concatenate([z(MLA_NOPE), -kpe[:, half:], kpe[:, :half],
                               z(HEAD_PAD - MLA_NOPE - MLA_ROPE)], axis=1)
    win2 = jnp.concatenate([cq, ckv, kpe_pad, kpe_rot, sb, ca], axis=1).astype(jnp.bfloat16)

    d = MLA_NOPE + MLA_ROPE
    wq_full = _pad_heads(w_uq, [(0, d, 1)])
    wq_rot = _pad_heads(w_uq, [(0, MLA_NOPE, 0), (MLA_NOPE + half, d, -1), (MLA_NOPE, MLA_NOPE + half, 1)])
    wq2 = jnp.concatenate([wq_full, wq_rot], axis=1).astype(jnp.bfloat16)

    wk = _pad_heads(w_ukv, [(0, MLA_NOPE, 1)])
    wv = w_ukv.reshape(MLA_KV_RANK, MLA_HEADS, MLA_NOPE + MLA_V)[:, :, MLA_NOPE:].reshape(MLA_KV_RANK, A_W)
    wkv2 = jnp.concatenate([wk, wv], axis=1).astype(jnp.bfloat16)

    wr = jnp.pad(w_router, ((0, 0), (0, LANES - N_EXPERTS)))
    wr_hi = wr.astype(jnp.bfloat16)
    wr_lo = (wr - wr_hi.astype(jnp.float32)).astype(jnp.bfloat16)
    br = jnp.pad(b_router, (0, LANES - N_EXPERTS), constant_values=NEG).reshape(1, LANES)

    ext = jnp.concatenate([rel_bias, jnp.broadcast_to(rel_bias[:, -1:], (CA_HEADS, REL_TAB - 2 * REL_CLIP - 1))], axis=1)
    tab = ext[:, ::-1]
    return win2, wq2, wkv2, wr_hi, wr_lo, br, tab


def kernel(x, positions, attn_norm, w_in, q_norm, w_uq, kv_norm, w_ukv, rel_bias, mix_norm,
           w_o, ffn_norm, w_router, b_router, w_gate_up, b_gate_up, w_down, b_down, final_norm):
    batch, seq, _ = x.shape
    n = batch * seq
    depth = w_in.shape[0]
    xf = x.reshape(n, D_MODEL)

    inv = ROPE_THETA ** (-jnp.arange(0, MLA_ROPE, 2, dtype=jnp.float32) / MLA_ROPE)
    inv_row = jnp.concatenate([jnp.zeros((MLA_NOPE,), jnp.float32), inv, inv,
                               jnp.zeros((HEAD_PAD - MLA_NOPE - MLA_ROPE,), jnp.float32)]).reshape(1, LANES)
    cos_t, sin_t = _rope_tables(positions.reshape(n, 1), inv_row)

    n_rows = n * TOP_K + N_EXPERTS * TM_EXP
    n_blk = n_rows // TM_EXP
    row2 = lambda v: v.reshape(1, -1)
    wgu_all = w_gate_up.reshape(depth * N_EXPERTS, D_MODEL, 2 * D_FF)
    bgu_all = b_gate_up.reshape(depth * N_EXPERTS, 1, 2 * D_FF)
    wdn_all = w_down.reshape(depth * N_EXPERTS, D_FF, D_MODEL)
    bdn_all = b_down.reshape(depth * N_EXPERTS, 1, D_MODEL)

    for l in range(depth):
        win2, wq2, wkv2, wr_hi, wr_lo, br, tab = _layer_weights(
            w_in[l], w_uq[l], w_ukv[l], w_router[l], b_router[l], rel_bias[l])
        q, k, v, sbq, sbk, sbv, caq, cak, cav = _inproj(
            xf, row2(attn_norm[l]), win2, cos_t, sin_t, row2(q_norm[l]), wq2, row2(kv_norm[l]), wkv2)

        g = mix_norm[l]
        ma = _mla_attention(q, k, v, row2(g[:A_W]), batch, seq)
        mb = _sb_attention(sbq, sbk, sbv, row2(g[A_W:A_W + SB_W]), batch, seq)
        pad = lambda a: jnp.pad(a.reshape(batch, seq, CA_W), ((0, 0), (CA_PAD, 0), (0, 0))).reshape(-1, CA_W)
        mc = _ca_attention(caq, pad(cak), pad(cav), tab, row2(g[A_W + SB_W:]), batch, seq)

        xn, h, idx8, gates, counts = _outproj_router(
            ma, mb, mc, w_o[l].astype(jnp.bfloat16), xf, row2(ffn_norm[l]), wr_hi, wr_lo, br)

        sizes = counts[0, :N_EXPERTS]
        padded = (sizes + TM_EXP - 1) // TM_EXP * TM_EXP
        p_ends = jnp.cumsum(padded)
        p_starts = (p_ends - padded).astype(jnp.int32)
        blk_start = jnp.arange(n_blk, dtype=jnp.int32) * TM_EXP
        blk_valid = (blk_start < p_ends[-1]).astype(jnp.int32)
        last_row = jnp.minimum(blk_start, p_ends[-1] - 1)
        blk_e = jnp.minimum(jnp.sum((last_row[:, None] >= p_ends[None, :]).astype(jnp.int32), axis=1),
                            N_EXPERTS - 1)
        blk_first = jnp.concatenate([jnp.ones((1,), jnp.int32), (blk_e[1:] != blk_e[:-1]).astype(jnp.int32)])

        xs = _scatter_rows(p_starts, idx8, h, jnp.zeros((n_rows, D_MODEL), jnp.float32))
        ys = _expert_ffn(blk_e + l * N_EXPERTS, blk_first, blk_valid, xs, wgu_all, bgu_all, wdn_all, bdn_all)
        xf = _combine(p_starts, idx8, gates, xn, ys, row2(final_norm), final=(l == depth - 1))

    return xf.reshape(batch, seq, D_MODEL)
```

```python
import functools
import math

import jax
import jax.numpy as jnp
from jax import lax
from jax.experimental import pallas as pl
from jax.experimental.pallas import tpu as pltpu

D_MODEL = 1024
RMS_EPS = 1e-6
MLA_NOPE, MLA_ROPE, MLA_V, MLA_HEADS = 64, 32, 64, 8
MLA_Q_RANK, MLA_KV_RANK = 256, 128
ROPE_THETA = 10000.0
SB_DIM, SB_HEADS = 64, 4
CA_DIM, CA_HEADS = 64, 4
CHUNK = 64
CA_LEFT_CHUNKS = 8
REL_CLIP = 256
N_EXPERTS, TOP_K = 32, 4
D_FF = 1024
SWIGLU_LIMIT, SWIGLU_ALPHA = 7.0, 1.702

LANES = 128
HEAD_PAD = 128
A_W = MLA_HEADS * MLA_V
SB_W = SB_HEADS * SB_DIM
CA_W = CA_HEADS * CA_DIM
CA_PAD = CA_LEFT_CHUNKS * CHUNK
CA_WIN = CA_PAD + 2 * CHUNK
REL_TAB = 1024

TM_TOK = 256
TQ_MLA = 256
TQ_SB = 128
TQ_CA = 2 * CHUNK
TM_EXP = 256
RUN_ALIGN = 8
LOCAL_ROWS = 1280
assert LOCAL_ROWS >= TM_TOK * TOP_K + N_EXPERTS * (RUN_ALIGN - 1) and LOCAL_ROWS % LANES == 0
assert TM_EXP % TM_TOK == 0

_C_CQ, _C_CKV, _C_KPE, _C_KROT = 0, 256, 384, 512
_C_SB, _C_CA, D_IN2 = 640, 1408, 2176

NEG = -1e30
VMEM_LIMIT = 56 * 1024 * 1024


def _rms(v, g):
    return v * lax.rsqrt(jnp.mean(v * v, axis=-1, keepdims=True) + RMS_EPS) * g


def _dot(a, b):
    return jnp.dot(a, b, preferred_element_type=jnp.float32)


def _dot_nt(a, b):
    return lax.dot_general(a, b, (((1,), (1,)), ((), ())), preferred_element_type=jnp.float32)


def _bf(v):
    return v.astype(jnp.bfloat16)


def _cparams(sem):
    return pltpu.CompilerParams(dimension_semantics=sem, vmem_limit_bytes=VMEM_LIMIT)


def _rope_kernel(pos_ref, inv_ref, cos_ref, sin_ref):
    ang = pos_ref[...].astype(jnp.float32) * inv_ref[...]
    cos_ref[...] = jnp.cos(ang)
    sin_ref[...] = jnp.sin(ang)


def _rope_tables(pos_col, inv_row):
    n = pos_col.shape[0]
    return pl.pallas_call(
        _rope_kernel,
        grid=(n // TM_TOK,),
        in_specs=[pl.BlockSpec((TM_TOK, 1), lambda i: (i, 0)),
                  pl.BlockSpec((1, LANES), lambda i: (0, 0))],
        out_specs=[pl.BlockSpec((TM_TOK, LANES), lambda i: (i, 0))] * 2,
        out_shape=[jax.ShapeDtypeStruct((n, LANES), jnp.float32)] * 2,
        compiler_params=_cparams(("parallel",)),
        name="rope_tables",
    )(pos_col, inv_row)


def _inproj_kernel(x_ref, g_ref, win_ref, cos_ref, sin_ref, qn_ref, wq_ref, kvn_ref, wkv_ref,
                   q_ref, k_ref, v_ref, sbq_ref, sbk_ref, sbv_ref, caq_ref, cak_ref, cav_ref):
    h = _rms(x_ref[...], g_ref[...])
    proj = _dot(_bf(h), win_ref[...])
    cos128, sin128 = cos_ref[...], sin_ref[...]
    cos_h = jnp.concatenate([cos128] * MLA_HEADS, axis=1)
    sin_h = jnp.concatenate([sin128] * MLA_HEADS, axis=1)
    w = MLA_HEADS * HEAD_PAD

    cqn = _rms(proj[:, _C_CQ:_C_CKV], qn_ref[...])
    q2 = _dot(_bf(cqn), wq_ref[...])
    q = (q2[:, :w] * cos_h + q2[:, w:] * sin_h) * (1.0 / math.sqrt(MLA_NOPE + MLA_ROPE))
    q_ref[...] = _bf(q)

    ckvn = _rms(proj[:, _C_CKV:_C_KPE], kvn_ref[...])
    kv2 = _dot(_bf(ckvn), wkv_ref[...])
    kpe = proj[:, _C_KPE:_C_KROT] * cos128 + proj[:, _C_KROT:_C_SB] * sin128
    k_ref[...] = _bf(kv2[:, :w] + jnp.concatenate([kpe] * MLA_HEADS, axis=1))
    v_ref[...] = _bf(kv2[:, w:])

    sb_scale = 1.0 / math.sqrt(SB_DIM)
    sbq_ref[...] = _bf(proj[:, _C_SB:_C_SB + SB_W] * sb_scale)
    sbk_ref[...] = _bf(proj[:, _C_SB + SB_W:_C_SB + 2 * SB_W])
    sbv_ref[...] = _bf(proj[:, _C_SB + 2 * SB_W:_C_CA])
    ca_scale = 1.0 / math.sqrt(CA_DIM)
    caq_ref[...] = _bf(proj[:, _C_CA:_C_CA + CA_W] * ca_scale)
    cak_ref[...] = _bf(proj[:, _C_CA + CA_W:_C_CA + 2 * CA_W])
    cav_ref[...] = _bf(proj[:, _C_CA + 2 * CA_W:D_IN2])


def _inproj(x, g, win2, cos_t, sin_t, qn, wq2, kvn, wkv2):
    n = x.shape[0]
    tm = TM_TOK
    row = lambda width: pl.BlockSpec((tm, width), lambda i: (i, 0))
    full = lambda a: pl.BlockSpec(a.shape, lambda i: (0,) * a.ndim)
    widths = [MLA_HEADS * HEAD_PAD, MLA_HEADS * HEAD_PAD, A_W] + [SB_W] * 3 + [CA_W] * 3
    return pl.pallas_call(
        _inproj_kernel,
        grid=(n // tm,),
        in_specs=[row(D_MODEL), full(g), full(win2), row(LANES), row(LANES),
                  full(qn), full(wq2), full(kvn), full(wkv2)],
        out_specs=[row(wd) for wd in widths],
        out_shape=[jax.ShapeDtypeStruct((n, wd), jnp.bfloat16) for wd in widths],
        compiler_params=_cparams(("parallel",)),
        name="inproj",
    )(x, g, win2, cos_t, sin_t, qn, wq2, kvn, wkv2)


def _mla_kernel(q_ref, k_ref, v_ref, g_ref, o_ref, m_ref, l_ref, acc_ref):
    i = pl.program_id(1)
    tq = TQ_MLA
    n_pairs = MLA_HEADS // 2
    rows = lax.broadcasted_iota(jnp.int32, (tq, tq), 0) // CHUNK
    cols = lax.broadcasted_iota(jnp.int32, (tq, tq), 1) // CHUNK
    diag_ok = cols <= rows
    low = lax.broadcasted_iota(jnp.int32, (tq, LANES), 1) < MLA_V

    m_ref[...] = jnp.full(m_ref.shape, -jnp.inf, jnp.float32)
    l_ref[...] = jnp.zeros(l_ref.shape, jnp.float32)
    acc_ref[...] = jnp.zeros(acc_ref.shape, jnp.float32)

    def block(kb, pairs, mask):
        start = pl.multiple_of(kb * tq, tq)
        for pr in pairs:
            vpair = v_ref[pl.ds(start, tq), pr * LANES:(pr + 1) * LANES]
            zero = jnp.zeros_like(vpair)
            v_bd = jnp.concatenate([jnp.where(low, vpair, zero), jnp.where(low, zero, vpair)], axis=0)
            ps, alphas = [], []
            for h in (2 * pr, 2 * pr + 1):
                qh = q_ref[:, h * HEAD_PAD:(h + 1) * HEAD_PAD]
                kblk = k_ref[pl.ds(start, tq), h * HEAD_PAD:(h + 1) * HEAD_PAD]
                s = _dot_nt(qh, kblk)
                if mask is not None:
                    s = jnp.where(mask, s, -jnp.inf)
                m_old = m_ref[h]
                m_new = jnp.maximum(m_old, jnp.max(s, axis=-1, keepdims=True))
                p = jnp.exp(s - m_new)
                alpha = jnp.exp(m_old - m_new)
                l_ref[h] = alpha * l_ref[h] + jnp.sum(p, axis=-1, keepdims=True)
                m_ref[h] = m_new
                ps.append(_bf(p))
                alphas.append(alpha)
            alpha_pair = jnp.where(low, alphas[0], alphas[1])
            acc_ref[pr] = alpha_pair * acc_ref[pr] + _dot(jnp.concatenate(ps, axis=1), v_bd)

    group = 2
    for g0 in range(0, n_pairs, group):
        pairs = tuple(range(g0, g0 + group))

        def body(kb, carry, pairs=pairs):
            block(kb, pairs, None)
            return carry

        lax.fori_loop(0, i, body, 0)
        block(i, pairs, diag_ok)

    outs = [acc_ref[pr] / jnp.where(low, l_ref[2 * pr], l_ref[2 * pr + 1]) for pr in range(n_pairs)]
    o = jnp.concatenate(outs, axis=1)
    o_ref[...] = _bf(_rms(o, g_ref[...]))


def _mla_attention(q, k, v, g, batch, seq):
    nq = seq // TQ_MLA
    w = MLA_HEADS * HEAD_PAD
    return pl.pallas_call(
        _mla_kernel,
        grid=(batch, nq),
        in_specs=[pl.BlockSpec((TQ_MLA, w), lambda b, i: (b * nq + i, 0)),
                  pl.BlockSpec((seq, w), lambda b, i: (b, 0)),
                  pl.BlockSpec((seq, A_W), lambda b, i: (b, 0)),
                  pl.BlockSpec((1, A_W), lambda b, i: (0, 0))],
        out_specs=pl.BlockSpec((TQ_MLA, A_W), lambda b, i: (b * nq + i, 0)),
        out_shape=jax.ShapeDtypeStruct((batch * seq, A_W), jnp.bfloat16),
        scratch_shapes=[pltpu.VMEM((MLA_HEADS, TQ_MLA, 1), jnp.float32),
                        pltpu.VMEM((MLA_HEADS, TQ_MLA, 1), jnp.float32),
                        pltpu.VMEM((MLA_HEADS // 2, TQ_MLA, LANES), jnp.float32)],
        compiler_params=_cparams(("parallel", "parallel")),
        name="mla_attention",
    )(q, k, v, g)


def _sb_kernel(q_ref, k_ref, v_ref, g_ref, o_ref, run_ref, acc_ref):
    i = pl.program_id(1)
    t = TQ_SB
    n_pairs = SB_HEADS // 2
    r = lax.broadcasted_iota(jnp.int32, (2 * t, t), 0)
    c = lax.broadcasted_iota(jnp.int32, (2 * t, t), 1)
    strict = c < jnp.where(r >= t, r - t, r)
    top_low = (r < t) == (c < SB_DIM)
    r2 = lax.broadcasted_iota(jnp.int32, (t, 2 * t), 0)
    c2 = lax.broadcasted_iota(jnp.int32, (t, 2 * t), 1)
    sum_mat = jnp.where((c2 >= t) | (r2 > c2), 1.0, 0.0).astype(jnp.bfloat16)

    run_ref[...] = jnp.zeros(run_ref.shape, jnp.float32)
    acc_ref[...] = jnp.zeros(acc_ref.shape, jnp.float32)

    def block(kb, mask):
        start = pl.multiple_of(kb * t, t)
        slowest = None
        for pr in range(n_pairs):
            qp = q_ref[:, pr * LANES:(pr + 1) * LANES]
            q2 = jnp.concatenate([qp, qp], axis=0)
            qm = jnp.where(top_low, q2, jnp.zeros_like(q2))
            kblk = k_ref[pl.ds(start, t), pr * LANES:(pr + 1) * LANES]
            vblk = v_ref[pl.ds(start, t), pr * LANES:(pr + 1) * LANES]
            z = _dot_nt(qm, kblk)
            log_keep = -(jnp.maximum(z, 0.0) + jnp.log1p(jnp.exp(-jnp.abs(z))))
            if mask is not None:
                log_keep = jnp.where(mask, log_keep, 0.0)
            hi = _bf(log_keep)
            lo = _bf(log_keep - hi.astype(jnp.float32))
            sums = _dot(hi, sum_mat) + _dot(lo, sum_mat)
            run = run_ref[pr]
            a = jnp.exp(z + log_keep + run + sums[:, :t])
            if mask is not None:
                a = jnp.where(mask, a, 0.0)
            acc_ref[pr] = acc_ref[pr] + _dot(_bf(a), vblk)
            run = run + sums[:, t:]
            run_ref[pr] = run
            top = jnp.max(run)
            slowest = top if slowest is None else jnp.maximum(slowest, top)
        return slowest

    underflow = -104.0
    first = block(i, strict)
    lax.while_loop(lambda cr: (cr[0] >= 0) & (cr[1] > underflow),
                   lambda cr: (cr[0] - 1, block(cr[0], None)),
                   (i - 1, first))

    lane = lax.broadcasted_iota(jnp.int32, (t, LANES), 1)
    outs = [jnp.where(lane < SB_DIM, acc_ref[pr, 0:t, :], acc_ref[pr, t:2 * t, :]) for pr in range(n_pairs)]
    o = jnp.concatenate(outs, axis=1)
    o_ref[...] = _bf(_rms(o, g_ref[...]))


def _sb_attention(q, k, v, g, batch, seq):
    nq = seq // TQ_SB
    return pl.pallas_call(
        _sb_kernel,
        grid=(batch, nq),
        in_specs=[pl.BlockSpec((TQ_SB, SB_W), lambda b, i: (b * nq + i, 0)),
                  pl.BlockSpec((seq, SB_W), lambda b, i: (b, 0)),
                  pl.BlockSpec((seq, SB_W), lambda b, i: (b, 0)),
                  pl.BlockSpec((1, SB_W), lambda b, i: (0, 0))],
        out_specs=pl.BlockSpec((TQ_SB, SB_W), lambda b, i: (b * nq + i, 0)),
        out_shape=jax.ShapeDtypeStruct((batch * seq, SB_W), jnp.bfloat16),
        scratch_shapes=[pltpu.VMEM((SB_HEADS // 2, 2 * TQ_SB, TQ_SB), jnp.float32),
                        pltpu.VMEM((SB_HEADS // 2, 2 * TQ_SB, LANES), jnp.float32)],
        compiler_params=_cparams(("parallel", "parallel")),
        name="sb_attention",
    )(q, k, v, g)


def _ca_kernel(q_ref, k_ref, v_ref, tab_ref, g_ref, o_ref):
    i = pl.program_id(1)
    t = TQ_CA
    start = pl.multiple_of(i * t, t)
    r = lax.broadcasted_iota(jnp.int32, (t, CA_WIN), 0)
    c = lax.broadcasted_iota(jnp.int32, (t, CA_WIN), 1)
    lo = (r // CHUNK) * CHUNK
    valid = (c >= lo) & (c < lo + CA_PAD + CHUNK) & (c + i * t >= CA_PAD)
    lane = lax.broadcasted_iota(jnp.int32, (t, LANES), 1)

    outs = []
    for h in range(CA_HEADS):
        pair, half = h // 2, h % 2
        qp = q_ref[:, pair * LANES:(pair + 1) * LANES]
        qm = jnp.where((lane >= half * CA_DIM) & (lane < (half + 1) * CA_DIM), qp, jnp.zeros_like(qp))
        kwin = k_ref[pl.ds(start, CA_WIN), pair * LANES:(pair + 1) * LANES]
        vwin = v_ref[pl.ds(start, CA_WIN), pair * LANES:(pair + 1) * LANES]
        tab = jnp.broadcast_to(tab_ref[h:h + 1, :], (t, REL_TAB))
        bias = pltpu.roll(tab, REL_TAB - (REL_CLIP - 1), 1, stride=1, stride_axis=0)[:, :CA_WIN]
        s = jnp.where(valid, _dot_nt(qm, kwin) + bias, -jnp.inf)
        m = jnp.max(s, axis=-1, keepdims=True)
        p = jnp.exp(s - m)
        l = jnp.sum(p, axis=-1, keepdims=True)
        outs.append(_dot(_bf(p), vwin) / l)
    pairs = [jnp.where(lane < CA_DIM, outs[2 * p], outs[2 * p + 1]) for p in range(CA_HEADS // 2)]
    o = jnp.concatenate(pairs, axis=1)
    o_ref[...] = _bf(_rms(o, g_ref[...]))


def _ca_attention(q, kpad, vpad, tab, g, batch, seq):
    nq = seq // TQ_CA
    return pl.pallas_call(
        _ca_kernel,
        grid=(batch, nq),
        in_specs=[pl.BlockSpec((TQ_CA, CA_W), lambda b, i: (b * nq + i, 0)),
                  pl.BlockSpec((seq + CA_PAD, CA_W), lambda b, i: (b, 0)),
                  pl.BlockSpec((seq + CA_PAD, CA_W), lambda b, i: (b, 0)),
                  pl.BlockSpec((CA_HEADS, REL_TAB), lambda b, i: (0, 0)),
                  pl.BlockSpec((1, CA_W), lambda b, i: (0, 0))],
        out_specs=pl.BlockSpec((TQ_CA, CA_W), lambda b, i: (b * nq + i, 0)),
        out_shape=jax.ShapeDtypeStruct((batch * seq, CA_W), jnp.bfloat16),
        compiler_params=_cparams(("parallel", "parallel")),
        name="ca_attention",
    )(q, kpad, vpad, tab, g)


def _outproj_router_kernel(ma_ref, mb_ref, mc_ref, wo_ref, x_ref, g_ref, wrh_ref, wrl_ref, br_ref,
                           xn_ref, h_ref, lpos_ref, meta_ref, runs_ref, tot_ref, base_ref):
    tm = TM_TOK

    @pl.when(pl.program_id(0) == 0)
    def _():
        base_ref[...] = jnp.zeros_like(base_ref)

    attn = (_dot(ma_ref[...], wo_ref[0:A_W, :]) + _dot(mb_ref[...], wo_ref[A_W:A_W + SB_W, :])
            + _dot(mc_ref[...], wo_ref[A_W + SB_W:, :]))
    xn = x_ref[...] + attn
    xn_ref[...] = xn
    h = _rms(xn, g_ref[...])

    h_hi = _bf(h)
    h_ref[...] = h_hi
    h_lo = _bf(h - h_hi.astype(jnp.float32))
    logits = (_dot(h_hi, wrh_ref[...]) + _dot(h_hi, wrl_ref[...]) + _dot(h_lo, wrh_ref[...])
              + br_ref[...])
    lane = lax.broadcasted_iota(jnp.int32, (tm, LANES), 1)
    lane_f = lane.astype(jnp.float32)

    work = logits
    vals, idxs, hots = [], [], []
    for _ in range(TOP_K):
        mx = jnp.max(work, axis=-1, keepdims=True)
        ix = jnp.min(jnp.where(work == mx, lane_f, float(LANES)), axis=-1, keepdims=True)
        hot = lane_f == ix
        work = jnp.where(hot, -jnp.inf, work)
        vals.append(mx)
        idxs.append(ix)
        hots.append(hot)
    exps = [jnp.exp(v - vals[0]) for v in vals]
    denom = exps[0] + exps[1] + exps[2] + exps[3]
    gates = [e / denom for e in exps]

    sel = jnp.zeros((tm, LANES), jnp.float32)
    for hot in hots:
        sel = sel + jnp.where(hot, 1.0, 0.0)
    r = lax.broadcasted_iota(jnp.int32, (tm, tm), 0)
    c = lax.broadcasted_iota(jnp.int32, (tm, tm), 1)
    before = jnp.where(c < r, 1.0, 0.0).astype(jnp.bfloat16)
    rank_in_tile = _dot(before, _bf(sel))

    cnt = jnp.sum(sel, axis=0, keepdims=True)
    cnt_al = jnp.ceil(cnt * (1.0 / RUN_ALIGN)) * RUN_ALIGN
    rl = lax.broadcasted_iota(jnp.int32, (LANES, LANES), 0)
    cl = lax.broadcasted_iota(jnp.int32, (LANES, LANES), 1)
    earlier = jnp.where(rl < cl, 1.0, 0.0).astype(jnp.bfloat16)
    loff = _dot(_bf(jnp.broadcast_to(cnt_al, (8, LANES))), earlier)[0:1, :]
    base = base_ref[...]
    base_ref[...] = base + cnt_al
    tot_ref[...] = base_ref[...].astype(jnp.int32)
    sub = lax.broadcasted_iota(jnp.int32, (8, LANES), 0)
    runs = jnp.where(sub == 0, loff, jnp.where(sub == 1, base, jnp.where(sub == 2, cnt_al, 0.0)))
    runs_ref[...] = runs.astype(jnp.int32)

    lpos_dense = loff + rank_in_tile
    meta = jnp.zeros((tm, LANES), jnp.float32)
    for kk in range(TOP_K):
        lpos = jnp.sum(jnp.where(hots[kk], lpos_dense, 0.0), axis=-1, keepdims=True)
        meta = meta + jnp.where(lane == kk, gates[kk], 0.0) + jnp.where(lane == TOP_K + kk, lpos, 0.0)
    meta_ref[...] = meta
    lpos_ref[...] = jnp.transpose(meta)[TOP_K:TOP_K + 8, :].astype(jnp.int32)


def _outproj_router(ma, mb, mc, wo, x, g, wrh, wrl, br):
    n = x.shape[0]
    tm = TM_TOK
    row = lambda width: pl.BlockSpec((tm, width), lambda i: (i, 0))
    full = lambda a: pl.BlockSpec(a.shape, lambda i: (0,) * a.ndim)
    return pl.pallas_call(
        _outproj_router_kernel,
        grid=(n // tm,),
        in_specs=[row(A_W), row(SB_W), row(CA_W), full(wo), row(D_MODEL), full(g),
                  full(wrh), full(wrl), full(br)],
        out_specs=[row(D_MODEL), row(D_MODEL), pl.BlockSpec((8, tm), lambda i: (0, i)),
                   row(LANES), pl.BlockSpec((8, LANES), lambda i: (i, 0)),
                   pl.BlockSpec((1, LANES), lambda i: (0, 0))],
        out_shape=[jax.ShapeDtypeStruct((n, D_MODEL), jnp.float32),
                   jax.ShapeDtypeStruct((n, D_MODEL), jnp.bfloat16),
                   jax.ShapeDtypeStruct((8, n), jnp.int32),
                   jax.ShapeDtypeStruct((n, LANES), jnp.float32),
                   jax.ShapeDtypeStruct((8 * (n // tm), LANES), jnp.int32),
                   jax.ShapeDtypeStruct((1, LANES), jnp.int32)],
        scratch_shapes=[pltpu.VMEM((1, LANES), jnp.float32)],
        compiler_params=_cparams(("arbitrary",)),
        name="outproj_router",
    )(ma, mb, mc, wo, x, g, wrh, wrl, br)


_RUN_CHUNKS = tuple(TM_TOK >> s for s in range(TM_TOK.bit_length()) if (TM_TOK >> s) >= RUN_ALIGN)


def _for_each_chunk(length, fn):
    for size in _RUN_CHUNKS:
        off = length & (~(2 * size - 1))

        @pl.when((length & size) != 0)
        def _(off=off, size=size):
            fn(off, size)


def _tile_runs(runs_ref, pstart_ref, fn):
    def body(e, _):
        local, glob, length = runs_ref[0, e], pstart_ref[e] + runs_ref[1, e], runs_ref[2, e]
        _for_each_chunk(length, lambda off, size: fn(pl.multiple_of(local + off, RUN_ALIGN),
                                                     pl.multiple_of(glob + off, RUN_ALIGN), size))
        return 0

    lax.fori_loop(0, N_EXPERTS, body, 0)


def _dispatch_kernel(pstart_ref, tot_ref, pad_ref, runs_ref, lpos_ref, h_ref, xs_ref, loc, zbuf, sem):
    def fill(start_or_wait):
        def body(e, _):
            first = pstart_ref[e] + tot_ref[e]
            _for_each_chunk(pad_ref[e] - tot_ref[e], lambda off, size: start_or_wait(pltpu.make_async_copy(
                zbuf.at[pl.ds(0, size)], xs_ref.at[pl.ds(pl.multiple_of(first + off, RUN_ALIGN), size)], sem)))
            return 0

        lax.fori_loop(0, N_EXPERTS, body, 0)

        used = pstart_ref[N_EXPERTS - 1] + pad_ref[N_EXPERTS - 1]

        def tail(b, _):
            start_or_wait(pltpu.make_async_copy(
                zbuf, xs_ref.at[pl.ds(pl.multiple_of(b * TM_TOK, TM_TOK), TM_TOK)], sem))
            return 0

        lax.fori_loop(used // TM_TOK, xs_ref.shape[0] // TM_TOK, tail, 0)

    @pl.when(pl.program_id(0) == 0)
    def _():
        zbuf[...] = jnp.zeros_like(zbuf)
        fill(lambda c: c.start())
        fill(lambda c: c.wait())

    r = lax.broadcasted_iota(jnp.int32, (LOCAL_ROWS, TM_TOK), 0)
    hit = r == lpos_ref[0:1, :]
    for kk in range(1, TOP_K):
        hit = hit | (r == lpos_ref[kk:kk + 1, :])
    loc[...] = _dot(jnp.where(hit, 1.0, 0.0).astype(jnp.bfloat16), h_ref[...])

    def copy(local, glob, size):
        return pltpu.make_async_copy(loc.at[pl.ds(local, size)], xs_ref.at[pl.ds(glob, size)], sem)

    _tile_runs(runs_ref, pstart_ref, lambda l, g, s: copy(l, g, s).start())
    _tile_runs(runs_ref, pstart_ref, lambda l, g, s: copy(l, g, s).wait())


def _dispatch(pstart, totals, padded, runs, lpos, h, n_rows):
    n = h.shape[0]
    tm = TM_TOK
    grid_spec = pltpu.PrefetchScalarGridSpec(
        num_scalar_prefetch=3,
        grid=(n // tm,),
        in_specs=[pl.BlockSpec((8, LANES), lambda i, *_: (i, 0), memory_space=pltpu.SMEM),
                  pl.BlockSpec((8, tm), lambda i, *_: (0, i)),
                  pl.BlockSpec((tm, D_MODEL), lambda i, *_: (i, 0))],
        out_specs=pl.BlockSpec(memory_space=pl.ANY),
        scratch_shapes=[pltpu.VMEM((LOCAL_ROWS, D_MODEL), jnp.float32),
                        pltpu.VMEM((TM_TOK, D_MODEL), jnp.float32),
                        pltpu.SemaphoreType.DMA],
    )
    return pl.pallas_call(
        _dispatch_kernel,
        grid_spec=grid_spec,
        out_shape=jax.ShapeDtypeStruct((n_rows, D_MODEL), jnp.float32),
        compiler_params=_cparams(("arbitrary",)),
        name="dispatch",
    )(pstart, totals, padded, runs, lpos, h)


def _expert_kernel(be_ref, bfirst_ref, bvalid_ref, xs_ref, wgu_ref, bgu_ref, wdn_ref, bdn_ref,
                   ys_ref, wgu_bf, wdn_bf):
    b = pl.program_id(0)

    @pl.when(bfirst_ref[b] == 1)
    def _():
        wgu_bf[...] = _bf(wgu_ref[0])
        wdn_bf[...] = _bf(wdn_ref[0])

    @pl.when(bvalid_ref[b] == 1)
    def _():
        gu = _dot(_bf(xs_ref[...]), wgu_bf[...]) + bgu_ref[0]
        gte = jnp.minimum(gu[:, :D_FF], SWIGLU_LIMIT)
        up = jnp.clip(gu[:, D_FF:], -SWIGLU_LIMIT, SWIGLU_LIMIT)
        act = (up + 1.0) * (gte * (1.0 / (1.0 + jnp.exp(-SWIGLU_ALPHA * gte))))
        ys_ref[...] = _dot(_bf(act), wdn_bf[...]) + bdn_ref[0]

    @pl.when(bvalid_ref[b] == 0)
    def _():
        ys_ref[...] = jnp.zeros_like(ys_ref)


def _expert_ffn(blk_e, blk_first, blk_valid, xs, wgu, bgu, wdn, bdn):
    n_rows = xs.shape[0]
    tm = TM_EXP
    grid_spec = pltpu.PrefetchScalarGridSpec(
        num_scalar_prefetch=3,
        grid=(n_rows // tm,),
        in_specs=[pl.BlockSpec((tm, D_MODEL), lambda b, e, f, v: (b, 0)),
                  pl.BlockSpec((1, D_MODEL, 2 * D_FF), lambda b, e, f, v: (e[b], 0, 0)),
                  pl.BlockSpec((1, 1, 2 * D_FF), lambda b, e, f, v: (e[b], 0, 0)),
                  pl.BlockSpec((1, D_FF, D_MODEL), lambda b, e, f, v: (e[b], 0, 0)),
                  pl.BlockSpec((1, 1, D_MODEL), lambda b, e, f, v: (e[b], 0, 0))],
        out_specs=pl.BlockSpec((tm, D_MODEL), lambda b, e, f, v: (b, 0)),
        scratch_shapes=[pltpu.VMEM((D_MODEL, 2 * D_FF), jnp.bfloat16),
                        pltpu.VMEM((D_FF, D_MODEL), jnp.bfloat16)],
    )
    return pl.pallas_call(
        _expert_kernel,
        grid_spec=grid_spec,
        out_shape=jax.ShapeDtypeStruct((n_rows, D_MODEL), jnp.float32),
        compiler_params=_cparams(("arbitrary",)),
        name="expert_ffn",
    )(blk_e, blk_first, blk_valid, xs, wgu, bgu, wdn, bdn)


def _combine_kernel(final, pstart_ref, runs_ref, meta_ref, x_ref, ys_ref, gfin_ref, o_ref, loc, sem):
    @pl.when(pl.program_id(0) == 0)
    def _():
        loc[...] = jnp.zeros_like(loc)

    def copy(local, glob, size):
        return pltpu.make_async_copy(ys_ref.at[pl.ds(glob, size)], loc.at[pl.ds(local, size)], sem)

    _tile_runs(runs_ref, pstart_ref, lambda l, g, s: copy(l, g, s).start())
    _tile_runs(runs_ref, pstart_ref, lambda l, g, s: copy(l, g, s).wait())

    meta = meta_ref[...]
    col = lax.broadcasted_iota(jnp.int32, (TM_TOK, LOCAL_ROWS), 1).astype(jnp.float32)
    wts = jnp.zeros((TM_TOK, LOCAL_ROWS), jnp.float32)
    for kk in range(TOP_K):
        wts = wts + jnp.where(col == meta[:, TOP_K + kk:TOP_K + kk + 1], meta[:, kk:kk + 1], 0.0)
    w_hi = _bf(wts)
    w_lo = _bf(wts - w_hi.astype(jnp.float32))
    ys_loc = _bf(loc[...])
    out = x_ref[...] + _dot(w_hi, ys_loc) + _dot(w_lo, ys_loc)
    if final:
        out = _rms(out, gfin_ref[...])
    o_ref[...] = out


def _combine(pstart, runs, meta, x, ys, gfin, final):
    n = x.shape[0]
    tm = TM_TOK
    grid_spec = pltpu.PrefetchScalarGridSpec(
        num_scalar_prefetch=1,
        grid=(n // tm,),
        in_specs=[pl.BlockSpec((8, LANES), lambda i, ps: (i, 0), memory_space=pltpu.SMEM),
                  pl.BlockSpec((tm, LANES), lambda i, ps: (i, 0)),
                  pl.BlockSpec((tm, D_MODEL), lambda i, ps: (i, 0)),
                  pl.BlockSpec(memory_space=pl.ANY),
                  pl.BlockSpec((1, D_MODEL), lambda i, ps: (0, 0))],
        out_specs=pl.BlockSpec((tm, D_MODEL), lambda i, ps: (i, 0)),
        scratch_shapes=[pltpu.VMEM((LOCAL_ROWS, D_MODEL), jnp.float32), pltpu.SemaphoreType.DMA],
    )
    return pl.pallas_call(
        functools.partial(_combine_kernel, final),
        grid_spec=grid_spec,
        out_shape=jax.ShapeDtypeStruct((n, D_MODEL), jnp.float32),
        compiler_params=_cparams(("arbitrary",)),
        name="combine_final" if final else "combine",
    )(pstart, runs, meta, x, ys, gfin)


def _pad_heads(w, parts):
    rows = w.shape[0]
    per = w.shape[1] // MLA_HEADS
    w3 = w.reshape(rows, MLA_HEADS, per)
    cols = [jnp.zeros((rows, MLA_HEADS, b - a), w.dtype) if sign == 0 else sign * w3[:, :, a:b]
            for a, b, sign in parts]
    used = sum(b - a for a, b, _ in parts)
    cols.append(jnp.zeros((rows, MLA_HEADS, HEAD_PAD - used), w.dtype))
    return jnp.concatenate(cols, axis=2).reshape(rows, MLA_HEADS * HEAD_PAD)


def _layer_weights(w_in, w_uq, w_ukv, w_router, b_router, rel_bias):
    half = MLA_ROPE // 2
    cq, ckv, kpe, sb, ca = (w_in[:, 0:256], w_in[:, 256:384], w_in[:, 384:416],
                            w_in[:, 416:1184], w_in[:, 1184:1952])
    z = lambda width: jnp.zeros((D_MODEL, width), w_in.dtype)
    kpe_pad = jnp.concatenate([z(MLA_NOPE), kpe, z(HEAD_PAD - MLA_NOPE - MLA_ROPE)], axis=1)
    kpe_rot = jnp.concatenate([z(MLA_NOPE), -kpe[:, half:], kpe[:, :half],
                               z(HEAD_PAD - MLA_NOPE - MLA_ROPE)], axis=1)
    win2 = jnp.concatenate([cq, ckv, kpe_pad, kpe_rot, sb, ca], axis=1).astype(jnp.bfloat16)

    d = MLA_NOPE + MLA_ROPE
    wq_full = _pad_heads(w_uq, [(0, d, 1)])
    wq_rot = _pad_heads(w_uq, [(0, MLA_NOPE, 0), (MLA_NOPE + half, d, -1), (MLA_NOPE, MLA_NOPE + half, 1)])
    wq2 = jnp.concatenate([wq_full, wq_rot], axis=1).astype(jnp.bfloat16)

    wk = _pad_heads(w_ukv, [(0, MLA_NOPE, 1)])
    wv = w_ukv.reshape(MLA_KV_RANK, MLA_HEADS, MLA_NOPE + MLA_V)[:, :, MLA_NOPE:].reshape(MLA_KV_RANK, A_W)
    wkv2 = jnp.concatenate([wk, wv], axis=1).astype(jnp.bfloat16)

    wr = jnp.pad(w_router, ((0, 0), (0, LANES - N_EXPERTS)))
    wr_hi = wr.astype(jnp.bfloat16)
    wr_lo = (wr - wr_hi.astype(jnp.float32)).astype(jnp.bfloat16)
    br = jnp.pad(b_router, (0, LANES - N_EXPERTS), constant_values=NEG).reshape(1, LANES)

    ext = jnp.concatenate([rel_bias, jnp.broadcast_to(rel_bias[:, -1:], (CA_HEADS, REL_TAB - 2 * REL_CLIP - 1))], axis=1)
    tab = ext[:, ::-1]
    return win2, wq2, wkv2, wr_hi, wr_lo, br, tab


def kernel(x, positions, attn_norm, w_in, q_norm, w_uq, kv_norm, w_ukv, rel_bias, mix_norm,
           w_o, ffn_norm, w_router, b_router, w_gate_up, b_gate_up, w_down, b_down, final_norm):
    batch, seq, _ = x.shape
    n = batch * seq
    depth = w_in.shape[0]
    xf = x.reshape(n, D_MODEL)

    inv = ROPE_THETA ** (-jnp.arange(0, MLA_ROPE, 2, dtype=jnp.float32) / MLA_ROPE)
    inv_row = jnp.concatenate([jnp.zeros((MLA_NOPE,), jnp.float32), inv, inv,
                               jnp.zeros((HEAD_PAD - MLA_NOPE - MLA_ROPE,), jnp.float32)]).reshape(1, LANES)
    cos_t, sin_t = _rope_tables(positions.reshape(n, 1), inv_row)

    n_rows = n * TOP_K + (n // TM_TOK) * N_EXPERTS * (RUN_ALIGN - 1) + N_EXPERTS * TM_EXP
    n_rows = -(-n_rows // TM_EXP) * TM_EXP
    n_blk = n_rows // TM_EXP
    row2 = lambda v: v.reshape(1, -1)
    wgu_all = w_gate_up.reshape(depth * N_EXPERTS, D_MODEL, 2 * D_FF)
    bgu_all = b_gate_up.reshape(depth * N_EXPERTS, 1, 2 * D_FF)
    wdn_all = w_down.reshape(depth * N_EXPERTS, D_FF, D_MODEL)
    bdn_all = b_down.reshape(depth * N_EXPERTS, 1, D_MODEL)

    for l in range(depth):
        win2, wq2, wkv2, wr_hi, wr_lo, br, tab = _layer_weights(
            w_in[l], w_uq[l], w_ukv[l], w_router[l], b_router[l], rel_bias[l])
        q, k, v, sbq, sbk, sbv, caq, cak, cav = _inproj(
            xf, row2(attn_norm[l]), win2, cos_t, sin_t, row2(q_norm[l]), wq2, row2(kv_norm[l]), wkv2)

        g = mix_norm[l]
        ma = _mla_attention(q, k, v, row2(g[:A_W]), batch, seq)
        mb = _sb_attention(sbq, sbk, sbv, row2(g[A_W:A_W + SB_W]), batch, seq)
        pad = lambda a: jnp.pad(a.reshape(batch, seq, CA_W), ((0, 0), (CA_PAD, 0), (0, 0))).reshape(-1, CA_W)
        mc = _ca_attention(caq, pad(cak), pad(cav), tab, row2(g[A_W + SB_W:]), batch, seq)

        xn, h, lpos, meta, runs, totals = _outproj_router(
            ma, mb, mc, w_o[l].astype(jnp.bfloat16), xf, row2(ffn_norm[l]), wr_hi, wr_lo, br)

        sizes = totals[0, :N_EXPERTS]
        padded = (sizes + TM_EXP - 1) // TM_EXP * TM_EXP
        p_ends = jnp.cumsum(padded)
        p_starts = (p_ends - padded).astype(jnp.int32)
        blk_start = jnp.arange(n_blk, dtype=jnp.int32) * TM_EXP
        blk_valid = (blk_start < p_ends[-1]).astype(jnp.int32)
        last_row = jnp.minimum(blk_start, p_ends[-1] - 1)
        blk_e = jnp.minimum(jnp.sum((last_row[:, None] >= p_ends[None, :]).astype(jnp.int32), axis=1),
                            N_EXPERTS - 1)
        blk_first = jnp.concatenate([jnp.ones((1,), jnp.int32), (blk_e[1:] != blk_e[:-1]).astype(jnp.int32)])

        xs = _dispatch(p_starts, sizes, padded.astype(jnp.int32), runs, lpos, h, n_rows)
        ys = _expert_ffn(blk_e + l * N_EXPERTS, blk_first, blk_valid, xs, wgu_all, bgu_all, wdn_all, bdn_all)
        xf = _combine(p_starts, runs, meta, xn, ys, row2(final_norm), final=(l == depth - 1))

    return xf.reshape(batch, seq, D_MODEL)
```

```python
import functools
import math

import jax
import jax.numpy as jnp
from jax import lax
from jax.experimental import pallas as pl
from jax.experimental.pallas import tpu as pltpu

D_MODEL = 1024
RMS_EPS = 1e-6
MLA_NOPE, MLA_ROPE, MLA_V, MLA_HEADS = 64, 32, 64, 8
MLA_Q_RANK, MLA_KV_RANK = 256, 128
ROPE_THETA = 10000.0
SB_DIM, SB_HEADS = 64, 4
CA_DIM, CA_HEADS = 64, 4
CHUNK = 64
CA_LEFT_CHUNKS = 8
REL_CLIP = 256
N_EXPERTS, TOP_K = 32, 4
D_FF = 1024
SWIGLU_LIMIT, SWIGLU_ALPHA = 7.0, 1.702

LANES = 128
HEAD_PAD = 128
A_W = MLA_HEADS * MLA_V
SB_W = SB_HEADS * SB_DIM
CA_W = CA_HEADS * CA_DIM
CA_PAD = CA_LEFT_CHUNKS * CHUNK
CA_WIN = CA_PAD + 2 * CHUNK
REL_TAB = 1024
HALF = D_MODEL // 2

TM_TOK = 256
TQ_MLA = 256
TQ_SB = 256
TK_SB = 128
TQ_CA = 2 * CHUNK
TM_EXP = 512
RUN_ALIGN = 8
LOCAL_ROWS = 1280
assert LOCAL_ROWS >= TM_TOK * TOP_K + N_EXPERTS * (RUN_ALIGN - 1) and LOCAL_ROWS % LANES == 0
assert TM_EXP % TM_TOK == 0 and TQ_MLA == TM_TOK

_C_CQ, _C_CKV, _C_KPE, _C_KROT = 0, 256, 384, 512
_C_SB, _C_CA, D_IN2 = 640, 1408, 2176

NEG = -1e30
VMEM_LIMIT = 56 * 1024 * 1024


def _rms(v, g):
    return v * lax.rsqrt(jnp.mean(v * v, axis=-1, keepdims=True) + RMS_EPS) * g


def _dot(a, b):
    return jnp.dot(a, b, preferred_element_type=jnp.float32)


def _dot_nt(a, b):
    return lax.dot_general(a, b, (((1,), (1,)), ((), ())), preferred_element_type=jnp.float32)


def _bf(v):
    return v.astype(jnp.bfloat16)


def _pack_rows(v):
    bits = lax.bitcast_convert_type(v, jnp.uint32)
    return (bits[:, :HALF] >> 16) | (bits[:, HALF:] & jnp.uint32(0xFFFF0000))


def _unpack_rows(u):
    lo = lax.bitcast_convert_type(u << 16, jnp.float32)
    hi = lax.bitcast_convert_type(u & jnp.uint32(0xFFFF0000), jnp.float32)
    return _bf(lo), _bf(hi)


def _cparams(sem):
    return pltpu.CompilerParams(dimension_semantics=sem, vmem_limit_bytes=VMEM_LIMIT)


def _rope_kernel(pos_ref, inv_ref, cos_ref, sin_ref):
    ang = pos_ref[...].astype(jnp.float32) * inv_ref[...]
    cos_ref[...] = jnp.cos(ang)
    sin_ref[...] = jnp.sin(ang)


def _rope_tables(pos_col, inv_row):
    n = pos_col.shape[0]
    return pl.pallas_call(
        _rope_kernel,
        grid=(n // TM_TOK,),
        in_specs=[pl.BlockSpec((TM_TOK, 1), lambda i: (i, 0)),
                  pl.BlockSpec((1, LANES), lambda i: (0, 0))],
        out_specs=[pl.BlockSpec((TM_TOK, LANES), lambda i: (i, 0))] * 2,
        out_shape=[jax.ShapeDtypeStruct((n, LANES), jnp.float32)] * 2,
        compiler_params=_cparams(("parallel",)),
        name="rope_tables",
    )(pos_col, inv_row)


def _inproj_kernel(x_ref, g_ref, win_ref, cos_ref, sin_ref, qn_ref, wq_ref, kvn_ref, wkv_ref,
                   q_ref, k_ref, v_ref, sbq_ref, sbk_ref, sbv_ref, caq_ref, cak_ref, cav_ref):
    h = _rms(x_ref[...], g_ref[...])
    proj = _dot(_bf(h), win_ref[...])
    cos128, sin128 = cos_ref[...], sin_ref[...]
    cos_h = jnp.concatenate([cos128] * MLA_HEADS, axis=1)
    sin_h = jnp.concatenate([sin128] * MLA_HEADS, axis=1)
    w = MLA_HEADS * HEAD_PAD

    cqn = _rms(proj[:, _C_CQ:_C_CKV], qn_ref[...])
    q2 = _dot(_bf(cqn), wq_ref[...])
    q = (q2[:, :w] * cos_h + q2[:, w:] * sin_h) * (1.0 / math.sqrt(MLA_NOPE + MLA_ROPE))
    q_ref[0] = _bf(jnp.transpose(q))

    ckvn = _rms(proj[:, _C_CKV:_C_KPE], kvn_ref[...])
    kv2 = _dot(_bf(ckvn), wkv_ref[...])
    kpe = proj[:, _C_KPE:_C_KROT] * cos128 + proj[:, _C_KROT:_C_SB] * sin128
    k_ref[...] = _bf(kv2[:, :w] + jnp.concatenate([kpe] * MLA_HEADS, axis=1))
    v_ref[0] = _bf(jnp.transpose(kv2[:, w:]))

    sb_scale = 1.0 / math.sqrt(SB_DIM)
    sbq_ref[...] = _bf(proj[:, _C_SB:_C_SB + SB_W] * sb_scale)
    sbk_ref[...] = _bf(proj[:, _C_SB + SB_W:_C_SB + 2 * SB_W])
    sbv_ref[...] = _bf(proj[:, _C_SB + 2 * SB_W:_C_CA])
    ca_scale = 1.0 / math.sqrt(CA_DIM)
    caq_ref[...] = _bf(proj[:, _C_CA:_C_CA + CA_W] * ca_scale)
    cak_ref[...] = _bf(proj[:, _C_CA + CA_W:_C_CA + 2 * CA_W])
    cav_ref[...] = _bf(proj[:, _C_CA + 2 * CA_W:D_IN2])


def _inproj(x, g, win2, cos_t, sin_t, qn, wq2, kvn, wkv2):
    n = x.shape[0]
    tm = TM_TOK
    row = lambda width: pl.BlockSpec((tm, width), lambda i: (i, 0))
    full = lambda a: pl.BlockSpec(a.shape, lambda i: (0,) * a.ndim)
    widths = [MLA_HEADS * HEAD_PAD] + [SB_W] * 3 + [CA_W] * 3
    tile_t = lambda rows: pl.BlockSpec((1, rows, tm), lambda i: (i, 0, 0))
    shape_t = lambda rows: jax.ShapeDtypeStruct((n // tm, rows, tm), jnp.bfloat16)
    return pl.pallas_call(
        _inproj_kernel,
        grid=(n // tm,),
        in_specs=[row(D_MODEL), full(g), full(win2), row(LANES), row(LANES),
                  full(qn), full(wq2), full(kvn), full(wkv2)],
        out_specs=[tile_t(MLA_HEADS * HEAD_PAD), row(widths[0]), tile_t(A_W)] + [row(wd) for wd in widths[1:]],
        out_shape=[shape_t(MLA_HEADS * HEAD_PAD), jax.ShapeDtypeStruct((n, widths[0]), jnp.bfloat16), shape_t(A_W)]
                  + [jax.ShapeDtypeStruct((n, wd), jnp.bfloat16) for wd in widths[1:]],
        compiler_params=_cparams(("parallel",)),
        name="inproj",
    )(x, g, win2, cos_t, sin_t, qn, wq2, kvn, wkv2)


def _mla_kernel(qt_ref, k_ref, vt_ref, g_ref, o_ref, m_ref, l_ref, acc_ref):
    i = pl.program_id(1)
    tq = TQ_MLA
    n_pairs = MLA_HEADS // 2
    key_chunk = lax.broadcasted_iota(jnp.int32, (tq, tq), 0) // CHUNK
    qry_chunk = lax.broadcasted_iota(jnp.int32, (tq, tq), 1) // CHUNK
    diag_ok = key_chunk <= qry_chunk
    top = lax.broadcasted_iota(jnp.int32, (LANES, tq), 0) < MLA_V

    m_ref[...] = jnp.full(m_ref.shape, -jnp.inf, jnp.float32)
    l_ref[...] = jnp.zeros(l_ref.shape, jnp.float32)
    acc_ref[...] = jnp.zeros(acc_ref.shape, jnp.float32)

    def block(kb, mask):
        start = pl.multiple_of(kb * tq, tq)
        for pr in range(n_pairs):
            vt = vt_ref[kb, pr * LANES:(pr + 1) * LANES, :]
            zero = jnp.zeros_like(vt)
            vt_bd = jnp.concatenate([jnp.where(top, vt, zero), jnp.where(top, zero, vt)], axis=1)
            pts, alphas = [], []
            for h in (2 * pr, 2 * pr + 1):
                qt = qt_ref[0, h * HEAD_PAD:(h + 1) * HEAD_PAD, :]
                kblk = k_ref[pl.ds(start, tq), h * HEAD_PAD:(h + 1) * HEAD_PAD]
                s = _dot(kblk, qt)
                if mask is not None:
                    s = jnp.where(mask, s, -jnp.inf)
                m_old = m_ref[h]
                m_new = jnp.maximum(m_old, jnp.max(s, axis=0, keepdims=True))
                p = jnp.exp(s - m_new)
                alpha = jnp.exp(m_old - m_new)
                l_ref[h] = alpha * l_ref[h] + jnp.sum(p, axis=0, keepdims=True)
                m_ref[h] = m_new
                pts.append(_bf(p))
                alphas.append(alpha)
            alpha_pair = jnp.where(top, alphas[0], alphas[1])
            acc_ref[pr] = alpha_pair * acc_ref[pr] + _dot(vt_bd, jnp.concatenate(pts, axis=0))

    def body(kb, carry):
        block(kb, None)
        return carry

    lax.fori_loop(0, i, body, 0)
    block(i, diag_ok)

    outs = [jnp.transpose(acc_ref[pr] / jnp.where(top, l_ref[2 * pr], l_ref[2 * pr + 1])) for pr in range(n_pairs)]
    o = jnp.concatenate(outs, axis=1)
    o_ref[...] = _bf(_rms(o, g_ref[...]))


def _mla_attention(qt, k, vt, g, batch, seq):
    nq = seq // TQ_MLA
    w = MLA_HEADS * HEAD_PAD
    return pl.pallas_call(
        _mla_kernel,
        grid=(batch, nq),
        in_specs=[pl.BlockSpec((1, w, TQ_MLA), lambda b, i: (b * nq + i, 0, 0)),
                  pl.BlockSpec((seq, w), lambda b, i: (b, 0)),
                  pl.BlockSpec((nq, A_W, TQ_MLA), lambda b, i: (b, 0, 0)),
                  pl.BlockSpec((1, A_W), lambda b, i: (0, 0))],
        out_specs=pl.BlockSpec((TQ_MLA, A_W), lambda b, i: (b * nq + i, 0)),
        out_shape=jax.ShapeDtypeStruct((batch * seq, A_W), jnp.bfloat16),
        scratch_shapes=[pltpu.VMEM((MLA_HEADS, 1, TQ_MLA), jnp.float32),
                        pltpu.VMEM((MLA_HEADS, 1, TQ_MLA), jnp.float32),
                        pltpu.VMEM((MLA_HEADS // 2, LANES, TQ_MLA), jnp.float32)],
        compiler_params=_cparams(("parallel", "parallel")),
        name="mla_attention",
    )(qt, k, vt, g)


def _sb_kernel(q_ref, k_ref, v_ref, g_ref, o_ref, run_ref, acc_ref):
    i = pl.program_id(1)
    tq, tk = TQ_SB, TK_SB
    n_pairs = SB_HEADS // 2
    r = lax.broadcasted_iota(jnp.int32, (2 * tq, tk), 0)
    c = lax.broadcasted_iota(jnp.int32, (2 * tq, tk), 1)
    q_off = jnp.where(r >= tq, r - tq, r)
    top_low = (r < tq) == (c < SB_DIM)
    r2 = lax.broadcasted_iota(jnp.int32, (tk, 2 * tk), 0)
    c2 = lax.broadcasted_iota(jnp.int32, (tk, 2 * tk), 1)
    sum_mat = jnp.where((c2 >= tk) | (r2 > c2), 1.0, 0.0).astype(jnp.bfloat16)

    run_ref[...] = jnp.zeros(run_ref.shape, jnp.float32)
    acc_ref[...] = jnp.zeros(acc_ref.shape, jnp.float32)

    def block(kb, masked):
        start = pl.multiple_of(kb * tk, tk)
        mask = (c + (kb * tk - i * tq)) < q_off if masked else None
        slowest = None
        for pr in range(n_pairs):
            qp = q_ref[:, pr * LANES:(pr + 1) * LANES]
            q2 = jnp.concatenate([qp, qp], axis=0)
            qm = jnp.where(top_low, q2, jnp.zeros_like(q2))
            kblk = k_ref[pl.ds(start, tk), pr * LANES:(pr + 1) * LANES]
            vblk = v_ref[pl.ds(start, tk), pr * LANES:(pr + 1) * LANES]
            z = _dot_nt(qm, kblk)
            log_keep = -(jnp.maximum(z, 0.0) + jnp.log1p(jnp.exp(-jnp.abs(z))))
            if masked:
                log_keep = jnp.where(mask, log_keep, 0.0)
            hi = _bf(log_keep)
            lo = _bf(log_keep - hi.astype(jnp.float32))
            sums = _dot(hi, sum_mat) + _dot(lo, sum_mat)
            run = run_ref[pr]
            a = jnp.exp(z + log_keep + run + sums[:, :tk])
            if masked:
                a = jnp.where(mask, a, 0.0)
            acc_ref[pr] = acc_ref[pr] + _dot(_bf(a), vblk)
            run = run + sums[:, tk:]
            run_ref[pr] = run
            top = jnp.max(run)
            slowest = top if slowest is None else jnp.maximum(slowest, top)
        return slowest

    underflow = -104.0
    n_diag = tq // tk
    for j in reversed(range(n_diag)):
        first = block(i * n_diag + j, True)
    lax.while_loop(lambda cr: (cr[0] >= 0) & (cr[1] > underflow),
                   lambda cr: (cr[0] - 1, block(cr[0], False)),
                   (i * n_diag - 1, first))

    lane = lax.broadcasted_iota(jnp.int32, (tq, LANES), 1)
    outs = [jnp.where(lane < SB_DIM, acc_ref[pr, 0:tq, :], acc_ref[pr, tq:2 * tq, :]) for pr in range(n_pairs)]
    o = jnp.concatenate(outs, axis=1)
    o_ref[...] = _bf(_rms(o, g_ref[...]))


def _sb_attention(q, k, v, g, batch, seq):
    nq = seq // TQ_SB
    return pl.pallas_call(
        _sb_kernel,
        grid=(batch, nq),
        in_specs=[pl.BlockSpec((TQ_SB, SB_W), lambda b, i: (b * nq + i, 0)),
                  pl.BlockSpec((seq, SB_W), lambda b, i: (b, 0)),
                  pl.BlockSpec((seq, SB_W), lambda b, i: (b, 0)),
                  pl.BlockSpec((1, SB_W), lambda b, i: (0, 0))],
        out_specs=pl.BlockSpec((TQ_SB, SB_W), lambda b, i: (b * nq + i, 0)),
        out_shape=jax.ShapeDtypeStruct((batch * seq, SB_W), jnp.bfloat16),
        scratch_shapes=[pltpu.VMEM((SB_HEADS // 2, 2 * TQ_SB, TK_SB), jnp.float32),
                        pltpu.VMEM((SB_HEADS // 2, 2 * TQ_SB, LANES), jnp.float32)],
        compiler_params=_cparams(("parallel", "parallel")),
        name="sb_attention",
    )(q, k, v, g)


def _ca_kernel(q_ref, k_ref, v_ref, tab_ref, g_ref, o_ref, bias_ref):
    i = pl.program_id(1)
    t = TQ_CA
    start = pl.multiple_of(i * t, t)

    @pl.when((pl.program_id(0) == 0) & (i == 0))
    def _():
        r = lax.broadcasted_iota(jnp.int32, (t, CA_WIN), 0)
        c = lax.broadcasted_iota(jnp.int32, (t, CA_WIN), 1)
        lo = (r // CHUNK) * CHUNK
        band = (c >= lo) & (c < lo + CA_PAD + CHUNK)
        for h in range(CA_HEADS):
            tab = jnp.broadcast_to(tab_ref[h:h + 1, :], (t, REL_TAB))
            bias = pltpu.roll(tab, REL_TAB - (REL_CLIP - 1), 1, stride=1, stride_axis=0)[:, :CA_WIN]
            bias_ref[h] = jnp.where(band, bias, -jnp.inf)

    r2 = lax.broadcasted_iota(jnp.int32, (2 * t, LANES), 0)
    c2 = lax.broadcasted_iota(jnp.int32, (2 * t, LANES), 1)
    top_low = (r2 < t) == (c2 < CA_DIM)
    in_seq = lax.broadcasted_iota(jnp.int32, (2 * t, CA_WIN), 1) + i * t >= CA_PAD
    lane = lax.broadcasted_iota(jnp.int32, (t, LANES), 1)

    outs = []
    for pair in range(CA_HEADS // 2):
        qp = q_ref[:, pair * LANES:(pair + 1) * LANES]
        q2 = jnp.concatenate([qp, qp], axis=0)
        qm = jnp.where(top_low, q2, jnp.zeros_like(q2))
        kwin = k_ref[pl.ds(start, CA_WIN), pair * LANES:(pair + 1) * LANES]
        vwin = v_ref[pl.ds(start, CA_WIN), pair * LANES:(pair + 1) * LANES]
        bias = jnp.concatenate([bias_ref[2 * pair], bias_ref[2 * pair + 1]], axis=0)
        s = jnp.where(in_seq, _dot_nt(qm, kwin) + bias, -jnp.inf)
        m = jnp.max(s, axis=-1, keepdims=True)
        p = jnp.exp(s - m)
        l = jnp.sum(p, axis=-1, keepdims=True)
        o2 = _dot(_bf(p), vwin) / l
        outs.append(jnp.where(lane < CA_DIM, o2[0:t], o2[t:2 * t]))
    o = jnp.concatenate(outs, axis=1)
    o_ref[...] = _bf(_rms(o, g_ref[...]))


def _ca_attention(q, kpad, vpad, tab, g, batch, seq):
    nq = seq // TQ_CA
    return pl.pallas_call(
        _ca_kernel,
        grid=(batch, nq),
        in_specs=[pl.BlockSpec((TQ_CA, CA_W), lambda b, i: (b * nq + i, 0)),
                  pl.BlockSpec((seq + CA_PAD, CA_W), lambda b, i: (b, 0)),
                  pl.BlockSpec((seq + CA_PAD, CA_W), lambda b, i: (b, 0)),
                  pl.BlockSpec((CA_HEADS, REL_TAB), lambda b, i: (0, 0)),
                  pl.BlockSpec((1, CA_W), lambda b, i: (0, 0))],
        out_specs=pl.BlockSpec((TQ_CA, CA_W), lambda b, i: (b * nq + i, 0)),
        out_shape=jax.ShapeDtypeStruct((batch * seq, CA_W), jnp.bfloat16),
        scratch_shapes=[pltpu.VMEM((CA_HEADS, TQ_CA, CA_WIN), jnp.float32)],
        compiler_params=_cparams(("arbitrary", "arbitrary")),
        name="ca_attention",
    )(q, kpad, vpad, tab, g)


def _outproj_router_kernel(ma_ref, mb_ref, mc_ref, wo_ref, x_ref, g_ref, wrh_ref, wrl_ref, br_ref,
                           xn_ref, h_ref, lpos_ref, meta_ref, runs_ref, tot_ref, base_ref):
    tm = TM_TOK

    @pl.when(pl.program_id(0) == 0)
    def _():
        base_ref[...] = jnp.zeros_like(base_ref)

    attn = (_dot(ma_ref[...], wo_ref[0:A_W, :]) + _dot(mb_ref[...], wo_ref[A_W:A_W + SB_W, :])
            + _dot(mc_ref[...], wo_ref[A_W + SB_W:, :]))
    xn = x_ref[...] + attn
    xn_ref[...] = xn
    h = _rms(xn, g_ref[...])

    h_hi = _bf(h)
    h_ref[...] = h_hi
    h_lo = _bf(h - h_hi.astype(jnp.float32))
    logits = (_dot(h_hi, wrh_ref[...]) + _dot(h_hi, wrl_ref[...]) + _dot(h_lo, wrh_ref[...])
              + br_ref[...])
    lane = lax.broadcasted_iota(jnp.int32, (tm, LANES), 1)
    lane_f = lane.astype(jnp.float32)

    work = logits
    vals, idxs, hots = [], [], []
    for _ in range(TOP_K):
        mx = jnp.max(work, axis=-1, keepdims=True)
        ix = jnp.min(jnp.where(work == mx, lane_f, float(LANES)), axis=-1, keepdims=True)
        hot = lane_f == ix
        work = jnp.where(hot, -jnp.inf, work)
        vals.append(mx)
        idxs.append(ix)
        hots.append(hot)
    exps = [jnp.exp(v - vals[0]) for v in vals]
    denom = exps[0] + exps[1] + exps[2] + exps[3]
    gates = [e / denom for e in exps]

    sel = jnp.zeros((tm, LANES), jnp.float32)
    for hot in hots:
        sel = sel + jnp.where(hot, 1.0, 0.0)
    r = lax.broadcasted_iota(jnp.int32, (tm, tm), 0)
    c = lax.broadcasted_iota(jnp.int32, (tm, tm), 1)
    before = jnp.where(c < r, 1.0, 0.0).astype(jnp.bfloat16)
    rank_in_tile = _dot(before, _bf(sel))

    cnt = jnp.sum(sel, axis=0, keepdims=True)
    cnt_al = jnp.ceil(cnt * (1.0 / RUN_ALIGN)) * RUN_ALIGN
    rl = lax.broadcasted_iota(jnp.int32, (LANES, LANES), 0)
    cl = lax.broadcasted_iota(jnp.int32, (LANES, LANES), 1)
    earlier = jnp.where(rl < cl, 1.0, 0.0).astype(jnp.bfloat16)
    loff = _dot(_bf(jnp.broadcast_to(cnt_al, (8, LANES))), earlier)[0:1, :]
    base = base_ref[...]
    base_ref[...] = base + cnt_al
    tot_ref[...] = base_ref[...].astype(jnp.int32)
    sub = lax.broadcasted_iota(jnp.int32, (8, LANES), 0)
    runs = jnp.where(sub == 0, loff, jnp.where(sub == 1, base, jnp.where(sub == 2, cnt_al, 0.0)))
    runs_ref[...] = runs.astype(jnp.int32)

    lpos_dense = loff + rank_in_tile
    meta = jnp.zeros((tm, LANES), jnp.float32)
    for kk in range(TOP_K):
        lpos = jnp.sum(jnp.where(hots[kk], lpos_dense, 0.0), axis=-1, keepdims=True)
        meta = meta + jnp.where(lane == kk, gates[kk], 0.0) + jnp.where(lane == TOP_K + kk, lpos, 0.0)
    meta_ref[...] = meta
    lpos_ref[...] = jnp.transpose(meta)[TOP_K:TOP_K + 8, :].astype(jnp.int32)


def _outproj_router(ma, mb, mc, wo, x, g, wrh, wrl, br):
    n = x.shape[0]
    tm = TM_TOK
    row = lambda width: pl.BlockSpec((tm, width), lambda i: (i, 0))
    full = lambda a: pl.BlockSpec(a.shape, lambda i: (0,) * a.ndim)
    return pl.pallas_call(
        _outproj_router_kernel,
        grid=(n // tm,),
        in_specs=[row(A_W), row(SB_W), row(CA_W), full(wo), row(D_MODEL), full(g),
                  full(wrh), full(wrl), full(br)],
        out_specs=[row(D_MODEL), row(D_MODEL), pl.BlockSpec((8, tm), lambda i: (0, i)),
                   row(LANES), pl.BlockSpec((8, LANES), lambda i: (i, 0)),
                   pl.BlockSpec((1, LANES), lambda i: (0, 0))],
        out_shape=[jax.ShapeDtypeStruct((n, D_MODEL), jnp.float32),
                   jax.ShapeDtypeStruct((n, D_MODEL), jnp.bfloat16),
                   jax.ShapeDtypeStruct((8, n), jnp.int32),
                   jax.ShapeDtypeStruct((n, LANES), jnp.float32),
                   jax.ShapeDtypeStruct((8 * (n // tm), LANES), jnp.int32),
                   jax.ShapeDtypeStruct((1, LANES), jnp.int32)],
        scratch_shapes=[pltpu.VMEM((1, LANES), jnp.float32)],
        compiler_params=_cparams(("arbitrary",)),
        name="outproj_router",
    )(ma, mb, mc, wo, x, g, wrh, wrl, br)


_RUN_CHUNKS = tuple(TM_TOK >> s for s in range(TM_TOK.bit_length()) if (TM_TOK >> s) >= RUN_ALIGN)


def _for_each_chunk(length, fn):
    for size in _RUN_CHUNKS:
        off = length & (~(2 * size - 1))

        @pl.when((length & size) != 0)
        def _(off=off, size=size):
            fn(off, size)


def _tile_runs(runs_ref, pstart_ref, fn):
    def body(e, _):
        local, glob, length = runs_ref[0, e], pstart_ref[e] + runs_ref[1, e], runs_ref[2, e]
        _for_each_chunk(length, lambda off, size: fn(pl.multiple_of(local + off, RUN_ALIGN),
                                                     pl.multiple_of(glob + off, RUN_ALIGN), size))
        return 0

    lax.fori_loop(0, N_EXPERTS, body, 0)


def _dispatch_kernel(pstart_ref, tot_ref, pad_ref, runs_ref, prev_runs_ref, lpos_ref, h_ref, xs_ref,
                     loc, zbuf, sem, fill_sem):
    i = pl.program_id(0)
    slot = i % 2

    def fill(start_or_wait):
        def body(e, _):
            first = pstart_ref[e] + tot_ref[e]
            _for_each_chunk(pad_ref[e] - tot_ref[e], lambda off, size: start_or_wait(pltpu.make_async_copy(
                zbuf.at[pl.ds(0, size)], xs_ref.at[pl.ds(pl.multiple_of(first + off, RUN_ALIGN), size)], fill_sem)))
            return 0

        lax.fori_loop(0, N_EXPERTS, body, 0)

        used = pstart_ref[N_EXPERTS - 1] + pad_ref[N_EXPERTS - 1]

        def tail(b, _):
            start_or_wait(pltpu.make_async_copy(
                zbuf, xs_ref.at[pl.ds(pl.multiple_of(b * TM_TOK, TM_TOK), TM_TOK)], fill_sem))
            return 0

        lax.fori_loop(used // TM_TOK, xs_ref.shape[0] // TM_TOK, tail, 0)

    @pl.when(i == 0)
    def _():
        zbuf[...] = jnp.zeros_like(zbuf)
        fill(lambda c: c.start())
        fill(lambda c: c.wait())

    r = lax.broadcasted_iota(jnp.int32, (LOCAL_ROWS, TM_TOK), 0)
    hit = r == lpos_ref[0:1, :]
    for kk in range(1, TOP_K):
        hit = hit | (r == lpos_ref[kk:kk + 1, :])
    loc[slot] = _pack_rows(_dot(jnp.where(hit, 1.0, 0.0).astype(jnp.bfloat16), h_ref[...]))

    def copy(buf, local, glob, size):
        return pltpu.make_async_copy(loc.at[buf, pl.ds(local, size)], xs_ref.at[pl.ds(glob, size)], sem.at[buf])

    _tile_runs(runs_ref, pstart_ref, lambda l, g, s: copy(slot, l, g, s).start())

    @pl.when(i > 0)
    def _():
        _tile_runs(prev_runs_ref, pstart_ref, lambda l, g, s: copy(1 - slot, l, g, s).wait())

    @pl.when(i == pl.num_programs(0) - 1)
    def _():
        _tile_runs(runs_ref, pstart_ref, lambda l, g, s: copy(slot, l, g, s).wait())


def _dispatch(pstart, totals, padded, runs, lpos, h, n_rows):
    n = h.shape[0]
    tm = TM_TOK
    grid_spec = pltpu.PrefetchScalarGridSpec(
        num_scalar_prefetch=3,
        grid=(n // tm,),
        in_specs=[pl.BlockSpec((8, LANES), lambda i, *_: (i, 0), memory_space=pltpu.SMEM),
                  pl.BlockSpec((8, LANES), lambda i, *_: (jnp.maximum(i - 1, 0), 0), memory_space=pltpu.SMEM),
                  pl.BlockSpec((8, tm), lambda i, *_: (0, i)),
                  pl.BlockSpec((tm, D_MODEL), lambda i, *_: (i, 0))],
        out_specs=pl.BlockSpec(memory_space=pl.ANY),
        scratch_shapes=[pltpu.VMEM((2, LOCAL_ROWS, HALF), jnp.uint32),
                        pltpu.VMEM((TM_TOK, HALF), jnp.uint32),
                        pltpu.SemaphoreType.DMA((2,)),
                        pltpu.SemaphoreType.DMA],
    )
    return pl.pallas_call(
        _dispatch_kernel,
        grid_spec=grid_spec,
        out_shape=jax.ShapeDtypeStruct((n_rows, HALF), jnp.uint32),
        compiler_params=_cparams(("arbitrary",)),
        name="dispatch",
    )(pstart, totals, padded, runs, runs, lpos, h)


def _expert_kernel(be_ref, bfirst_ref, bvalid_ref, xs_ref, wgu_ref, bgu_ref, wdn_ref, bdn_ref,
                   ys_ref, wgu_bf, wdn_bf):
    b = pl.program_id(0)

    @pl.when(bfirst_ref[b] == 1)
    def _():
        wgu_bf[...] = _bf(wgu_ref[0])
        wdn_bf[...] = _bf(wdn_ref[0])

    @pl.when(bvalid_ref[b] == 1)
    def _():
        x_lo, x_hi = _unpack_rows(xs_ref[...])
        gu = _dot(x_lo, wgu_bf[0:HALF, :]) + _dot(x_hi, wgu_bf[HALF:, :]) + bgu_ref[0]
        gte = jnp.minimum(gu[:, :D_FF], SWIGLU_LIMIT)
        up = jnp.clip(gu[:, D_FF:], -SWIGLU_LIMIT, SWIGLU_LIMIT)
        act = (up + 1.0) * (gte * (1.0 / (1.0 + jnp.exp(-SWIGLU_ALPHA * gte))))
        y = _dot(_bf(act), wdn_bf[...]) + bdn_ref[0]
        ys_ref[...] = _pack_rows(_bf(y).astype(jnp.float32))

    @pl.when(bvalid_ref[b] == 0)
    def _():
        ys_ref[...] = jnp.zeros_like(ys_ref)


def _expert_ffn(blk_e, blk_first, blk_valid, xs, wgu, bgu, wdn, bdn):
    n_rows = xs.shape[0]
    tm = TM_EXP
    grid_spec = pltpu.PrefetchScalarGridSpec(
        num_scalar_prefetch=3,
        grid=(n_rows // tm,),
        in_specs=[pl.BlockSpec((tm, HALF), lambda b, e, f, v: (b, 0)),
                  pl.BlockSpec((1, D_MODEL, 2 * D_FF), lambda b, e, f, v: (e[b], 0, 0)),
                  pl.BlockSpec((1, 1, 2 * D_FF), lambda b, e, f, v: (e[b], 0, 0)),
                  pl.BlockSpec((1, D_FF, D_MODEL), lambda b, e, f, v: (e[b], 0, 0)),
                  pl.BlockSpec((1, 1, D_MODEL), lambda b, e, f, v: (e[b], 0, 0))],
        out_specs=pl.BlockSpec((tm, HALF), lambda b, e, f, v: (b, 0)),
        scratch_shapes=[pltpu.VMEM((D_MODEL, 2 * D_FF), jnp.bfloat16),
                        pltpu.VMEM((D_FF, D_MODEL), jnp.bfloat16)],
    )
    return pl.pallas_call(
        _expert_kernel,
        grid_spec=grid_spec,
        out_shape=jax.ShapeDtypeStruct((n_rows, HALF), jnp.uint32),
        compiler_params=_cparams(("arbitrary",)),
        name="expert_ffn",
    )(blk_e, blk_first, blk_valid, xs, wgu, bgu, wdn, bdn)


def _combine_kernel(final, pstart_ref, runs_ref, next_runs_ref, meta_ref, x_ref, ys_ref, gfin_ref, o_ref, loc, sem):
    i = pl.program_id(0)
    slot = i % 2

    def copy(buf, local, glob, size):
        return pltpu.make_async_copy(ys_ref.at[pl.ds(glob, size)], loc.at[buf, pl.ds(local, size)], sem.at[buf])

    @pl.when(i == 0)
    def _():
        loc[...] = jnp.zeros_like(loc)
        _tile_runs(runs_ref, pstart_ref, lambda l, g, s: copy(slot, l, g, s).start())

    @pl.when(i + 1 < pl.num_programs(0))
    def _():
        _tile_runs(next_runs_ref, pstart_ref, lambda l, g, s: copy(1 - slot, l, g, s).start())

    _tile_runs(runs_ref, pstart_ref, lambda l, g, s: copy(slot, l, g, s).wait())

    meta = meta_ref[...]
    col = lax.broadcasted_iota(jnp.int32, (TM_TOK, LOCAL_ROWS), 1).astype(jnp.float32)
    wts = jnp.zeros((TM_TOK, LOCAL_ROWS), jnp.float32)
    for kk in range(TOP_K):
        wts = wts + jnp.where(col == meta[:, TOP_K + kk:TOP_K + kk + 1], meta[:, kk:kk + 1], 0.0)
    w_hi = _bf(wts)
    w_lo = _bf(wts - w_hi.astype(jnp.float32))
    y_lo, y_hi = _unpack_rows(loc[slot])
    out = x_ref[...] + jnp.concatenate([_dot(w_hi, y_lo) + _dot(w_lo, y_lo),
                                        _dot(w_hi, y_hi) + _dot(w_lo, y_hi)], axis=1)
    if final:
        out = _rms(out, gfin_ref[...])
    o_ref[...] = out


def _combine(pstart, runs, meta, x, ys, gfin, final):
    n = x.shape[0]
    tm = TM_TOK
    n_tiles = n // tm
    grid_spec = pltpu.PrefetchScalarGridSpec(
        num_scalar_prefetch=1,
        grid=(n_tiles,),
        in_specs=[pl.BlockSpec((8, LANES), lambda i, ps: (i, 0), memory_space=pltpu.SMEM),
                  pl.BlockSpec((8, LANES), lambda i, ps: (jnp.minimum(i + 1, n_tiles - 1), 0), memory_space=pltpu.SMEM),
                  pl.BlockSpec((tm, LANES), lambda i, ps: (i, 0)),
                  pl.BlockSpec((tm, D_MODEL), lambda i, ps: (i, 0)),
                  pl.BlockSpec(memory_space=pl.ANY),
                  pl.BlockSpec((1, D_MODEL), lambda i, ps: (0, 0))],
        out_specs=pl.BlockSpec((tm, D_MODEL), lambda i, ps: (i, 0)),
        scratch_shapes=[pltpu.VMEM((2, LOCAL_ROWS, HALF), jnp.uint32), pltpu.SemaphoreType.DMA((2,))],
    )
    return pl.pallas_call(
        functools.partial(_combine_kernel, final),
        grid_spec=grid_spec,
        out_shape=jax.ShapeDtypeStruct((n, D_MODEL), jnp.float32),
        compiler_params=_cparams(("arbitrary",)),
        name="combine_final" if final else "combine",
    )(pstart, runs, runs, meta, x, ys, gfin)


def _pad_heads(w, parts):
    rows = w.shape[0]
    per = w.shape[1] // MLA_HEADS
    w3 = w.reshape(rows, MLA_HEADS, per)
    cols = [jnp.zeros((rows, MLA_HEADS, b - a), w.dtype) if sign == 0 else sign * w3[:, :, a:b]
            for a, b, sign in parts]
    used = sum(b - a for a, b, _ in parts)
    cols.append(jnp.zeros((rows, MLA_HEADS, HEAD_PAD - used), w.dtype))
    return jnp.concatenate(cols, axis=2).reshape(rows, MLA_HEADS * HEAD_PAD)


def _layer_weights(w_in, w_uq, w_ukv, w_router, b_router, rel_bias):
    half = MLA_ROPE // 2
    cq, ckv, kpe, sb, ca = (w_in[:, 0:256], w_in[:, 256:384], w_in[:, 384:416],
                            w_in[:, 416:1184], w_in[:, 1184:1952])
    z = lambda width: jnp.zeros((D_MODEL, width), w_in.dtype)
    kpe_pad = jnp.concatenate([z(MLA_NOPE), kpe, z(HEAD_PAD - MLA_NOPE - MLA_ROPE)], axis=1)
    kpe_rot = jnp.concatenate([z(MLA_NOPE), -kpe[:, half:], kpe[:, :half],
                               z(HEAD_PAD - MLA_NOPE - MLA_ROPE)], axis=1)
    win2 = jnp.concatenate([cq, ckv, kpe_pad, kpe_rot, sb, ca], axis=1).astype(jnp.bfloat16)

    d = MLA_NOPE + MLA_ROPE
    wq_full = _pad_heads(w_uq, [(0, d, 1)])
    wq_rot = _pad_heads(w_uq, [(0, MLA_NOPE, 0), (MLA_NOPE + half, d, -1), (MLA_NOPE, MLA_NOPE + half, 1)])
    wq2 = jnp.concatenate([wq_full, wq_rot], axis=1).astype(jnp.bfloat16)

    wk = _pad_heads(w_ukv, [(0, MLA_NOPE, 1)])
    wv = w_ukv.reshape(MLA_KV_RANK, MLA_HEADS, MLA_NOPE + MLA_V)[:, :, MLA_NOPE:].reshape(MLA_KV_RANK, A_W)
    wkv2 = jnp.concatenate([wk, wv], axis=1).astype(jnp.bfloat16)

    wr = jnp.pad(w_router, ((0, 0), (0, LANES - N_EXPERTS)))
    wr_hi = wr.astype(jnp.bfloat16)
    wr_lo = (wr - wr_hi.astype(jnp.float32)).astype(jnp.bfloat16)
    br = jnp.pad(b_router, (0, LANES - N_EXPERTS), constant_values=NEG).reshape(1, LANES)

    ext = jnp.concatenate([rel_bias, jnp.broadcast_to(rel_bias[:, -1:], (CA_HEADS, REL_TAB - 2 * REL_CLIP - 1))], axis=1)
    tab = ext[:, ::-1]
    return win2, wq2, wkv2, wr_hi, wr_lo, br, tab


def kernel(x, positions, attn_norm, w_in, q_norm, w_uq, kv_norm, w_ukv, rel_bias, mix_norm,
           w_o, ffn_norm, w_router, b_router, w_gate_up, b_gate_up, w_down, b_down, final_norm):
    batch, seq, _ = x.shape
    n = batch * seq
    depth = w_in.shape[0]
    xf = x.reshape(n, D_MODEL)

    inv = ROPE_THETA ** (-jnp.arange(0, MLA_ROPE, 2, dtype=jnp.float32) / MLA_ROPE)
    inv_row = jnp.concatenate([jnp.zeros((MLA_NOPE,), jnp.float32), inv, inv,
                               jnp.zeros((HEAD_PAD - MLA_NOPE - MLA_ROPE,), jnp.float32)]).reshape(1, LANES)
    cos_t, sin_t = _rope_tables(positions.reshape(n, 1), inv_row)

    n_rows = n * TOP_K + (n // TM_TOK) * N_EXPERTS * (RUN_ALIGN - 1) + N_EXPERTS * TM_EXP
    n_rows = -(-n_rows // TM_EXP) * TM_EXP
    n_blk = n_rows // TM_EXP
    row2 = lambda v: v.reshape(1, -1)
    wgu_all = w_gate_up.reshape(depth * N_EXPERTS, D_MODEL, 2 * D_FF)
    bgu_all = b_gate_up.reshape(depth * N_EXPERTS, 1, 2 * D_FF)
    wdn_all = w_down.reshape(depth * N_EXPERTS, D_FF, D_MODEL)
    bdn_all = b_down.reshape(depth * N_EXPERTS, 1, D_MODEL)

    for l in range(depth):
        win2, wq2, wkv2, wr_hi, wr_lo, br, tab = _layer_weights(
            w_in[l], w_uq[l], w_ukv[l], w_router[l], b_router[l], rel_bias[l])
        qt, k, vt, sbq, sbk, sbv, caq, cak, cav = _inproj(
            xf, row2(attn_norm[l]), win2, cos_t, sin_t, row2(q_norm[l]), wq2, row2(kv_norm[l]), wkv2)

        g = mix_norm[l]
        ma = _mla_attention(qt, k, vt, row2(g[:A_W]), batch, seq)
        mb = _sb_attention(sbq, sbk, sbv, row2(g[A_W:A_W + SB_W]), batch, seq)
        pad = lambda a: jnp.pad(a.reshape(batch, seq, CA_W), ((0, 0), (CA_PAD, 0), (0, 0))).reshape(-1, CA_W)
        mc = _ca_attention(caq, pad(cak), pad(cav), tab, row2(g[A_W + SB_W:]), batch, seq)

        xn, h, lpos, meta, runs, totals = _outproj_router(
            ma, mb, mc, w_o[l].astype(jnp.bfloat16), xf, row2(ffn_norm[l]), wr_hi, wr_lo, br)

        sizes = totals[0, :N_EXPERTS]
        padded = (sizes + TM_EXP - 1) // TM_EXP * TM_EXP
        p_ends = jnp.cumsum(padded)
        p_starts = (p_ends - padded).astype(jnp.int32)
        blk_start = jnp.arange(n_blk, dtype=jnp.int32) * TM_EXP
        blk_valid = (blk_start < p_ends[-1]).astype(jnp.int32)
        last_row = jnp.minimum(blk_start, p_ends[-1] - 1)
        blk_e = jnp.minimum(jnp.sum((last_row[:, None] >= p_ends[None, :]).astype(jnp.int32), axis=1),
                            N_EXPERTS - 1)
        blk_first = jnp.concatenate([jnp.ones((1,), jnp.int32), (blk_e[1:] != blk_e[:-1]).astype(jnp.int32)])

        xs = _dispatch(p_starts, sizes, padded.astype(jnp.int32), runs, lpos, h, n_rows)
        ys = _expert_ffn(blk_e + l * N_EXPERTS, blk_first, blk_valid, xs, wgu_all, bgu_all, wdn_all, bdn_all)
        xf = _combine(p_starts, runs, meta, xn, ys, row2(final_norm), final=(l == depth - 1))

    return xf.reshape(batch, seq, D_MODEL)
```

```python
import functools
import math

import jax
import jax.numpy as jnp
from jax import lax
from jax.experimental import pallas as pl
from jax.experimental.pallas import tpu as pltpu

D_MODEL = 1024
RMS_EPS = 1e-6
MLA_NOPE, MLA_ROPE, MLA_V, MLA_HEADS = 64, 32, 64, 8
MLA_Q_RANK, MLA_KV_RANK = 256, 128
ROPE_THETA = 10000.0
SB_DIM, SB_HEADS = 64, 4
CA_DIM, CA_HEADS = 64, 4
CHUNK = 64
CA_LEFT_CHUNKS = 8
REL_CLIP = 256
N_EXPERTS, TOP_K = 32, 4
D_FF = 1024
SWIGLU_LIMIT, SWIGLU_ALPHA = 7.0, 1.702

LANES = 128
HEAD_PAD = 128
A_W = MLA_HEADS * MLA_V
SB_W = SB_HEADS * SB_DIM
CA_W = CA_HEADS * CA_DIM
CA_PAD = CA_LEFT_CHUNKS * CHUNK
CA_WIN = CA_PAD + 2 * CHUNK
REL_TAB = 1024
HALF = D_MODEL // 2

TM_TOK = 256
TQ_MLA = 256
TQ_SB = 256
TK_SB = 128
TQ_CA = 2 * CHUNK
TM_EXP = 512
RUN_ALIGN = 8
LOCAL_ROWS = 1280
assert LOCAL_ROWS >= TM_TOK * TOP_K + N_EXPERTS * (RUN_ALIGN - 1) and LOCAL_ROWS % LANES == 0
assert TM_EXP % TM_TOK == 0 and TQ_MLA == TM_TOK

_C_CQ, _C_CKV, _C_KPE, _C_KROT = 0, 256, 384, 512
_C_SB, _C_CA, D_IN2 = 640, 1408, 2176

NEG = -1e30
VMEM_LIMIT = 56 * 1024 * 1024


def _rms(v, g):
    return v * lax.rsqrt(jnp.mean(v * v, axis=-1, keepdims=True) + RMS_EPS) * g


def _dot(a, b):
    return jnp.dot(a, b, preferred_element_type=jnp.float32)


def _dot_nt(a, b):
    return lax.dot_general(a, b, (((1,), (1,)), ((), ())), preferred_element_type=jnp.float32)


def _bf(v):
    return v.astype(jnp.bfloat16)


def _pack_rows(v):
    bits = lax.bitcast_convert_type(v, jnp.uint32)
    return (bits[:, :HALF] >> 16) | (bits[:, HALF:] & jnp.uint32(0xFFFF0000))


def _unpack_rows(u):
    lo = lax.bitcast_convert_type(u << 16, jnp.float32)
    hi = lax.bitcast_convert_type(u & jnp.uint32(0xFFFF0000), jnp.float32)
    return _bf(lo), _bf(hi)


def _cparams(sem):
    return pltpu.CompilerParams(dimension_semantics=sem, vmem_limit_bytes=VMEM_LIMIT)


def _rope_kernel(pos_ref, inv_ref, cos_ref, sin_ref):
    ang = pos_ref[...].astype(jnp.float32) * inv_ref[...]
    cos_ref[...] = jnp.cos(ang)
    sin_ref[...] = jnp.sin(ang)


def _rope_tables(pos_col, inv_row):
    n = pos_col.shape[0]
    return pl.pallas_call(
        _rope_kernel,
        grid=(n // TM_TOK,),
        in_specs=[pl.BlockSpec((TM_TOK, 1), lambda i: (i, 0)),
                  pl.BlockSpec((1, LANES), lambda i: (0, 0))],
        out_specs=[pl.BlockSpec((TM_TOK, LANES), lambda i: (i, 0))] * 2,
        out_shape=[jax.ShapeDtypeStruct((n, LANES), jnp.float32)] * 2,
        compiler_params=_cparams(("parallel",)),
        name="rope_tables",
    )(pos_col, inv_row)


def _inproj_kernel(x_ref, g_ref, win_ref, cos_ref, sin_ref, qn_ref, wq_ref, kvn_ref, wkv_ref,
                   q_ref, k_ref, v_ref, sbq_ref, sbk_ref, sbv_ref, caq_ref, cak_ref, cav_ref):
    h = _rms(x_ref[...], g_ref[...])
    proj = _dot(_bf(h), win_ref[...])
    cos128, sin128 = cos_ref[...], sin_ref[...]
    cos_h = jnp.concatenate([cos128] * MLA_HEADS, axis=1)
    sin_h = jnp.concatenate([sin128] * MLA_HEADS, axis=1)
    w = MLA_HEADS * HEAD_PAD

    cqn = _rms(proj[:, _C_CQ:_C_CKV], qn_ref[...])
    q2 = _dot(_bf(cqn), wq_ref[...])
    q = (q2[:, :w] * cos_h + q2[:, w:] * sin_h) * (1.0 / math.sqrt(MLA_NOPE + MLA_ROPE))
    q_ref[0] = _bf(jnp.transpose(q))

    ckvn = _rms(proj[:, _C_CKV:_C_KPE], kvn_ref[...])
    kv2 = _dot(_bf(ckvn), wkv_ref[...])
    kpe = proj[:, _C_KPE:_C_KROT] * cos128 + proj[:, _C_KROT:_C_SB] * sin128
    k_ref[...] = _bf(kv2[:, :w] + jnp.concatenate([kpe] * MLA_HEADS, axis=1))
    v_ref[0] = _bf(jnp.transpose(kv2[:, w:]))

    sb_scale = 1.0 / math.sqrt(SB_DIM)
    sbq_ref[...] = _bf(proj[:, _C_SB:_C_SB + SB_W] * sb_scale)
    sbk_ref[...] = _bf(proj[:, _C_SB + SB_W:_C_SB + 2 * SB_W])
    sbv_ref[...] = _bf(proj[:, _C_SB + 2 * SB_W:_C_CA])
    ca_scale = 1.0 / math.sqrt(CA_DIM)
    caq_ref[...] = _bf(proj[:, _C_CA:_C_CA + CA_W] * ca_scale)
    cak_ref[...] = _bf(proj[:, _C_CA + CA_W:_C_CA + 2 * CA_W])
    cav_ref[...] = _bf(proj[:, _C_CA + 2 * CA_W:D_IN2])


def _inproj(x, g, win2, cos_t, sin_t, qn, wq2, kvn, wkv2):
    n = x.shape[0]
    tm = TM_TOK
    row = lambda width: pl.BlockSpec((tm, width), lambda i: (i, 0))
    full = lambda a: pl.BlockSpec(a.shape, lambda i: (0,) * a.ndim)
    widths = [MLA_HEADS * HEAD_PAD] + [SB_W] * 3 + [CA_W] * 3
    tile_t = lambda rows: pl.BlockSpec((1, rows, tm), lambda i: (i, 0, 0))
    shape_t = lambda rows: jax.ShapeDtypeStruct((n // tm, rows, tm), jnp.bfloat16)
    return pl.pallas_call(
        _inproj_kernel,
        grid=(n // tm,),
        in_specs=[row(D_MODEL), full(g), full(win2), row(LANES), row(LANES),
                  full(qn), full(wq2), full(kvn), full(wkv2)],
        out_specs=[tile_t(MLA_HEADS * HEAD_PAD), row(widths[0]), tile_t(A_W)] + [row(wd) for wd in widths[1:]],
        out_shape=[shape_t(MLA_HEADS * HEAD_PAD), jax.ShapeDtypeStruct((n, widths[0]), jnp.bfloat16), shape_t(A_W)]
                  + [jax.ShapeDtypeStruct((n, wd), jnp.bfloat16) for wd in widths[1:]],
        compiler_params=_cparams(("parallel",)),
        name="inproj",
    )(x, g, win2, cos_t, sin_t, qn, wq2, kvn, wkv2)


def _mla_kernel(qt_ref, k_ref, vt_ref, g_ref, o_ref, m_ref, l_ref, acc_ref, s_ref):
    i = pl.program_id(1)
    tq = TQ_MLA
    n_pairs = MLA_HEADS // 2
    key_chunk = lax.broadcasted_iota(jnp.int32, (tq, tq), 0) // CHUNK
    qry_chunk = lax.broadcasted_iota(jnp.int32, (tq, tq), 1) // CHUNK
    diag_ok = key_chunk <= qry_chunk
    top = lax.broadcasted_iota(jnp.int32, (LANES, tq), 0) < MLA_V

    m_ref[...] = jnp.full(m_ref.shape, -jnp.inf, jnp.float32)
    l_ref[...] = jnp.zeros(l_ref.shape, jnp.float32)
    acc_ref[...] = jnp.zeros(acc_ref.shape, jnp.float32)

    def scores(kb, slot):
        start = pl.multiple_of(kb * tq, tq)
        for h in range(MLA_HEADS):
            qt = qt_ref[0, h * HEAD_PAD:(h + 1) * HEAD_PAD, :]
            kblk = k_ref[pl.ds(start, tq), h * HEAD_PAD:(h + 1) * HEAD_PAD]
            s_ref[slot, h] = _dot(kblk, qt)

    def absorb(kb, slot, mask):
        for pr in range(n_pairs):
            vt = vt_ref[kb, pr * LANES:(pr + 1) * LANES, :]
            zero = jnp.zeros_like(vt)
            vt_bd = jnp.concatenate([jnp.where(top, vt, zero), jnp.where(top, zero, vt)], axis=1)
            pts, alphas = [], []
            for h in (2 * pr, 2 * pr + 1):
                s = s_ref[slot, h]
                if mask is not None:
                    s = jnp.where(mask, s, -jnp.inf)
                m_old = m_ref[h]
                m_new = jnp.maximum(m_old, jnp.max(s, axis=0, keepdims=True))
                p = jnp.exp(s - m_new)
                alpha = jnp.exp(m_old - m_new)
                l_ref[h] = alpha * l_ref[h] + jnp.sum(p, axis=0, keepdims=True)
                m_ref[h] = m_new
                pts.append(_bf(p))
                alphas.append(alpha)
            alpha_pair = jnp.where(top, alphas[0], alphas[1])
            acc_ref[pr] = alpha_pair * acc_ref[pr] + _dot(vt_bd, jnp.concatenate(pts, axis=0))

    def body(j, carry):
        scores(2 * j + 1, 1)
        absorb(2 * j, 0, None)
        scores(2 * j + 2, 0)
        absorb(2 * j + 1, 1, None)
        return carry

    scores(0, 0)
    lax.fori_loop(0, i // 2, body, 0)

    @pl.when(i % 2 == 0)
    def _():
        absorb(i, 0, diag_ok)

    @pl.when(i % 2 == 1)
    def _():
        scores(i, 1)
        absorb(i - 1, 0, None)
        absorb(i, 1, diag_ok)

    outs = [jnp.transpose(acc_ref[pr] / jnp.where(top, l_ref[2 * pr], l_ref[2 * pr + 1])) for pr in range(n_pairs)]
    o = jnp.concatenate(outs, axis=1)
    o_ref[...] = _bf(_rms(o, g_ref[...]))


def _mla_attention(qt, k, vt, g, batch, seq):
    nq = seq // TQ_MLA
    w = MLA_HEADS * HEAD_PAD
    return pl.pallas_call(
        _mla_kernel,
        grid=(batch, nq),
        in_specs=[pl.BlockSpec((1, w, TQ_MLA), lambda b, i: (b * nq + i, 0, 0)),
                  pl.BlockSpec((seq, w), lambda b, i: (b, 0)),
                  pl.BlockSpec((nq, A_W, TQ_MLA), lambda b, i: (b, 0, 0)),
                  pl.BlockSpec((1, A_W), lambda b, i: (0, 0))],
        out_specs=pl.BlockSpec((TQ_MLA, A_W), lambda b, i: (b * nq + i, 0)),
        out_shape=jax.ShapeDtypeStruct((batch * seq, A_W), jnp.bfloat16),
        scratch_shapes=[pltpu.VMEM((MLA_HEADS, 1, TQ_MLA), jnp.float32),
                        pltpu.VMEM((MLA_HEADS, 1, TQ_MLA), jnp.float32),
                        pltpu.VMEM((MLA_HEADS // 2, LANES, TQ_MLA), jnp.float32),
                        pltpu.VMEM((2, MLA_HEADS, TQ_MLA, TQ_MLA), jnp.float32)],
        compiler_params=_cparams(("parallel", "parallel")),
        name="mla_attention",
    )(qt, k, vt, g)


def _sb_kernel(q_ref, k_ref, v_ref, g_ref, o_ref, run_ref, acc_ref, zl_ref, sums_ref):
    i = pl.program_id(1)
    tq, tk = TQ_SB, TK_SB
    n_pairs = SB_HEADS // 2
    r = lax.broadcasted_iota(jnp.int32, (2 * tq, tk), 0)
    c = lax.broadcasted_iota(jnp.int32, (2 * tq, tk), 1)
    q_off = jnp.where(r >= tq, r - tq, r)
    top_low = (r < tq) == (c < SB_DIM)
    r2 = lax.broadcasted_iota(jnp.int32, (tk, 2 * tk), 0)
    c2 = lax.broadcasted_iota(jnp.int32, (tk, 2 * tk), 1)
    sum_mat = jnp.where((c2 >= tk) | (r2 > c2), 1.0, 0.0).astype(jnp.bfloat16)

    run_ref[...] = jnp.zeros(run_ref.shape, jnp.float32)
    acc_ref[...] = jnp.zeros(acc_ref.shape, jnp.float32)

    def step(kb_hi, masked):
        masks = []
        for d in range(2):
            kb = kb_hi - d
            start = pl.multiple_of(kb * tk, tk)
            mask = (c + (kb * tk - i * tq)) < q_off if masked else None
            masks.append(mask)
            for pr in range(n_pairs):
                qp = q_ref[:, pr * LANES:(pr + 1) * LANES]
                q2 = jnp.concatenate([qp, qp], axis=0)
                qm = jnp.where(top_low, q2, jnp.zeros_like(q2))
                kblk = k_ref[pl.ds(start, tk), pr * LANES:(pr + 1) * LANES]
                z = _dot_nt(qm, kblk)
                log_keep = -(jnp.maximum(z, 0.0) + jnp.log1p(jnp.exp(-jnp.abs(z))))
                if masked:
                    log_keep = jnp.where(mask, log_keep, 0.0)
                hi = _bf(log_keep)
                lo = _bf(log_keep - hi.astype(jnp.float32))
                sums_ref[2 * d + pr] = _dot(hi, sum_mat) + _dot(lo, sum_mat)
                zl_ref[2 * d + pr] = z + log_keep
        slowest = None
        for d in range(2):
            start = pl.multiple_of((kb_hi - d) * tk, tk)
            for pr in range(n_pairs):
                vblk = v_ref[pl.ds(start, tk), pr * LANES:(pr + 1) * LANES]
                run = run_ref[pr]
                a = jnp.exp(zl_ref[2 * d + pr] + run + sums_ref[2 * d + pr, :, 0:tk])
                if masked:
                    a = jnp.where(masks[d], a, 0.0)
                acc_ref[pr] = acc_ref[pr] + _dot(_bf(a), vblk)
                run = run + sums_ref[2 * d + pr, :, tk:2 * tk]
                run_ref[pr] = run
                if d == 1:
                    top = jnp.max(run)
                    slowest = top if slowest is None else jnp.maximum(slowest, top)
        return slowest

    underflow = -104.0
    assert tq == 2 * tk
    first = step(2 * i + 1, True)
    lax.while_loop(lambda cr: (cr[0] >= 0) & (cr[1] > underflow),
                   lambda cr: (cr[0] - 1, step(2 * cr[0] + 1, False)),
                   (i - 1, first))

    lane = lax.broadcasted_iota(jnp.int32, (tq, LANES), 1)
    outs = [jnp.where(lane < SB_DIM, acc_ref[pr, 0:tq, :], acc_ref[pr, tq:2 * tq, :]) for pr in range(n_pairs)]
    o = jnp.concatenate(outs, axis=1)
    o_ref[...] = _bf(_rms(o, g_ref[...]))


def _sb_attention(q, k, v, g, batch, seq):
    nq = seq // TQ_SB
    return pl.pallas_call(
        _sb_kernel,
        grid=(batch, nq),
        in_specs=[pl.BlockSpec((TQ_SB, SB_W), lambda b, i: (b * nq + i, 0)),
                  pl.BlockSpec((seq, SB_W), lambda b, i: (b, 0)),
                  pl.BlockSpec((seq, SB_W), lambda b, i: (b, 0)),
                  pl.BlockSpec((1, SB_W), lambda b, i: (0, 0))],
        out_specs=pl.BlockSpec((TQ_SB, SB_W), lambda b, i: (b * nq + i, 0)),
        out_shape=jax.ShapeDtypeStruct((batch * seq, SB_W), jnp.bfloat16),
        scratch_shapes=[pltpu.VMEM((SB_HEADS // 2, 2 * TQ_SB, TK_SB), jnp.float32),
                        pltpu.VMEM((SB_HEADS // 2, 2 * TQ_SB, LANES), jnp.float32),
                        pltpu.VMEM((SB_HEADS, 2 * TQ_SB, TK_SB), jnp.float32),
                        pltpu.VMEM((SB_HEADS, 2 * TQ_SB, 2 * TK_SB), jnp.float32)],
        compiler_params=_cparams(("parallel", "parallel")),
        name="sb_attention",
    )(q, k, v, g)


def _ca_kernel(q_ref, k_ref, v_ref, tab_ref, g_ref, o_ref, bias_ref):
    i = pl.program_id(1)
    t = TQ_CA
    start = pl.multiple_of(i * t, t)

    @pl.when((pl.program_id(0) == 0) & (i == 0))
    def _():
        r = lax.broadcasted_iota(jnp.int32, (t, CA_WIN), 0)
        c = lax.broadcasted_iota(jnp.int32, (t, CA_WIN), 1)
        lo = (r // CHUNK) * CHUNK
        band = (c >= lo) & (c < lo + CA_PAD + CHUNK)
        for h in range(CA_HEADS):
            tab = jnp.broadcast_to(tab_ref[h:h + 1, :], (t, REL_TAB))
            bias = pltpu.roll(tab, REL_TAB - (REL_CLIP - 1), 1, stride=1, stride_axis=0)[:, :CA_WIN]
            bias_ref[h] = jnp.where(band, bias, -jnp.inf)

    r2 = lax.broadcasted_iota(jnp.int32, (2 * t, LANES), 0)
    c2 = lax.broadcasted_iota(jnp.int32, (2 * t, LANES), 1)
    top_low = (r2 < t) == (c2 < CA_DIM)
    in_seq = lax.broadcasted_iota(jnp.int32, (2 * t, CA_WIN), 1) + i * t >= CA_PAD
    lane = lax.broadcasted_iota(jnp.int32, (t, LANES), 1)

    outs = []
    for pair in range(CA_HEADS // 2):
        qp = q_ref[:, pair * LANES:(pair + 1) * LANES]
        q2 = jnp.concatenate([qp, qp], axis=0)
        qm = jnp.where(top_low, q2, jnp.zeros_like(q2))
        kwin = k_ref[pl.ds(start, CA_WIN), pair * LANES:(pair + 1) * LANES]
        vwin = v_ref[pl.ds(start, CA_WIN), pair * LANES:(pair + 1) * LANES]
        bias = jnp.concatenate([bias_ref[2 * pair], bias_ref[2 * pair + 1]], axis=0)
        s = jnp.where(in_seq, _dot_nt(qm, kwin) + bias, -jnp.inf)
        m = jnp.max(s, axis=-1, keepdims=True)
        p = jnp.exp(s - m)
        l = jnp.sum(p, axis=-1, keepdims=True)
        o2 = _dot(_bf(p), vwin) / l
        outs.append(jnp.where(lane < CA_DIM, o2[0:t], o2[t:2 * t]))
    o = jnp.concatenate(outs, axis=1)
    o_ref[...] = _bf(_rms(o, g_ref[...]))


def _ca_attention(q, kpad, vpad, tab, g, batch, seq):
    nq = seq // TQ_CA
    return pl.pallas_call(
        _ca_kernel,
        grid=(batch, nq),
        in_specs=[pl.BlockSpec((TQ_CA, CA_W), lambda b, i: (b * nq + i, 0)),
                  pl.BlockSpec((seq + CA_PAD, CA_W), lambda b, i: (b, 0)),
                  pl.BlockSpec((seq + CA_PAD, CA_W), lambda b, i: (b, 0)),
                  pl.BlockSpec((CA_HEADS, REL_TAB), lambda b, i: (0, 0)),
                  pl.BlockSpec((1, CA_W), lambda b, i: (0, 0))],
        out_specs=pl.BlockSpec((TQ_CA, CA_W), lambda b, i: (b * nq + i, 0)),
        out_shape=jax.ShapeDtypeStruct((batch * seq, CA_W), jnp.bfloat16),
        scratch_shapes=[pltpu.VMEM((CA_HEADS, TQ_CA, CA_WIN), jnp.float32)],
        compiler_params=_cparams(("arbitrary", "arbitrary")),
        name="ca_attention",
    )(q, kpad, vpad, tab, g)


def _outproj_router_kernel(ma_ref, mb_ref, mc_ref, wo_ref, x_ref, g_ref, wrh_ref, wrl_ref, br_ref,
                           xn_ref, h_ref, lpos_ref, meta_ref, runs_ref, tot_ref, base_ref):
    tm = TM_TOK

    @pl.when(pl.program_id(0) == 0)
    def _():
        base_ref[...] = jnp.zeros_like(base_ref)

    attn = (_dot(ma_ref[...], wo_ref[0:A_W, :]) + _dot(mb_ref[...], wo_ref[A_W:A_W + SB_W, :])
            + _dot(mc_ref[...], wo_ref[A_W + SB_W:, :]))
    xn = x_ref[...] + attn
    xn_ref[...] = xn
    h = _rms(xn, g_ref[...])

    h_hi = _bf(h)
    h_ref[...] = h_hi
    h_lo = _bf(h - h_hi.astype(jnp.float32))
    logits = (_dot(h_hi, wrh_ref[...]) + _dot(h_hi, wrl_ref[...]) + _dot(h_lo, wrh_ref[...])
              + br_ref[...])
    lane = lax.broadcasted_iota(jnp.int32, (tm, LANES), 1)
    lane_f = lane.astype(jnp.float32)

    work = logits
    vals, idxs, hots = [], [], []
    for _ in range(TOP_K):
        mx = jnp.max(work, axis=-1, keepdims=True)
        ix = jnp.min(jnp.where(work == mx, lane_f, float(LANES)), axis=-1, keepdims=True)
        hot = lane_f == ix
        work = jnp.where(hot, -jnp.inf, work)
        vals.append(mx)
        idxs.append(ix)
        hots.append(hot)
    exps = [jnp.exp(v - vals[0]) for v in vals]
    denom = exps[0] + exps[1] + exps[2] + exps[3]
    gates = [e / denom for e in exps]

    sel = jnp.zeros((tm, LANES), jnp.float32)
    for hot in hots:
        sel = sel + jnp.where(hot, 1.0, 0.0)
    r = lax.broadcasted_iota(jnp.int32, (tm, tm), 0)
    c = lax.broadcasted_iota(jnp.int32, (tm, tm), 1)
    before = jnp.where(c < r, 1.0, 0.0).astype(jnp.bfloat16)
    rank_in_tile = _dot(before, _bf(sel))

    cnt = jnp.sum(sel, axis=0, keepdims=True)
    cnt_al = jnp.ceil(cnt * (1.0 / RUN_ALIGN)) * RUN_ALIGN
    rl = lax.broadcasted_iota(jnp.int32, (LANES, LANES), 0)
    cl = lax.broadcasted_iota(jnp.int32, (LANES, LANES), 1)
    earlier = jnp.where(rl < cl, 1.0, 0.0).astype(jnp.bfloat16)
    loff = _dot(_bf(jnp.broadcast_to(cnt_al, (8, LANES))), earlier)[0:1, :]
    base = base_ref[...]
    base_ref[...] = base + cnt_al
    tot_ref[...] = base_ref[...].astype(jnp.int32)
    sub = lax.broadcasted_iota(jnp.int32, (8, LANES), 0)
    runs = jnp.where(sub == 0, loff, jnp.where(sub == 1, base, jnp.where(sub == 2, cnt_al, 0.0)))
    runs_ref[...] = runs.astype(jnp.int32)

    lpos_dense = loff + rank_in_tile
    meta = jnp.zeros((tm, LANES), jnp.float32)
    for kk in range(TOP_K):
        lpos = jnp.sum(jnp.where(hots[kk], lpos_dense, 0.0), axis=-1, keepdims=True)
        meta = meta + jnp.where(lane == kk, gates[kk], 0.0) + jnp.where(lane == TOP_K + kk, lpos, 0.0)
    meta_ref[...] = meta
    lpos_ref[...] = jnp.transpose(meta)[TOP_K:TOP_K + 8, :].astype(jnp.int32)


def _outproj_router(ma, mb, mc, wo, x, g, wrh, wrl, br):
    n = x.shape[0]
    tm = TM_TOK
    row = lambda width: pl.BlockSpec((tm, width), lambda i: (i, 0))
    full = lambda a: pl.BlockSpec(a.shape, lambda i: (0,) * a.ndim)
    return pl.pallas_call(
        _outproj_router_kernel,
        grid=(n // tm,),
        in_specs=[row(A_W), row(SB_W), row(CA_W), full(wo), row(D_MODEL), full(g),
                  full(wrh), full(wrl), full(br)],
        out_specs=[row(D_MODEL), row(D_MODEL), pl.BlockSpec((8, tm), lambda i: (0, i)),
                   row(LANES), pl.BlockSpec((8, LANES), lambda i: (i, 0)),
                   pl.BlockSpec((1, LANES), lambda i: (0, 0))],
        out_shape=[jax.ShapeDtypeStruct((n, D_MODEL), jnp.float32),
                   jax.ShapeDtypeStruct((n, D_MODEL), jnp.bfloat16),
                   jax.ShapeDtypeStruct((8, n), jnp.int32),
                   jax.ShapeDtypeStruct((n, LANES), jnp.float32),
                   jax.ShapeDtypeStruct((8 * (n // tm), LANES), jnp.int32),
                   jax.ShapeDtypeStruct((1, LANES), jnp.int32)],
        scratch_shapes=[pltpu.VMEM((1, LANES), jnp.float32)],
        compiler_params=_cparams(("arbitrary",)),
        name="outproj_router",
    )(ma, mb, mc, wo, x, g, wrh, wrl, br)


_RUN_CHUNKS = tuple(TM_TOK >> s for s in range(TM_TOK.bit_length()) if (TM_TOK >> s) >= RUN_ALIGN)


def _for_each_chunk(length, fn):
    for size in _RUN_CHUNKS:
        off = length & (~(2 * size - 1))

        @pl.when((length & size) != 0)
        def _(off=off, size=size):
            fn(off, size)


_WAIT_CHUNKS = tuple(1 << b for b in range((LOCAL_ROWS).bit_length() - 1, RUN_ALIGN.bit_length() - 2, -1))


def _wait_tile_runs(runs_ref, make_copy):
    total = runs_ref[0, N_EXPERTS - 1] + runs_ref[2, N_EXPERTS - 1]
    for size in _WAIT_CHUNKS:
        @pl.when((total & size) != 0)
        def _(size=size):
            make_copy(size).wait()


def _tile_runs(runs_ref, pstart_ref, fn):
    def body(e, _):
        local, glob, length = runs_ref[0, e], pstart_ref[e] + runs_ref[1, e], runs_ref[2, e]
        _for_each_chunk(length, lambda off, size: fn(pl.multiple_of(local + off, RUN_ALIGN),
                                                     pl.multiple_of(glob + off, RUN_ALIGN), size))
        return 0

    lax.fori_loop(0, N_EXPERTS, body, 0)


def _dispatch_kernel(pstart_ref, tot_ref, pad_ref, runs_ref, prev_runs_ref, lpos_ref, h_ref, xs_ref,
                     loc, zbuf, sem, fill_sem):
    i = pl.program_id(0)
    slot = i % 2

    def fill(start_or_wait):
        def body(e, _):
            first = pstart_ref[e] + tot_ref[e]
            _for_each_chunk(pad_ref[e] - tot_ref[e], lambda off, size: start_or_wait(pltpu.make_async_copy(
                zbuf.at[pl.ds(0, size)], xs_ref.at[pl.ds(pl.multiple_of(first + off, RUN_ALIGN), size)], fill_sem)))
            return 0

        lax.fori_loop(0, N_EXPERTS, body, 0)

        used = pstart_ref[N_EXPERTS - 1] + pad_ref[N_EXPERTS - 1]

        def tail(b, _):
            start_or_wait(pltpu.make_async_copy(
                zbuf, xs_ref.at[pl.ds(pl.multiple_of(b * TM_TOK, TM_TOK), TM_TOK)], fill_sem))
            return 0

        lax.fori_loop(used // TM_TOK, xs_ref.shape[0] // TM_TOK, tail, 0)

    @pl.when(i == 0)
    def _():
        zbuf[...] = jnp.zeros_like(zbuf)
        fill(lambda c: c.start())
        fill(lambda c: c.wait())

    r = lax.broadcasted_iota(jnp.int32, (LOCAL_ROWS, TM_TOK), 0)
    hit = r == lpos_ref[0:1, :]
    for kk in range(1, TOP_K):
        hit = hit | (r == lpos_ref[kk:kk + 1, :])
    loc[slot] = _pack_rows(_dot(jnp.where(hit, 1.0, 0.0).astype(jnp.bfloat16), h_ref[...]))

    def copy(buf, local, glob, size):
        return pltpu.make_async_copy(loc.at[buf, pl.ds(local, size)], xs_ref.at[pl.ds(glob, size)], sem.at[buf])

    _tile_runs(runs_ref, pstart_ref, lambda l, g, s: copy(slot, l, g, s).start())

    @pl.when(i > 0)
    def _():
        _wait_tile_runs(prev_runs_ref, lambda size: copy(1 - slot, 0, 0, size))

    @pl.when(i == pl.num_programs(0) - 1)
    def _():
        _wait_tile_runs(runs_ref, lambda size: copy(slot, 0, 0, size))


def _dispatch(pstart, totals, padded, runs, lpos, h, n_rows):
    n = h.shape[0]
    tm = TM_TOK
    grid_spec = pltpu.PrefetchScalarGridSpec(
        num_scalar_prefetch=3,
        grid=(n // tm,),
        in_specs=[pl.BlockSpec((8, LANES), lambda i, *_: (i, 0), memory_space=pltpu.SMEM),
                  pl.BlockSpec((8, LANES), lambda i, *_: (jnp.maximum(i - 1, 0), 0), memory_space=pltpu.SMEM),
                  pl.BlockSpec((8, tm), lambda i, *_: (0, i)),
                  pl.BlockSpec((tm, D_MODEL), lambda i, *_: (i, 0))],
        out_specs=pl.BlockSpec(memory_space=pl.ANY),
        scratch_shapes=[pltpu.VMEM((2, LOCAL_ROWS, HALF), jnp.uint32),
                        pltpu.VMEM((TM_TOK, HALF), jnp.uint32),
                        pltpu.SemaphoreType.DMA((2,)),
                        pltpu.SemaphoreType.DMA],
    )
    return pl.pallas_call(
        _dispatch_kernel,
        grid_spec=grid_spec,
        out_shape=jax.ShapeDtypeStruct((n_rows, HALF), jnp.uint32),
        compiler_params=_cparams(("arbitrary",)),
        name="dispatch",
    )(pstart, totals, padded, runs, runs, lpos, h)


def _expert_kernel(be_ref, bfirst_ref, bvalid_ref, xs_ref, wgu_ref, bgu_ref, wdn_ref, bdn_ref,
                   ys_ref, wgu_bf, wdn_bf):
    b = pl.program_id(0)

    @pl.when(bfirst_ref[b] == 1)
    def _():
        wgu_bf[...] = _bf(wgu_ref[0])
        wdn_bf[...] = _bf(wdn_ref[0])

    @pl.when(bvalid_ref[b] == 1)
    def _():
        x_lo, x_hi = _unpack_rows(xs_ref[...])
        gu = _dot(x_lo, wgu_bf[0:HALF, :]) + _dot(x_hi, wgu_bf[HALF:, :]) + bgu_ref[0]
        gte = jnp.minimum(gu[:, :D_FF], SWIGLU_LIMIT)
        up = jnp.clip(gu[:, D_FF:], -SWIGLU_LIMIT, SWIGLU_LIMIT)
        act = (up + 1.0) * (gte * (1.0 / (1.0 + jnp.exp(-SWIGLU_ALPHA * gte))))
        y = _dot(_bf(act), wdn_bf[...]) + bdn_ref[0]
        ys_ref[...] = _pack_rows(_bf(y).astype(jnp.float32))

    @pl.when(bvalid_ref[b] == 0)
    def _():
        ys_ref[...] = jnp.zeros_like(ys_ref)


def _expert_ffn(blk_e, blk_first, blk_valid, xs, wgu, bgu, wdn, bdn):
    n_rows = xs.shape[0]
    tm = TM_EXP
    grid_spec = pltpu.PrefetchScalarGridSpec(
        num_scalar_prefetch=3,
        grid=(n_rows // tm,),
        in_specs=[pl.BlockSpec((tm, HALF), lambda b, e, f, v: (b, 0)),
                  pl.BlockSpec((1, D_MODEL, 2 * D_FF), lambda b, e, f, v: (e[b], 0, 0)),
                  pl.BlockSpec((1, 1, 2 * D_FF), lambda b, e, f, v: (e[b], 0, 0)),
                  pl.BlockSpec((1, D_FF, D_MODEL), lambda b, e, f, v: (e[b], 0, 0)),
                  pl.BlockSpec((1, 1, D_MODEL), lambda b, e, f, v: (e[b], 0, 0))],
        out_specs=pl.BlockSpec((tm, HALF), lambda b, e, f, v: (b, 0)),
        scratch_shapes=[pltpu.VMEM((D_MODEL, 2 * D_FF), jnp.bfloat16),
                        pltpu.VMEM((D_FF, D_MODEL), jnp.bfloat16)],
    )
    return pl.pallas_call(
        _expert_kernel,
        grid_spec=grid_spec,
        out_shape=jax.ShapeDtypeStruct((n_rows, HALF), jnp.uint32),
        compiler_params=_cparams(("arbitrary",)),
        name="expert_ffn",
    )(blk_e, blk_first, blk_valid, xs, wgu, bgu, wdn, bdn)


def _combine_kernel(final, pstart_ref, runs_ref, next_runs_ref, meta_ref, x_ref, ys_ref, gfin_ref, o_ref, loc, sem):
    i = pl.program_id(0)
    slot = i % 2

    def copy(buf, local, glob, size):
        return pltpu.make_async_copy(ys_ref.at[pl.ds(glob, size)], loc.at[buf, pl.ds(local, size)], sem.at[buf])

    @pl.when(i == 0)
    def _():
        loc[...] = jnp.zeros_like(loc)
        _tile_runs(runs_ref, pstart_ref, lambda l, g, s: copy(slot, l, g, s).start())

    @pl.when(i + 1 < pl.num_programs(0))
    def _():
        _tile_runs(next_runs_ref, pstart_ref, lambda l, g, s: copy(1 - slot, l, g, s).start())

    _wait_tile_runs(runs_ref, lambda size: copy(slot, 0, 0, size))

    meta = meta_ref[...]
    col = lax.broadcasted_iota(jnp.int32, (TM_TOK, LOCAL_ROWS), 1).astype(jnp.float32)
    wts = jnp.zeros((TM_TOK, LOCAL_ROWS), jnp.float32)
    for kk in range(TOP_K):
        wts = wts + jnp.where(col == meta[:, TOP_K + kk:TOP_K + kk + 1], meta[:, kk:kk + 1], 0.0)
    w_hi = _bf(wts)
    w_lo = _bf(wts - w_hi.astype(jnp.float32))
    y_lo, y_hi = _unpack_rows(loc[slot])
    out = x_ref[...] + jnp.concatenate([_dot(w_hi, y_lo) + _dot(w_lo, y_lo),
                                        _dot(w_hi, y_hi) + _dot(w_lo, y_hi)], axis=1)
    if final:
        out = _rms(out, gfin_ref[...])
    o_ref[...] = out


def _combine(pstart, runs, meta, x, ys, gfin, final):
    n = x.shape[0]
    tm = TM_TOK
    n_tiles = n // tm
    grid_spec = pltpu.PrefetchScalarGridSpec(
        num_scalar_prefetch=1,
        grid=(n_tiles,),
        in_specs=[pl.BlockSpec((8, LANES), lambda i, ps: (i, 0), memory_space=pltpu.SMEM),
                  pl.BlockSpec((8, LANES), lambda i, ps: (jnp.minimum(i + 1, n_tiles - 1), 0), memory_space=pltpu.SMEM),
                  pl.BlockSpec((tm, LANES), lambda i, ps: (i, 0)),
                  pl.BlockSpec((tm, D_MODEL), lambda i, ps: (i, 0)),
                  pl.BlockSpec(memory_space=pl.ANY),
                  pl.BlockSpec((1, D_MODEL), lambda i, ps: (0, 0))],
        out_specs=pl.BlockSpec((tm, D_MODEL), lambda i, ps: (i, 0)),
        scratch_shapes=[pltpu.VMEM((2, LOCAL_ROWS, HALF), jnp.uint32), pltpu.SemaphoreType.DMA((2,))],
    )
    return pl.pallas_call(
        functools.partial(_combine_kernel, final),
        grid_spec=grid_spec,
        out_shape=jax.ShapeDtypeStruct((n, D_MODEL), jnp.float32),
        compiler_params=_cparams(("arbitrary",)),
        name="combine_final" if final else "combine",
    )(pstart, runs, runs, meta, x, ys, gfin)


def _pad_heads(w, parts):
    rows = w.shape[0]
    per = w.shape[1] // MLA_HEADS
    w3 = w.reshape(rows, MLA_HEADS, per)
    cols = [jnp.zeros((rows, MLA_HEADS, b - a), w.dtype) if sign == 0 else sign * w3[:, :, a:b]
            for a, b, sign in parts]
    used = sum(b - a for a, b, _ in parts)
    cols.append(jnp.zeros((rows, MLA_HEADS, HEAD_PAD - used), w.dtype))
    return jnp.concatenate(cols, axis=2).reshape(rows, MLA_HEADS * HEAD_PAD)


def _layer_weights(w_in, w_uq, w_ukv, w_router, b_router, rel_bias):
    half = MLA_ROPE // 2
    cq, ckv, kpe, sb, ca = (w_in[:, 0:256], w_in[:, 256:384], w_in[:, 384:416],
                            w_in[:, 416:1184], w_in[:, 1184:1952])
    z = lambda width: jnp.zeros((D_MODEL, width), w_in.dtype)
    kpe_pad = jnp.concatenate([z(MLA_NOPE), kpe, z(HEAD_PAD - MLA_NOPE - MLA_ROPE)], axis=1)
    kpe_rot = jnp.concatenate([z(MLA_NOPE), -kpe[:, half:], kpe[:, :half],
                               z(HEAD_PAD - MLA_NOPE - MLA_ROPE)], axis=1)
    win2 = jnp.concatenate([cq, ckv, kpe_pad, kpe_rot, sb, ca], axis=1).astype(jnp.bfloat16)

    d = MLA_NOPE + MLA_ROPE
    wq_full = _pad_heads(w_uq, [(0, d, 1)])
    wq_rot = _pad_heads(w_uq, [(0, MLA_NOPE, 0), (MLA_NOPE + half, d, -1), (MLA_NOPE, MLA_NOPE + half, 1)])
    wq2 = jnp.concatenate([wq_full, wq_rot], axis=1).astype(jnp.bfloat16)

    wk = _pad_heads(w_ukv, [(0, MLA_NOPE, 1)])
    wv = w_ukv.reshape(MLA_KV_RANK, MLA_HEADS, MLA_NOPE + MLA_V)[:, :, MLA_NOPE:].reshape(MLA_KV_RANK, A_W)
    wkv2 = jnp.concatenate([wk, wv], axis=1).astype(jnp.bfloat16)

    wr = jnp.pad(w_router, ((0, 0), (0, LANES - N_EXPERTS)))
    wr_hi = wr.astype(jnp.bfloat16)
    wr_lo = (wr - wr_hi.astype(jnp.float32)).astype(jnp.bfloat16)
    br = jnp.pad(b_router, (0, LANES - N_EXPERTS), constant_values=NEG).reshape(1, LANES)

    ext = jnp.concatenate([rel_bias, jnp.broadcast_to(rel_bias[:, -1:], (CA_HEADS, REL_TAB - 2 * REL_CLIP - 1))], axis=1)
    tab = ext[:, ::-1]
    return win2, wq2, wkv2, wr_hi, wr_lo, br, tab


def kernel(x, positions, attn_norm, w_in, q_norm, w_uq, kv_norm, w_ukv, rel_bias, mix_norm,
           w_o, ffn_norm, w_router, b_router, w_gate_up, b_gate_up, w_down, b_down, final_norm):
    batch, seq, _ = x.shape
    n = batch * seq
    depth = w_in.shape[0]
    xf = x.reshape(n, D_MODEL)

    inv = ROPE_THETA ** (-jnp.arange(0, MLA_ROPE, 2, dtype=jnp.float32) / MLA_ROPE)
    inv_row = jnp.concatenate([jnp.zeros((MLA_NOPE,), jnp.float32), inv, inv,
                               jnp.zeros((HEAD_PAD - MLA_NOPE - MLA_ROPE,), jnp.float32)]).reshape(1, LANES)
    cos_t, sin_t = _rope_tables(positions.reshape(n, 1), inv_row)

    n_rows = n * TOP_K + (n // TM_TOK) * N_EXPERTS * (RUN_ALIGN - 1) + N_EXPERTS * TM_EXP
    n_rows = -(-n_rows // TM_EXP) * TM_EXP
    n_blk = n_rows // TM_EXP
    row2 = lambda v: v.reshape(1, -1)
    wgu_all = w_gate_up.reshape(depth * N_EXPERTS, D_MODEL, 2 * D_FF)
    bgu_all = b_gate_up.reshape(depth * N_EXPERTS, 1, 2 * D_FF)
    wdn_all = w_down.reshape(depth * N_EXPERTS, D_FF, D_MODEL)
    bdn_all = b_down.reshape(depth * N_EXPERTS, 1, D_MODEL)

    for l in range(depth):
        win2, wq2, wkv2, wr_hi, wr_lo, br, tab = _layer_weights(
            w_in[l], w_uq[l], w_ukv[l], w_router[l], b_router[l], rel_bias[l])
        qt, k, vt, sbq, sbk, sbv, caq, cak, cav = _inproj(
            xf, row2(attn_norm[l]), win2, cos_t, sin_t, row2(q_norm[l]), wq2, row2(kv_norm[l]), wkv2)

        g = mix_norm[l]
        ma = _mla_attention(qt, k, vt, row2(g[:A_W]), batch, seq)
        mb = _sb_attention(sbq, sbk, sbv, row2(g[A_W:A_W + SB_W]), batch, seq)
        pad = lambda a: jnp.pad(a.reshape(batch, seq, CA_W), ((0, 0), (CA_PAD, 0), (0, 0))).reshape(-1, CA_W)
        mc = _ca_attention(caq, pad(cak), pad(cav), tab, row2(g[A_W + SB_W:]), batch, seq)

        xn, h, lpos, meta, runs, totals = _outproj_router(
            ma, mb, mc, w_o[l].astype(jnp.bfloat16), xf, row2(ffn_norm[l]), wr_hi, wr_lo, br)

        sizes = totals[0, :N_EXPERTS]
        padded = (sizes + TM_EXP - 1) // TM_EXP * TM_EXP
        p_ends = jnp.cumsum(padded)
        p_starts = (p_ends - padded).astype(jnp.int32)
        blk_start = jnp.arange(n_blk, dtype=jnp.int32) * TM_EXP
        blk_valid = (blk_start < p_ends[-1]).astype(jnp.int32)
        last_row = jnp.minimum(blk_start, p_ends[-1] - 1)
        blk_e = jnp.minimum(jnp.sum((last_row[:, None] >= p_ends[None, :]).astype(jnp.int32), axis=1),
                            N_EXPERTS - 1)
        blk_first = jnp.concatenate([jnp.ones((1,), jnp.int32), (blk_e[1:] != blk_e[:-1]).astype(jnp.int32)])

        xs = _dispatch(p_starts, sizes, padded.astype(jnp.int32), runs, lpos, h, n_rows)
        ys = _expert_ffn(blk_e + l * N_EXPERTS, blk_first, blk_valid, xs, wgu_all, bgu_all, wdn_all, bdn_all)
        xf = _combine(p_starts, runs, meta, xn, ys, row2(final_norm), final=(l == depth - 1))

    return xf.reshape(batch, seq, D_MODEL)
```

```python
import functools
import math

import jax
import jax.numpy as jnp
from jax import lax
from jax.experimental import pallas as pl
from jax.experimental.pallas import tpu as pltpu

D_MODEL = 1024
RMS_EPS = 1e-6
MLA_NOPE, MLA_ROPE, MLA_V, MLA_HEADS = 64, 32, 64, 8
MLA_Q_RANK, MLA_KV_RANK = 256, 128
ROPE_THETA = 10000.0
SB_DIM, SB_HEADS = 64, 4
CA_DIM, CA_HEADS = 64, 4
CHUNK = 64
CA_LEFT_CHUNKS = 8
REL_CLIP = 256
N_EXPERTS, TOP_K = 32, 4
D_FF = 1024
SWIGLU_LIMIT, SWIGLU_ALPHA = 7.0, 1.702

LANES = 128
HEAD_PAD = 128
A_W = MLA_HEADS * MLA_V
SB_W = SB_HEADS * SB_DIM
CA_W = CA_HEADS * CA_DIM
CA_PAD = CA_LEFT_CHUNKS * CHUNK
CA_WIN = CA_PAD + 2 * CHUNK
REL_TAB = 1024
HALF = D_MODEL // 2

TM_TOK = 256
TQ_MLA = 256
TQ_SB = 256
TK_SB = 128
TQ_CA = 2 * CHUNK
ROUTER_TILES = 2
TM_EXP = 512
RUN_ALIGN = 8
LOCAL_ROWS = 1280
assert LOCAL_ROWS >= TM_TOK * TOP_K + N_EXPERTS * (RUN_ALIGN - 1) and LOCAL_ROWS % LANES == 0
assert TM_EXP % TM_TOK == 0 and TQ_MLA == TM_TOK

_C_CQ, _C_CKV, _C_KPE, _C_KROT = 0, 256, 384, 512
_C_SB, _C_CA, D_IN2 = 640, 1408, 2176

NEG = -1e30
LOG2E = math.log2(math.e)
VMEM_LIMIT = 56 * 1024 * 1024


def _rms(v, g):
    return v * lax.rsqrt(jnp.mean(v * v, axis=-1, keepdims=True) + RMS_EPS) * g


def _dot(a, b):
    return jnp.dot(a, b, preferred_element_type=jnp.float32)


def _dot_nt(a, b):
    return lax.dot_general(a, b, (((1,), (1,)), ((), ())), preferred_element_type=jnp.float32)


def _bf(v):
    return v.astype(jnp.bfloat16)


def _pack_rows(v):
    bits = lax.bitcast_convert_type(v, jnp.uint32)
    return (bits[:, :HALF] >> 16) | (bits[:, HALF:] & jnp.uint32(0xFFFF0000))


def _unpack_rows(u):
    lo = lax.bitcast_convert_type(u << 16, jnp.float32)
    hi = lax.bitcast_convert_type(u & jnp.uint32(0xFFFF0000), jnp.float32)
    return _bf(lo), _bf(hi)


def _cparams(sem):
    return pltpu.CompilerParams(dimension_semantics=sem, vmem_limit_bytes=VMEM_LIMIT)


def _rope_kernel(pos_ref, inv_ref, cos_ref, sin_ref):
    ang = pos_ref[...].astype(jnp.float32) * inv_ref[...]
    cos_ref[...] = jnp.cos(ang)
    sin_ref[...] = jnp.sin(ang)


def _rope_tables(pos_col, inv_row):
    n = pos_col.shape[0]
    return pl.pallas_call(
        _rope_kernel,
        grid=(n // TM_TOK,),
        in_specs=[pl.BlockSpec((TM_TOK, 1), lambda i: (i, 0)),
                  pl.BlockSpec((1, LANES), lambda i: (0, 0))],
        out_specs=[pl.BlockSpec((TM_TOK, LANES), lambda i: (i, 0))] * 2,
        out_shape=[jax.ShapeDtypeStruct((n, LANES), jnp.float32)] * 2,
        compiler_params=_cparams(("parallel",)),
        name="rope_tables",
    )(pos_col, inv_row)


def _inproj_kernel(x_ref, g_ref, win_ref, cos_ref, sin_ref, qn_ref, wq_ref, kvn_ref, wkv_ref,
                   q_ref, k_ref, v_ref, sbq_ref, sbk_ref, sbv_ref, caq_ref, cak_ref, cav_ref):
    h = _rms(x_ref[...], g_ref[...])
    proj = _dot(_bf(h), win_ref[...])
    cos128, sin128 = cos_ref[...], sin_ref[...]
    cos_h = jnp.concatenate([cos128] * MLA_HEADS, axis=1)
    sin_h = jnp.concatenate([sin128] * MLA_HEADS, axis=1)
    w = MLA_HEADS * HEAD_PAD

    cqn = _rms(proj[:, _C_CQ:_C_CKV], qn_ref[...])
    q2 = _dot(_bf(cqn), wq_ref[...])
    q = (q2[:, :w] * cos_h + q2[:, w:] * sin_h) * (LOG2E / math.sqrt(MLA_NOPE + MLA_ROPE))
    q_ref[0] = _bf(jnp.transpose(q))

    ckvn = _rms(proj[:, _C_CKV:_C_KPE], kvn_ref[...])
    kv2 = _dot(_bf(ckvn), wkv_ref[...])
    kpe = proj[:, _C_KPE:_C_KROT] * cos128 + proj[:, _C_KROT:_C_SB] * sin128
    k_ref[...] = _bf(kv2[:, :w] + jnp.concatenate([kpe] * MLA_HEADS, axis=1))
    v_ref[0] = _bf(jnp.transpose(kv2[:, w:]))

    sb_scale = 1.0 / math.sqrt(SB_DIM)
    sbq_ref[...] = _bf(proj[:, _C_SB:_C_SB + SB_W] * sb_scale)
    sbk_ref[...] = _bf(proj[:, _C_SB + SB_W:_C_SB + 2 * SB_W])
    sbv_ref[...] = _bf(proj[:, _C_SB + 2 * SB_W:_C_CA])
    ca_scale = LOG2E / math.sqrt(CA_DIM)
    caq_ref[...] = _bf(proj[:, _C_CA:_C_CA + CA_W] * ca_scale)
    cak_ref[...] = _bf(proj[:, _C_CA + CA_W:_C_CA + 2 * CA_W])
    cav_ref[...] = _bf(proj[:, _C_CA + 2 * CA_W:D_IN2])


def _inproj(x, g, win2, cos_t, sin_t, qn, wq2, kvn, wkv2):
    n = x.shape[0]
    tm = TM_TOK
    row = lambda width: pl.BlockSpec((tm, width), lambda i: (i, 0))
    full = lambda a: pl.BlockSpec(a.shape, lambda i: (0,) * a.ndim)
    widths = [MLA_HEADS * HEAD_PAD] + [SB_W] * 3 + [CA_W] * 3
    tile_t = lambda rows: pl.BlockSpec((1, rows, tm), lambda i: (i, 0, 0))
    shape_t = lambda rows: jax.ShapeDtypeStruct((n // tm, rows, tm), jnp.bfloat16)
    return pl.pallas_call(
        _inproj_kernel,
        grid=(n // tm,),
        in_specs=[row(D_MODEL), full(g), full(win2), row(LANES), row(LANES),
                  full(qn), full(wq2), full(kvn), full(wkv2)],
        out_specs=[tile_t(MLA_HEADS * HEAD_PAD), row(widths[0]), tile_t(A_W)] + [row(wd) for wd in widths[1:]],
        out_shape=[shape_t(MLA_HEADS * HEAD_PAD), jax.ShapeDtypeStruct((n, widths[0]), jnp.bfloat16), shape_t(A_W)]
                  + [jax.ShapeDtypeStruct((n, wd), jnp.bfloat16) for wd in widths[1:]],
        compiler_params=_cparams(("parallel",)),
        name="inproj",
    )(x, g, win2, cos_t, sin_t, qn, wq2, kvn, wkv2)


def _mla_kernel(qt_ref, k_ref, vt_ref, g_ref, o_ref, m_ref, l_ref, acc_ref, s_ref):
    i = pl.program_id(1)
    tq = TQ_MLA
    n_pairs = MLA_HEADS // 2
    key_chunk = lax.broadcasted_iota(jnp.int32, (tq, tq), 0) // CHUNK
    qry_chunk = lax.broadcasted_iota(jnp.int32, (tq, tq), 1) // CHUNK
    diag_ok = key_chunk <= qry_chunk
    top = lax.broadcasted_iota(jnp.int32, (LANES, tq), 0) < MLA_V

    m_ref[...] = jnp.full(m_ref.shape, -jnp.inf, jnp.float32)
    l_ref[...] = jnp.zeros(l_ref.shape, jnp.float32)
    acc_ref[...] = jnp.zeros(acc_ref.shape, jnp.float32)

    def scores(kb, slot):
        start = pl.multiple_of(kb * tq, tq)
        for h in range(MLA_HEADS):
            qt = qt_ref[0, h * HEAD_PAD:(h + 1) * HEAD_PAD, :]
            kblk = k_ref[pl.ds(start, tq), h * HEAD_PAD:(h + 1) * HEAD_PAD]
            s_ref[slot, h] = _dot(kblk, qt)

    def absorb(kb, slot, mask):
        for pr in range(n_pairs):
            vt = vt_ref[kb, pr * LANES:(pr + 1) * LANES, :]
            zero = jnp.zeros_like(vt)
            vt_bd = jnp.concatenate([jnp.where(top, vt, zero), jnp.where(top, zero, vt)], axis=1)
            pts, alphas = [], []
            for h in (2 * pr, 2 * pr + 1):
                s = s_ref[slot, h]
                if mask is not None:
                    s = jnp.where(mask, s, -jnp.inf)
                m_old = m_ref[h]
                m_new = jnp.maximum(m_old, jnp.max(s, axis=0, keepdims=True))
                p = jnp.exp2(s - m_new)
                alpha = jnp.exp2(m_old - m_new)
                l_ref[h] = alpha * l_ref[h] + jnp.sum(p, axis=0, keepdims=True)
                m_ref[h] = m_new
                pts.append(_bf(p))
                alphas.append(alpha)
            alpha_pair = jnp.where(top, alphas[0], alphas[1])
            acc_ref[pr] = alpha_pair * acc_ref[pr] + _dot(vt_bd, jnp.concatenate(pts, axis=0))

    def body(j, carry):
        scores(2 * j + 1, 1)
        absorb(2 * j, 0, None)
        scores(2 * j + 2, 0)
        absorb(2 * j + 1, 1, None)
        return carry

    scores(0, 0)
    lax.fori_loop(0, i // 2, body, 0)

    @pl.when(i % 2 == 0)
    def _():
        absorb(i, 0, diag_ok)

    @pl.when(i % 2 == 1)
    def _():
        scores(i, 1)
        absorb(i - 1, 0, None)
        absorb(i, 1, diag_ok)

    outs = [jnp.transpose(acc_ref[pr] / jnp.where(top, l_ref[2 * pr], l_ref[2 * pr + 1])) for pr in range(n_pairs)]
    o = jnp.concatenate(outs, axis=1)
    o_ref[...] = _bf(_rms(o, g_ref[...]))


def _mla_attention(qt, k, vt, g, batch, seq):
    nq = seq // TQ_MLA
    w = MLA_HEADS * HEAD_PAD
    return pl.pallas_call(
        _mla_kernel,
        grid=(batch, nq),
        in_specs=[pl.BlockSpec((1, w, TQ_MLA), lambda b, i: (b * nq + i, 0, 0)),
                  pl.BlockSpec((seq, w), lambda b, i: (b, 0)),
                  pl.BlockSpec((nq, A_W, TQ_MLA), lambda b, i: (b, 0, 0)),
                  pl.BlockSpec((1, A_W), lambda b, i: (0, 0))],
        out_specs=pl.BlockSpec((TQ_MLA, A_W), lambda b, i: (b * nq + i, 0)),
        out_shape=jax.ShapeDtypeStruct((batch * seq, A_W), jnp.bfloat16),
        scratch_shapes=[pltpu.VMEM((MLA_HEADS, 1, TQ_MLA), jnp.float32),
                        pltpu.VMEM((MLA_HEADS, 1, TQ_MLA), jnp.float32),
                        pltpu.VMEM((MLA_HEADS // 2, LANES, TQ_MLA), jnp.float32),
                        pltpu.VMEM((2, MLA_HEADS, TQ_MLA, TQ_MLA), jnp.float32)],
        compiler_params=_cparams(("parallel", "parallel")),
        name="mla_attention",
    )(qt, k, vt, g)


def _sb_kernel(q_ref, k_ref, v_ref, g_ref, o_ref, run_ref, acc_ref, zl_ref, sums_ref):
    i = pl.program_id(1)
    tq, tk = TQ_SB, TK_SB
    n_pairs = SB_HEADS // 2
    r = lax.broadcasted_iota(jnp.int32, (2 * tq, tk), 0)
    c = lax.broadcasted_iota(jnp.int32, (2 * tq, tk), 1)
    q_off = jnp.where(r >= tq, r - tq, r)
    top_low = (r < tq) == (c < SB_DIM)
    r2 = lax.broadcasted_iota(jnp.int32, (tk, 2 * tk), 0)
    c2 = lax.broadcasted_iota(jnp.int32, (tk, 2 * tk), 1)
    sum_mat = jnp.where((c2 >= tk) | (r2 > c2), 1.0, 0.0).astype(jnp.bfloat16)

    run_ref[...] = jnp.zeros(run_ref.shape, jnp.float32)
    acc_ref[...] = jnp.zeros(acc_ref.shape, jnp.float32)

    def step(kb_hi, masked):
        masks = []
        for d in range(2):
            kb = kb_hi - d
            start = pl.multiple_of(kb * tk, tk)
            mask = (c + (kb * tk - i * tq)) < q_off if masked else None
            masks.append(mask)
            for pr in range(n_pairs):
                qp = q_ref[:, pr * LANES:(pr + 1) * LANES]
                q2 = jnp.concatenate([qp, qp], axis=0)
                qm = jnp.where(top_low, q2, jnp.zeros_like(q2))
                kblk = k_ref[pl.ds(start, tk), pr * LANES:(pr + 1) * LANES]
                z = _dot_nt(qm, kblk)
                log_keep = -(jnp.maximum(z, 0.0) + jnp.log(1.0 + jnp.exp(-jnp.abs(z))))
                if masked:
                    log_keep = jnp.where(mask, log_keep, 0.0)
                hi = _bf(log_keep)
                lo = _bf(log_keep - hi.astype(jnp.float32))
                sums_ref[2 * d + pr] = _dot(hi, sum_mat) + _dot(lo, sum_mat)
                zl_ref[2 * d + pr] = z + log_keep
        slowest = None
        for d in range(2):
            start = pl.multiple_of((kb_hi - d) * tk, tk)
            for pr in range(n_pairs):
                vblk = v_ref[pl.ds(start, tk), pr * LANES:(pr + 1) * LANES]
                run = run_ref[pr]
                a = jnp.exp(zl_ref[2 * d + pr] + run + sums_ref[2 * d + pr, :, 0:tk])
                if masked:
                    a = jnp.where(masks[d], a, 0.0)
                acc_ref[pr] = acc_ref[pr] + _dot(_bf(a), vblk)
                run = run + sums_ref[2 * d + pr, :, tk:2 * tk]
                run_ref[pr] = run
                if d == 1:
                    top = jnp.max(run)
                    slowest = top if slowest is None else jnp.maximum(slowest, top)
        return slowest

    underflow = -104.0
    assert tq == 2 * tk
    first = step(2 * i + 1, True)
    lax.while_loop(lambda cr: (cr[0] >= 0) & (cr[1] > underflow),
                   lambda cr: (cr[0] - 1, step(2 * cr[0] + 1, False)),
                   (i - 1, first))

    lane = lax.broadcasted_iota(jnp.int32, (tq, LANES), 1)
    outs = [jnp.where(lane < SB_DIM, acc_ref[pr, 0:tq, :], acc_ref[pr, tq:2 * tq, :]) for pr in range(n_pairs)]
    o = jnp.concatenate(outs, axis=1)
    o_ref[...] = _bf(_rms(o, g_ref[...]))


def _sb_attention(q, k, v, g, batch, seq):
    nq = seq // TQ_SB
    return pl.pallas_call(
        _sb_kernel,
        grid=(batch, nq),
        in_specs=[pl.BlockSpec((TQ_SB, SB_W), lambda b, i: (b * nq + i, 0)),
                  pl.BlockSpec((seq, SB_W), lambda b, i: (b, 0)),
                  pl.BlockSpec((seq, SB_W), lambda b, i: (b, 0)),
                  pl.BlockSpec((1, SB_W), lambda b, i: (0, 0))],
        out_specs=pl.BlockSpec((TQ_SB, SB_W), lambda b, i: (b * nq + i, 0)),
        out_shape=jax.ShapeDtypeStruct((batch * seq, SB_W), jnp.bfloat16),
        scratch_shapes=[pltpu.VMEM((SB_HEADS // 2, 2 * TQ_SB, TK_SB), jnp.float32),
                        pltpu.VMEM((SB_HEADS // 2, 2 * TQ_SB, LANES), jnp.float32),
                        pltpu.VMEM((SB_HEADS, 2 * TQ_SB, TK_SB), jnp.float32),
                        pltpu.VMEM((SB_HEADS, 2 * TQ_SB, 2 * TK_SB), jnp.float32)],
        compiler_params=_cparams(("parallel", "parallel")),
        name="sb_attention",
    )(q, k, v, g)


def _ca_kernel(q_ref, k_ref, v_ref, tab_ref, g_ref, o_ref, bias_ref):
    i = pl.program_id(1)
    t = TQ_CA
    start = pl.multiple_of(i * t, t)

    @pl.when((pl.program_id(0) == 0) & (i == 0))
    def _():
        r = lax.broadcasted_iota(jnp.int32, (t, CA_WIN), 0)
        c = lax.broadcasted_iota(jnp.int32, (t, CA_WIN), 1)
        lo = (r // CHUNK) * CHUNK
        band = (c >= lo) & (c < lo + CA_PAD + CHUNK)
        for h in range(CA_HEADS):
            tab = jnp.broadcast_to(tab_ref[h:h + 1, :], (t, REL_TAB))
            bias = pltpu.roll(tab, REL_TAB - (REL_CLIP - 1), 1, stride=1, stride_axis=0)[:, :CA_WIN]
            bias_ref[h] = jnp.where(band, bias * LOG2E, -jnp.inf)

    r2 = lax.broadcasted_iota(jnp.int32, (2 * t, LANES), 0)
    c2 = lax.broadcasted_iota(jnp.int32, (2 * t, LANES), 1)
    top_low = (r2 < t) == (c2 < CA_DIM)
    in_seq = lax.broadcasted_iota(jnp.int32, (2 * t, CA_WIN), 1) + i * t >= CA_PAD
    lane = lax.broadcasted_iota(jnp.int32, (t, LANES), 1)

    outs = []
    for pair in range(CA_HEADS // 2):
        qp = q_ref[:, pair * LANES:(pair + 1) * LANES]
        q2 = jnp.concatenate([qp, qp], axis=0)
        qm = jnp.where(top_low, q2, jnp.zeros_like(q2))
        kwin = k_ref[pl.ds(start, CA_WIN), pair * LANES:(pair + 1) * LANES]
        vwin = v_ref[pl.ds(start, CA_WIN), pair * LANES:(pair + 1) * LANES]
        bias = jnp.concatenate([bias_ref[2 * pair], bias_ref[2 * pair + 1]], axis=0)
        s = jnp.where(in_seq, _dot_nt(qm, kwin) + bias, -jnp.inf)
        m = jnp.max(s, axis=-1, keepdims=True)
        p = jnp.exp2(s - m)
        l = jnp.sum(p, axis=-1, keepdims=True)
        o2 = _dot(_bf(p), vwin) / l
        outs.append(jnp.where(lane < CA_DIM, o2[0:t], o2[t:2 * t]))
    o = jnp.concatenate(outs, axis=1)
    o_ref[...] = _bf(_rms(o, g_ref[...]))


def _ca_attention(q, kpad, vpad, tab, g, batch, seq):
    nq = seq // TQ_CA
    return pl.pallas_call(
        _ca_kernel,
        grid=(batch, nq),
        in_specs=[pl.BlockSpec((TQ_CA, CA_W), lambda b, i: (b * nq + i, 0)),
                  pl.BlockSpec((seq + CA_PAD, CA_W), lambda b, i: (b, 0)),
                  pl.BlockSpec((seq + CA_PAD, CA_W), lambda b, i: (b, 0)),
                  pl.BlockSpec((CA_HEADS, REL_TAB), lambda b, i: (0, 0)),
                  pl.BlockSpec((1, CA_W), lambda b, i: (0, 0))],
        out_specs=pl.BlockSpec((TQ_CA, CA_W), lambda b, i: (b * nq + i, 0)),
        out_shape=jax.ShapeDtypeStruct((batch * seq, CA_W), jnp.bfloat16),
        scratch_shapes=[pltpu.VMEM((CA_HEADS, TQ_CA, CA_WIN), jnp.float32)],
        compiler_params=_cparams(("arbitrary", "arbitrary")),
        name="ca_attention",
    )(q, kpad, vpad, tab, g)


def _outproj_router_kernel(ma_ref, mb_ref, mc_ref, wo_ref, x_ref, g_ref, wrh_ref, wrl_ref, br_ref,
                           xn_ref, h_ref, lpos_ref, meta_ref, runs_ref, tot_ref, base_ref):
    @pl.when(pl.program_id(0) == 0)
    def _():
        base_ref[...] = jnp.zeros_like(base_ref)

    base = base_ref[...]
    for j in range(ROUTER_TILES):
        base = _route_tile(j, base, ma_ref, mb_ref, mc_ref, wo_ref, x_ref, g_ref, wrh_ref, wrl_ref, br_ref,
                           xn_ref, h_ref, lpos_ref, meta_ref, runs_ref)
    base_ref[...] = base
    tot_ref[...] = base.astype(jnp.int32)


def _route_tile(j, base, ma_ref, mb_ref, mc_ref, wo_ref, x_ref, g_ref, wrh_ref, wrl_ref, br_ref,
                xn_ref, h_ref, lpos_ref, meta_ref, runs_ref):
    tm = TM_TOK
    rows = slice(j * tm, (j + 1) * tm)
    attn = (_dot(ma_ref[rows, :], wo_ref[0:A_W, :]) + _dot(mb_ref[rows, :], wo_ref[A_W:A_W + SB_W, :])
            + _dot(mc_ref[rows, :], wo_ref[A_W + SB_W:, :]))
    xn = x_ref[rows, :] + attn
    xn_ref[rows, :] = xn
    h = _rms(xn, g_ref[...])

    h_hi = _bf(h)
    h_ref[rows, :] = h_hi
    h_lo = _bf(h - h_hi.astype(jnp.float32))
    logits = (_dot(h_hi, wrh_ref[...]) + _dot(h_hi, wrl_ref[...]) + _dot(h_lo, wrh_ref[...])
              + br_ref[...])
    lane = lax.broadcasted_iota(jnp.int32, (tm, LANES), 1)
    lane_f = lane.astype(jnp.float32)

    work = logits
    vals, idxs, hots = [], [], []
    for _ in range(TOP_K):
        mx = jnp.max(work, axis=-1, keepdims=True)
        ix = jnp.min(jnp.where(work == mx, lane_f, float(LANES)), axis=-1, keepdims=True)
        hot = lane_f == ix
        work = jnp.where(hot, -jnp.inf, work)
        vals.append(mx)
        idxs.append(ix)
        hots.append(hot)
    exps = [jnp.exp(v - vals[0]) for v in vals]
    denom = exps[0] + exps[1] + exps[2] + exps[3]
    gates = [e / denom for e in exps]

    sel = jnp.zeros((tm, LANES), jnp.float32)
    for hot in hots:
        sel = sel + jnp.where(hot, 1.0, 0.0)
    r = lax.broadcasted_iota(jnp.int32, (tm, tm), 0)
    c = lax.broadcasted_iota(jnp.int32, (tm, tm), 1)
    before = jnp.where(c < r, 1.0, 0.0).astype(jnp.bfloat16)
    rank_in_tile = _dot(before, _bf(sel))

    cnt = jnp.sum(sel, axis=0, keepdims=True)
    cnt_al = jnp.ceil(cnt * (1.0 / RUN_ALIGN)) * RUN_ALIGN
    rl = lax.broadcasted_iota(jnp.int32, (LANES, LANES), 0)
    cl = lax.broadcasted_iota(jnp.int32, (LANES, LANES), 1)
    earlier = jnp.where(rl < cl, 1.0, 0.0).astype(jnp.bfloat16)
    loff = _dot(_bf(jnp.broadcast_to(cnt_al, (8, LANES))), earlier)[0:1, :]
    sub = lax.broadcasted_iota(jnp.int32, (8, LANES), 0)
    runs = jnp.where(sub == 0, loff, jnp.where(sub == 1, base, jnp.where(sub == 2, cnt_al, 0.0)))
    runs_ref[8 * j:8 * j + 8, :] = runs.astype(jnp.int32)

    lpos_dense = loff + rank_in_tile
    meta = jnp.zeros((tm, LANES), jnp.float32)
    for kk in range(TOP_K):
        lpos = jnp.sum(jnp.where(hots[kk], lpos_dense, 0.0), axis=-1, keepdims=True)
        meta = meta + jnp.where(lane == kk, gates[kk], 0.0) + jnp.where(lane == TOP_K + kk, lpos, 0.0)
    meta_ref[rows, :] = meta
    lpos_ref[:, rows] = jnp.transpose(meta)[TOP_K:TOP_K + 8, :].astype(jnp.int32)
    return base + cnt_al


def _outproj_router(ma, mb, mc, wo, x, g, wrh, wrl, br):
    n = x.shape[0]
    tm = TM_TOK * ROUTER_TILES
    row = lambda width: pl.BlockSpec((tm, width), lambda i: (i, 0))
    full = lambda a: pl.BlockSpec(a.shape, lambda i: (0,) * a.ndim)
    return pl.pallas_call(
        _outproj_router_kernel,
        grid=(n // tm,),
        in_specs=[row(A_W), row(SB_W), row(CA_W), full(wo), row(D_MODEL), full(g),
                  full(wrh), full(wrl), full(br)],
        out_specs=[row(D_MODEL), row(D_MODEL), pl.BlockSpec((8, tm), lambda i: (0, i)),
                   row(LANES), pl.BlockSpec((8 * ROUTER_TILES, LANES), lambda i: (i, 0)),
                   pl.BlockSpec((1, LANES), lambda i: (0, 0))],
        out_shape=[jax.ShapeDtypeStruct((n, D_MODEL), jnp.float32),
                   jax.ShapeDtypeStruct((n, D_MODEL), jnp.bfloat16),
                   jax.ShapeDtypeStruct((8, n), jnp.int32),
                   jax.ShapeDtypeStruct((n, LANES), jnp.float32),
                   jax.ShapeDtypeStruct((8 * (n // TM_TOK), LANES), jnp.int32),
                   jax.ShapeDtypeStruct((1, LANES), jnp.int32)],
        scratch_shapes=[pltpu.VMEM((1, LANES), jnp.float32)],
        compiler_params=_cparams(("arbitrary",)),
        name="outproj_router",
    )(ma, mb, mc, wo, x, g, wrh, wrl, br)


_RUN_CHUNKS = tuple(TM_TOK >> s for s in range(TM_TOK.bit_length()) if (TM_TOK >> s) >= RUN_ALIGN)


def _for_each_chunk(length, fn):
    for size in _RUN_CHUNKS:
        off = length & (~(2 * size - 1))

        @pl.when((length & size) != 0)
        def _(off=off, size=size):
            fn(off, size)


_WAIT_CHUNKS = tuple(1 << b for b in range((LOCAL_ROWS).bit_length() - 1, RUN_ALIGN.bit_length() - 2, -1))


def _wait_tile_runs(runs_ref, make_copy):
    total = runs_ref[0, N_EXPERTS - 1] + runs_ref[2, N_EXPERTS - 1]
    for size in _WAIT_CHUNKS:
        @pl.when((total & size) != 0)
        def _(size=size):
            make_copy(size).wait()


def _tile_runs(runs_ref, pstart_ref, fn):
    def body(e, _):
        local, glob, length = runs_ref[0, e], pstart_ref[e] + runs_ref[1, e], runs_ref[2, e]
        _for_each_chunk(length, lambda off, size: fn(pl.multiple_of(local + off, RUN_ALIGN),
                                                     pl.multiple_of(glob + off, RUN_ALIGN), size))
        return 0

    lax.fori_loop(0, N_EXPERTS, body, 0, unroll=2)


def _dispatch_kernel(pstart_ref, tot_ref, pad_ref, runs_ref, prev_runs_ref, lpos_ref, h_ref, xs_ref,
                     loc, zbuf, sem, fill_sem):
    i = pl.program_id(0)
    slot = i % 2

    def fill(start_or_wait):
        def body(e, _):
            first = pstart_ref[e] + tot_ref[e]
            _for_each_chunk(pad_ref[e] - tot_ref[e], lambda off, size: start_or_wait(pltpu.make_async_copy(
                zbuf.at[pl.ds(0, size)], xs_ref.at[pl.ds(pl.multiple_of(first + off, RUN_ALIGN), size)], fill_sem)))
            return 0

        lax.fori_loop(0, N_EXPERTS, body, 0)

        used = pstart_ref[N_EXPERTS - 1] + pad_ref[N_EXPERTS - 1]

        def tail(b, _):
            start_or_wait(pltpu.make_async_copy(
                zbuf, xs_ref.at[pl.ds(pl.multiple_of(b * TM_TOK, TM_TOK), TM_TOK)], fill_sem))
            return 0

        lax.fori_loop(used // TM_TOK, xs_ref.shape[0] // TM_TOK, tail, 0)

    @pl.when(i == 0)
    def _():
        zbuf[...] = jnp.zeros_like(zbuf)
        fill(lambda c: c.start())
        fill(lambda c: c.wait())

    r = lax.broadcasted_iota(jnp.int32, (LOCAL_ROWS, TM_TOK), 0)
    hit = r == lpos_ref[0:1, :]
    for kk in range(1, TOP_K):
        hit = hit | (r == lpos_ref[kk:kk + 1, :])
    loc[slot] = _pack_rows(_dot(jnp.where(hit, 1.0, 0.0).astype(jnp.bfloat16), h_ref[...]))

    def copy(buf, local, glob, size):
        return pltpu.make_async_copy(loc.at[buf, pl.ds(local, size)], xs_ref.at[pl.ds(glob, size)], sem.at[buf])

    _tile_runs(runs_ref, pstart_ref, lambda l, g, s: copy(slot, l, g, s).start())

    @pl.when(i > 0)
    def _():
        _wait_tile_runs(prev_runs_ref, lambda size: copy(1 - slot, 0, 0, size))

    @pl.when(i == pl.num_programs(0) - 1)
    def _():
        _wait_tile_runs(runs_ref, lambda size: copy(slot, 0, 0, size))


def _dispatch(pstart, totals, padded, runs, lpos, h, n_rows):
    n = h.shape[0]
    tm = TM_TOK
    grid_spec = pltpu.PrefetchScalarGridSpec(
        num_scalar_prefetch=3,
        grid=(n // tm,),
        in_specs=[pl.BlockSpec((8, LANES), lambda i, *_: (i, 0), memory_space=pltpu.SMEM),
                  pl.BlockSpec((8, LANES), lambda i, *_: (jnp.maximum(i - 1, 0), 0), memory_space=pltpu.SMEM),
                  pl.BlockSpec((8, tm), lambda i, *_: (0, i)),
                  pl.BlockSpec((tm, D_MODEL), lambda i, *_: (i, 0))],
        out_specs=pl.BlockSpec(memory_space=pl.ANY),
        scratch_shapes=[pltpu.VMEM((2, LOCAL_ROWS, HALF), jnp.uint32),
                        pltpu.VMEM((TM_TOK, HALF), jnp.uint32),
                        pltpu.SemaphoreType.DMA((2,)),
                        pltpu.SemaphoreType.DMA],
    )
    return pl.pallas_call(
        _dispatch_kernel,
        grid_spec=grid_spec,
        out_shape=jax.ShapeDtypeStruct((n_rows, HALF), jnp.uint32),
        compiler_params=_cparams(("arbitrary",)),
        name="dispatch",
    )(pstart, totals, padded, runs, runs, lpos, h)


def _expert_kernel(be_ref, bfirst_ref, bvalid_ref, xs_ref, wgu_ref, bgu_ref, wdn_ref, bdn_ref,
                   ys_ref, wgu_bf, wdn_bf):
    b = pl.program_id(0)

    @pl.when(bfirst_ref[b] == 1)
    def _():
        wgu_bf[...] = _bf(wgu_ref[0])
        wdn_bf[...] = _bf(wdn_ref[0])

    @pl.when(bvalid_ref[b] == 1)
    def _():
        x_lo, x_hi = _unpack_rows(xs_ref[...])
        gu = _dot(x_lo, wgu_bf[0:HALF, :]) + _dot(x_hi, wgu_bf[HALF:, :]) + bgu_ref[0]
        gte = jnp.minimum(gu[:, :D_FF], SWIGLU_LIMIT)
        up = jnp.clip(gu[:, D_FF:], -SWIGLU_LIMIT, SWIGLU_LIMIT)
        act = (up + 1.0) * (gte * (1.0 / (1.0 + jnp.exp(-SWIGLU_ALPHA * gte))))
        y = _dot(_bf(act), wdn_bf[...]) + bdn_ref[0]
        ys_ref[...] = _pack_rows(_bf(y).astype(jnp.float32))

    @pl.when(bvalid_ref[b] == 0)
    def _():
        ys_ref[...] = jnp.zeros_like(ys_ref)


def _expert_ffn(blk_e, blk_first, blk_valid, xs, wgu, bgu, wdn, bdn):
    n_rows = xs.shape[0]
    tm = TM_EXP
    grid_spec = pltpu.PrefetchScalarGridSpec(
        num_scalar_prefetch=3,
        grid=(n_rows // tm,),
        in_specs=[pl.BlockSpec((tm, HALF), lambda b, e, f, v: (b, 0)),
                  pl.BlockSpec((1, D_MODEL, 2 * D_FF), lambda b, e, f, v: (e[b], 0, 0)),
                  pl.BlockSpec((1, 1, 2 * D_FF), lambda b, e, f, v: (e[b], 0, 0)),
                  pl.BlockSpec((1, D_FF, D_MODEL), lambda b, e, f, v: (e[b], 0, 0)),
                  pl.BlockSpec((1, 1, D_MODEL), lambda b, e, f, v: (e[b], 0, 0))],
        out_specs=pl.BlockSpec((tm, HALF), lambda b, e, f, v: (b, 0)),
        scratch_shapes=[pltpu.VMEM((D_MODEL, 2 * D_FF), jnp.bfloat16),
                        pltpu.VMEM((D_FF, D_MODEL), jnp.bfloat16)],
    )
    return pl.pallas_call(
        _expert_kernel,
        grid_spec=grid_spec,
        out_shape=jax.ShapeDtypeStruct((n_rows, HALF), jnp.uint32),
        compiler_params=_cparams(("arbitrary",)),
        name="expert_ffn",
    )(blk_e, blk_first, blk_valid, xs, wgu, bgu, wdn, bdn)


def _combine_kernel(final, pstart_ref, runs_ref, next_runs_ref, meta_ref, x_ref, ys_ref, gfin_ref, o_ref, loc, sem):
    i = pl.program_id(0)
    slot = i % 2

    def copy(buf, local, glob, size):
        return pltpu.make_async_copy(ys_ref.at[pl.ds(glob, size)], loc.at[buf, pl.ds(local, size)], sem.at[buf])

    @pl.when(i == 0)
    def _():
        loc[...] = jnp.zeros_like(loc)
        _tile_runs(runs_ref, pstart_ref, lambda l, g, s: copy(slot, l, g, s).start())

    @pl.when(i + 1 < pl.num_programs(0))
    def _():
        _tile_runs(next_runs_ref, pstart_ref, lambda l, g, s: copy(1 - slot, l, g, s).start())

    _wait_tile_runs(runs_ref, lambda size: copy(slot, 0, 0, size))

    meta = meta_ref[...]
    col = lax.broadcasted_iota(jnp.int32, (TM_TOK, LOCAL_ROWS), 1).astype(jnp.float32)
    wts = jnp.zeros((TM_TOK, LOCAL_ROWS), jnp.float32)
    for kk in range(TOP_K):
        wts = wts + jnp.where(col == meta[:, TOP_K + kk:TOP_K + kk + 1], meta[:, kk:kk + 1], 0.0)
    w_hi = _bf(wts)
    w_lo = _bf(wts - w_hi.astype(jnp.float32))
    w2 = jnp.concatenate([w_hi, w_lo], axis=0)
    y_lo, y_hi = _unpack_rows(loc[slot])
    r_lo, r_hi = _dot(w2, y_lo), _dot(w2, y_hi)
    out = x_ref[...] + jnp.concatenate([r_lo[:TM_TOK] + r_lo[TM_TOK:], r_hi[:TM_TOK] + r_hi[TM_TOK:]], axis=1)
    if final:
        out = _rms(out, gfin_ref[...])
    o_ref[...] = out


def _combine(pstart, runs, meta, x, ys, gfin, final):
    n = x.shape[0]
    tm = TM_TOK
    n_tiles = n // tm
    grid_spec = pltpu.PrefetchScalarGridSpec(
        num_scalar_prefetch=1,
        grid=(n_tiles,),
        in_specs=[pl.BlockSpec((8, LANES), lambda i, ps: (i, 0), memory_space=pltpu.SMEM),
                  pl.BlockSpec((8, LANES), lambda i, ps: (jnp.minimum(i + 1, n_tiles - 1), 0), memory_space=pltpu.SMEM),
                  pl.BlockSpec((tm, LANES), lambda i, ps: (i, 0)),
                  pl.BlockSpec((tm, D_MODEL), lambda i, ps: (i, 0)),
                  pl.BlockSpec(memory_space=pl.ANY),
                  pl.BlockSpec((1, D_MODEL), lambda i, ps: (0, 0))],
        out_specs=pl.BlockSpec((tm, D_MODEL), lambda i, ps: (i, 0)),
        scratch_shapes=[pltpu.VMEM((2, LOCAL_ROWS, HALF), jnp.uint32), pltpu.SemaphoreType.DMA((2,))],
    )
    return pl.pallas_call(
        functools.partial(_combine_kernel, final),
        grid_spec=grid_spec,
        out_shape=jax.ShapeDtypeStruct((n, D_MODEL), jnp.float32),
        compiler_params=_cparams(("arbitrary",)),
        name="combine_final" if final else "combine",
    )(pstart, runs, runs, meta, x, ys, gfin)


def _pad_heads(w, parts):
    rows = w.shape[0]
    per = w.shape[1] // MLA_HEADS
    w3 = w.reshape(rows, MLA_HEADS, per)
    cols = [jnp.zeros((rows, MLA_HEADS, b - a), w.dtype) if sign == 0 else sign * w3[:, :, a:b]
            for a, b, sign in parts]
    used = sum(b - a for a, b, _ in parts)
    cols.append(jnp.zeros((rows, MLA_HEADS, HEAD_PAD - used), w.dtype))
    return jnp.concatenate(cols, axis=2).reshape(rows, MLA_HEADS * HEAD_PAD)


def _layer_weights(w_in, w_uq, w_ukv, w_router, b_router, rel_bias):
    half = MLA_ROPE // 2
    cq, ckv, kpe, sb, ca = (w_in[:, 0:256], w_in[:, 256:384], w_in[:, 384:416],
                            w_in[:, 416:1184], w_in[:, 1184:1952])
    z = lambda width: jnp.zeros((D_MODEL, width), w_in.dtype)
    kpe_pad = jnp.concatenate([z(MLA_NOPE), kpe, z(HEAD_PAD - MLA_NOPE - MLA_ROPE)], axis=1)
    kpe_rot = jnp.concatenate([z(MLA_NOPE), -kpe[:, half:], kpe[:, :half],
                               z(HEAD_PAD - MLA_NOPE - MLA_ROPE)], axis=1)
    win2 = jnp.concatenate([cq, ckv, kpe_pad, kpe_rot, sb, ca], axis=1).astype(jnp.bfloat16)

    d = MLA_NOPE + MLA_ROPE
    wq_full = _pad_heads(w_uq, [(0, d, 1)])
    wq_rot = _pad_heads(w_uq, [(0, MLA_NOPE, 0), (MLA_NOPE + half, d, -1), (MLA_NOPE, MLA_NOPE + half, 1)])
    wq2 = jnp.concatenate([wq_full, wq_rot], axis=1).astype(jnp.bfloat16)

    wk = _pad_heads(w_ukv, [(0, MLA_NOPE, 1)])
    wv = w_ukv.reshape(MLA_KV_RANK, MLA_HEADS, MLA_NOPE + MLA_V)[:, :, MLA_NOPE:].reshape(MLA_KV_RANK, A_W)
    wkv2 = jnp.concatenate([wk, wv], axis=1).astype(jnp.bfloat16)

    wr = jnp.pad(w_router, ((0, 0), (0, LANES - N_EXPERTS)))
    wr_hi = wr.astype(jnp.bfloat16)
    wr_lo = (wr - wr_hi.astype(jnp.float32)).astype(jnp.bfloat16)
    br = jnp.pad(b_router, (0, LANES - N_EXPERTS), constant_values=NEG).reshape(1, LANES)

    ext = jnp.concatenate([rel_bias, jnp.broadcast_to(rel_bias[:, -1:], (CA_HEADS, REL_TAB - 2 * REL_CLIP - 1))], axis=1)
    tab = ext[:, ::-1]
    return win2, wq2, wkv2, wr_hi, wr_lo, br, tab


def kernel(x, positions, attn_norm, w_in, q_norm, w_uq, kv_norm, w_ukv, rel_bias, mix_norm,
           w_o, ffn_norm, w_router, b_router, w_gate_up, b_gate_up, w_down, b_down, final_norm):
    batch, seq, _ = x.shape
    n = batch * seq
    depth = w_in.shape[0]
    xf = x.reshape(n, D_MODEL)

    inv = ROPE_THETA ** (-jnp.arange(0, MLA_ROPE, 2, dtype=jnp.float32) / MLA_ROPE)
    inv_row = jnp.concatenate([jnp.zeros((MLA_NOPE,), jnp.float32), inv, inv,
                               jnp.zeros((HEAD_PAD - MLA_NOPE - MLA_ROPE,), jnp.float32)]).reshape(1, LANES)
    cos_t, sin_t = _rope_tables(positions.reshape(n, 1), inv_row)

    n_rows = n * TOP_K + (n // TM_TOK) * N_EXPERTS * (RUN_ALIGN - 1) + N_EXPERTS * TM_EXP
    n_rows = -(-n_rows // TM_EXP) * TM_EXP
    n_blk = n_rows // TM_EXP
    row2 = lambda v: v.reshape(1, -1)
    wgu_all = w_gate_up.reshape(depth * N_EXPERTS, D_MODEL, 2 * D_FF)
    bgu_all = b_gate_up.reshape(depth * N_EXPERTS, 1, 2 * D_FF)
    wdn_all = w_down.reshape(depth * N_EXPERTS, D_FF, D_MODEL)
    bdn_all = b_down.reshape(depth * N_EXPERTS, 1, D_MODEL)

    for l in range(depth):
        win2, wq2, wkv2, wr_hi, wr_lo, br, tab = _layer_weights(
            w_in[l], w_uq[l], w_ukv[l], w_router[l], b_router[l], rel_bias[l])
        qt, k, vt, sbq, sbk, sbv, caq, cak, cav = _inproj(
            xf, row2(attn_norm[l]), win2, cos_t, sin_t, row2(q_norm[l]), wq2, row2(kv_norm[l]), wkv2)

        g = mix_norm[l]
        ma = _mla_attention(qt, k, vt, row2(g[:A_W]), batch, seq)
        mb = _sb_attention(sbq, sbk, sbv, row2(g[A_W:A_W + SB_W]), batch, seq)
        pad = lambda a: jnp.pad(a.reshape(batch, seq, CA_W), ((0, 0), (CA_PAD, 0), (0, 0))).reshape(-1, CA_W)
        mc = _ca_attention(caq, pad(cak), pad(cav), tab, row2(g[A_W + SB_W:]), batch, seq)

        xn, h, lpos, meta, runs, totals = _outproj_router(
            ma, mb, mc, w_o[l].astype(jnp.bfloat16), xf, row2(ffn_norm[l]), wr_hi, wr_lo, br)

        sizes = totals[0, :N_EXPERTS]
        padded = (sizes + TM_EXP - 1) // TM_EXP * TM_EXP
        p_ends = jnp.cumsum(padded)
        p_starts = (p_ends - padded).astype(jnp.int32)
        blk_start = jnp.arange(n_blk, dtype=jnp.int32) * TM_EXP
        blk_valid = (blk_start < p_ends[-1]).astype(jnp.int32)
        last_row = jnp.minimum(blk_start, p_ends[-1] - 1)
        blk_e = jnp.minimum(jnp.sum((last_row[:, None] >= p_ends[None, :]).astype(jnp.int32), axis=1),
                            N_EXPERTS - 1)
        blk_first = jnp.concatenate([jnp.ones((1,), jnp.int32), (blk_e[1:] != blk_e[:-1]).astype(jnp.int32)])

        xs = _dispatch(p_starts, sizes, padded.astype(jnp.int32), runs, lpos, h, n_rows)
        ys = _expert_ffn(blk_e + l * N_EXPERTS, blk_first, blk_valid, xs, wgu_all, bgu_all, wdn_all, bdn_all)
        xf = _combine(p_starts, runs, meta, xn, ys, row2(final_norm), final=(l == depth - 1))

    return xf.reshape(batch, seq, D_MODEL)
```

```python
import functools
import math

import jax
import jax.numpy as jnp
from jax import lax
from jax.experimental import pallas as pl
from jax.experimental.pallas import tpu as pltpu

D_MODEL = 1024
RMS_EPS = 1e-6
MLA_NOPE, MLA_ROPE, MLA_V, MLA_HEADS = 64, 32, 64, 8
MLA_Q_RANK, MLA_KV_RANK = 256, 128
ROPE_THETA = 10000.0
SB_DIM, SB_HEADS = 64, 4
CA_DIM, CA_HEADS = 64, 4
CHUNK = 64
CA_LEFT_CHUNKS = 8
REL_CLIP = 256
N_EXPERTS, TOP_K = 32, 4
D_FF = 1024
SWIGLU_LIMIT, SWIGLU_ALPHA = 7.0, 1.702

LANES = 128
HEAD_PAD = 128
A_W = MLA_HEADS * MLA_V
SB_W = SB_HEADS * SB_DIM
CA_W = CA_HEADS * CA_DIM
CA_PAD = CA_LEFT_CHUNKS * CHUNK
CA_WIN = CA_PAD + 2 * CHUNK
REL_TAB = 1024
HALF = D_MODEL // 2

TM_TOK = 256
TQ_MLA = 256
TQ_SB = 256
TK_SB = 128
TQ_CA = 2 * CHUNK
ROUTER_TILES = 2
TM_EXP = 512
RUN_ALIGN = 8
LOCAL_ROWS = 1280
assert LOCAL_ROWS >= TM_TOK * TOP_K + N_EXPERTS * (RUN_ALIGN - 1) and LOCAL_ROWS % LANES == 0
assert TM_EXP % TM_TOK == 0 and TQ_MLA == TM_TOK

_C_CQ, _C_CKV, _C_KPE, _C_KROT = 0, 256, 384, 512
_C_SB, _C_CA, D_IN2 = 640, 1408, 2176

NEG = -1e30
LOG2E = math.log2(math.e)
VMEM_LIMIT = 56 * 1024 * 1024


def _rms(v, g):
    return v * lax.rsqrt(jnp.mean(v * v, axis=-1, keepdims=True) + RMS_EPS) * g


def _dot(a, b):
    return jnp.dot(a, b, preferred_element_type=jnp.float32)


def _dot_nt(a, b):
    return lax.dot_general(a, b, (((1,), (1,)), ((), ())), preferred_element_type=jnp.float32)


def _bf(v):
    return v.astype(jnp.bfloat16)


def _pack_rows(v):
    bits = lax.bitcast_convert_type(v, jnp.uint32)
    return (bits[:, :HALF] >> 16) | (bits[:, HALF:] & jnp.uint32(0xFFFF0000))


def _unpack_rows(u):
    lo = lax.bitcast_convert_type(u << 16, jnp.float32)
    hi = lax.bitcast_convert_type(u & jnp.uint32(0xFFFF0000), jnp.float32)
    return _bf(lo), _bf(hi)


def _cparams(sem):
    return pltpu.CompilerParams(dimension_semantics=sem, vmem_limit_bytes=VMEM_LIMIT)


ROPE_PACK = LANES // MLA_ROPE


def _rope_kernel(pos_ref, inv_ref, cos_ref, sin_ref):
    ang = pos_ref[...].astype(jnp.float32) * inv_ref[...]
    cos_ref[...] = jnp.cos(ang)
    sin_ref[...] = jnp.sin(ang)


def _rope_tables(positions, n):
    inv = ROPE_THETA ** (-jnp.arange(0, MLA_ROPE, 2, dtype=jnp.float32) / MLA_ROPE)
    inv_row = jnp.tile(jnp.concatenate([inv, inv]), ROPE_PACK).reshape(1, LANES)
    rows = n // ROPE_PACK
    pos_packed = jnp.repeat(positions.reshape(rows, ROPE_PACK), MLA_ROPE, axis=1)
    tm = min(TM_TOK, rows)
    cos_p, sin_p = pl.pallas_call(
        _rope_kernel,
        grid=(rows // tm,),
        in_specs=[pl.BlockSpec((tm, LANES), lambda i: (i, 0)),
                  pl.BlockSpec((1, LANES), lambda i: (0, 0))],
        out_specs=[pl.BlockSpec((tm, LANES), lambda i: (i, 0))] * 2,
        out_shape=[jax.ShapeDtypeStruct((rows, LANES), jnp.float32)] * 2,
        compiler_params=_cparams(("parallel",)),
        name="rope_tables",
    )(pos_packed, inv_row)
    pad = HEAD_PAD - MLA_NOPE - MLA_ROPE
    ones = lambda width: jnp.ones((n, width), jnp.float32)
    zeros = lambda width: jnp.zeros((n, width), jnp.float32)
    cos_t = jnp.concatenate([ones(MLA_NOPE), cos_p.reshape(n, MLA_ROPE), ones(pad)], axis=1)
    sin_t = jnp.concatenate([zeros(MLA_NOPE), sin_p.reshape(n, MLA_ROPE), zeros(pad)], axis=1)
    return cos_t, sin_t


def _inproj_kernel(x_ref, g_ref, win_ref, cos_ref, sin_ref, qn_ref, wq_ref, kvn_ref, wkv_ref,
                   q_ref, k_ref, v_ref, sbq_ref, sbk_ref, sbv_ref, caq_ref, cak_ref, cav_ref):
    h = _rms(x_ref[...], g_ref[...])
    proj = _dot(_bf(h), win_ref[...])
    cos128, sin128 = cos_ref[...], sin_ref[...]
    cos_h = jnp.concatenate([cos128] * MLA_HEADS, axis=1)
    sin_h = jnp.concatenate([sin128] * MLA_HEADS, axis=1)
    w = MLA_HEADS * HEAD_PAD

    cqn = _rms(proj[:, _C_CQ:_C_CKV], qn_ref[...])
    q2 = _dot(_bf(cqn), wq_ref[...])
    q = (q2[:, :w] * cos_h + q2[:, w:] * sin_h) * (LOG2E / math.sqrt(MLA_NOPE + MLA_ROPE))
    q_ref[0] = _bf(jnp.transpose(q))

    ckvn = _rms(proj[:, _C_CKV:_C_KPE], kvn_ref[...])
    kv2 = _dot(_bf(ckvn), wkv_ref[...])
    kpe = proj[:, _C_KPE:_C_KROT] * cos128 + proj[:, _C_KROT:_C_SB] * sin128
    k_ref[...] = _bf(kv2[:, :w] + jnp.concatenate([kpe] * MLA_HEADS, axis=1))
    v_ref[0] = _bf(jnp.transpose(kv2[:, w:]))

    sb_scale = 1.0 / math.sqrt(SB_DIM)
    sbq_ref[...] = _bf(proj[:, _C_SB:_C_SB + SB_W] * sb_scale)
    sbk_ref[...] = _bf(proj[:, _C_SB + SB_W:_C_SB + 2 * SB_W])
    sbv_ref[...] = _bf(proj[:, _C_SB + 2 * SB_W:_C_CA])
    ca_scale = LOG2E / math.sqrt(CA_DIM)
    caq_ref[...] = _bf(proj[:, _C_CA:_C_CA + CA_W] * ca_scale)
    cak_ref[...] = _bf(proj[:, _C_CA + CA_W:_C_CA + 2 * CA_W])
    cav_ref[...] = _bf(proj[:, _C_CA + 2 * CA_W:D_IN2])


def _inproj(x, g, win2, cos_t, sin_t, qn, wq2, kvn, wkv2):
    n = x.shape[0]
    tm = TM_TOK
    row = lambda width: pl.BlockSpec((tm, width), lambda i: (i, 0))
    full = lambda a: pl.BlockSpec(a.shape, lambda i: (0,) * a.ndim)
    widths = [MLA_HEADS * HEAD_PAD] + [SB_W] * 3 + [CA_W] * 3
    tile_t = lambda rows: pl.BlockSpec((1, rows, tm), lambda i: (i, 0, 0))
    shape_t = lambda rows: jax.ShapeDtypeStruct((n // tm, rows, tm), jnp.bfloat16)
    return pl.pallas_call(
        _inproj_kernel,
        grid=(n // tm,),
        in_specs=[row(D_MODEL), full(g), full(win2), row(LANES), row(LANES),
                  full(qn), full(wq2), full(kvn), full(wkv2)],
        out_specs=[tile_t(MLA_HEADS * HEAD_PAD), row(widths[0]), tile_t(A_W)] + [row(wd) for wd in widths[1:]],
        out_shape=[shape_t(MLA_HEADS * HEAD_PAD), jax.ShapeDtypeStruct((n, widths[0]), jnp.bfloat16), shape_t(A_W)]
                  + [jax.ShapeDtypeStruct((n, wd), jnp.bfloat16) for wd in widths[1:]],
        compiler_params=_cparams(("parallel",)),
        name="inproj",
    )(x, g, win2, cos_t, sin_t, qn, wq2, kvn, wkv2)


def _mla_kernel(qt_ref, k_ref, vt_ref, g_ref, o_ref, m_ref, l_ref, acc_ref, s_ref):
    i = pl.program_id(1)
    tq = TQ_MLA
    n_pairs = MLA_HEADS // 2
    key_chunk = lax.broadcasted_iota(jnp.int32, (tq, tq), 0) // CHUNK
    qry_chunk = lax.broadcasted_iota(jnp.int32, (tq, tq), 1) // CHUNK
    diag_ok = key_chunk <= qry_chunk
    top = lax.broadcasted_iota(jnp.int32, (LANES, tq), 0) < MLA_V

    m_ref[...] = jnp.full(m_ref.shape, -jnp.inf, jnp.float32)
    l_ref[...] = jnp.zeros(l_ref.shape, jnp.float32)
    acc_ref[...] = jnp.zeros(acc_ref.shape, jnp.float32)

    def scores(kb, slot):
        start = pl.multiple_of(kb * tq, tq)
        for h in range(MLA_HEADS):
            qt = qt_ref[0, h * HEAD_PAD:(h + 1) * HEAD_PAD, :]
            kblk = k_ref[pl.ds(start, tq), h * HEAD_PAD:(h + 1) * HEAD_PAD]
            s_ref[slot, h] = _dot(kblk, qt)

    def absorb(kb, slot, mask):
        for pr in range(n_pairs):
            vt = vt_ref[kb, pr * LANES:(pr + 1) * LANES, :]
            zero = jnp.zeros_like(vt)
            vt_bd = jnp.concatenate([jnp.where(top, vt, zero), jnp.where(top, zero, vt)], axis=1)
            pts, alphas = [], []
            for h in (2 * pr, 2 * pr + 1):
                s = s_ref[slot, h]
                if mask is not None:
                    s = jnp.where(mask, s, -jnp.inf)
                m_old = m_ref[h]
                m_new = jnp.maximum(m_old, jnp.max(s, axis=0, keepdims=True))
                p = jnp.exp2(s - m_new)
                alpha = jnp.exp2(m_old - m_new)
                l_ref[h] = alpha * l_ref[h] + jnp.sum(p, axis=0, keepdims=True)
                m_ref[h] = m_new
                pts.append(_bf(p))
                alphas.append(alpha)
            alpha_pair = jnp.where(top, alphas[0], alphas[1])
            acc_ref[pr] = alpha_pair * acc_ref[pr] + _dot(vt_bd, jnp.concatenate(pts, axis=0))

    def body(j, carry):
        scores(2 * j + 1, 1)
        absorb(2 * j, 0, None)
        scores(2 * j + 2, 0)
        absorb(2 * j + 1, 1, None)
        return carry

    scores(0, 0)
    lax.fori_loop(0, i // 2, body, 0)

    @pl.when(i % 2 == 0)
    def _():
        absorb(i, 0, diag_ok)

    @pl.when(i % 2 == 1)
    def _():
        scores(i, 1)
        absorb(i - 1, 0, None)
        absorb(i, 1, diag_ok)

    outs = [jnp.transpose(acc_ref[pr] / jnp.where(top, l_ref[2 * pr], l_ref[2 * pr + 1])) for pr in range(n_pairs)]
    o = jnp.concatenate(outs, axis=1)
    o_ref[...] = _bf(_rms(o, g_ref[...]))


def _mla_attention(qt, k, vt, g, batch, seq):
    nq = seq // TQ_MLA
    w = MLA_HEADS * HEAD_PAD
    return pl.pallas_call(
        _mla_kernel,
        grid=(batch, nq),
        in_specs=[pl.BlockSpec((1, w, TQ_MLA), lambda b, i: (b * nq + i, 0, 0)),
                  pl.BlockSpec((seq, w), lambda b, i: (b, 0)),
                  pl.BlockSpec((nq, A_W, TQ_MLA), lambda b, i: (b, 0, 0)),
                  pl.BlockSpec((1, A_W), lambda b, i: (0, 0))],
        out_specs=pl.BlockSpec((TQ_MLA, A_W), lambda b, i: (b * nq + i, 0)),
        out_shape=jax.ShapeDtypeStruct((batch * seq, A_W), jnp.bfloat16),
        scratch_shapes=[pltpu.VMEM((MLA_HEADS, 1, TQ_MLA), jnp.float32),
                        pltpu.VMEM((MLA_HEADS, 1, TQ_MLA), jnp.float32),
                        pltpu.VMEM((MLA_HEADS // 2, LANES, TQ_MLA), jnp.float32),
                        pltpu.VMEM((2, MLA_HEADS, TQ_MLA, TQ_MLA), jnp.float32)],
        compiler_params=_cparams(("parallel", "parallel")),
        name="mla_attention",
    )(qt, k, vt, g)


def _sb_kernel(q_ref, k_ref, v_ref, g_ref, o_ref, run_ref, acc_ref, zl_ref, sums_ref):
    i = pl.program_id(1)
    tq, tk = TQ_SB, TK_SB
    n_pairs = SB_HEADS // 2
    r = lax.broadcasted_iota(jnp.int32, (2 * tq, tk), 0)
    c = lax.broadcasted_iota(jnp.int32, (2 * tq, tk), 1)
    q_off = jnp.where(r >= tq, r - tq, r)
    top_low = (r < tq) == (c < SB_DIM)
    r2 = lax.broadcasted_iota(jnp.int32, (tk, 2 * tk), 0)
    c2 = lax.broadcasted_iota(jnp.int32, (tk, 2 * tk), 1)
    sum_mat = jnp.where((c2 >= tk) | (r2 > c2), 1.0, 0.0).astype(jnp.bfloat16)

    run_ref[...] = jnp.zeros(run_ref.shape, jnp.float32)
    acc_ref[...] = jnp.zeros(acc_ref.shape, jnp.float32)

    def step(kb_hi, masked):
        masks = []
        for d in range(2):
            kb = kb_hi - d
            start = pl.multiple_of(kb * tk, tk)
            mask = (c + (kb * tk - i * tq)) < q_off if masked else None
            masks.append(mask)
            for pr in range(n_pairs):
                qp = q_ref[:, pr * LANES:(pr + 1) * LANES]
                q2 = jnp.concatenate([qp, qp], axis=0)
                qm = jnp.where(top_low, q2, jnp.zeros_like(q2))
                kblk = k_ref[pl.ds(start, tk), pr * LANES:(pr + 1) * LANES]
                z = _dot_nt(qm, kblk)
                log_keep = -(jnp.maximum(z, 0.0) + jnp.log(1.0 + jnp.exp(-jnp.abs(z))))
                if masked:
                    log_keep = jnp.where(mask, log_keep, 0.0)
                hi = _bf(log_keep)
                lo = _bf(log_keep - hi.astype(jnp.float32))
                sums_ref[2 * d + pr] = _dot(hi, sum_mat) + _dot(lo, sum_mat)
                zl_ref[2 * d + pr] = z + log_keep
        slowest = None
        for d in range(2):
            start = pl.multiple_of((kb_hi - d) * tk, tk)
            for pr in range(n_pairs):
                vblk = v_ref[pl.ds(start, tk), pr * LANES:(pr + 1) * LANES]
                run = run_ref[pr]
                a = jnp.exp(zl_ref[2 * d + pr] + run + sums_ref[2 * d + pr, :, 0:tk])
                if masked:
                    a = jnp.where(masks[d], a, 0.0)
                acc_ref[pr] = acc_ref[pr] + _dot(_bf(a), vblk)
                run = run + sums_ref[2 * d + pr, :, tk:2 * tk]
                run_ref[pr] = run
                if d == 1:
                    top = jnp.max(run)
                    slowest = top if slowest is None else jnp.maximum(slowest, top)
        return slowest

    underflow = -104.0
    assert tq == 2 * tk
    first = step(2 * i + 1, True)
    lax.while_loop(lambda cr: (cr[0] >= 0) & (cr[1] > underflow),
                   lambda cr: (cr[0] - 1, step(2 * cr[0] + 1, False)),
                   (i - 1, first))

    lane = lax.broadcasted_iota(jnp.int32, (tq, LANES), 1)
    outs = [jnp.where(lane < SB_DIM, acc_ref[pr, 0:tq, :], acc_ref[pr, tq:2 * tq, :]) for pr in range(n_pairs)]
    o = jnp.concatenate(outs, axis=1)
    o_ref[...] = _bf(_rms(o, g_ref[...]))


def _sb_attention(q, k, v, g, batch, seq):
    nq = seq // TQ_SB
    return pl.pallas_call(
        _sb_kernel,
        grid=(batch, nq),
        in_specs=[pl.BlockSpec((TQ_SB, SB_W), lambda b, i: (b * nq + i, 0)),
                  pl.BlockSpec((seq, SB_W), lambda b, i: (b, 0)),
                  pl.BlockSpec((seq, SB_W), lambda b, i: (b, 0)),
                  pl.BlockSpec((1, SB_W), lambda b, i: (0, 0))],
        out_specs=pl.BlockSpec((TQ_SB, SB_W), lambda b, i: (b * nq + i, 0)),
        out_shape=jax.ShapeDtypeStruct((batch * seq, SB_W), jnp.bfloat16),
        scratch_shapes=[pltpu.VMEM((SB_HEADS // 2, 2 * TQ_SB, TK_SB), jnp.float32),
                        pltpu.VMEM((SB_HEADS // 2, 2 * TQ_SB, LANES), jnp.float32),
                        pltpu.VMEM((SB_HEADS, 2 * TQ_SB, TK_SB), jnp.float32),
                        pltpu.VMEM((SB_HEADS, 2 * TQ_SB, 2 * TK_SB), jnp.float32)],
        compiler_params=_cparams(("parallel", "parallel")),
        name="sb_attention",
    )(q, k, v, g)


def _ca_kernel(q_ref, k_ref, v_ref, tab_ref, g_ref, o_ref, bias_ref):
    i = pl.program_id(1)
    t = TQ_CA
    start = pl.multiple_of(i * t, t)

    @pl.when((pl.program_id(0) == 0) & (i == 0))
    def _():
        r = lax.broadcasted_iota(jnp.int32, (t, CA_WIN), 0)
        c = lax.broadcasted_iota(jnp.int32, (t, CA_WIN), 1)
        lo = (r // CHUNK) * CHUNK
        band = (c >= lo) & (c < lo + CA_PAD + CHUNK)
        for h in range(CA_HEADS):
            tab = jnp.broadcast_to(tab_ref[h:h + 1, :], (t, REL_TAB))
            bias = pltpu.roll(tab, REL_TAB - (REL_CLIP - 1), 1, stride=1, stride_axis=0)[:, :CA_WIN]
            bias_ref[h] = jnp.where(band, bias * LOG2E, -jnp.inf)

    r2 = lax.broadcasted_iota(jnp.int32, (2 * t, LANES), 0)
    c2 = lax.broadcasted_iota(jnp.int32, (2 * t, LANES), 1)
    top_low = (r2 < t) == (c2 < CA_DIM)
    in_seq = lax.broadcasted_iota(jnp.int32, (2 * t, CA_WIN), 1) + i * t >= CA_PAD
    lane = lax.broadcasted_iota(jnp.int32, (t, LANES), 1)

    outs = []
    for pair in range(CA_HEADS // 2):
        qp = q_ref[:, pair * LANES:(pair + 1) * LANES]
        q2 = jnp.concatenate([qp, qp], axis=0)
        qm = jnp.where(top_low, q2, jnp.zeros_like(q2))
        kwin = k_ref[pl.ds(start, CA_WIN), pair * LANES:(pair + 1) * LANES]
        vwin = v_ref[pl.ds(start, CA_WIN), pair * LANES:(pair + 1) * LANES]
        bias = jnp.concatenate([bias_ref[2 * pair], bias_ref[2 * pair + 1]], axis=0)
        s = jnp.where(in_seq, _dot_nt(qm, kwin) + bias, -jnp.inf)
        m = jnp.max(s, axis=-1, keepdims=True)
        p = jnp.exp2(s - m)
        l = jnp.sum(p, axis=-1, keepdims=True)
        o2 = _dot(_bf(p), vwin) / l
        outs.append(jnp.where(lane < CA_DIM, o2[0:t], o2[t:2 * t]))
    o = jnp.concatenate(outs, axis=1)
    o_ref[...] = _bf(_rms(o, g_ref[...]))


def _ca_attention(q, kpad, vpad, tab, g, batch, seq):
    nq = seq // TQ_CA
    return pl.pallas_call(
        _ca_kernel,
        grid=(batch, nq),
        in_specs=[pl.BlockSpec((TQ_CA, CA_W), lambda b, i: (b * nq + i, 0)),
                  pl.BlockSpec((seq + CA_PAD, CA_W), lambda b, i: (b, 0)),
                  pl.BlockSpec((seq + CA_PAD, CA_W), lambda b, i: (b, 0)),
                  pl.BlockSpec((CA_HEADS, REL_TAB), lambda b, i: (0, 0)),
                  pl.BlockSpec((1, CA_W), lambda b, i: (0, 0))],
        out_specs=pl.BlockSpec((TQ_CA, CA_W), lambda b, i: (b * nq + i, 0)),
        out_shape=jax.ShapeDtypeStruct((batch * seq, CA_W), jnp.bfloat16),
        scratch_shapes=[pltpu.VMEM((CA_HEADS, TQ_CA, CA_WIN), jnp.float32)],
        compiler_params=_cparams(("arbitrary", "arbitrary")),
        name="ca_attention",
    )(q, kpad, vpad, tab, g)


def _outproj_router_kernel(ma_ref, mb_ref, mc_ref, wo_ref, x_ref, g_ref, wrh_ref, wrl_ref, br_ref,
                           xn_ref, h_ref, lpos_ref, meta_ref, runs_ref, tot_ref, base_ref):
    @pl.when(pl.program_id(0) == 0)
    def _():
        base_ref[...] = jnp.zeros_like(base_ref)

    base = base_ref[...]
    for j in range(ROUTER_TILES):
        base = _route_tile(j, base, ma_ref, mb_ref, mc_ref, wo_ref, x_ref, g_ref, wrh_ref, wrl_ref, br_ref,
                           xn_ref, h_ref, lpos_ref, meta_ref, runs_ref)
    base_ref[...] = base
    tot_ref[...] = base.astype(jnp.int32)


def _route_tile(j, base, ma_ref, mb_ref, mc_ref, wo_ref, x_ref, g_ref, wrh_ref, wrl_ref, br_ref,
                xn_ref, h_ref, lpos_ref, meta_ref, runs_ref):
    tm = TM_TOK
    rows = slice(j * tm, (j + 1) * tm)
    attn = (_dot(ma_ref[rows, :], wo_ref[0:A_W, :]) + _dot(mb_ref[rows, :], wo_ref[A_W:A_W + SB_W, :])
            + _dot(mc_ref[rows, :], wo_ref[A_W + SB_W:, :]))
    xn = x_ref[rows, :] + attn
    xn_ref[rows, :] = xn
    h = _rms(xn, g_ref[...])

    h_hi = _bf(h)
    h_ref[rows, :] = h_hi
    h_lo = _bf(h - h_hi.astype(jnp.float32))
    logits = (_dot(h_hi, wrh_ref[...]) + _dot(h_hi, wrl_ref[...]) + _dot(h_lo, wrh_ref[...])
              + br_ref[...])
    lane = lax.broadcasted_iota(jnp.int32, (tm, LANES), 1)
    lane_f = lane.astype(jnp.float32)

    work = logits
    vals, idxs, hots = [], [], []
    for _ in range(TOP_K):
        mx = jnp.max(work, axis=-1, keepdims=True)
        ix = jnp.min(jnp.where(work == mx, lane_f, float(LANES)), axis=-1, keepdims=True)
        hot = lane_f == ix
        work = jnp.where(hot, -jnp.inf, work)
        vals.append(mx)
        idxs.append(ix)
        hots.append(hot)
    exps = [jnp.exp(v - vals[0]) for v in vals]
    denom = exps[0] + exps[1] + exps[2] + exps[3]
    gates = [e / denom for e in exps]

    sel = jnp.zeros((tm, LANES), jnp.float32)
    for hot in hots:
        sel = sel + jnp.where(hot, 1.0, 0.0)
    r = lax.broadcasted_iota(jnp.int32, (tm, tm), 0)
    c = lax.broadcasted_iota(jnp.int32, (tm, tm), 1)
    before = jnp.where(c < r, 1.0, 0.0).astype(jnp.bfloat16)
    rank_in_tile = _dot(before, _bf(sel))

    cnt = jnp.sum(sel, axis=0, keepdims=True)
    cnt_al = jnp.ceil(cnt * (1.0 / RUN_ALIGN)) * RUN_ALIGN
    rl = lax.broadcasted_iota(jnp.int32, (LANES, LANES), 0)
    cl = lax.broadcasted_iota(jnp.int32, (LANES, LANES), 1)
    earlier = jnp.where(rl < cl, 1.0, 0.0).astype(jnp.bfloat16)
    loff = _dot(_bf(jnp.broadcast_to(cnt_al, (8, LANES))), earlier)[0:1, :]
    sub = lax.broadcasted_iota(jnp.int32, (8, LANES), 0)
    runs = jnp.where(sub == 0, loff, jnp.where(sub == 1, base, jnp.where(sub == 2, cnt_al, 0.0)))
    runs_ref[8 * j:8 * j + 8, :] = runs.astype(jnp.int32)

    lpos_dense = loff + rank_in_tile
    meta = jnp.zeros((tm, LANES), jnp.float32)
    for kk in range(TOP_K):
        lpos = jnp.sum(jnp.where(hots[kk], lpos_dense, 0.0), axis=-1, keepdims=True)
        meta = meta + jnp.where(lane == kk, gates[kk], 0.0) + jnp.where(lane == TOP_K + kk, lpos, 0.0)
    meta_ref[rows, :] = meta
    lpos_ref[:, rows] = jnp.transpose(meta)[TOP_K:TOP_K + 8, :].astype(jnp.int32)
    return base + cnt_al


def _outproj_router(ma, mb, mc, wo, x, g, wrh, wrl, br):
    n = x.shape[0]
    tm = TM_TOK * ROUTER_TILES
    row = lambda width: pl.BlockSpec((tm, width), lambda i: (i, 0))
    full = lambda a: pl.BlockSpec(a.shape, lambda i: (0,) * a.ndim)
    return pl.pallas_call(
        _outproj_router_kernel,
        grid=(n // tm,),
        in_specs=[row(A_W), row(SB_W), row(CA_W), full(wo), row(D_MODEL), full(g),
                  full(wrh), full(wrl), full(br)],
        out_specs=[row(D_MODEL), row(D_MODEL), pl.BlockSpec((8, tm), lambda i: (0, i)),
                   row(LANES), pl.BlockSpec((8 * ROUTER_TILES, LANES), lambda i: (i, 0)),
                   pl.BlockSpec((1, LANES), lambda i: (0, 0))],
        out_shape=[jax.ShapeDtypeStruct((n, D_MODEL), jnp.float32),
                   jax.ShapeDtypeStruct((n, D_MODEL), jnp.bfloat16),
                   jax.ShapeDtypeStruct((8, n), jnp.int32),
                   jax.ShapeDtypeStruct((n, LANES), jnp.float32),
                   jax.ShapeDtypeStruct((8 * (n // TM_TOK), LANES), jnp.int32),
                   jax.ShapeDtypeStruct((1, LANES), jnp.int32)],
        scratch_shapes=[pltpu.VMEM((1, LANES), jnp.float32)],
        compiler_params=_cparams(("arbitrary",)),
        name="outproj_router",
    )(ma, mb, mc, wo, x, g, wrh, wrl, br)


_RUN_CHUNKS = tuple(TM_TOK >> s for s in range(TM_TOK.bit_length()) if (TM_TOK >> s) >= RUN_ALIGN)


def _for_each_chunk(length, fn):
    for size in _RUN_CHUNKS:
        off = length & (~(2 * size - 1))

        @pl.when((length & size) != 0)
        def _(off=off, size=size):
            fn(off, size)


_WAIT_CHUNKS = tuple(1 << b for b in range((LOCAL_ROWS).bit_length() - 1, RUN_ALIGN.bit_length() - 2, -1))


def _wait_tile_runs(runs_ref, make_copy):
    total = runs_ref[0, N_EXPERTS - 1] + runs_ref[2, N_EXPERTS - 1]
    for size in _WAIT_CHUNKS:
        @pl.when((total & size) != 0)
        def _(size=size):
            make_copy(size).wait()


def _tile_runs(runs_ref, pstart_ref, fn):
    def body(e, _):
        local, glob, length = runs_ref[0, e], pstart_ref[e] + runs_ref[1, e], runs_ref[2, e]
        _for_each_chunk(length, lambda off, size: fn(pl.multiple_of(local + off, RUN_ALIGN),
                                                     pl.multiple_of(glob + off, RUN_ALIGN), size))
        return 0

    lax.fori_loop(0, N_EXPERTS, body, 0, unroll=2)


def _dispatch_kernel(pstart_ref, tot_ref, pad_ref, runs_ref, prev_runs_ref, lpos_ref, h_ref, xs_ref,
                     loc, zbuf, sem, fill_sem):
    i = pl.program_id(0)
    slot = i % 2

    def fill(start_or_wait):
        def body(e, _):
            first = pstart_ref[e] + tot_ref[e]
            _for_each_chunk(pad_ref[e] - tot_ref[e], lambda off, size: start_or_wait(pltpu.make_async_copy(
                zbuf.at[pl.ds(0, size)], xs_ref.at[pl.ds(pl.multiple_of(first + off, RUN_ALIGN), size)], fill_sem)))
            return 0

        lax.fori_loop(0, N_EXPERTS, body, 0)

        used = pstart_ref[N_EXPERTS - 1] + pad_ref[N_EXPERTS - 1]

        def tail(b, _):
            start_or_wait(pltpu.make_async_copy(
                zbuf, xs_ref.at[pl.ds(pl.multiple_of(b * TM_TOK, TM_TOK), TM_TOK)], fill_sem))
            return 0

        lax.fori_loop(used // TM_TOK, xs_ref.shape[0] // TM_TOK, tail, 0)

    @pl.when(i == 0)
    def _():
        zbuf[...] = jnp.zeros_like(zbuf)
        fill(lambda c: c.start())
        fill(lambda c: c.wait())

    r = lax.broadcasted_iota(jnp.int32, (LOCAL_ROWS, TM_TOK), 0)
    hit = r == lpos_ref[0:1, :]
    for kk in range(1, TOP_K):
        hit = hit | (r == lpos_ref[kk:kk + 1, :])
    loc[slot] = _pack_rows(_dot(jnp.where(hit, 1.0, 0.0).astype(jnp.bfloat16), h_ref[...]))

    def copy(buf, local, glob, size):
        return pltpu.make_async_copy(loc.at[buf, pl.ds(local, size)], xs_ref.at[pl.ds(glob, size)], sem.at[buf])

    _tile_runs(runs_ref, pstart_ref, lambda l, g, s: copy(slot, l, g, s).start())

    @pl.when(i > 0)
    def _():
        _wait_tile_runs(prev_runs_ref, lambda size: copy(1 - slot, 0, 0, size))

    @pl.when(i == pl.num_programs(0) - 1)
    def _():
        _wait_tile_runs(runs_ref, lambda size: copy(slot, 0, 0, size))


def _dispatch(pstart, totals, padded, runs, lpos, h, n_rows):
    n = h.shape[0]
    tm = TM_TOK
    grid_spec = pltpu.PrefetchScalarGridSpec(
        num_scalar_prefetch=3,
        grid=(n // tm,),
        in_specs=[pl.BlockSpec((8, LANES), lambda i, *_: (i, 0), memory_space=pltpu.SMEM),
                  pl.BlockSpec((8, LANES), lambda i, *_: (jnp.maximum(i - 1, 0), 0), memory_space=pltpu.SMEM),
                  pl.BlockSpec((8, tm), lambda i, *_: (0, i)),
                  pl.BlockSpec((tm, D_MODEL), lambda i, *_: (i, 0))],
        out_specs=pl.BlockSpec(memory_space=pl.ANY),
        scratch_shapes=[pltpu.VMEM((2, LOCAL_ROWS, HALF), jnp.uint32),
                        pltpu.VMEM((TM_TOK, HALF), jnp.uint32),
                        pltpu.SemaphoreType.DMA((2,)),
                        pltpu.SemaphoreType.DMA],
    )
    return pl.pallas_call(
        _dispatch_kernel,
        grid_spec=grid_spec,
        out_shape=jax.ShapeDtypeStruct((n_rows, HALF), jnp.uint32),
        compiler_params=_cparams(("arbitrary",)),
        name="dispatch",
    )(pstart, totals, padded, runs, runs, lpos, h)


def _expert_kernel(be_ref, bfirst_ref, bvalid_ref, xs_ref, wgu_ref, bgu_ref, wdn_ref, bdn_ref,
                   ys_ref, wgu_bf, wdn_bf):
    b = pl.program_id(0)

    @pl.when(bfirst_ref[b] == 1)
    def _():
        wgu_bf[...] = _bf(wgu_ref[0])
        wdn_bf[...] = _bf(wdn_ref[0])

    @pl.when(bvalid_ref[b] == 1)
    def _():
        xb = jnp.concatenate(_unpack_rows(xs_ref[...]), axis=1)
        gu = _dot(xb, wgu_bf[...]) + bgu_ref[0]
        gte = jnp.minimum(gu[:, :D_FF], SWIGLU_LIMIT)
        up = jnp.clip(gu[:, D_FF:], -SWIGLU_LIMIT, SWIGLU_LIMIT)
        act = (up + 1.0) * (gte * (1.0 / (1.0 + jnp.exp(-SWIGLU_ALPHA * gte))))
        y = _dot(_bf(act), wdn_bf[...]) + bdn_ref[0]
        ys_ref[...] = _pack_rows(_bf(y).astype(jnp.float32))

    @pl.when(bvalid_ref[b] == 0)
    def _():
        ys_ref[...] = jnp.zeros_like(ys_ref)


def _expert_ffn(blk_e, blk_first, blk_valid, xs, wgu, bgu, wdn, bdn):
    n_rows = xs.shape[0]
    tm = TM_EXP
    grid_spec = pltpu.PrefetchScalarGridSpec(
        num_scalar_prefetch=3,
        grid=(n_rows // tm,),
        in_specs=[pl.BlockSpec((tm, HALF), lambda b, e, f, v: (b, 0)),
                  pl.BlockSpec((1, D_MODEL, 2 * D_FF), lambda b, e, f, v: (e[b], 0, 0)),
                  pl.BlockSpec((1, 1, 2 * D_FF), lambda b, e, f, v: (e[b], 0, 0)),
                  pl.BlockSpec((1, D_FF, D_MODEL), lambda b, e, f, v: (e[b], 0, 0)),
                  pl.BlockSpec((1, 1, D_MODEL), lambda b, e, f, v: (e[b], 0, 0))],
        out_specs=pl.BlockSpec((tm, HALF), lambda b, e, f, v: (b, 0)),
        scratch_shapes=[pltpu.VMEM((D_MODEL, 2 * D_FF), jnp.bfloat16),
                        pltpu.VMEM((D_FF, D_MODEL), jnp.bfloat16)],
    )
    return pl.pallas_call(
        _expert_kernel,
        grid_spec=grid_spec,
        out_shape=jax.ShapeDtypeStruct((n_rows, HALF), jnp.uint32),
        compiler_params=_cparams(("arbitrary",)),
        name="expert_ffn",
    )(blk_e, blk_first, blk_valid, xs, wgu, bgu, wdn, bdn)


def _combine_kernel(final, pstart_ref, runs_ref, next_runs_ref, meta_ref, x_ref, ys_ref, gfin_ref, o_ref, loc, sem):
    i = pl.program_id(0)
    slot = i % 2

    def copy(buf, local, glob, size):
        return pltpu.make_async_copy(ys_ref.at[pl.ds(glob, size)], loc.at[buf, pl.ds(local, size)], sem.at[buf])

    @pl.when(i == 0)
    def _():
        loc[...] = jnp.zeros_like(loc)
        _tile_runs(runs_ref, pstart_ref, lambda l, g, s: copy(slot, l, g, s).start())

    @pl.when(i + 1 < pl.num_programs(0))
    def _():
        _tile_runs(next_runs_ref, pstart_ref, lambda l, g, s: copy(1 - slot, l, g, s).start())

    _wait_tile_runs(runs_ref, lambda size: copy(slot, 0, 0, size))

    meta = meta_ref[...]
    col = lax.broadcasted_iota(jnp.int32, (TM_TOK, LOCAL_ROWS), 1).astype(jnp.float32)
    wts = jnp.zeros((TM_TOK, LOCAL_ROWS), jnp.float32)
    for kk in range(TOP_K):
        wts = wts + jnp.where(col == meta[:, TOP_K + kk:TOP_K + kk + 1], meta[:, kk:kk + 1], 0.0)
    w_hi = _bf(wts)
    w_lo = _bf(wts - w_hi.astype(jnp.float32))
    w2 = jnp.concatenate([w_hi, w_lo], axis=0)
    y_lo, y_hi = _unpack_rows(loc[slot])
    r_lo, r_hi = _dot(w2, y_lo), _dot(w2, y_hi)
    out = x_ref[...] + jnp.concatenate([r_lo[:TM_TOK] + r_lo[TM_TOK:], r_hi[:TM_TOK] + r_hi[TM_TOK:]], axis=1)
    if final:
        out = _rms(out, gfin_ref[...])
    o_ref[...] = out


def _combine(pstart, runs, meta, x, ys, gfin, final):
    n = x.shape[0]
    tm = TM_TOK
    n_tiles = n // tm
    grid_spec = pltpu.PrefetchScalarGridSpec(
        num_scalar_prefetch=1,
        grid=(n_tiles,),
        in_specs=[pl.BlockSpec((8, LANES), lambda i, ps: (i, 0), memory_space=pltpu.SMEM),
                  pl.BlockSpec((8, LANES), lambda i, ps: (jnp.minimum(i + 1, n_tiles - 1), 0), memory_space=pltpu.SMEM),
                  pl.BlockSpec((tm, LANES), lambda i, ps: (i, 0)),
                  pl.BlockSpec((tm, D_MODEL), lambda i, ps: (i, 0)),
                  pl.BlockSpec(memory_space=pl.ANY),
                  pl.BlockSpec((1, D_MODEL), lambda i, ps: (0, 0))],
        out_specs=pl.BlockSpec((tm, D_MODEL), lambda i, ps: (i, 0)),
        scratch_shapes=[pltpu.VMEM((2, LOCAL_ROWS, HALF), jnp.uint32), pltpu.SemaphoreType.DMA((2,))],
    )
    return pl.pallas_call(
        functools.partial(_combine_kernel, final),
        grid_spec=grid_spec,
        out_shape=jax.ShapeDtypeStruct((n, D_MODEL), jnp.float32),
        compiler_params=_cparams(("arbitrary",)),
        name="combine_final" if final else "combine",
    )(pstart, runs, runs, meta, x, ys, gfin)


def _pad_heads(w, parts):
    rows = w.shape[0]
    per = w.shape[1] // MLA_HEADS
    w3 = w.reshape(rows, MLA_HEADS, per)
    cols = [jnp.zeros((rows, MLA_HEADS, b - a), w.dtype) if sign == 0 else sign * w3[:, :, a:b]
            for a, b, sign in parts]
    used = sum(b - a for a, b, _ in parts)
    cols.append(jnp.zeros((rows, MLA_HEADS, HEAD_PAD - used), w.dtype))
    return jnp.concatenate(cols, axis=2).reshape(rows, MLA_HEADS * HEAD_PAD)


def _layer_weights(w_in, w_uq, w_ukv, w_router, b_router, rel_bias):
    half = MLA_ROPE // 2
    cq, ckv, kpe, sb, ca = (w_in[:, 0:256], w_in[:, 256:384], w_in[:, 384:416],
                            w_in[:, 416:1184], w_in[:, 1184:1952])
    z = lambda width: jnp.zeros((D_MODEL, width), w_in.dtype)
    kpe_pad = jnp.concatenate([z(MLA_NOPE), kpe, z(HEAD_PAD - MLA_NOPE - MLA_ROPE)], axis=1)
    kpe_rot = jnp.concatenate([z(MLA_NOPE), -kpe[:, half:], kpe[:, :half],
                               z(HEAD_PAD - MLA_NOPE - MLA_ROPE)], axis=1)
    win2 = jnp.concatenate([cq, ckv, kpe_pad, kpe_rot, sb, ca], axis=1).astype(jnp.bfloat16)

    d = MLA_NOPE + MLA_ROPE
    wq_full = _pad_heads(w_uq, [(0, d, 1)])
    wq_rot = _pad_heads(w_uq, [(0, MLA_NOPE, 0), (MLA_NOPE + half, d, -1), (MLA_NOPE, MLA_NOPE + half, 1)])
    wq2 = jnp.concatenate([wq_full, wq_rot], axis=1).astype(jnp.bfloat16)

    wk = _pad_heads(w_ukv, [(0, MLA_NOPE, 1)])
    wv = w_ukv.reshape(MLA_KV_RANK, MLA_HEADS, MLA_NOPE + MLA_V)[:, :, MLA_NOPE:].reshape(MLA_KV_RANK, A_W)
    wkv2 = jnp.concatenate([wk, wv], axis=1).astype(jnp.bfloat16)

    wr = jnp.pad(w_router, ((0, 0), (0, LANES - N_EXPERTS)))
    wr_hi = wr.astype(jnp.bfloat16)
    wr_lo = (wr - wr_hi.astype(jnp.float32)).astype(jnp.bfloat16)
    br = jnp.pad(b_router, (0, LANES - N_EXPERTS), constant_values=NEG).reshape(1, LANES)

    ext = jnp.concatenate([rel_bias, jnp.broadcast_to(rel_bias[:, -1:], (CA_HEADS, REL_TAB - 2 * REL_CLIP - 1))], axis=1)
    tab = ext[:, ::-1]
    return win2, wq2, wkv2, wr_hi, wr_lo, br, tab


def kernel(x, positions, attn_norm, w_in, q_norm, w_uq, kv_norm, w_ukv, rel_bias, mix_norm,
           w_o, ffn_norm, w_router, b_router, w_gate_up, b_gate_up, w_down, b_down, final_norm):
    batch, seq, _ = x.shape
    n = batch * seq
    depth = w_in.shape[0]
    xf = x.reshape(n, D_MODEL)

    cos_t, sin_t = _rope_tables(positions, n)

    n_rows = n * TOP_K + (n // TM_TOK) * N_EXPERTS * (RUN_ALIGN - 1) + N_EXPERTS * TM_EXP
    n_rows = -(-n_rows // TM_EXP) * TM_EXP
    n_blk = n_rows // TM_EXP
    row2 = lambda v: v.reshape(1, -1)
    wgu_all = w_gate_up.reshape(depth * N_EXPERTS, D_MODEL, 2 * D_FF)
    bgu_all = b_gate_up.reshape(depth * N_EXPERTS, 1, 2 * D_FF)
    wdn_all = w_down.reshape(depth * N_EXPERTS, D_FF, D_MODEL)
    bdn_all = b_down.reshape(depth * N_EXPERTS, 1, D_MODEL)

    for l in range(depth):
        win2, wq2, wkv2, wr_hi, wr_lo, br, tab = _layer_weights(
            w_in[l], w_uq[l], w_ukv[l], w_router[l], b_router[l], rel_bias[l])
        qt, k, vt, sbq, sbk, sbv, caq, cak, cav = _inproj(
            xf, row2(attn_norm[l]), win2, cos_t, sin_t, row2(q_norm[l]), wq2, row2(kv_norm[l]), wkv2)

        g = mix_norm[l]
        ma = _mla_attention(qt, k, vt, row2(g[:A_W]), batch, seq)
        mb = _sb_attention(sbq, sbk, sbv, row2(g[A_W:A_W + SB_W]), batch, seq)
        pad = lambda a: jnp.pad(a.reshape(batch, seq, CA_W), ((0, 0), (CA_PAD, 0), (0, 0))).reshape(-1, CA_W)
        mc = _ca_attention(caq, pad(cak), pad(cav), tab, row2(g[A_W + SB_W:]), batch, seq)

        xn, h, lpos, meta, runs, totals = _outproj_router(
            ma, mb, mc, w_o[l].astype(jnp.bfloat16), xf, row2(ffn_norm[l]), wr_hi, wr_lo, br)

        sizes = totals[0, :N_EXPERTS]
        padded = (sizes + TM_EXP - 1) // TM_EXP * TM_EXP
        p_ends = jnp.cumsum(padded)
        p_starts = (p_ends - padded).astype(jnp.int32)
        blk_start = jnp.arange(n_blk, dtype=jnp.int32) * TM_EXP
        blk_valid = (blk_start < p_ends[-1]).astype(jnp.int32)
        last_row = jnp.minimum(blk_start, p_ends[-1] - 1)
        blk_e = jnp.minimum(jnp.sum((last_row[:, None] >= p_ends[None, :]).astype(jnp.int32), axis=1),
                            N_EXPERTS - 1)
        blk_first = jnp.concatenate([jnp.ones((1,), jnp.int32), (blk_e[1:] != blk_e[:-1]).astype(jnp.int32)])

        xs = _dispatch(p_starts, sizes, padded.astype(jnp.int32), runs, lpos, h, n_rows)
        ys = _expert_ffn(blk_e + l * N_EXPERTS, blk_first, blk_valid, xs, wgu_all, bgu_all, wdn_all, bdn_all)
        xf = _combine(p_starts, runs, meta, xn, ys, row2(final_norm), final=(l == depth - 1))

    return xf.reshape(batch, seq, D_MODEL)
```

```python
import functools
import math

import jax
import jax.numpy as jnp
from jax import lax
from jax.experimental import pallas as pl
from jax.experimental.pallas import tpu as pltpu

D_MODEL = 1024
RMS_EPS = 1e-6
MLA_NOPE, MLA_ROPE, MLA_V, MLA_HEADS = 64, 32, 64, 8
MLA_Q_RANK, MLA_KV_RANK = 256, 128
ROPE_THETA = 10000.0
SB_DIM, SB_HEADS = 64, 4
CA_DIM, CA_HEADS = 64, 4
CHUNK = 64
CA_LEFT_CHUNKS = 8
REL_CLIP = 256
N_EXPERTS, TOP_K = 32, 4
D_FF = 1024
SWIGLU_LIMIT, SWIGLU_ALPHA = 7.0, 1.702

LANES = 128
HEAD_PAD = 128
A_W = MLA_HEADS * MLA_V
SB_W = SB_HEADS * SB_DIM
CA_W = CA_HEADS * CA_DIM
CA_PAD = CA_LEFT_CHUNKS * CHUNK
CA_WIN = CA_PAD + 2 * CHUNK
REL_TAB = 1024
HALF = D_MODEL // 2

TM_TOK = 256
TQ_MLA = 256
TQ_SB = 256
TK_SB = 128
TQ_CA = 2 * CHUNK
ROUTER_TILES = 2
TM_EXP = 512
RUN_ALIGN = 8
LOCAL_ROWS = 1280
assert LOCAL_ROWS >= TM_TOK * TOP_K + N_EXPERTS * (RUN_ALIGN - 1) and LOCAL_ROWS % LANES == 0
assert TM_EXP % TM_TOK == 0 and TQ_MLA == TM_TOK

_C_CQ, _C_CKV, _C_KPE, _C_KROT = 0, 256, 384, 512
_C_SB, _C_CA, D_IN2 = 640, 1408, 2176

NEG = -1e30
LOG2E = math.log2(math.e)
VMEM_LIMIT = 56 * 1024 * 1024


def _rms(v, g):
    return v * lax.rsqrt(jnp.mean(v * v, axis=-1, keepdims=True) + RMS_EPS) * g


def _dot(a, b):
    return jnp.dot(a, b, preferred_element_type=jnp.float32)


def _dot_nt(a, b):
    return lax.dot_general(a, b, (((1,), (1,)), ((), ())), preferred_element_type=jnp.float32)


def _bf(v):
    return v.astype(jnp.bfloat16)


def _pack_rows(v):
    bits = lax.bitcast_convert_type(v, jnp.uint32)
    return (bits[:, :HALF] >> 16) | (bits[:, HALF:] & jnp.uint32(0xFFFF0000))


def _unpack_rows(u):
    lo = lax.bitcast_convert_type(u << 16, jnp.float32)
    hi = lax.bitcast_convert_type(u & jnp.uint32(0xFFFF0000), jnp.float32)
    return _bf(lo), _bf(hi)


def _cparams(sem):
    return pltpu.CompilerParams(dimension_semantics=sem, vmem_limit_bytes=VMEM_LIMIT)


ROPE_PACK = LANES // MLA_ROPE


def _rope_kernel(pos_ref, inv_ref, cos_ref, sin_ref):
    ang = pos_ref[...].astype(jnp.float32) * inv_ref[...]
    cos_ref[...] = jnp.cos(ang)
    sin_ref[...] = jnp.sin(ang)


def _rope_tables(positions, n):
    inv = ROPE_THETA ** (-jnp.arange(0, MLA_ROPE, 2, dtype=jnp.float32) / MLA_ROPE)
    inv_row = jnp.tile(jnp.concatenate([inv, inv]), ROPE_PACK).reshape(1, LANES)
    rows = n // ROPE_PACK
    pos_packed = jnp.repeat(positions.reshape(rows, ROPE_PACK), MLA_ROPE, axis=1)
    tm = min(TM_TOK, rows)
    cos_p, sin_p = pl.pallas_call(
        _rope_kernel,
        grid=(rows // tm,),
        in_specs=[pl.BlockSpec((tm, LANES), lambda i: (i, 0)),
                  pl.BlockSpec((1, LANES), lambda i: (0, 0))],
        out_specs=[pl.BlockSpec((tm, LANES), lambda i: (i, 0))] * 2,
        out_shape=[jax.ShapeDtypeStruct((rows, LANES), jnp.float32)] * 2,
        compiler_params=_cparams(("parallel",)),
        name="rope_tables",
    )(pos_packed, inv_row)
    pad = HEAD_PAD - MLA_NOPE - MLA_ROPE
    ones = lambda width: jnp.ones((n, width), jnp.float32)
    zeros = lambda width: jnp.zeros((n, width), jnp.float32)
    cos_t = jnp.concatenate([ones(MLA_NOPE), cos_p.reshape(n, MLA_ROPE), ones(pad)], axis=1)
    sin_t = jnp.concatenate([zeros(MLA_NOPE), sin_p.reshape(n, MLA_ROPE), zeros(pad)], axis=1)
    return cos_t, sin_t


def _inproj_kernel(x_ref, g_ref, win_ref, cos_ref, sin_ref, qn_ref, wq_ref, kvn_ref, wkv_ref,
                   q_ref, k_ref, v_ref, sbq_ref, sbk_ref, sbv_ref, caq_ref, cak_ref, cav_ref):
    h = _rms(x_ref[...], g_ref[...])
    proj = _dot(_bf(h), win_ref[...])
    cos128, sin128 = cos_ref[...], sin_ref[...]
    cos_h = jnp.concatenate([cos128] * MLA_HEADS, axis=1)
    sin_h = jnp.concatenate([sin128] * MLA_HEADS, axis=1)
    w = MLA_HEADS * HEAD_PAD

    cqn = _rms(proj[:, _C_CQ:_C_CKV], qn_ref[...])
    q2 = _dot(_bf(cqn), wq_ref[...])
    q = (q2[:, :w] * cos_h + q2[:, w:] * sin_h) * (LOG2E / math.sqrt(MLA_NOPE + MLA_ROPE))
    q_ref[0] = _bf(jnp.transpose(q))

    ckvn = _rms(proj[:, _C_CKV:_C_KPE], kvn_ref[...])
    kv2 = _dot(_bf(ckvn), wkv_ref[...])
    kpe = proj[:, _C_KPE:_C_KROT] * cos128 + proj[:, _C_KROT:_C_SB] * sin128
    k_ref[...] = _bf(kv2[:, :w] + jnp.concatenate([kpe] * MLA_HEADS, axis=1))
    v_ref[0] = _bf(jnp.transpose(kv2[:, w:]))

    sb_scale = 1.0 / math.sqrt(SB_DIM)
    sbq_ref[...] = _bf(proj[:, _C_SB:_C_SB + SB_W] * sb_scale)
    sbk_ref[...] = _bf(proj[:, _C_SB + SB_W:_C_SB + 2 * SB_W])
    sbv_ref[...] = _bf(proj[:, _C_SB + 2 * SB_W:_C_CA])
    ca_scale = LOG2E / math.sqrt(CA_DIM)
    caq_ref[...] = _bf(proj[:, _C_CA:_C_CA + CA_W] * ca_scale)
    cak_ref[...] = _bf(proj[:, _C_CA + CA_W:_C_CA + 2 * CA_W])
    cav_ref[...] = _bf(proj[:, _C_CA + 2 * CA_W:D_IN2])


def _inproj(x, g, win2, cos_t, sin_t, qn, wq2, kvn, wkv2):
    n = x.shape[0]
    tm = TM_TOK
    row = lambda width: pl.BlockSpec((tm, width), lambda i: (i, 0))
    full = lambda a: pl.BlockSpec(a.shape, lambda i: (0,) * a.ndim)
    widths = [MLA_HEADS * HEAD_PAD] + [SB_W] * 3 + [CA_W] * 3
    tile_t = lambda rows: pl.BlockSpec((1, rows, tm), lambda i: (i, 0, 0))
    shape_t = lambda rows: jax.ShapeDtypeStruct((n // tm, rows, tm), jnp.bfloat16)
    return pl.pallas_call(
        _inproj_kernel,
        grid=(n // tm,),
        in_specs=[row(D_MODEL), full(g), full(win2), row(LANES), row(LANES),
                  full(qn), full(wq2), full(kvn), full(wkv2)],
        out_specs=[tile_t(MLA_HEADS * HEAD_PAD), row(widths[0]), tile_t(A_W)] + [row(wd) for wd in widths[1:]],
        out_shape=[shape_t(MLA_HEADS * HEAD_PAD), jax.ShapeDtypeStruct((n, widths[0]), jnp.bfloat16), shape_t(A_W)]
                  + [jax.ShapeDtypeStruct((n, wd), jnp.bfloat16) for wd in widths[1:]],
        compiler_params=_cparams(("parallel",)),
        name="inproj",
    )(x, g, win2, cos_t, sin_t, qn, wq2, kvn, wkv2)


def _mla_kernel(qt_ref, k_ref, vt_ref, g_ref, o_ref, m_ref, l_ref, acc_ref, s_ref):
    i = pl.program_id(1)
    tq = TQ_MLA
    n_pairs = MLA_HEADS // 2
    key_chunk = lax.broadcasted_iota(jnp.int32, (tq, tq), 0) // CHUNK
    qry_chunk = lax.broadcasted_iota(jnp.int32, (tq, tq), 1) // CHUNK
    diag_ok = key_chunk <= qry_chunk
    top = lax.broadcasted_iota(jnp.int32, (LANES, tq), 0) < MLA_V

    m_ref[...] = jnp.full(m_ref.shape, -jnp.inf, jnp.float32)
    l_ref[...] = jnp.zeros(l_ref.shape, jnp.float32)
    acc_ref[...] = jnp.zeros(acc_ref.shape, jnp.float32)

    def scores(kb, slot):
        start = pl.multiple_of(kb * tq, tq)
        for h in range(MLA_HEADS):
            qt = qt_ref[0, h * HEAD_PAD:(h + 1) * HEAD_PAD, :]
            kblk = k_ref[pl.ds(start, tq), h * HEAD_PAD:(h + 1) * HEAD_PAD]
            s_ref[slot, h] = _dot(kblk, qt)

    def absorb(kb, slot, mask):
        for pr in range(n_pairs):
            vt = vt_ref[kb, pr * LANES:(pr + 1) * LANES, :]
            zero = jnp.zeros_like(vt)
            vt_bd = jnp.concatenate([jnp.where(top, vt, zero), jnp.where(top, zero, vt)], axis=1)
            pts, alphas = [], []
            for h in (2 * pr, 2 * pr + 1):
                s = s_ref[slot, h]
                if mask is not None:
                    s = jnp.where(mask, s, -jnp.inf)
                m_old = m_ref[h]
                m_new = jnp.maximum(m_old, jnp.max(s, axis=0, keepdims=True))
                p = jnp.exp2(s - m_new)
                alpha = jnp.exp2(m_old - m_new)
                l_ref[h] = alpha * l_ref[h] + jnp.sum(p, axis=0, keepdims=True)
                m_ref[h] = m_new
                pts.append(_bf(p))
                alphas.append(alpha)
            alpha_pair = jnp.where(top, alphas[0], alphas[1])
            acc_ref[pr] = alpha_pair * acc_ref[pr] + _dot(vt_bd, jnp.concatenate(pts, axis=0))

    def body(j, carry):
        scores(2 * j + 1, 1)
        absorb(2 * j, 0, None)
        scores(2 * j + 2, 0)
        absorb(2 * j + 1, 1, None)
        return carry

    scores(0, 0)
    lax.fori_loop(0, i // 2, body, 0)

    @pl.when(i % 2 == 0)
    def _():
        absorb(i, 0, diag_ok)

    @pl.when(i % 2 == 1)
    def _():
        scores(i, 1)
        absorb(i - 1, 0, None)
        absorb(i, 1, diag_ok)

    outs = [jnp.transpose(acc_ref[pr] / jnp.where(top, l_ref[2 * pr], l_ref[2 * pr + 1])) for pr in range(n_pairs)]
    o = jnp.concatenate(outs, axis=1)
    o_ref[...] = _bf(_rms(o, g_ref[...]))


def _mla_attention(qt, k, vt, g, batch, seq):
    nq = seq // TQ_MLA
    w = MLA_HEADS * HEAD_PAD
    return pl.pallas_call(
        _mla_kernel,
        grid=(batch, nq),
        in_specs=[pl.BlockSpec((1, w, TQ_MLA), lambda b, i: (b * nq + i, 0, 0)),
                  pl.BlockSpec((seq, w), lambda b, i: (b, 0)),
                  pl.BlockSpec((nq, A_W, TQ_MLA), lambda b, i: (b, 0, 0)),
                  pl.BlockSpec((1, A_W), lambda b, i: (0, 0))],
        out_specs=pl.BlockSpec((TQ_MLA, A_W), lambda b, i: (b * nq + i, 0)),
        out_shape=jax.ShapeDtypeStruct((batch * seq, A_W), jnp.bfloat16),
        scratch_shapes=[pltpu.VMEM((MLA_HEADS, 1, TQ_MLA), jnp.float32),
                        pltpu.VMEM((MLA_HEADS, 1, TQ_MLA), jnp.float32),
                        pltpu.VMEM((MLA_HEADS // 2, LANES, TQ_MLA), jnp.float32),
                        pltpu.VMEM((2, MLA_HEADS, TQ_MLA, TQ_MLA), jnp.float32)],
        compiler_params=_cparams(("parallel", "parallel")),
        name="mla_attention",
    )(qt, k, vt, g)


def _sb_kernel(q_ref, k_ref, v_ref, g_ref, o_ref, run_ref, acc_ref, zl_ref, sums_ref):
    i = pl.program_id(1)
    tq, tk = TQ_SB, TK_SB
    n_pairs = SB_HEADS // 2
    r = lax.broadcasted_iota(jnp.int32, (2 * tq, tk), 0)
    c = lax.broadcasted_iota(jnp.int32, (2 * tq, tk), 1)
    q_off = jnp.where(r >= tq, r - tq, r)
    top_low = (r < tq) == (c < SB_DIM)
    r2 = lax.broadcasted_iota(jnp.int32, (tk, 2 * tk), 0)
    c2 = lax.broadcasted_iota(jnp.int32, (tk, 2 * tk), 1)
    sum_mat = jnp.where((c2 >= tk) | (r2 > c2), 1.0, 0.0).astype(jnp.bfloat16)

    run_ref[...] = jnp.zeros(run_ref.shape, jnp.float32)
    acc_ref[...] = jnp.zeros(acc_ref.shape, jnp.float32)

    def step(kb_hi, masked):
        masks = []
        for d in range(2):
            kb = kb_hi - d
            start = pl.multiple_of(kb * tk, tk)
            mask = (c + (kb * tk - i * tq)) < q_off if masked else None
            masks.append(mask)
            for pr in range(n_pairs):
                qp = q_ref[:, pr * LANES:(pr + 1) * LANES]
                q2 = jnp.concatenate([qp, qp], axis=0)
                qm = jnp.where(top_low, q2, jnp.zeros_like(q2))
                kblk = k_ref[pl.ds(start, tk), pr * LANES:(pr + 1) * LANES]
                z = _dot_nt(qm, kblk)
                log_keep = -(jnp.maximum(z, 0.0) + jnp.log(1.0 + jnp.exp(-jnp.abs(z))))
                if masked:
                    log_keep = jnp.where(mask, log_keep, 0.0)
                hi = _bf(log_keep)
                lo = _bf(log_keep - hi.astype(jnp.float32))
                sums_ref[2 * d + pr] = _dot(hi, sum_mat) + _dot(lo, sum_mat)
                zl_ref[2 * d + pr] = z + log_keep
        slowest = None
        for d in range(2):
            start = pl.multiple_of((kb_hi - d) * tk, tk)
            for pr in range(n_pairs):
                vblk = v_ref[pl.ds(start, tk), pr * LANES:(pr + 1) * LANES]
                run = run_ref[pr]
                a = jnp.exp(zl_ref[2 * d + pr] + run + sums_ref[2 * d + pr, :, 0:tk])
                if masked:
                    a = jnp.where(masks[d], a, 0.0)
                acc_ref[pr] = acc_ref[pr] + _dot(_bf(a), vblk)
                run = run + sums_ref[2 * d + pr, :, tk:2 * tk]
                run_ref[pr] = run
                if d == 1:
                    top = jnp.max(run)
                    slowest = top if slowest is None else jnp.maximum(slowest, top)
        return slowest

    underflow = -104.0
    assert tq == 2 * tk
    first = step(2 * i + 1, True)
    lax.while_loop(lambda cr: (cr[0] >= 0) & (cr[1] > underflow),
                   lambda cr: (cr[0] - 1, step(2 * cr[0] + 1, False)),
                   (i - 1, first))

    lane = lax.broadcasted_iota(jnp.int32, (tq, LANES), 1)
    outs = [jnp.where(lane < SB_DIM, acc_ref[pr, 0:tq, :], acc_ref[pr, tq:2 * tq, :]) for pr in range(n_pairs)]
    o = jnp.concatenate(outs, axis=1)
    o_ref[...] = _bf(_rms(o, g_ref[...]))


def _sb_attention(q, k, v, g, batch, seq):
    nq = seq // TQ_SB
    return pl.pallas_call(
        _sb_kernel,
        grid=(batch, nq),
        in_specs=[pl.BlockSpec((TQ_SB, SB_W), lambda b, i: (b * nq + i, 0)),
                  pl.BlockSpec((seq, SB_W), lambda b, i: (b, 0)),
                  pl.BlockSpec((seq, SB_W), lambda b, i: (b, 0)),
                  pl.BlockSpec((1, SB_W), lambda b, i: (0, 0))],
        out_specs=pl.BlockSpec((TQ_SB, SB_W), lambda b, i: (b * nq + i, 0)),
        out_shape=jax.ShapeDtypeStruct((batch * seq, SB_W), jnp.bfloat16),
        scratch_shapes=[pltpu.VMEM((SB_HEADS // 2, 2 * TQ_SB, TK_SB), jnp.float32),
                        pltpu.VMEM((SB_HEADS // 2, 2 * TQ_SB, LANES), jnp.float32),
                        pltpu.VMEM((SB_HEADS, 2 * TQ_SB, TK_SB), jnp.float32),
                        pltpu.VMEM((SB_HEADS, 2 * TQ_SB, 2 * TK_SB), jnp.float32)],
        compiler_params=_cparams(("parallel", "parallel")),
        name="sb_attention",
    )(q, k, v, g)


def _ca_kernel(q_ref, k_ref, v_ref, tab_ref, g_ref, o_ref, bias_ref):
    i = pl.program_id(1)
    t = TQ_CA
    start = pl.multiple_of(i * t, t)

    @pl.when((pl.program_id(0) == 0) & (i == 0))
    def _():
        r = lax.broadcasted_iota(jnp.int32, (t, CA_WIN), 0)
        c = lax.broadcasted_iota(jnp.int32, (t, CA_WIN), 1)
        lo = (r // CHUNK) * CHUNK
        band = (c >= lo) & (c < lo + CA_PAD + CHUNK)
        for h in range(CA_HEADS):
            tab = jnp.broadcast_to(tab_ref[h:h + 1, :], (t, REL_TAB))
            bias = pltpu.roll(tab, REL_TAB - (REL_CLIP - 1), 1, stride=1, stride_axis=0)[:, :CA_WIN]
            bias_ref[h] = jnp.where(band, bias * LOG2E, -jnp.inf)

    r2 = lax.broadcasted_iota(jnp.int32, (2 * t, LANES), 0)
    c2 = lax.broadcasted_iota(jnp.int32, (2 * t, LANES), 1)
    top_low = (r2 < t) == (c2 < CA_DIM)
    in_seq = lax.broadcasted_iota(jnp.int32, (2 * t, CA_WIN), 1) + i * t >= CA_PAD
    lane = lax.broadcasted_iota(jnp.int32, (t, LANES), 1)

    outs = []
    for pair in range(CA_HEADS // 2):
        qp = q_ref[:, pair * LANES:(pair + 1) * LANES]
        q2 = jnp.concatenate([qp, qp], axis=0)
        qm = jnp.where(top_low, q2, jnp.zeros_like(q2))
        kwin = k_ref[pl.ds(start, CA_WIN), pair * LANES:(pair + 1) * LANES]
        vwin = v_ref[pl.ds(start, CA_WIN), pair * LANES:(pair + 1) * LANES]
        bias = jnp.concatenate([bias_ref[2 * pair], bias_ref[2 * pair + 1]], axis=0)
        s = jnp.where(in_seq, _dot_nt(qm, kwin) + bias, -jnp.inf)
        m = jnp.max(s, axis=-1, keepdims=True)
        p = jnp.exp2(s - m)
        l = jnp.sum(p, axis=-1, keepdims=True)
        o2 = _dot(_bf(p), vwin) / l
        outs.append(jnp.where(lane < CA_DIM, o2[0:t], o2[t:2 * t]))
    o = jnp.concatenate(outs, axis=1)
    o_ref[...] = _bf(_rms(o, g_ref[...]))


def _ca_attention(q, kpad, vpad, tab, g, batch, seq):
    nq = seq // TQ_CA
    return pl.pallas_call(
        _ca_kernel,
        grid=(batch, nq),
        in_specs=[pl.BlockSpec((TQ_CA, CA_W), lambda b, i: (b * nq + i, 0)),
                  pl.BlockSpec((seq + CA_PAD, CA_W), lambda b, i: (b, 0)),
                  pl.BlockSpec((seq + CA_PAD, CA_W), lambda b, i: (b, 0)),
                  pl.BlockSpec((CA_HEADS, REL_TAB), lambda b, i: (0, 0)),
                  pl.BlockSpec((1, CA_W), lambda b, i: (0, 0))],
        out_specs=pl.BlockSpec((TQ_CA, CA_W), lambda b, i: (b * nq + i, 0)),
        out_shape=jax.ShapeDtypeStruct((batch * seq, CA_W), jnp.bfloat16),
        scratch_shapes=[pltpu.VMEM((CA_HEADS, TQ_CA, CA_WIN), jnp.float32)],
        compiler_params=_cparams(("arbitrary", "arbitrary")),
        name="ca_attention",
    )(q, kpad, vpad, tab, g)


def _outproj_router_kernel(ma_ref, mb_ref, mc_ref, wo_ref, x_ref, g_ref, wr2_ref, br_ref,
                           xn_ref, h_ref, lpos_ref, meta_ref, runs_ref, tot_ref, base_ref):
    @pl.when(pl.program_id(0) == 0)
    def _():
        base_ref[...] = jnp.zeros_like(base_ref)

    base = base_ref[...]
    for j in range(ROUTER_TILES):
        base = _route_tile(j, base, ma_ref, mb_ref, mc_ref, wo_ref, x_ref, g_ref, wr2_ref, br_ref,
                           xn_ref, h_ref, lpos_ref, meta_ref, runs_ref)
    base_ref[...] = base
    tot_ref[...] = base.astype(jnp.int32)


def _route_tile(j, base, ma_ref, mb_ref, mc_ref, wo_ref, x_ref, g_ref, wr2_ref, br_ref,
                xn_ref, h_ref, lpos_ref, meta_ref, runs_ref):
    tm = TM_TOK
    rows = slice(j * tm, (j + 1) * tm)
    attn = (_dot(ma_ref[rows, :], wo_ref[0:A_W, :]) + _dot(mb_ref[rows, :], wo_ref[A_W:A_W + SB_W, :])
            + _dot(mc_ref[rows, :], wo_ref[A_W + SB_W:, :]))
    xn = x_ref[rows, :] + attn
    xn_ref[rows, :] = xn
    h = _rms(xn, g_ref[...])

    h_hi = _bf(h)
    h_ref[rows, :] = h_hi
    h_lo = _bf(h - h_hi.astype(jnp.float32))
    both = _dot(h_hi, wr2_ref[...])
    logits = (both[:, :LANES] + both[:, LANES:] + _dot(h_lo, wr2_ref[:, 0:LANES])
              + br_ref[...])
    lane = lax.broadcasted_iota(jnp.int32, (tm, LANES), 1)
    lane_f = lane.astype(jnp.float32)

    work = logits
    vals, idxs, hots = [], [], []
    for _ in range(TOP_K):
        mx = jnp.max(work, axis=-1, keepdims=True)
        ix = jnp.min(jnp.where(work == mx, lane_f, float(LANES)), axis=-1, keepdims=True)
        hot = lane_f == ix
        work = jnp.where(hot, -jnp.inf, work)
        vals.append(mx)
        idxs.append(ix)
        hots.append(hot)
    exps = [jnp.exp(v - vals[0]) for v in vals]
    denom = exps[0] + exps[1] + exps[2] + exps[3]
    gates = [e / denom for e in exps]

    sel = jnp.zeros((tm, LANES), jnp.float32)
    for hot in hots:
        sel = sel + jnp.where(hot, 1.0, 0.0)
    r = lax.broadcasted_iota(jnp.int32, (tm, tm), 0)
    c = lax.broadcasted_iota(jnp.int32, (tm, tm), 1)
    before = jnp.where(c < r, 1.0, 0.0).astype(jnp.bfloat16)
    rank_in_tile = _dot(before, _bf(sel))

    cnt = jnp.sum(sel, axis=0, keepdims=True)
    cnt_al = jnp.ceil(cnt * (1.0 / RUN_ALIGN)) * RUN_ALIGN
    rl = lax.broadcasted_iota(jnp.int32, (LANES, LANES), 0)
    cl = lax.broadcasted_iota(jnp.int32, (LANES, LANES), 1)
    earlier = jnp.where(rl < cl, 1.0, 0.0).astype(jnp.bfloat16)
    loff = _dot(_bf(jnp.broadcast_to(cnt_al, (8, LANES))), earlier)[0:1, :]
    sub = lax.broadcasted_iota(jnp.int32, (8, LANES), 0)
    has_long = jnp.where(jnp.max(cnt_al, axis=-1, keepdims=True) >= LONG_RUN, 1.0, 0.0)
    runs = jnp.where(sub == 0, loff, jnp.where(sub == 1, base, jnp.where(sub == 2, cnt_al,
                     jnp.where(sub == 3, has_long, 0.0))))
    runs_ref[8 * j:8 * j + 8, :] = runs.astype(jnp.int32)

    lpos_dense = loff + rank_in_tile
    meta = jnp.zeros((tm, LANES), jnp.float32)
    for kk in range(TOP_K):
        lpos = jnp.sum(jnp.where(hots[kk], lpos_dense, 0.0), axis=-1, keepdims=True)
        meta = meta + jnp.where(lane == kk, gates[kk], 0.0) + jnp.where(lane == TOP_K + kk, lpos, 0.0)
    meta_ref[rows, :] = meta
    lpos_ref[:, rows] = jnp.transpose(meta)[TOP_K:TOP_K + 8, :].astype(jnp.int32)
    return base + cnt_al


def _outproj_router(ma, mb, mc, wo, x, g, wr2, br):
    n = x.shape[0]
    tm = TM_TOK * ROUTER_TILES
    row = lambda width: pl.BlockSpec((tm, width), lambda i: (i, 0))
    full = lambda a: pl.BlockSpec(a.shape, lambda i: (0,) * a.ndim)
    return pl.pallas_call(
        _outproj_router_kernel,
        grid=(n // tm,),
        in_specs=[row(A_W), row(SB_W), row(CA_W), full(wo), row(D_MODEL), full(g), full(wr2), full(br)],
        out_specs=[row(D_MODEL), row(D_MODEL), pl.BlockSpec((8, tm), lambda i: (0, i)),
                   row(LANES), pl.BlockSpec((8 * ROUTER_TILES, LANES), lambda i: (i, 0)),
                   pl.BlockSpec((1, LANES), lambda i: (0, 0))],
        out_shape=[jax.ShapeDtypeStruct((n, D_MODEL), jnp.float32),
                   jax.ShapeDtypeStruct((n, D_MODEL), jnp.bfloat16),
                   jax.ShapeDtypeStruct((8, n), jnp.int32),
                   jax.ShapeDtypeStruct((n, LANES), jnp.float32),
                   jax.ShapeDtypeStruct((8 * (n // TM_TOK), LANES), jnp.int32),
                   jax.ShapeDtypeStruct((1, LANES), jnp.int32)],
        scratch_shapes=[pltpu.VMEM((1, LANES), jnp.float32)],
        compiler_params=_cparams(("arbitrary",)),
        name="outproj_router",
    )(ma, mb, mc, wo, x, g, wr2, br)


_RUN_CHUNKS = tuple(TM_TOK >> s for s in range(TM_TOK.bit_length()) if (TM_TOK >> s) >= RUN_ALIGN)


LONG_RUN = 64


def _for_each_chunk(length, fn, sizes=_RUN_CHUNKS):
    for size in sizes:
        off = length & (~(2 * size - 1))

        @pl.when((length & size) != 0)
        def _(off=off, size=size):
            fn(off, size)


_WAIT_CHUNKS = tuple(1 << b for b in range((LOCAL_ROWS).bit_length() - 1, RUN_ALIGN.bit_length() - 2, -1))


def _wait_tile_runs(runs_ref, make_copy):
    total = runs_ref[0, N_EXPERTS - 1] + runs_ref[2, N_EXPERTS - 1]
    for size in _WAIT_CHUNKS:
        @pl.when((total & size) != 0)
        def _(size=size):
            make_copy(size).wait()


def _tile_runs(runs_ref, pstart_ref, fn):
    def experts(sizes, unroll):
        def body(e, _):
            local, glob, length = runs_ref[0, e], pstart_ref[e] + runs_ref[1, e], runs_ref[2, e]
            _for_each_chunk(length, lambda off, size: fn(pl.multiple_of(local + off, RUN_ALIGN),
                                                         pl.multiple_of(glob + off, RUN_ALIGN), size), sizes)
            return 0

        lax.fori_loop(0, N_EXPERTS, body, 0, unroll=unroll)

    experts(tuple(sz for sz in _RUN_CHUNKS if sz < LONG_RUN), 2)

    @pl.when(runs_ref[3, 0] != 0)
    def _():
        experts(tuple(sz for sz in _RUN_CHUNKS if sz >= LONG_RUN), 1)


def _dispatch_kernel(pstart_ref, tot_ref, pad_ref, runs_ref, prev_runs_ref, lpos_ref, h_ref, xs_ref,
                     loc, zbuf, sem, fill_sem):
    i = pl.program_id(0)
    slot = i % 2

    def fill(start_or_wait):
        def body(e, _):
            first = pstart_ref[e] + tot_ref[e]
            _for_each_chunk(pad_ref[e] - tot_ref[e], lambda off, size: start_or_wait(pltpu.make_async_copy(
                zbuf.at[pl.ds(0, size)], xs_ref.at[pl.ds(pl.multiple_of(first + off, RUN_ALIGN), size)], fill_sem)))
            return 0

        lax.fori_loop(0, N_EXPERTS, body, 0)

        used = pstart_ref[N_EXPERTS - 1] + pad_ref[N_EXPERTS - 1]

        def tail(b, _):
            start_or_wait(pltpu.make_async_copy(
                zbuf, xs_ref.at[pl.ds(pl.multiple_of(b * TM_TOK, TM_TOK), TM_TOK)], fill_sem))
            return 0

        lax.fori_loop(used // TM_TOK, xs_ref.shape[0] // TM_TOK, tail, 0)

    @pl.when(i == 0)
    def _():
        zbuf[...] = jnp.zeros_like(zbuf)
        fill(lambda c: c.start())
        fill(lambda c: c.wait())

    r = lax.broadcasted_iota(jnp.int32, (LOCAL_ROWS, TM_TOK), 0)
    hit = r == lpos_ref[0:1, :]
    for kk in range(1, TOP_K):
        hit = hit | (r == lpos_ref[kk:kk + 1, :])
    loc[slot] = _pack_rows(_dot(jnp.where(hit, 1.0, 0.0).astype(jnp.bfloat16), h_ref[...]))

    def copy(buf, local, glob, size):
        return pltpu.make_async_copy(loc.at[buf, pl.ds(local, size)], xs_ref.at[pl.ds(glob, size)], sem.at[buf])

    _tile_runs(runs_ref, pstart_ref, lambda l, g, s: copy(slot, l, g, s).start())

    @pl.when(i > 0)
    def _():
        _wait_tile_runs(prev_runs_ref, lambda size: copy(1 - slot, 0, 0, size))

    @pl.when(i == pl.num_programs(0) - 1)
    def _():
        _wait_tile_runs(runs_ref, lambda size: copy(slot, 0, 0, size))


def _dispatch(pstart, totals, padded, runs, lpos, h, n_rows):
    n = h.shape[0]
    tm = TM_TOK
    grid_spec = pltpu.PrefetchScalarGridSpec(
        num_scalar_prefetch=3,
        grid=(n // tm,),
        in_specs=[pl.BlockSpec((8, LANES), lambda i, *_: (i, 0), memory_space=pltpu.SMEM),
                  pl.BlockSpec((8, LANES), lambda i, *_: (jnp.maximum(i - 1, 0), 0), memory_space=pltpu.SMEM),
                  pl.BlockSpec((8, tm), lambda i, *_: (0, i)),
                  pl.BlockSpec((tm, D_MODEL), lambda i, *_: (i, 0))],
        out_specs=pl.BlockSpec(memory_space=pl.ANY),
        scratch_shapes=[pltpu.VMEM((2, LOCAL_ROWS, HALF), jnp.uint32),
                        pltpu.VMEM((TM_TOK, HALF), jnp.uint32),
                        pltpu.SemaphoreType.DMA((2,)),
                        pltpu.SemaphoreType.DMA],
    )
    return pl.pallas_call(
        _dispatch_kernel,
        grid_spec=grid_spec,
        out_shape=jax.ShapeDtypeStruct((n_rows, HALF), jnp.uint32),
        compiler_params=_cparams(("arbitrary",)),
        name="dispatch",
    )(pstart, totals, padded, runs, runs, lpos, h)


def _expert_kernel(be_ref, bfirst_ref, bvalid_ref, xs_ref, wgu_ref, bgu_ref, wdn_ref, bdn_ref,
                   ys_ref, wgu_bf, wdn_bf):
    b = pl.program_id(0)

    @pl.when(bfirst_ref[b] == 1)
    def _():
        wgu_bf[...] = _bf(wgu_ref[0])
        wdn_bf[...] = _bf(wdn_ref[0])

    @pl.when(bvalid_ref[b] == 1)
    def _():
        xb = jnp.concatenate(_unpack_rows(xs_ref[...]), axis=1)
        gu = _dot(xb, wgu_bf[...]) + bgu_ref[0]
        gte = jnp.minimum(gu[:, :D_FF], SWIGLU_LIMIT)
        up = jnp.clip(gu[:, D_FF:], -SWIGLU_LIMIT, SWIGLU_LIMIT)
        act = (up + 1.0) * (gte * (1.0 / (1.0 + jnp.exp(-SWIGLU_ALPHA * gte))))
        y = _dot(_bf(act), wdn_bf[...]) + bdn_ref[0]
        ys_ref[...] = _pack_rows(_bf(y).astype(jnp.float32))

    @pl.when(bvalid_ref[b] == 0)
    def _():
        ys_ref[...] = jnp.zeros_like(ys_ref)


def _expert_ffn(blk_e, blk_first, blk_valid, xs, wgu, bgu, wdn, bdn):
    n_rows = xs.shape[0]
    tm = TM_EXP
    grid_spec = pltpu.PrefetchScalarGridSpec(
        num_scalar_prefetch=3,
        grid=(n_rows // tm,),
        in_specs=[pl.BlockSpec((tm, HALF), lambda b, e, f, v: (b, 0)),
                  pl.BlockSpec((1, D_MODEL, 2 * D_FF), lambda b, e, f, v: (e[b], 0, 0)),
                  pl.BlockSpec((1, 1, 2 * D_FF), lambda b, e, f, v: (e[b], 0, 0)),
                  pl.BlockSpec((1, D_FF, D_MODEL), lambda b, e, f, v: (e[b], 0, 0)),
                  pl.BlockSpec((1, 1, D_MODEL), lambda b, e, f, v: (e[b], 0, 0))],
        out_specs=pl.BlockSpec((tm, HALF), lambda b, e, f, v: (b, 0)),
        scratch_shapes=[pltpu.VMEM((D_MODEL, 2 * D_FF), jnp.bfloat16),
                        pltpu.VMEM((D_FF, D_MODEL), jnp.bfloat16)],
    )
    return pl.pallas_call(
        _expert_kernel,
        grid_spec=grid_spec,
        out_shape=jax.ShapeDtypeStruct((n_rows, HALF), jnp.uint32),
        compiler_params=_cparams(("arbitrary",)),
        name="expert_ffn",
    )(blk_e, blk_first, blk_valid, xs, wgu, bgu, wdn, bdn)


def _combine_kernel(final, pstart_ref, runs_ref, next_runs_ref, meta_ref, x_ref, ys_ref, gfin_ref, o_ref, loc, sem):
    i = pl.program_id(0)
    slot = i % 2

    def copy(buf, local, glob, size):
        return pltpu.make_async_copy(ys_ref.at[pl.ds(glob, size)], loc.at[buf, pl.ds(local, size)], sem.at[buf])

    @pl.when(i == 0)
    def _():
        loc[...] = jnp.zeros_like(loc)
        _tile_runs(runs_ref, pstart_ref, lambda l, g, s: copy(slot, l, g, s).start())

    @pl.when(i + 1 < pl.num_programs(0))
    def _():
        _tile_runs(next_runs_ref, pstart_ref, lambda l, g, s: copy(1 - slot, l, g, s).start())

    _wait_tile_runs(runs_ref, lambda size: copy(slot, 0, 0, size))

    meta = meta_ref[...]
    col = lax.broadcasted_iota(jnp.int32, (TM_TOK, LOCAL_ROWS), 1).astype(jnp.float32)
    wts = jnp.zeros((TM_TOK, LOCAL_ROWS), jnp.float32)
    for kk in range(TOP_K):
        wts = wts + jnp.where(col == meta[:, TOP_K + kk:TOP_K + kk + 1], meta[:, kk:kk + 1], 0.0)
    w_hi = _bf(wts)
    w_lo = _bf(wts - w_hi.astype(jnp.float32))
    w2 = jnp.concatenate([w_hi, w_lo], axis=0)
    y_lo, y_hi = _unpack_rows(loc[slot])
    r_lo, r_hi = _dot(w2, y_lo), _dot(w2, y_hi)
    out = x_ref[...] + jnp.concatenate([r_lo[:TM_TOK] + r_lo[TM_TOK:], r_hi[:TM_TOK] + r_hi[TM_TOK:]], axis=1)
    if final:
        out = _rms(out, gfin_ref[...])
    o_ref[...] = out


def _combine(pstart, runs, meta, x, ys, gfin, final):
    n = x.shape[0]
    tm = TM_TOK
    n_tiles = n // tm
    grid_spec = pltpu.PrefetchScalarGridSpec(
        num_scalar_prefetch=1,
        grid=(n_tiles,),
        in_specs=[pl.BlockSpec((8, LANES), lambda i, ps: (i, 0), memory_space=pltpu.SMEM),
                  pl.BlockSpec((8, LANES), lambda i, ps: (jnp.minimum(i + 1, n_tiles - 1), 0), memory_space=pltpu.SMEM),
                  pl.BlockSpec((tm, LANES), lambda i, ps: (i, 0)),
                  pl.BlockSpec((tm, D_MODEL), lambda i, ps: (i, 0)),
                  pl.BlockSpec(memory_space=pl.ANY),
                  pl.BlockSpec((1, D_MODEL), lambda i, ps: (0, 0))],
        out_specs=pl.BlockSpec((tm, D_MODEL), lambda i, ps: (i, 0)),
        scratch_shapes=[pltpu.VMEM((2, LOCAL_ROWS, HALF), jnp.uint32), pltpu.SemaphoreType.DMA((2,))],
    )
    return pl.pallas_call(
        functools.partial(_combine_kernel, final),
        grid_spec=grid_spec,
        out_shape=jax.ShapeDtypeStruct((n, D_MODEL), jnp.float32),
        compiler_params=_cparams(("arbitrary",)),
        name="combine_final" if final else "combine",
    )(pstart, runs, runs, meta, x, ys, gfin)


def _pad_heads(w, parts):
    rows = w.shape[0]
    per = w.shape[1] // MLA_HEADS
    w3 = w.reshape(rows, MLA_HEADS, per)
    cols = [jnp.zeros((rows, MLA_HEADS, b - a), w.dtype) if sign == 0 else sign * w3[:, :, a:b]
            for a, b, sign in parts]
    used = sum(b - a for a, b, _ in parts)
    cols.append(jnp.zeros((rows, MLA_HEADS, HEAD_PAD - used), w.dtype))
    return jnp.concatenate(cols, axis=2).reshape(rows, MLA_HEADS * HEAD_PAD)


def _layer_weights(w_in, w_uq, w_ukv, w_router, b_router, rel_bias):
    half = MLA_ROPE // 2
    cq, ckv, kpe, sb, ca = (w_in[:, 0:256], w_in[:, 256:384], w_in[:, 384:416],
                            w_in[:, 416:1184], w_in[:, 1184:1952])
    z = lambda width: jnp.zeros((D_MODEL, width), w_in.dtype)
    kpe_pad = jnp.concatenate([z(MLA_NOPE), kpe, z(HEAD_PAD - MLA_NOPE - MLA_ROPE)], axis=1)
    kpe_rot = jnp.concatenate([z(MLA_NOPE), -kpe[:, half:], kpe[:, :half],
                               z(HEAD_PAD - MLA_NOPE - MLA_ROPE)], axis=1)
    win2 = jnp.concatenate([cq, ckv, kpe_pad, kpe_rot, sb, ca], axis=1).astype(jnp.bfloat16)

    d = MLA_NOPE + MLA_ROPE
    wq_full = _pad_heads(w_uq, [(0, d, 1)])
    wq_rot = _pad_heads(w_uq, [(0, MLA_NOPE, 0), (MLA_NOPE + half, d, -1), (MLA_NOPE, MLA_NOPE + half, 1)])
    wq2 = jnp.concatenate([wq_full, wq_rot], axis=1).astype(jnp.bfloat16)

    wk = _pad_heads(w_ukv, [(0, MLA_NOPE, 1)])
    wv = w_ukv.reshape(MLA_KV_RANK, MLA_HEADS, MLA_NOPE + MLA_V)[:, :, MLA_NOPE:].reshape(MLA_KV_RANK, A_W)
    wkv2 = jnp.concatenate([wk, wv], axis=1).astype(jnp.bfloat16)

    wr = jnp.pad(w_router, ((0, 0), (0, LANES - N_EXPERTS)))
    wr_hi = wr.astype(jnp.bfloat16)
    wr_lo = (wr - wr_hi.astype(jnp.float32)).astype(jnp.bfloat16)
    wr2 = jnp.concatenate([wr_hi, wr_lo], axis=1)
    br = jnp.pad(b_router, (0, LANES - N_EXPERTS), constant_values=NEG).reshape(1, LANES)

    ext = jnp.concatenate([rel_bias, jnp.broadcast_to(rel_bias[:, -1:], (CA_HEADS, REL_TAB - 2 * REL_CLIP - 1))], axis=1)
    tab = ext[:, ::-1]
    return win2, wq2, wkv2, wr2, br, tab


def kernel(x, positions, attn_norm, w_in, q_norm, w_uq, kv_norm, w_ukv, rel_bias, mix_norm,
           w_o, ffn_norm, w_router, b_router, w_gate_up, b_gate_up, w_down, b_down, final_norm):
    batch, seq, _ = x.shape
    n = batch * seq
    depth = w_in.shape[0]
    xf = x.reshape(n, D_MODEL)

    cos_t, sin_t = _rope_tables(positions, n)

    n_rows = n * TOP_K + (n // TM_TOK) * N_EXPERTS * (RUN_ALIGN - 1) + N_EXPERTS * TM_EXP
    n_rows = -(-n_rows // TM_EXP) * TM_EXP
    n_blk = n_rows // TM_EXP
    row2 = lambda v: v.reshape(1, -1)
    wgu_all = w_gate_up.reshape(depth * N_EXPERTS, D_MODEL, 2 * D_FF)
    bgu_all = b_gate_up.reshape(depth * N_EXPERTS, 1, 2 * D_FF)
    wdn_all = w_down.reshape(depth * N_EXPERTS, D_FF, D_MODEL)
    bdn_all = b_down.reshape(depth * N_EXPERTS, 1, D_MODEL)

    for l in range(depth):
        win2, wq2, wkv2, wr2, br, tab = _layer_weights(
            w_in[l], w_uq[l], w_ukv[l], w_router[l], b_router[l], rel_bias[l])
        qt, k, vt, sbq, sbk, sbv, caq, cak, cav = _inproj(
            xf, row2(attn_norm[l]), win2, cos_t, sin_t, row2(q_norm[l]), wq2, row2(kv_norm[l]), wkv2)

        g = mix_norm[l]
        ma = _mla_attention(qt, k, vt, row2(g[:A_W]), batch, seq)
        mb = _sb_attention(sbq, sbk, sbv, row2(g[A_W:A_W + SB_W]), batch, seq)
        pad = lambda a: jnp.pad(a.reshape(batch, seq, CA_W), ((0, 0), (CA_PAD, 0), (0, 0))).reshape(-1, CA_W)
        mc = _ca_attention(caq, pad(cak), pad(cav), tab, row2(g[A_W + SB_W:]), batch, seq)

        xn, h, lpos, meta, runs, totals = _outproj_router(
            ma, mb, mc, w_o[l].astype(jnp.bfloat16), xf, row2(ffn_norm[l]), wr2, br)

        sizes = totals[0, :N_EXPERTS]
        padded = (sizes + TM_EXP - 1) // TM_EXP * TM_EXP
        p_ends = jnp.cumsum(padded)
        p_starts = (p_ends - padded).astype(jnp.int32)
        blk_start = jnp.arange(n_blk, dtype=jnp.int32) * TM_EXP
        blk_valid = (blk_start < p_ends[-1]).astype(jnp.int32)
        last_row = jnp.minimum(blk_start, p_ends[-1] - 1)
        blk_e = jnp.minimum(jnp.sum((last_row[:, None] >= p_ends[None, :]).astype(jnp.int32), axis=1),
                            N_EXPERTS - 1)
        blk_first = jnp.concatenate([jnp.ones((1,), jnp.int32), (blk_e[1:] != blk_e[:-1]).astype(jnp.int32)])

        xs = _dispatch(p_starts, sizes, padded.astype(jnp.int32), runs, lpos, h, n_rows)
        ys = _expert_ffn(blk_e + l * N_EXPERTS, blk_first, blk_valid, xs, wgu_all, bgu_all, wdn_all, bdn_all)
        xf = _combine(p_starts, runs, meta, xn, ys, row2(final_norm), final=(l == depth - 1))

    return xf.reshape(batch, seq, D_MODEL)
```

```python
import functools
import math

import jax
import jax.numpy as jnp
from jax import lax
from jax.experimental import pallas as pl
from jax.experimental.pallas import tpu as pltpu

D_MODEL = 1024
RMS_EPS = 1e-6
MLA_NOPE, MLA_ROPE, MLA_V, MLA_HEADS = 64, 32, 64, 8
MLA_Q_RANK, MLA_KV_RANK = 256, 128
ROPE_THETA = 10000.0
SB_DIM, SB_HEADS = 64, 4
CA_DIM, CA_HEADS = 64, 4
CHUNK = 64
CA_LEFT_CHUNKS = 8
REL_CLIP = 256
N_EXPERTS, TOP_K = 32, 4
D_FF = 1024
SWIGLU_LIMIT, SWIGLU_ALPHA = 7.0, 1.702

LANES = 128
HEAD_PAD = 128
A_W = MLA_HEADS * MLA_V
SB_W = SB_HEADS * SB_DIM
CA_W = CA_HEADS * CA_DIM
CA_PAD = CA_LEFT_CHUNKS * CHUNK
CA_WIN = CA_PAD + 2 * CHUNK
REL_TAB = 1024
HALF = D_MODEL // 2

TM_TOK = 256
TQ_MLA = 256
TQ_SB = 256
TK_SB = 128
TQ_CA = 2 * CHUNK
ROUTER_TILES = 2
TM_EXP = 512
RUN_ALIGN = 8
LOCAL_ROWS = 1280
assert LOCAL_ROWS >= TM_TOK * TOP_K + N_EXPERTS * (RUN_ALIGN - 1) and LOCAL_ROWS % LANES == 0
assert TM_EXP % TM_TOK == 0 and TQ_MLA == TM_TOK

_C_CQ, _C_CKV, _C_KPE, _C_KROT = 0, 256, 384, 512
_C_SB, _C_CA, D_IN2 = 640, 1408, 2176

NEG = -1e30
LOG2E = math.log2(math.e)
VMEM_LIMIT = 56 * 1024 * 1024


def _rms(v, g):
    return v * lax.rsqrt(jnp.mean(v * v, axis=-1, keepdims=True) + RMS_EPS) * g


def _dot(a, b):
    return jnp.dot(a, b, preferred_element_type=jnp.float32)


def _dot_nt(a, b):
    return lax.dot_general(a, b, (((1,), (1,)), ((), ())), preferred_element_type=jnp.float32)


def _bf(v):
    return v.astype(jnp.bfloat16)


def _pack_rows(v):
    bits = lax.bitcast_convert_type(v, jnp.uint32)
    return (bits[:, :HALF] >> 16) | (bits[:, HALF:] & jnp.uint32(0xFFFF0000))


def _unpack_rows(u):
    lo = lax.bitcast_convert_type(u << 16, jnp.float32)
    hi = lax.bitcast_convert_type(u & jnp.uint32(0xFFFF0000), jnp.float32)
    return _bf(lo), _bf(hi)


def _cparams(sem):
    return pltpu.CompilerParams(dimension_semantics=sem, vmem_limit_bytes=VMEM_LIMIT)


ROPE_PACK = LANES // MLA_ROPE


def _rope_kernel(pos_ref, inv_ref, cos_ref, sin_ref):
    ang = pos_ref[...].astype(jnp.float32) * inv_ref[...]
    cos_ref[...] = jnp.cos(ang)
    sin_ref[...] = jnp.sin(ang)


def _rope_tables(positions, n):
    inv = ROPE_THETA ** (-jnp.arange(0, MLA_ROPE, 2, dtype=jnp.float32) / MLA_ROPE)
    inv_row = jnp.tile(jnp.concatenate([inv, inv]), ROPE_PACK).reshape(1, LANES)
    rows = n // ROPE_PACK
    pos_packed = jnp.repeat(positions.reshape(rows, ROPE_PACK), MLA_ROPE, axis=1)
    tm = min(TM_TOK, rows)
    cos_p, sin_p = pl.pallas_call(
        _rope_kernel,
        grid=(rows // tm,),
        in_specs=[pl.BlockSpec((tm, LANES), lambda i: (i, 0)),
                  pl.BlockSpec((1, LANES), lambda i: (0, 0))],
        out_specs=[pl.BlockSpec((tm, LANES), lambda i: (i, 0))] * 2,
        out_shape=[jax.ShapeDtypeStruct((rows, LANES), jnp.float32)] * 2,
        compiler_params=_cparams(("parallel",)),
        name="rope_tables",
    )(pos_packed, inv_row)
    pad = HEAD_PAD - MLA_NOPE - MLA_ROPE
    ones = lambda width: jnp.ones((n, width), jnp.float32)
    zeros = lambda width: jnp.zeros((n, width), jnp.float32)
    cos_t = jnp.concatenate([ones(MLA_NOPE), cos_p.reshape(n, MLA_ROPE), ones(pad)], axis=1)
    sin_t = jnp.concatenate([zeros(MLA_NOPE), sin_p.reshape(n, MLA_ROPE), zeros(pad)], axis=1)
    return cos_t, sin_t


def _inproj_kernel(x_ref, g_ref, win_ref, cos_ref, sin_ref, qn_ref, wq_ref, kvn_ref, wkv_ref,
                   q_ref, k_ref, v_ref, sbq_ref, sbk_ref, sbv_ref, caq_ref, cak_ref, cav_ref):
    h = _rms(x_ref[...], g_ref[...])
    proj = _dot(_bf(h), win_ref[...])
    cos128, sin128 = cos_ref[...], sin_ref[...]
    cos_h = jnp.concatenate([cos128] * MLA_HEADS, axis=1)
    sin_h = jnp.concatenate([sin128] * MLA_HEADS, axis=1)
    w = MLA_HEADS * HEAD_PAD

    cqn = _rms(proj[:, _C_CQ:_C_CKV], qn_ref[...])
    q2 = _dot(_bf(cqn), wq_ref[...])
    q = (q2[:, :w] * cos_h + q2[:, w:] * sin_h) * (LOG2E / math.sqrt(MLA_NOPE + MLA_ROPE))
    q_ref[0] = _bf(jnp.transpose(q))

    ckvn = _rms(proj[:, _C_CKV:_C_KPE], kvn_ref[...])
    kv2 = _dot(_bf(ckvn), wkv_ref[...])
    kpe = proj[:, _C_KPE:_C_KROT] * cos128 + proj[:, _C_KROT:_C_SB] * sin128
    k_ref[...] = _bf(kv2[:, :w] + jnp.concatenate([kpe] * MLA_HEADS, axis=1))
    v_ref[0] = _bf(jnp.transpose(kv2[:, w:]))

    sb_scale = 1.0 / math.sqrt(SB_DIM)
    sbq_ref[...] = _bf(proj[:, _C_SB:_C_SB + SB_W] * sb_scale)
    sbk_ref[...] = _bf(proj[:, _C_SB + SB_W:_C_SB + 2 * SB_W])
    sbv_ref[...] = _bf(proj[:, _C_SB + 2 * SB_W:_C_CA])
    ca_scale = LOG2E / math.sqrt(CA_DIM)
    caq_ref[...] = _bf(proj[:, _C_CA:_C_CA + CA_W] * ca_scale)
    cak_ref[...] = _bf(proj[:, _C_CA + CA_W:_C_CA + 2 * CA_W])
    cav_ref[...] = _bf(proj[:, _C_CA + 2 * CA_W:D_IN2])


def _inproj(x, g, win2, cos_t, sin_t, qn, wq2, kvn, wkv2):
    n = x.shape[0]
    tm = TM_TOK
    row = lambda width: pl.BlockSpec((tm, width), lambda i: (i, 0))
    full = lambda a: pl.BlockSpec(a.shape, lambda i: (0,) * a.ndim)
    widths = [MLA_HEADS * HEAD_PAD] + [SB_W] * 3 + [CA_W] * 3
    tile_t = lambda rows: pl.BlockSpec((1, rows, tm), lambda i: (i, 0, 0))
    shape_t = lambda rows: jax.ShapeDtypeStruct((n // tm, rows, tm), jnp.bfloat16)
    return pl.pallas_call(
        _inproj_kernel,
        grid=(n // tm,),
        in_specs=[row(D_MODEL), full(g), full(win2), row(LANES), row(LANES),
                  full(qn), full(wq2), full(kvn), full(wkv2)],
        out_specs=[tile_t(MLA_HEADS * HEAD_PAD), row(widths[0]), tile_t(A_W)] + [row(wd) for wd in widths[1:]],
        out_shape=[shape_t(MLA_HEADS * HEAD_PAD), jax.ShapeDtypeStruct((n, widths[0]), jnp.bfloat16), shape_t(A_W)]
                  + [jax.ShapeDtypeStruct((n, wd), jnp.bfloat16) for wd in widths[1:]],
        compiler_params=_cparams(("parallel",)),
        name="inproj",
    )(x, g, win2, cos_t, sin_t, qn, wq2, kvn, wkv2)


def _mla_kernel(qt_ref, k_ref, vt_ref, g_ref, o_ref, m_ref, l_ref, acc_ref, s_ref):
    i = pl.program_id(1)
    tq = TQ_MLA
    n_pairs = MLA_HEADS // 2
    key_chunk = lax.broadcasted_iota(jnp.int32, (tq, tq), 0) // CHUNK
    qry_chunk = lax.broadcasted_iota(jnp.int32, (tq, tq), 1) // CHUNK
    diag_ok = key_chunk <= qry_chunk
    top = lax.broadcasted_iota(jnp.int32, (LANES, tq), 0) < MLA_V

    m_ref[...] = jnp.full(m_ref.shape, -jnp.inf, jnp.float32)
    l_ref[...] = jnp.zeros(l_ref.shape, jnp.float32)
    acc_ref[...] = jnp.zeros(acc_ref.shape, jnp.float32)

    def scores(kb, slot):
        start = pl.multiple_of(kb * tq, tq)
        for h in range(MLA_HEADS):
            qt = qt_ref[0, h * HEAD_PAD:(h + 1) * HEAD_PAD, :]
            kblk = k_ref[pl.ds(start, tq), h * HEAD_PAD:(h + 1) * HEAD_PAD]
            s_ref[slot, h] = _dot(kblk, qt)

    def absorb(kb, slot, mask):
        for pr in range(n_pairs):
            vt = vt_ref[kb, pr * LANES:(pr + 1) * LANES, :]
            zero = jnp.zeros_like(vt)
            vt_bd = jnp.concatenate([jnp.where(top, vt, zero), jnp.where(top, zero, vt)], axis=1)
            pts, alphas = [], []
            for h in (2 * pr, 2 * pr + 1):
                s = s_ref[slot, h]
                if mask is not None:
                    s = jnp.where(mask, s, -jnp.inf)
                m_old = m_ref[h]
                m_new = jnp.maximum(m_old, jnp.max(s, axis=0, keepdims=True))
                p = jnp.exp2(s - m_new)
                alpha = jnp.exp2(m_old - m_new)
                l_ref[h] = alpha * l_ref[h] + jnp.sum(p, axis=0, keepdims=True)
                m_ref[h] = m_new
                pts.append(_bf(p))
                alphas.append(alpha)
            alpha_pair = jnp.where(top, alphas[0], alphas[1])
            acc_ref[pr] = alpha_pair * acc_ref[pr] + _dot(vt_bd, jnp.concatenate(pts, axis=0))

    def body(j, carry):
        scores(2 * j + 1, 1)
        absorb(2 * j, 0, None)
        scores(2 * j + 2, 0)
        absorb(2 * j + 1, 1, None)
        return carry

    scores(0, 0)
    lax.fori_loop(0, i // 2, body, 0)

    @pl.when(i % 2 == 0)
    def _():
        absorb(i, 0, diag_ok)

    @pl.when(i % 2 == 1)
    def _():
        scores(i, 1)
        absorb(i - 1, 0, None)
        absorb(i, 1, diag_ok)

    outs = [jnp.transpose(acc_ref[pr] / jnp.where(top, l_ref[2 * pr], l_ref[2 * pr + 1])) for pr in range(n_pairs)]
    o = jnp.concatenate(outs, axis=1)
    o_ref[...] = _bf(_rms(o, g_ref[...]))


def _mla_attention(qt, k, vt, g, batch, seq):
    nq = seq // TQ_MLA
    w = MLA_HEADS * HEAD_PAD
    return pl.pallas_call(
        _mla_kernel,
        grid=(batch, nq),
        in_specs=[pl.BlockSpec((1, w, TQ_MLA), lambda b, i: (b * nq + i, 0, 0)),
                  pl.BlockSpec((seq, w), lambda b, i: (b, 0)),
                  pl.BlockSpec((nq, A_W, TQ_MLA), lambda b, i: (b, 0, 0)),
                  pl.BlockSpec((1, A_W), lambda b, i: (0, 0))],
        out_specs=pl.BlockSpec((TQ_MLA, A_W), lambda b, i: (b * nq + i, 0)),
        out_shape=jax.ShapeDtypeStruct((batch * seq, A_W), jnp.bfloat16),
        scratch_shapes=[pltpu.VMEM((MLA_HEADS, 1, TQ_MLA), jnp.float32),
                        pltpu.VMEM((MLA_HEADS, 1, TQ_MLA), jnp.float32),
                        pltpu.VMEM((MLA_HEADS // 2, LANES, TQ_MLA), jnp.float32),
                        pltpu.VMEM((2, MLA_HEADS, TQ_MLA, TQ_MLA), jnp.float32)],
        compiler_params=_cparams(("parallel", "parallel")),
        name="mla_attention",
    )(qt, k, vt, g)


def _sb_kernel(q_ref, k_ref, v_ref, g_ref, o_ref, run_ref, acc_ref, zl_ref, sums_ref):
    i = pl.program_id(1)
    tq, tk = TQ_SB, TK_SB
    n_pairs = SB_HEADS // 2
    r = lax.broadcasted_iota(jnp.int32, (2 * tq, tk), 0)
    c = lax.broadcasted_iota(jnp.int32, (2 * tq, tk), 1)
    q_off = jnp.where(r >= tq, r - tq, r)
    top_low = (r < tq) == (c < SB_DIM)
    r2 = lax.broadcasted_iota(jnp.int32, (tk, 2 * tk), 0)
    c2 = lax.broadcasted_iota(jnp.int32, (tk, 2 * tk), 1)
    sum_mat = jnp.where((c2 >= tk) | (r2 > c2), 1.0, 0.0).astype(jnp.bfloat16)

    run_ref[...] = jnp.zeros(run_ref.shape, jnp.float32)
    acc_ref[...] = jnp.zeros(acc_ref.shape, jnp.float32)

    def step(kb_hi, masked):
        masks = []
        for d in range(2):
            kb = kb_hi - d
            start = pl.multiple_of(kb * tk, tk)
            mask = (c + (kb * tk - i * tq)) < q_off if masked else None
            masks.append(mask)
            for pr in range(n_pairs):
                qp = q_ref[:, pr * LANES:(pr + 1) * LANES]
                q2 = jnp.concatenate([qp, qp], axis=0)
                qm = jnp.where(top_low, q2, jnp.zeros_like(q2))
                kblk = k_ref[pl.ds(start, tk), pr * LANES:(pr + 1) * LANES]
                z = _dot_nt(qm, kblk)
                log_keep = -(jnp.maximum(z, 0.0) + jnp.log(1.0 + jnp.exp(-jnp.abs(z))))
                if masked:
                    log_keep = jnp.where(mask, log_keep, 0.0)
                hi = _bf(log_keep)
                lo = _bf(log_keep - hi.astype(jnp.float32))
                sums_ref[2 * d + pr] = _dot(hi, sum_mat) + _dot(lo, sum_mat)
                zl_ref[2 * d + pr] = z + log_keep
        slowest = None
        for d in range(2):
            start = pl.multiple_of((kb_hi - d) * tk, tk)
            for pr in range(n_pairs):
                vblk = v_ref[pl.ds(start, tk), pr * LANES:(pr + 1) * LANES]
                run = run_ref[pr]
                a = jnp.exp(zl_ref[2 * d + pr] + run + sums_ref[2 * d + pr, :, 0:tk])
                if masked:
                    a = jnp.where(masks[d], a, 0.0)
                acc_ref[pr] = acc_ref[pr] + _dot(_bf(a), vblk)
                run = run + sums_ref[2 * d + pr, :, tk:2 * tk]
                run_ref[pr] = run
                if d == 1:
                    top = jnp.max(run)
                    slowest = top if slowest is None else jnp.maximum(slowest, top)
        return slowest

    underflow = -104.0
    assert tq == 2 * tk
    first = step(2 * i + 1, True)
    lax.while_loop(lambda cr: (cr[0] >= 0) & (cr[1] > underflow),
                   lambda cr: (cr[0] - 1, step(2 * cr[0] + 1, False)),
                   (i - 1, first))

    lane = lax.broadcasted_iota(jnp.int32, (tq, LANES), 1)
    outs = [jnp.where(lane < SB_DIM, acc_ref[pr, 0:tq, :], acc_ref[pr, tq:2 * tq, :]) for pr in range(n_pairs)]
    o = jnp.concatenate(outs, axis=1)
    o_ref[...] = _bf(_rms(o, g_ref[...]))


def _sb_attention(q, k, v, g, batch, seq):
    nq = seq // TQ_SB
    return pl.pallas_call(
        _sb_kernel,
        grid=(batch, nq),
        in_specs=[pl.BlockSpec((TQ_SB, SB_W), lambda b, i: (b * nq + i, 0)),
                  pl.BlockSpec((seq, SB_W), lambda b, i: (b, 0)),
                  pl.BlockSpec((seq, SB_W), lambda b, i: (b, 0)),
                  pl.BlockSpec((1, SB_W), lambda b, i: (0, 0))],
        out_specs=pl.BlockSpec((TQ_SB, SB_W), lambda b, i: (b * nq + i, 0)),
        out_shape=jax.ShapeDtypeStruct((batch * seq, SB_W), jnp.bfloat16),
        scratch_shapes=[pltpu.VMEM((SB_HEADS // 2, 2 * TQ_SB, TK_SB), jnp.float32),
                        pltpu.VMEM((SB_HEADS // 2, 2 * TQ_SB, LANES), jnp.float32),
                        pltpu.VMEM((SB_HEADS, 2 * TQ_SB, TK_SB), jnp.float32),
                        pltpu.VMEM((SB_HEADS, 2 * TQ_SB, 2 * TK_SB), jnp.float32)],
        compiler_params=_cparams(("parallel", "parallel")),
        name="sb_attention",
    )(q, k, v, g)


def _ca_kernel(q_ref, k_ref, v_ref, tab_ref, g_ref, o_ref, bias_ref):
    i = pl.program_id(1)
    t = TQ_CA
    start = pl.multiple_of(i * t, t)

    @pl.when((pl.program_id(0) == 0) & (i == 0))
    def _():
        r = lax.broadcasted_iota(jnp.int32, (t, CA_WIN), 0)
        c = lax.broadcasted_iota(jnp.int32, (t, CA_WIN), 1)
        lo = (r // CHUNK) * CHUNK
        band = (c >= lo) & (c < lo + CA_PAD + CHUNK)
        for h in range(CA_HEADS):
            tab = jnp.broadcast_to(tab_ref[h:h + 1, :], (t, REL_TAB))
            bias = pltpu.roll(tab, REL_TAB - (REL_CLIP - 1), 1, stride=1, stride_axis=0)[:, :CA_WIN]
            bias_ref[h] = jnp.where(band, bias * LOG2E, -jnp.inf)

    r2 = lax.broadcasted_iota(jnp.int32, (2 * t, LANES), 0)
    c2 = lax.broadcasted_iota(jnp.int32, (2 * t, LANES), 1)
    top_low = (r2 < t) == (c2 < CA_DIM)
    in_seq = lax.broadcasted_iota(jnp.int32, (2 * t, CA_WIN), 1) + i * t >= CA_PAD
    lane = lax.broadcasted_iota(jnp.int32, (t, LANES), 1)

    outs = []
    for pair in range(CA_HEADS // 2):
        qp = q_ref[:, pair * LANES:(pair + 1) * LANES]
        q2 = jnp.concatenate([qp, qp], axis=0)
        qm = jnp.where(top_low, q2, jnp.zeros_like(q2))
        kwin = k_ref[pl.ds(start, CA_WIN), pair * LANES:(pair + 1) * LANES]
        vwin = v_ref[pl.ds(start, CA_WIN), pair * LANES:(pair + 1) * LANES]
        bias = jnp.concatenate([bias_ref[2 * pair], bias_ref[2 * pair + 1]], axis=0)
        s = jnp.where(in_seq, _dot_nt(qm, kwin) + bias, -jnp.inf)
        m = jnp.max(s, axis=-1, keepdims=True)
        p = jnp.exp2(s - m)
        l = jnp.sum(p, axis=-1, keepdims=True)
        o2 = _dot(_bf(p), vwin) / l
        outs.append(jnp.where(lane < CA_DIM, o2[0:t], o2[t:2 * t]))
    o = jnp.concatenate(outs, axis=1)
    o_ref[...] = _bf(_rms(o, g_ref[...]))


def _ca_attention(q, kpad, vpad, tab, g, batch, seq):
    nq = seq // TQ_CA
    return pl.pallas_call(
        _ca_kernel,
        grid=(batch, nq),
        in_specs=[pl.BlockSpec((TQ_CA, CA_W), lambda b, i: (b * nq + i, 0)),
                  pl.BlockSpec((seq + CA_PAD, CA_W), lambda b, i: (b, 0)),
                  pl.BlockSpec((seq + CA_PAD, CA_W), lambda b, i: (b, 0)),
                  pl.BlockSpec((CA_HEADS, REL_TAB), lambda b, i: (0, 0)),
                  pl.BlockSpec((1, CA_W), lambda b, i: (0, 0))],
        out_specs=pl.BlockSpec((TQ_CA, CA_W), lambda b, i: (b * nq + i, 0)),
        out_shape=jax.ShapeDtypeStruct((batch * seq, CA_W), jnp.bfloat16),
        scratch_shapes=[pltpu.VMEM((CA_HEADS, TQ_CA, CA_WIN), jnp.float32)],
        compiler_params=_cparams(("arbitrary", "arbitrary")),
        name="ca_attention",
    )(q, kpad, vpad, tab, g)


def _outproj_router_kernel(ma_ref, mb_ref, mc_ref, wo_ref, x_ref, g_ref, wr2_ref, br_ref,
                           xn_ref, h_ref, lpos_ref, meta_ref, runs_ref, tot_ref, base_ref):
    @pl.when(pl.program_id(0) == 0)
    def _():
        base_ref[...] = jnp.zeros_like(base_ref)

    base = base_ref[...]
    for j in range(ROUTER_TILES):
        base = _route_tile(j, base, ma_ref, mb_ref, mc_ref, wo_ref, x_ref, g_ref, wr2_ref, br_ref,
                           xn_ref, h_ref, lpos_ref, meta_ref, runs_ref)
    base_ref[...] = base
    tot_ref[...] = base.astype(jnp.int32)


def _route_tile(j, base, ma_ref, mb_ref, mc_ref, wo_ref, x_ref, g_ref, wr2_ref, br_ref,
                xn_ref, h_ref, lpos_ref, meta_ref, runs_ref):
    tm = TM_TOK
    rows = slice(j * tm, (j + 1) * tm)
    attn = (_dot(ma_ref[rows, :], wo_ref[0:A_W, :]) + _dot(mb_ref[rows, :], wo_ref[A_W:A_W + SB_W, :])
            + _dot(mc_ref[rows, :], wo_ref[A_W + SB_W:, :]))
    xn = x_ref[rows, :] + attn
    xn_ref[rows, :] = xn
    h = _rms(xn, g_ref[...])

    h_hi = _bf(h)
    h_ref[rows, :] = h_hi
    h_lo = _bf(h - h_hi.astype(jnp.float32))
    both = _dot(h_hi, wr2_ref[...])
    logits = (both[:, :LANES] + both[:, LANES:] + _dot(h_lo, wr2_ref[:, 0:LANES])
              + br_ref[...])
    lane = lax.broadcasted_iota(jnp.int32, (tm, LANES), 1)
    lane_f = lane.astype(jnp.float32)

    work = logits
    vals, idxs, hots = [], [], []
    for _ in range(TOP_K):
        mx = jnp.max(work, axis=-1, keepdims=True)
        ix = jnp.min(jnp.where(work == mx, lane_f, float(LANES)), axis=-1, keepdims=True)
        hot = lane_f == ix
        work = jnp.where(hot, -jnp.inf, work)
        vals.append(mx)
        idxs.append(ix)
        hots.append(hot)
    exps = [jnp.exp(v - vals[0]) for v in vals]
    denom = exps[0] + exps[1] + exps[2] + exps[3]
    gates = [e / denom for e in exps]

    sel = jnp.zeros((tm, LANES), jnp.float32)
    for hot in hots:
        sel = sel + jnp.where(hot, 1.0, 0.0)
    r = lax.broadcasted_iota(jnp.int32, (tm, tm), 0)
    c = lax.broadcasted_iota(jnp.int32, (tm, tm), 1)
    before = jnp.where(c < r, 1.0, 0.0).astype(jnp.bfloat16)
    rank_in_tile = _dot(before, _bf(sel))

    cnt = jnp.sum(sel, axis=0, keepdims=True)
    cnt_al = jnp.ceil(cnt * (1.0 / RUN_ALIGN)) * RUN_ALIGN
    rl = lax.broadcasted_iota(jnp.int32, (LANES, LANES), 0)
    cl = lax.broadcasted_iota(jnp.int32, (LANES, LANES), 1)
    earlier = jnp.where(rl < cl, 1.0, 0.0).astype(jnp.bfloat16)
    loff = _dot(_bf(jnp.broadcast_to(cnt_al, (8, LANES))), earlier)[0:1, :]
    sub = lax.broadcasted_iota(jnp.int32, (8, LANES), 0)
    has_long = jnp.where(jnp.max(cnt_al, axis=-1, keepdims=True) >= LONG_RUN, 1.0, 0.0)
    runs = jnp.where(sub == 0, loff, jnp.where(sub == 1, base, jnp.where(sub == 2, cnt_al,
                     jnp.where(sub == 3, has_long, 0.0))))
    runs_ref[8 * j:8 * j + 8, :] = runs.astype(jnp.int32)

    lpos_dense = loff + rank_in_tile
    meta = jnp.zeros((tm, LANES), jnp.float32)
    for kk in range(TOP_K):
        lpos = jnp.sum(jnp.where(hots[kk], lpos_dense, 0.0), axis=-1, keepdims=True)
        meta = meta + jnp.where(lane == kk, gates[kk], 0.0) + jnp.where(lane == TOP_K + kk, lpos, 0.0)
    meta_ref[rows, :] = meta
    lpos_ref[:, rows] = jnp.transpose(meta)[TOP_K:TOP_K + 8, :].astype(jnp.int32)
    return base + cnt_al


def _outproj_router(ma, mb, mc, wo, x, g, wr2, br):
    n = x.shape[0]
    tm = TM_TOK * ROUTER_TILES
    row = lambda width: pl.BlockSpec((tm, width), lambda i: (i, 0))
    full = lambda a: pl.BlockSpec(a.shape, lambda i: (0,) * a.ndim)
    return pl.pallas_call(
        _outproj_router_kernel,
        grid=(n // tm,),
        in_specs=[row(A_W), row(SB_W), row(CA_W), full(wo), row(D_MODEL), full(g), full(wr2), full(br)],
        out_specs=[row(D_MODEL), row(D_MODEL), pl.BlockSpec((8, tm), lambda i: (0, i)),
                   row(LANES), pl.BlockSpec((8 * ROUTER_TILES, LANES), lambda i: (i, 0)),
                   pl.BlockSpec((1, LANES), lambda i: (0, 0))],
        out_shape=[jax.ShapeDtypeStruct((n, D_MODEL), jnp.float32),
                   jax.ShapeDtypeStruct((n, D_MODEL), jnp.bfloat16),
                   jax.ShapeDtypeStruct((8, n), jnp.int32),
                   jax.ShapeDtypeStruct((n, LANES), jnp.float32),
                   jax.ShapeDtypeStruct((8 * (n // TM_TOK), LANES), jnp.int32),
                   jax.ShapeDtypeStruct((1, LANES), jnp.int32)],
        scratch_shapes=[pltpu.VMEM((1, LANES), jnp.float32)],
        compiler_params=_cparams(("arbitrary",)),
        name="outproj_router",
    )(ma, mb, mc, wo, x, g, wr2, br)


_RUN_CHUNKS = tuple(TM_TOK >> s for s in range(TM_TOK.bit_length()) if (TM_TOK >> s) >= RUN_ALIGN)


LONG_RUN = 128


def _for_each_chunk(length, fn, sizes=_RUN_CHUNKS):
    for size in sizes:
        off = length & (~(2 * size - 1))

        @pl.when((length & size) != 0)
        def _(off=off, size=size):
            fn(off, size)


_WAIT_CHUNKS = tuple(1 << b for b in range((LOCAL_ROWS).bit_length() - 1, RUN_ALIGN.bit_length() - 2, -1))


def _wait_tile_runs(runs_ref, make_copy):
    total = runs_ref[0, N_EXPERTS - 1] + runs_ref[2, N_EXPERTS - 1]
    for size in _WAIT_CHUNKS:
        @pl.when((total & size) != 0)
        def _(size=size):
            make_copy(size).wait()


def _tile_runs(runs_ref, pstart_ref, fn):
    def experts(sizes, unroll):
        def body(e, _):
            local, glob, length = runs_ref[0, e], pstart_ref[e] + runs_ref[1, e], runs_ref[2, e]
            _for_each_chunk(length, lambda off, size: fn(pl.multiple_of(local + off, RUN_ALIGN),
                                                         pl.multiple_of(glob + off, RUN_ALIGN), size), sizes)
            return 0

        lax.fori_loop(0, N_EXPERTS, body, 0, unroll=unroll)

    experts(tuple(sz for sz in _RUN_CHUNKS if sz < LONG_RUN), 2)

    @pl.when(runs_ref[3, 0] != 0)
    def _():
        experts(tuple(sz for sz in _RUN_CHUNKS if sz >= LONG_RUN), 1)


def _dispatch_kernel(pstart_ref, tot_ref, pad_ref, runs_ref, prev_runs_ref, lpos_ref, h_ref, xs_ref,
                     loc, zbuf, sem, fill_sem):
    i = pl.program_id(0)
    slot = i % 2

    def fill(start_or_wait):
        def body(e, _):
            first = pstart_ref[e] + tot_ref[e]
            _for_each_chunk(pad_ref[e] - tot_ref[e], lambda off, size: start_or_wait(pltpu.make_async_copy(
                zbuf.at[pl.ds(0, size)], xs_ref.at[pl.ds(pl.multiple_of(first + off, RUN_ALIGN), size)], fill_sem)))
            return 0

        lax.fori_loop(0, N_EXPERTS, body, 0)

        used = pstart_ref[N_EXPERTS - 1] + pad_ref[N_EXPERTS - 1]

        def tail(b, _):
            start_or_wait(pltpu.make_async_copy(
                zbuf, xs_ref.at[pl.ds(pl.multiple_of(b * TM_TOK, TM_TOK), TM_TOK)], fill_sem))
            return 0

        lax.fori_loop(used // TM_TOK, xs_ref.shape[0] // TM_TOK, tail, 0)

    @pl.when(i == 0)
    def _():
        zbuf[...] = jnp.zeros_like(zbuf)
        fill(lambda c: c.start())
        fill(lambda c: c.wait())

    r = lax.broadcasted_iota(jnp.int32, (LOCAL_ROWS, TM_TOK), 0)
    hit = r == lpos_ref[0:1, :]
    for kk in range(1, TOP_K):
        hit = hit | (r == lpos_ref[kk:kk + 1, :])
    loc[slot] = _pack_rows(_dot(jnp.where(hit, 1.0, 0.0).astype(jnp.bfloat16), h_ref[...]))

    def copy(buf, local, glob, size):
        return pltpu.make_async_copy(loc.at[buf, pl.ds(local, size)], xs_ref.at[pl.ds(glob, size)], sem.at[buf])

    _tile_runs(runs_ref, pstart_ref, lambda l, g, s: copy(slot, l, g, s).start())

    @pl.when(i > 0)
    def _():
        _wait_tile_runs(prev_runs_ref, lambda size: copy(1 - slot, 0, 0, size))

    @pl.when(i == pl.num_programs(0) - 1)
    def _():
        _wait_tile_runs(runs_ref, lambda size: copy(slot, 0, 0, size))


def _dispatch(pstart, totals, padded, runs, lpos, h, n_rows):
    n = h.shape[0]
    tm = TM_TOK
    grid_spec = pltpu.PrefetchScalarGridSpec(
        num_scalar_prefetch=3,
        grid=(n // tm,),
        in_specs=[pl.BlockSpec((8, LANES), lambda i, *_: (i, 0), memory_space=pltpu.SMEM),
                  pl.BlockSpec((8, LANES), lambda i, *_: (jnp.maximum(i - 1, 0), 0), memory_space=pltpu.SMEM),
                  pl.BlockSpec((8, tm), lambda i, *_: (0, i)),
                  pl.BlockSpec((tm, D_MODEL), lambda i, *_: (i, 0))],
        out_specs=pl.BlockSpec(memory_space=pl.ANY),
        scratch_shapes=[pltpu.VMEM((2, LOCAL_ROWS, HALF), jnp.uint32),
                        pltpu.VMEM((TM_TOK, HALF), jnp.uint32),
                        pltpu.SemaphoreType.DMA((2,)),
                        pltpu.SemaphoreType.DMA],
    )
    return pl.pallas_call(
        _dispatch_kernel,
        grid_spec=grid_spec,
        out_shape=jax.ShapeDtypeStruct((n_rows, HALF), jnp.uint32),
        compiler_params=_cparams(("arbitrary",)),
        name="dispatch",
    )(pstart, totals, padded, runs, runs, lpos, h)


def _expert_kernel(be_ref, bfirst_ref, bvalid_ref, xs_ref, wgu_ref, bgu_ref, wdn_ref, bdn_ref,
                   ys_ref, wgu_bf, wdn_bf):
    b = pl.program_id(0)

    @pl.when(bfirst_ref[b] == 1)
    def _():
        wgu_bf[...] = _bf(wgu_ref[0])
        wdn_bf[...] = _bf(wdn_ref[0])

    @pl.when(bvalid_ref[b] == 1)
    def _():
        xb = jnp.concatenate(_unpack_rows(xs_ref[...]), axis=1)
        gu = _dot(xb, wgu_bf[...]) + bgu_ref[0]
        gte = jnp.minimum(gu[:, :D_FF], SWIGLU_LIMIT)
        up = jnp.clip(gu[:, D_FF:], -SWIGLU_LIMIT, SWIGLU_LIMIT)
        act = (up + 1.0) * (gte * (1.0 / (1.0 + jnp.exp(-SWIGLU_ALPHA * gte))))
        y = _dot(_bf(act), wdn_bf[...]) + bdn_ref[0]
        ys_ref[...] = _pack_rows(_bf(y).astype(jnp.float32))

    @pl.when(bvalid_ref[b] == 0)
    def _():
        ys_ref[...] = jnp.zeros_like(ys_ref)


def _expert_ffn(blk_e, blk_first, blk_valid, xs, wgu, bgu, wdn, bdn):
    n_rows = xs.shape[0]
    tm = TM_EXP
    grid_spec = pltpu.PrefetchScalarGridSpec(
        num_scalar_prefetch=3,
        grid=(n_rows // tm,),
        in_specs=[pl.BlockSpec((tm, HALF), lambda b, e, f, v: (b, 0)),
                  pl.BlockSpec((1, D_MODEL, 2 * D_FF), lambda b, e, f, v: (e[b], 0, 0)),
                  pl.BlockSpec((1, 1, 2 * D_FF), lambda b, e, f, v: (e[b], 0, 0)),
                  pl.BlockSpec((1, D_FF, D_MODEL), lambda b, e, f, v: (e[b], 0, 0)),
                  pl.BlockSpec((1, 1, D_MODEL), lambda b, e, f, v: (e[b], 0, 0))],
        out_specs=pl.BlockSpec((tm, HALF), lambda b, e, f, v: (b, 0)),
        scratch_shapes=[pltpu.VMEM((D_MODEL, 2 * D_FF), jnp.bfloat16),
                        pltpu.VMEM((D_FF, D_MODEL), jnp.bfloat16)],
    )
    return pl.pallas_call(
        _expert_kernel,
        grid_spec=grid_spec,
        out_shape=jax.ShapeDtypeStruct((n_rows, HALF), jnp.uint32),
        compiler_params=_cparams(("arbitrary",)),
        name="expert_ffn",
    )(blk_e, blk_first, blk_valid, xs, wgu, bgu, wdn, bdn)


def _combine_kernel(final, pstart_ref, runs_ref, next_runs_ref, meta_ref, x_ref, ys_ref, gfin_ref, o_ref, loc, sem):
    i = pl.program_id(0)
    slot = i % 2

    def copy(buf, local, glob, size):
        return pltpu.make_async_copy(ys_ref.at[pl.ds(glob, size)], loc.at[buf, pl.ds(local, size)], sem.at[buf])

    @pl.when(i == 0)
    def _():
        loc[...] = jnp.zeros_like(loc)
        _tile_runs(runs_ref, pstart_ref, lambda l, g, s: copy(slot, l, g, s).start())

    @pl.when(i + 1 < pl.num_programs(0))
    def _():
        _tile_runs(next_runs_ref, pstart_ref, lambda l, g, s: copy(1 - slot, l, g, s).start())

    _wait_tile_runs(runs_ref, lambda size: copy(slot, 0, 0, size))

    meta = meta_ref[...]
    col = lax.broadcasted_iota(jnp.int32, (TM_TOK, LOCAL_ROWS), 1).astype(jnp.float32)
    wts = jnp.zeros((TM_TOK, LOCAL_ROWS), jnp.float32)
    for kk in range(TOP_K):
        wts = wts + jnp.where(col == meta[:, TOP_K + kk:TOP_K + kk + 1], meta[:, kk:kk + 1], 0.0)
    w_hi = _bf(wts)
    w_lo = _bf(wts - w_hi.astype(jnp.float32))
    w2 = jnp.concatenate([w_hi, w_lo], axis=0)
    y_lo, y_hi = _unpack_rows(loc[slot])
    r_lo, r_hi = _dot(w2, y_lo), _dot(w2, y_hi)
    out = x_ref[...] + jnp.concatenate([r_lo[:TM_TOK] + r_lo[TM_TOK:], r_hi[:TM_TOK] + r_hi[TM_TOK:]], axis=1)
    if final:
        out = _rms(out, gfin_ref[...])
    o_ref[...] = out


def _combine(pstart, runs, meta, x, ys, gfin, final):
    n = x.shape[0]
    tm = TM_TOK
    n_tiles = n // tm
    grid_spec = pltpu.PrefetchScalarGridSpec(
        num_scalar_prefetch=1,
        grid=(n_tiles,),
        in_specs=[pl.BlockSpec((8, LANES), lambda i, ps: (i, 0), memory_space=pltpu.SMEM),
                  pl.BlockSpec((8, LANES), lambda i, ps: (jnp.minimum(i + 1, n_tiles - 1), 0), memory_space=pltpu.SMEM),
                  pl.BlockSpec((tm, LANES), lambda i, ps: (i, 0)),
                  pl.BlockSpec((tm, D_MODEL), lambda i, ps: (i, 0)),
                  pl.BlockSpec(memory_space=pl.ANY),
                  pl.BlockSpec((1, D_MODEL), lambda i, ps: (0, 0))],
        out_specs=pl.BlockSpec((tm, D_MODEL), lambda i, ps: (i, 0)),
        scratch_shapes=[pltpu.VMEM((2, LOCAL_ROWS, HALF), jnp.uint32), pltpu.SemaphoreType.DMA((2,))],
    )
    return pl.pallas_call(
        functools.partial(_combine_kernel, final),
        grid_spec=grid_spec,
        out_shape=jax.ShapeDtypeStruct((n, D_MODEL), jnp.float32),
        compiler_params=_cparams(("arbitrary",)),
        name="combine_final" if final else "combine",
    )(pstart, runs, runs, meta, x, ys, gfin)


def _pad_heads(w, parts):
    rows = w.shape[0]
    per = w.shape[1] // MLA_HEADS
    w3 = w.reshape(rows, MLA_HEADS, per)
    cols = [jnp.zeros((rows, MLA_HEADS, b - a), w.dtype) if sign == 0 else sign * w3[:, :, a:b]
            for a, b, sign in parts]
    used = sum(b - a for a, b, _ in parts)
    cols.append(jnp.zeros((rows, MLA_HEADS, HEAD_PAD - used), w.dtype))
    return jnp.concatenate(cols, axis=2).reshape(rows, MLA_HEADS * HEAD_PAD)


def _layer_weights(w_in, w_uq, w_ukv, w_router, b_router, rel_bias):
    half = MLA_ROPE // 2
    cq, ckv, kpe, sb, ca = (w_in[:, 0:256], w_in[:, 256:384], w_in[:, 384:416],
                            w_in[:, 416:1184], w_in[:, 1184:1952])
    z = lambda width: jnp.zeros((D_MODEL, width), w_in.dtype)
    kpe_pad = jnp.concatenate([z(MLA_NOPE), kpe, z(HEAD_PAD - MLA_NOPE - MLA_ROPE)], axis=1)
    kpe_rot = jnp.concatenate([z(MLA_NOPE), -kpe[:, half:], kpe[:, :half],
                               z(HEAD_PAD - MLA_NOPE - MLA_ROPE)], axis=1)
    win2 = jnp.concatenate([cq, ckv, kpe_pad, kpe_rot, sb, ca], axis=1).astype(jnp.bfloat16)

    d = MLA_NOPE + MLA_ROPE
    wq_full = _pad_heads(w_uq, [(0, d, 1)])
    wq_rot = _pad_heads(w_uq, [(0, MLA_NOPE, 0), (MLA_NOPE + half, d, -1), (MLA_NOPE, MLA_NOPE + half, 1)])
    wq2 = jnp.concatenate([wq_full, wq_rot], axis=1).astype(jnp.bfloat16)

    wk = _pad_heads(w_ukv, [(0, MLA_NOPE, 1)])
    wv = w_ukv.reshape(MLA_KV_RANK, MLA_HEADS, MLA_NOPE + MLA_V)[:, :, MLA_NOPE:].reshape(MLA_KV_RANK, A_W)
    wkv2 = jnp.concatenate([wk, wv], axis=1).astype(jnp.bfloat16)

    wr = jnp.pad(w_router, ((0, 0), (0, LANES - N_EXPERTS)))
    wr_hi = wr.astype(jnp.bfloat16)
    wr_lo = (wr - wr_hi.astype(jnp.float32)).astype(jnp.bfloat16)
    wr2 = jnp.concatenate([wr_hi, wr_lo], axis=1)
    br = jnp.pad(b_router, (0, LANES - N_EXPERTS), constant_values=NEG).reshape(1, LANES)

    ext = jnp.concatenate([rel_bias, jnp.broadcast_to(rel_bias[:, -1:], (CA_HEADS, REL_TAB - 2 * REL_CLIP - 1))], axis=1)
    tab = ext[:, ::-1]
    return win2, wq2, wkv2, wr2, br, tab


def kernel(x, positions, attn_norm, w_in, q_norm, w_uq, kv_norm, w_ukv, rel_bias, mix_norm,
           w_o, ffn_norm, w_router, b_router, w_gate_up, b_gate_up, w_down, b_down, final_norm):
    batch, seq, _ = x.shape
    n = batch * seq
    depth = w_in.shape[0]
    xf = x.reshape(n, D_MODEL)

    cos_t, sin_t = _rope_tables(positions, n)

    n_rows = n * TOP_K + (n // TM_TOK) * N_EXPERTS * (RUN_ALIGN - 1) + N_EXPERTS * TM_EXP
    n_rows = -(-n_rows // TM_EXP) * TM_EXP
    n_blk = n_rows // TM_EXP
    row2 = lambda v: v.reshape(1, -1)
    wgu_all = w_gate_up.reshape(depth * N_EXPERTS, D_MODEL, 2 * D_FF)
    bgu_all = b_gate_up.reshape(depth * N_EXPERTS, 1, 2 * D_FF)
    wdn_all = w_down.reshape(depth * N_EXPERTS, D_FF, D_MODEL)
    bdn_all = b_down.reshape(depth * N_EXPERTS, 1, D_MODEL)

    for l in range(depth):
        win2, wq2, wkv2, wr2, br, tab = _layer_weights(
            w_in[l], w_uq[l], w_ukv[l], w_router[l], b_router[l], rel_bias[l])
        qt, k, vt, sbq, sbk, sbv, caq, cak, cav = _inproj(
            xf, row2(attn_norm[l]), win2, cos_t, sin_t, row2(q_norm[l]), wq2, row2(kv_norm[l]), wkv2)

        g = mix_norm[l]
        ma = _mla_attention(qt, k, vt, row2(g[:A_W]), batch, seq)
        mb = _sb_attention(sbq, sbk, sbv, row2(g[A_W:A_W + SB_W]), batch, seq)
        pad = lambda a: jnp.pad(a.reshape(batch, seq, CA_W), ((0, 0), (CA_PAD, 0), (0, 0))).reshape(-1, CA_W)
        mc = _ca_attention(caq, pad(cak), pad(cav), tab, row2(g[A_W + SB_W:]), batch, seq)

        xn, h, lpos, meta, runs, totals = _outproj_router(
            ma, mb, mc, w_o[l].astype(jnp.bfloat16), xf, row2(ffn_norm[l]), wr2, br)

        sizes = totals[0, :N_EXPERTS]
        padded = (sizes + TM_EXP - 1) // TM_EXP * TM_EXP
        p_ends = jnp.cumsum(padded)
        p_starts = (p_ends - padded).astype(jnp.int32)
        blk_start = jnp.arange(n_blk, dtype=jnp.int32) * TM_EXP
        blk_valid = (blk_start < p_ends[-1]).astype(jnp.int32)
        last_row = jnp.minimum(blk_start, p_ends[-1] - 1)
        blk_e = jnp.minimum(jnp.sum((last_row[:, None] >= p_ends[None, :]).astype(jnp.int32), axis=1),
                            N_EXPERTS - 1)
        blk_first = jnp.concatenate([jnp.ones((1,), jnp.int32), (blk_e[1:] != blk_e[:-1]).astype(jnp.int32)])

        xs = _dispatch(p_starts, sizes, padded.astype(jnp.int32), runs, lpos, h, n_rows)
        ys = _expert_ffn(blk_e + l * N_EXPERTS, blk_first, blk_valid, xs, wgu_all, bgu_all, wdn_all, bdn_all)
        xf = _combine(p_starts, runs, meta, xn, ys, row2(final_norm), final=(l == depth - 1))

    return xf.reshape(batch, seq, D_MODEL)
```

```python
import functools
import math

import jax
import jax.numpy as jnp
from jax import lax
from jax.experimental import pallas as pl
from jax.experimental.pallas import tpu as pltpu

D_MODEL = 1024
RMS_EPS = 1e-6
MLA_NOPE, MLA_ROPE, MLA_V, MLA_HEADS = 64, 32, 64, 8
MLA_Q_RANK, MLA_KV_RANK = 256, 128
ROPE_THETA = 10000.0
SB_DIM, SB_HEADS = 64, 4
CA_DIM, CA_HEADS = 64, 4
CHUNK = 64
CA_LEFT_CHUNKS = 8
REL_CLIP = 256
N_EXPERTS, TOP_K = 32, 4
D_FF = 1024
SWIGLU_LIMIT, SWIGLU_ALPHA = 7.0, 1.702

LANES = 128
HEAD_PAD = 128
A_W = MLA_HEADS * MLA_V
SB_W = SB_HEADS * SB_DIM
CA_W = CA_HEADS * CA_DIM
CA_PAD = CA_LEFT_CHUNKS * CHUNK
CA_WIN = CA_PAD + 2 * CHUNK
REL_TAB = 1024
HALF = D_MODEL // 2

TM_TOK = 256
TQ_MLA = 256
TQ_SB = 256
TK_SB = 128
TQ_CA = 2 * CHUNK
CA_BLOCKS = 2
ROUTER_TILES = 2
INPROJ_TILES = 2
TM_EXP = 512
RUN_ALIGN = 8
LOCAL_ROWS = 1280
assert LOCAL_ROWS >= TM_TOK * TOP_K + N_EXPERTS * (RUN_ALIGN - 1) and LOCAL_ROWS % LANES == 0
assert TM_EXP % TM_TOK == 0 and TQ_MLA == TM_TOK

_C_CQ, _C_CKV, _C_KPE, _C_KROT = 0, 256, 384, 512
_C_SB, _C_CA, D_IN2 = 640, 1408, 2176

NEG = -1e30
LOG2E = math.log2(math.e)
VMEM_LIMIT = 56 * 1024 * 1024


def _rms(v, g):
    return v * lax.rsqrt(jnp.mean(v * v, axis=-1, keepdims=True) + RMS_EPS) * g


def _dot(a, b):
    return jnp.dot(a, b, preferred_element_type=jnp.float32)


def _dot_nt(a, b):
    return lax.dot_general(a, b, (((1,), (1,)), ((), ())), preferred_element_type=jnp.float32)


def _bf(v):
    return v.astype(jnp.bfloat16)


def _pack_rows(v):
    bits = lax.bitcast_convert_type(v, jnp.uint32)
    return (bits[:, :HALF] >> 16) | (bits[:, HALF:] & jnp.uint32(0xFFFF0000))


def _unpack_rows(u):
    lo = lax.bitcast_convert_type(u << 16, jnp.float32)
    hi = lax.bitcast_convert_type(u & jnp.uint32(0xFFFF0000), jnp.float32)
    return _bf(lo), _bf(hi)


def _cparams(sem):
    return pltpu.CompilerParams(dimension_semantics=sem, vmem_limit_bytes=VMEM_LIMIT)


ROPE_PACK = LANES // MLA_ROPE


def _rope_kernel(pos_ref, inv_ref, cos_ref, sin_ref):
    ang = pos_ref[...].astype(jnp.float32) * inv_ref[...]
    cos_ref[...] = jnp.cos(ang)
    sin_ref[...] = jnp.sin(ang)


def _rope_tables(positions, n):
    inv = ROPE_THETA ** (-jnp.arange(0, MLA_ROPE, 2, dtype=jnp.float32) / MLA_ROPE)
    inv_row = jnp.tile(jnp.concatenate([inv, inv]), ROPE_PACK).reshape(1, LANES)
    rows = n // ROPE_PACK
    pos_packed = jnp.repeat(positions.reshape(rows, ROPE_PACK), MLA_ROPE, axis=1)
    tm = min(TM_TOK, rows)
    cos_p, sin_p = pl.pallas_call(
        _rope_kernel,
        grid=(rows // tm,),
        in_specs=[pl.BlockSpec((tm, LANES), lambda i: (i, 0)),
                  pl.BlockSpec((1, LANES), lambda i: (0, 0))],
        out_specs=[pl.BlockSpec((tm, LANES), lambda i: (i, 0))] * 2,
        out_shape=[jax.ShapeDtypeStruct((rows, LANES), jnp.float32)] * 2,
        compiler_params=_cparams(("parallel",)),
        name="rope_tables",
    )(pos_packed, inv_row)
    pad = HEAD_PAD - MLA_NOPE - MLA_ROPE
    ones = lambda width: jnp.ones((n, width), jnp.float32)
    zeros = lambda width: jnp.zeros((n, width), jnp.float32)
    cos_t = jnp.concatenate([ones(MLA_NOPE), cos_p.reshape(n, MLA_ROPE), ones(pad)], axis=1)
    sin_t = jnp.concatenate([zeros(MLA_NOPE), sin_p.reshape(n, MLA_ROPE), zeros(pad)], axis=1)
    return cos_t, sin_t


def _inproj_kernel(x_ref, g_ref, win_ref, cos_ref, sin_ref, qn_ref, wq_ref, kvn_ref, wkv_ref,
                   q_ref, k_ref, v_ref, sbq_ref, sbk_ref, sbv_ref, caq_ref, cak_ref, cav_ref):
    for j in range(INPROJ_TILES):
        _inproj_tile(j, x_ref, g_ref, win_ref, cos_ref, sin_ref, qn_ref, wq_ref, kvn_ref, wkv_ref,
                     q_ref, k_ref, v_ref, sbq_ref, sbk_ref, sbv_ref, caq_ref, cak_ref, cav_ref)


def _inproj_tile(j, x_ref, g_ref, win_ref, cos_ref, sin_ref, qn_ref, wq_ref, kvn_ref, wkv_ref,
                 q_ref, k_ref, v_ref, sbq_ref, sbk_ref, sbv_ref, caq_ref, cak_ref, cav_ref):
    rows = slice(j * TM_TOK, (j + 1) * TM_TOK)
    h = _rms(x_ref[rows, :], g_ref[...])
    proj = _dot(_bf(h), win_ref[...])
    cos128, sin128 = cos_ref[rows, :], sin_ref[rows, :]
    cos_h = jnp.concatenate([cos128] * MLA_HEADS, axis=1)
    sin_h = jnp.concatenate([sin128] * MLA_HEADS, axis=1)
    w = MLA_HEADS * HEAD_PAD

    cqn = _rms(proj[:, _C_CQ:_C_CKV], qn_ref[...])
    q2 = _dot(_bf(cqn), wq_ref[...])
    q = (q2[:, :w] * cos_h + q2[:, w:] * sin_h) * (LOG2E / math.sqrt(MLA_NOPE + MLA_ROPE))
    q_ref[j] = _bf(jnp.transpose(q))

    ckvn = _rms(proj[:, _C_CKV:_C_KPE], kvn_ref[...])
    kv2 = _dot(_bf(ckvn), wkv_ref[...])
    kpe = proj[:, _C_KPE:_C_KROT] * cos128 + proj[:, _C_KROT:_C_SB] * sin128
    k_ref[rows, :] = _bf(kv2[:, :w] + jnp.concatenate([kpe] * MLA_HEADS, axis=1))
    v_ref[j] = _bf(jnp.transpose(kv2[:, w:]))

    sb_scale = 1.0 / math.sqrt(SB_DIM)
    sbq_ref[rows, :] = _bf(proj[:, _C_SB:_C_SB + SB_W] * sb_scale)
    sbk_ref[rows, :] = _bf(proj[:, _C_SB + SB_W:_C_SB + 2 * SB_W])
    sbv_ref[rows, :] = _bf(proj[:, _C_SB + 2 * SB_W:_C_CA])
    ca_scale = LOG2E / math.sqrt(CA_DIM)
    caq_ref[rows, :] = _bf(proj[:, _C_CA:_C_CA + CA_W] * ca_scale)
    cak_ref[rows, :] = _bf(proj[:, _C_CA + CA_W:_C_CA + 2 * CA_W])
    cav_ref[rows, :] = _bf(proj[:, _C_CA + 2 * CA_W:D_IN2])


def _inproj(x, g, win2, cos_t, sin_t, qn, wq2, kvn, wkv2):
    n = x.shape[0]
    tm = TM_TOK * INPROJ_TILES
    row = lambda width: pl.BlockSpec((tm, width), lambda i: (i, 0))
    full = lambda a: pl.BlockSpec(a.shape, lambda i: (0,) * a.ndim)
    widths = [MLA_HEADS * HEAD_PAD] + [SB_W] * 3 + [CA_W] * 3
    tile_t = lambda rows: pl.BlockSpec((INPROJ_TILES, rows, TM_TOK), lambda i: (i, 0, 0))
    shape_t = lambda rows: jax.ShapeDtypeStruct((n // TM_TOK, rows, TM_TOK), jnp.bfloat16)
    return pl.pallas_call(
        _inproj_kernel,
        grid=(n // tm,),
        in_specs=[row(D_MODEL), full(g), full(win2), row(LANES), row(LANES),
                  full(qn), full(wq2), full(kvn), full(wkv2)],
        out_specs=[tile_t(MLA_HEADS * HEAD_PAD), row(widths[0]), tile_t(A_W)] + [row(wd) for wd in widths[1:]],
        out_shape=[shape_t(MLA_HEADS * HEAD_PAD), jax.ShapeDtypeStruct((n, widths[0]), jnp.bfloat16), shape_t(A_W)]
                  + [jax.ShapeDtypeStruct((n, wd), jnp.bfloat16) for wd in widths[1:]],
        compiler_params=_cparams(("parallel",)),
        name="inproj",
    )(x, g, win2, cos_t, sin_t, qn, wq2, kvn, wkv2)


def _mla_kernel(qt_ref, k_ref, vt_ref, g_ref, o_ref, m_ref, l_ref, acc_ref, s_ref):
    i = pl.program_id(1)
    tq = TQ_MLA
    n_pairs = MLA_HEADS // 2
    key_chunk = lax.broadcasted_iota(jnp.int32, (tq, tq), 0) // CHUNK
    qry_chunk = lax.broadcasted_iota(jnp.int32, (tq, tq), 1) // CHUNK
    diag_ok = key_chunk <= qry_chunk
    top = lax.broadcasted_iota(jnp.int32, (LANES, tq), 0) < MLA_V

    m_ref[...] = jnp.full(m_ref.shape, -jnp.inf, jnp.float32)
    l_ref[...] = jnp.zeros(l_ref.shape, jnp.float32)
    acc_ref[...] = jnp.zeros(acc_ref.shape, jnp.float32)

    def scores(kb, slot):
        start = pl.multiple_of(kb * tq, tq)
        for h in range(MLA_HEADS):
            qt = qt_ref[0, h * HEAD_PAD:(h + 1) * HEAD_PAD, :]
            kblk = k_ref[pl.ds(start, tq), h * HEAD_PAD:(h + 1) * HEAD_PAD]
            s_ref[slot, h] = _dot(kblk, qt)

    def absorb(kb, slot, mask):
        for pr in range(n_pairs):
            vt = vt_ref[kb, pr * LANES:(pr + 1) * LANES, :]
            zero = jnp.zeros_like(vt)
            vt_bd = jnp.concatenate([jnp.where(top, vt, zero), jnp.where(top, zero, vt)], axis=1)
            pts, alphas = [], []
            for h in (2 * pr, 2 * pr + 1):
                s = s_ref[slot, h]
                if mask is not None:
                    s = jnp.where(mask, s, -jnp.inf)
                m_old = m_ref[h]
                m_new = jnp.maximum(m_old, jnp.max(s, axis=0, keepdims=True))
                p = jnp.exp2(s - m_new)
                alpha = jnp.exp2(m_old - m_new)
                l_ref[h] = alpha * l_ref[h] + jnp.sum(p, axis=0, keepdims=True)
                m_ref[h] = m_new
                pts.append(_bf(p))
                alphas.append(alpha)
            alpha_pair = jnp.where(top, alphas[0], alphas[1])
            acc_ref[pr] = alpha_pair * acc_ref[pr] + _dot(vt_bd, jnp.concatenate(pts, axis=0))

    def body(j, carry):
        scores(2 * j + 1, 1)
        absorb(2 * j, 0, None)
        scores(2 * j + 2, 0)
        absorb(2 * j + 1, 1, None)
        return carry

    scores(0, 0)
    lax.fori_loop(0, i // 2, body, 0)

    @pl.when(i % 2 == 0)
    def _():
        absorb(i, 0, diag_ok)

    @pl.when(i % 2 == 1)
    def _():
        scores(i, 1)
        absorb(i - 1, 0, None)
        absorb(i, 1, diag_ok)

    outs = [jnp.transpose(acc_ref[pr] / jnp.where(top, l_ref[2 * pr], l_ref[2 * pr + 1])) for pr in range(n_pairs)]
    o = jnp.concatenate(outs, axis=1)
    o_ref[...] = _bf(_rms(o, g_ref[...]))


def _mla_attention(qt, k, vt, g, batch, seq):
    nq = seq // TQ_MLA
    w = MLA_HEADS * HEAD_PAD
    return pl.pallas_call(
        _mla_kernel,
        grid=(batch, nq),
        in_specs=[pl.BlockSpec((1, w, TQ_MLA), lambda b, i: (b * nq + i, 0, 0)),
                  pl.BlockSpec((seq, w), lambda b, i: (b, 0)),
                  pl.BlockSpec((nq, A_W, TQ_MLA), lambda b, i: (b, 0, 0)),
                  pl.BlockSpec((1, A_W), lambda b, i: (0, 0))],
        out_specs=pl.BlockSpec((TQ_MLA, A_W), lambda b, i: (b * nq + i, 0)),
        out_shape=jax.ShapeDtypeStruct((batch * seq, A_W), jnp.bfloat16),
        scratch_shapes=[pltpu.VMEM((MLA_HEADS, 1, TQ_MLA), jnp.float32),
                        pltpu.VMEM((MLA_HEADS, 1, TQ_MLA), jnp.float32),
                        pltpu.VMEM((MLA_HEADS // 2, LANES, TQ_MLA), jnp.float32),
                        pltpu.VMEM((2, MLA_HEADS, TQ_MLA, TQ_MLA), jnp.float32)],
        compiler_params=_cparams(("parallel", "parallel")),
        name="mla_attention",
    )(qt, k, vt, g)


def _sb_kernel(q_ref, k_ref, v_ref, g_ref, o_ref, run_ref, acc_ref, zl_ref, sums_ref):
    i = pl.program_id(1)
    tq, tk = TQ_SB, TK_SB
    n_pairs = SB_HEADS // 2
    r = lax.broadcasted_iota(jnp.int32, (2 * tq, tk), 0)
    c = lax.broadcasted_iota(jnp.int32, (2 * tq, tk), 1)
    q_off = jnp.where(r >= tq, r - tq, r)
    top_low = (r < tq) == (c < SB_DIM)
    r2 = lax.broadcasted_iota(jnp.int32, (tk, 2 * tk), 0)
    c2 = lax.broadcasted_iota(jnp.int32, (tk, 2 * tk), 1)
    sum_mat = jnp.where((c2 >= tk) | (r2 > c2), 1.0, 0.0).astype(jnp.bfloat16)

    run_ref[...] = jnp.zeros(run_ref.shape, jnp.float32)
    acc_ref[...] = jnp.zeros(acc_ref.shape, jnp.float32)

    def step(kb_hi, masked):
        masks = []
        for d in range(2):
            kb = kb_hi - d
            start = pl.multiple_of(kb * tk, tk)
            mask = (c + (kb * tk - i * tq)) < q_off if masked else None
            masks.append(mask)
            for pr in range(n_pairs):
                qp = q_ref[:, pr * LANES:(pr + 1) * LANES]
                q2 = jnp.concatenate([qp, qp], axis=0)
                qm = jnp.where(top_low, q2, jnp.zeros_like(q2))
                kblk = k_ref[pl.ds(start, tk), pr * LANES:(pr + 1) * LANES]
                z = _dot_nt(qm, kblk)
                log_keep = -(jnp.maximum(z, 0.0) + jnp.log(1.0 + jnp.exp(-jnp.abs(z))))
                if masked:
                    log_keep = jnp.where(mask, log_keep, 0.0)
                hi = _bf(log_keep)
                lo = _bf(log_keep - hi.astype(jnp.float32))
                sums_ref[2 * d + pr] = _dot(hi, sum_mat) + _dot(lo, sum_mat)
                zl_ref[2 * d + pr] = z + log_keep
        slowest = None
        for d in range(2):
            start = pl.multiple_of((kb_hi - d) * tk, tk)
            for pr in range(n_pairs):
                vblk = v_ref[pl.ds(start, tk), pr * LANES:(pr + 1) * LANES]
                run = run_ref[pr]
                a = jnp.exp(zl_ref[2 * d + pr] + run + sums_ref[2 * d + pr, :, 0:tk])
                if masked:
                    a = jnp.where(masks[d], a, 0.0)
                acc_ref[pr] = acc_ref[pr] + _dot(_bf(a), vblk)
                run = run + sums_ref[2 * d + pr, :, tk:2 * tk]
                run_ref[pr] = run
                if d == 1:
                    top = jnp.max(run)
                    slowest = top if slowest is None else jnp.maximum(slowest, top)
        return slowest

    underflow = -104.0
    assert tq == 2 * tk
    first = step(2 * i + 1, True)
    lax.while_loop(lambda cr: (cr[0] >= 0) & (cr[1] > underflow),
                   lambda cr: (cr[0] - 1, step(2 * cr[0] + 1, False)),
                   (i - 1, first))

    lane = lax.broadcasted_iota(jnp.int32, (tq, LANES), 1)
    outs = [jnp.where(lane < SB_DIM, acc_ref[pr, 0:tq, :], acc_ref[pr, tq:2 * tq, :]) for pr in range(n_pairs)]
    o = jnp.concatenate(outs, axis=1)
    o_ref[...] = _bf(_rms(o, g_ref[...]))


def _sb_attention(q, k, v, g, batch, seq):
    nq = seq // TQ_SB
    return pl.pallas_call(
        _sb_kernel,
        grid=(batch, nq),
        in_specs=[pl.BlockSpec((TQ_SB, SB_W), lambda b, i: (b * nq + i, 0)),
                  pl.BlockSpec((seq, SB_W), lambda b, i: (b, 0)),
                  pl.BlockSpec((seq, SB_W), lambda b, i: (b, 0)),
                  pl.BlockSpec((1, SB_W), lambda b, i: (0, 0))],
        out_specs=pl.BlockSpec((TQ_SB, SB_W), lambda b, i: (b * nq + i, 0)),
        out_shape=jax.ShapeDtypeStruct((batch * seq, SB_W), jnp.bfloat16),
        scratch_shapes=[pltpu.VMEM((SB_HEADS // 2, 2 * TQ_SB, TK_SB), jnp.float32),
                        pltpu.VMEM((SB_HEADS // 2, 2 * TQ_SB, LANES), jnp.float32),
                        pltpu.VMEM((SB_HEADS, 2 * TQ_SB, TK_SB), jnp.float32),
                        pltpu.VMEM((SB_HEADS, 2 * TQ_SB, 2 * TK_SB), jnp.float32)],
        compiler_params=_cparams(("parallel", "parallel")),
        name="sb_attention",
    )(q, k, v, g)


def _ca_kernel(q_ref, k_ref, v_ref, tab_ref, g_ref, o_ref, bias_ref):
    t = TQ_CA

    @pl.when((pl.program_id(0) == 0) & (pl.program_id(1) == 0))
    def _():
        r = lax.broadcasted_iota(jnp.int32, (t, CA_WIN), 0)
        c = lax.broadcasted_iota(jnp.int32, (t, CA_WIN), 1)
        lo = (r // CHUNK) * CHUNK
        band = (c >= lo) & (c < lo + CA_PAD + CHUNK)
        for h in range(CA_HEADS):
            tab = jnp.broadcast_to(tab_ref[h:h + 1, :], (t, REL_TAB))
            bias = pltpu.roll(tab, REL_TAB - (REL_CLIP - 1), 1, stride=1, stride_axis=0)[:, :CA_WIN]
            bias_ref[h] = jnp.where(band, bias * LOG2E, -jnp.inf)

    for j in range(CA_BLOCKS):
        _ca_block(pl.program_id(1) * CA_BLOCKS + j, slice(j * t, (j + 1) * t), q_ref, k_ref, v_ref, g_ref, o_ref, bias_ref)


def _ca_block(i, rows, q_ref, k_ref, v_ref, g_ref, o_ref, bias_ref):
    t = TQ_CA
    start = pl.multiple_of(i * t, t)
    r2 = lax.broadcasted_iota(jnp.int32, (2 * t, LANES), 0)
    c2 = lax.broadcasted_iota(jnp.int32, (2 * t, LANES), 1)
    top_low = (r2 < t) == (c2 < CA_DIM)
    in_seq = lax.broadcasted_iota(jnp.int32, (2 * t, CA_WIN), 1) + i * t >= CA_PAD
    lane = lax.broadcasted_iota(jnp.int32, (t, LANES), 1)

    outs = []
    for pair in range(CA_HEADS // 2):
        qp = q_ref[rows, pair * LANES:(pair + 1) * LANES]
        q2 = jnp.concatenate([qp, qp], axis=0)
        qm = jnp.where(top_low, q2, jnp.zeros_like(q2))
        kwin = k_ref[pl.ds(start, CA_WIN), pair * LANES:(pair + 1) * LANES]
        vwin = v_ref[pl.ds(start, CA_WIN), pair * LANES:(pair + 1) * LANES]
        bias = jnp.concatenate([bias_ref[2 * pair], bias_ref[2 * pair + 1]], axis=0)
        s = jnp.where(in_seq, _dot_nt(qm, kwin) + bias, -jnp.inf)
        m = jnp.max(s, axis=-1, keepdims=True)
        p = jnp.exp2(s - m)
        l = jnp.sum(p, axis=-1, keepdims=True)
        o2 = _dot(_bf(p), vwin) / l
        outs.append(jnp.where(lane < CA_DIM, o2[0:t], o2[t:2 * t]))
    o = jnp.concatenate(outs, axis=1)
    o_ref[rows, :] = _bf(_rms(o, g_ref[...]))


def _ca_attention(q, kpad, vpad, tab, g, batch, seq):
    tq = TQ_CA * CA_BLOCKS
    nq = seq // tq
    return pl.pallas_call(
        _ca_kernel,
        grid=(batch, nq),
        in_specs=[pl.BlockSpec((tq, CA_W), lambda b, i: (b * nq + i, 0)),
                  pl.BlockSpec((seq + CA_PAD, CA_W), lambda b, i: (b, 0)),
                  pl.BlockSpec((seq + CA_PAD, CA_W), lambda b, i: (b, 0)),
                  pl.BlockSpec((CA_HEADS, REL_TAB), lambda b, i: (0, 0)),
                  pl.BlockSpec((1, CA_W), lambda b, i: (0, 0))],
        out_specs=pl.BlockSpec((tq, CA_W), lambda b, i: (b * nq + i, 0)),
        out_shape=jax.ShapeDtypeStruct((batch * seq, CA_W), jnp.bfloat16),
        scratch_shapes=[pltpu.VMEM((CA_HEADS, TQ_CA, CA_WIN), jnp.float32)],
        compiler_params=_cparams(("arbitrary", "arbitrary")),
        name="ca_attention",
    )(q, kpad, vpad, tab, g)


def _outproj_router_kernel(ma_ref, mb_ref, mc_ref, wo_ref, x_ref, g_ref, wr2_ref, br_ref,
                           xn_ref, h_ref, lpos_ref, meta_ref, runs_ref, tot_ref, base_ref):
    @pl.when(pl.program_id(0) == 0)
    def _():
        base_ref[...] = jnp.zeros_like(base_ref)

    base = base_ref[...]
    for j in range(ROUTER_TILES):
        base = _route_tile(j, base, ma_ref, mb_ref, mc_ref, wo_ref, x_ref, g_ref, wr2_ref, br_ref,
                           xn_ref, h_ref, lpos_ref, meta_ref, runs_ref)
    base_ref[...] = base
    tot_ref[...] = base.astype(jnp.int32)


def _route_tile(j, base, ma_ref, mb_ref, mc_ref, wo_ref, x_ref, g_ref, wr2_ref, br_ref,
                xn_ref, h_ref, lpos_ref, meta_ref, runs_ref):
    tm = TM_TOK
    rows = slice(j * tm, (j + 1) * tm)
    attn = (_dot(ma_ref[rows, :], wo_ref[0:A_W, :]) + _dot(mb_ref[rows, :], wo_ref[A_W:A_W + SB_W, :])
            + _dot(mc_ref[rows, :], wo_ref[A_W + SB_W:, :]))
    xn = x_ref[rows, :] + attn
    xn_ref[rows, :] = xn
    h = _rms(xn, g_ref[...])

    h_hi = _bf(h)
    h_ref[rows, :] = h_hi
    h_lo = _bf(h - h_hi.astype(jnp.float32))
    both = _dot(h_hi, wr2_ref[...])
    logits = (both[:, :LANES] + both[:, LANES:] + _dot(h_lo, wr2_ref[:, 0:LANES])
              + br_ref[...])
    lane = lax.broadcasted_iota(jnp.int32, (tm, LANES), 1)
    lane_f = lane.astype(jnp.float32)

    work = logits
    vals, idxs, hots = [], [], []
    for _ in range(TOP_K):
        mx = jnp.max(work, axis=-1, keepdims=True)
        ix = jnp.min(jnp.where(work == mx, lane_f, float(LANES)), axis=-1, keepdims=True)
        hot = lane_f == ix
        work = jnp.where(hot, -jnp.inf, work)
        vals.append(mx)
        idxs.append(ix)
        hots.append(hot)
    exps = [jnp.exp(v - vals[0]) for v in vals]
    denom = exps[0] + exps[1] + exps[2] + exps[3]
    gates = [e / denom for e in exps]

    sel = jnp.zeros((tm, LANES), jnp.float32)
    for hot in hots:
        sel = sel + jnp.where(hot, 1.0, 0.0)
    r = lax.broadcasted_iota(jnp.int32, (tm, tm), 0)
    c = lax.broadcasted_iota(jnp.int32, (tm, tm), 1)
    before = jnp.where(c < r, 1.0, 0.0).astype(jnp.bfloat16)
    rank_in_tile = _dot(before, _bf(sel))

    cnt = jnp.sum(sel, axis=0, keepdims=True)
    cnt_al = jnp.ceil(cnt * (1.0 / RUN_ALIGN)) * RUN_ALIGN
    rl = lax.broadcasted_iota(jnp.int32, (LANES, LANES), 0)
    cl = lax.broadcasted_iota(jnp.int32, (LANES, LANES), 1)
    earlier = jnp.where(rl < cl, 1.0, 0.0).astype(jnp.bfloat16)
    loff = _dot(_bf(jnp.broadcast_to(cnt_al, (8, LANES))), earlier)[0:1, :]
    sub = lax.broadcasted_iota(jnp.int32, (8, LANES), 0)
    has_long = jnp.where(jnp.max(cnt_al, axis=-1, keepdims=True) >= LONG_RUN, 1.0, 0.0)
    runs = jnp.where(sub == 0, loff, jnp.where(sub == 1, base, jnp.where(sub == 2, cnt_al,
                     jnp.where(sub == 3, has_long, 0.0))))
    runs_ref[8 * j:8 * j + 8, :] = runs.astype(jnp.int32)

    lpos_dense = loff + rank_in_tile
    meta = jnp.zeros((tm, LANES), jnp.float32)
    for kk in range(TOP_K):
        lpos = jnp.sum(jnp.where(hots[kk], lpos_dense, 0.0), axis=-1, keepdims=True)
        meta = meta + jnp.where(lane == kk, gates[kk], 0.0) + jnp.where(lane == TOP_K + kk, lpos, 0.0)
    meta_ref[rows, :] = meta
    lpos_ref[:, rows] = jnp.transpose(meta)[TOP_K:TOP_K + 8, :].astype(jnp.int32)
    return base + cnt_al


def _outproj_router(ma, mb, mc, wo, x, g, wr2, br):
    n = x.shape[0]
    tm = TM_TOK * ROUTER_TILES
    row = lambda width: pl.BlockSpec((tm, width), lambda i: (i, 0))
    full = lambda a: pl.BlockSpec(a.shape, lambda i: (0,) * a.ndim)
    return pl.pallas_call(
        _outproj_router_kernel,
        grid=(n // tm,),
        in_specs=[row(A_W), row(SB_W), row(CA_W), full(wo), row(D_MODEL), full(g), full(wr2), full(br)],
        out_specs=[row(D_MODEL), row(D_MODEL), pl.BlockSpec((8, tm), lambda i: (0, i)),
                   row(LANES), pl.BlockSpec((8 * ROUTER_TILES, LANES), lambda i: (i, 0)),
                   pl.BlockSpec((1, LANES), lambda i: (0, 0))],
        out_shape=[jax.ShapeDtypeStruct((n, D_MODEL), jnp.float32),
                   jax.ShapeDtypeStruct((n, D_MODEL), jnp.bfloat16),
                   jax.ShapeDtypeStruct((8, n), jnp.int32),
                   jax.ShapeDtypeStruct((n, LANES), jnp.float32),
                   jax.ShapeDtypeStruct((8 * (n // TM_TOK), LANES), jnp.int32),
                   jax.ShapeDtypeStruct((1, LANES), jnp.int32)],
        scratch_shapes=[pltpu.VMEM((1, LANES), jnp.float32)],
        compiler_params=_cparams(("arbitrary",)),
        name="outproj_router",
    )(ma, mb, mc, wo, x, g, wr2, br)


_RUN_CHUNKS = tuple(TM_TOK >> s for s in range(TM_TOK.bit_length()) if (TM_TOK >> s) >= RUN_ALIGN)


LONG_RUN = 128


def _for_each_chunk(length, fn, sizes=_RUN_CHUNKS):
    for size in sizes:
        off = length & (~(2 * size - 1))

        @pl.when((length & size) != 0)
        def _(off=off, size=size):
            fn(off, size)


_WAIT_CHUNKS = tuple(1 << b for b in range((LOCAL_ROWS).bit_length() - 1, RUN_ALIGN.bit_length() - 2, -1))


def _wait_tile_runs(runs_ref, make_copy):
    total = runs_ref[0, N_EXPERTS - 1] + runs_ref[2, N_EXPERTS - 1]
    for size in _WAIT_CHUNKS:
        @pl.when((total & size) != 0)
        def _(size=size):
            make_copy(size).wait()


def _tile_runs(runs_ref, pstart_ref, fn):
    def experts(sizes, unroll):
        def body(e, _):
            local, glob, length = runs_ref[0, e], pstart_ref[e] + runs_ref[1, e], runs_ref[2, e]
            _for_each_chunk(length, lambda off, size: fn(pl.multiple_of(local + off, RUN_ALIGN),
                                                         pl.multiple_of(glob + off, RUN_ALIGN), size), sizes)
            return 0

        lax.fori_loop(0, N_EXPERTS, body, 0, unroll=unroll)

    experts(tuple(sz for sz in _RUN_CHUNKS if sz < LONG_RUN), 2)

    @pl.when(runs_ref[3, 0] != 0)
    def _():
        experts(tuple(sz for sz in _RUN_CHUNKS if sz >= LONG_RUN), 1)


def _dispatch_kernel(pstart_ref, tot_ref, pad_ref, runs_ref, prev_runs_ref, lpos_ref, h_ref, xs_ref,
                     loc, zbuf, sem, fill_sem):
    i = pl.program_id(0)
    slot = i % 2

    def fill(start_or_wait):
        def body(e, _):
            first = pstart_ref[e] + tot_ref[e]
            _for_each_chunk(pad_ref[e] - tot_ref[e], lambda off, size: start_or_wait(pltpu.make_async_copy(
                zbuf.at[pl.ds(0, size)], xs_ref.at[pl.ds(pl.multiple_of(first + off, RUN_ALIGN), size)], fill_sem)))
            return 0

        lax.fori_loop(0, N_EXPERTS, body, 0)

        used = pstart_ref[N_EXPERTS - 1] + pad_ref[N_EXPERTS - 1]

        def tail(b, _):
            start_or_wait(pltpu.make_async_copy(
                zbuf, xs_ref.at[pl.ds(pl.multiple_of(b * TM_TOK, TM_TOK), TM_TOK)], fill_sem))
            return 0

        lax.fori_loop(used // TM_TOK, xs_ref.shape[0] // TM_TOK, tail, 0)

    @pl.when(i == 0)
    def _():
        zbuf[...] = jnp.zeros_like(zbuf)
        fill(lambda c: c.start())
        fill(lambda c: c.wait())

    r = lax.broadcasted_iota(jnp.int32, (LOCAL_ROWS, TM_TOK), 0)
    hit = r == lpos_ref[0:1, :]
    for kk in range(1, TOP_K):
        hit = hit | (r == lpos_ref[kk:kk + 1, :])
    loc[slot] = _pack_rows(_dot(jnp.where(hit, 1.0, 0.0).astype(jnp.bfloat16), h_ref[...]))

    def copy(buf, local, glob, size):
        return pltpu.make_async_copy(loc.at[buf, pl.ds(local, size)], xs_ref.at[pl.ds(glob, size)], sem.at[buf])

    _tile_runs(runs_ref, pstart_ref, lambda l, g, s: copy(slot, l, g, s).start())

    @pl.when(i > 0)
    def _():
        _wait_tile_runs(prev_runs_ref, lambda size: copy(1 - slot, 0, 0, size))

    @pl.when(i == pl.num_programs(0) - 1)
    def _():
        _wait_tile_runs(runs_ref, lambda size: copy(slot, 0, 0, size))


def _dispatch(pstart, totals, padded, runs, lpos, h, n_rows):
    n = h.shape[0]
    tm = TM_TOK
    grid_spec = pltpu.PrefetchScalarGridSpec(
        num_scalar_prefetch=3,
        grid=(n // tm,),
        in_specs=[pl.BlockSpec((8, LANES), lambda i, *_: (i, 0), memory_space=pltpu.SMEM),
                  pl.BlockSpec((8, LANES), lambda i, *_: (jnp.maximum(i - 1, 0), 0), memory_space=pltpu.SMEM),
                  pl.BlockSpec((8, tm), lambda i, *_: (0, i)),
                  pl.BlockSpec((tm, D_MODEL), lambda i, *_: (i, 0))],
        out_specs=pl.BlockSpec(memory_space=pl.ANY),
        scratch_shapes=[pltpu.VMEM((2, LOCAL_ROWS, HALF), jnp.uint32),
                        pltpu.VMEM((TM_TOK, HALF), jnp.uint32),
                        pltpu.SemaphoreType.DMA((2,)),
                        pltpu.SemaphoreType.DMA],
    )
    return pl.pallas_call(
        _dispatch_kernel,
        grid_spec=grid_spec,
        out_shape=jax.ShapeDtypeStruct((n_rows, HALF), jnp.uint32),
        compiler_params=_cparams(("arbitrary",)),
        name="dispatch",
    )(pstart, totals, padded, runs, runs, lpos, h)


def _expert_kernel(be_ref, bfirst_ref, bvalid_ref, xs_ref, wgu_ref, bgu_ref, wdn_ref, bdn_ref,
                   ys_ref, wgu_bf, wdn_bf):
    b = pl.program_id(0)

    @pl.when(bfirst_ref[b] == 1)
    def _():
        wgu_bf[...] = _bf(wgu_ref[0])
        wdn_bf[...] = _bf(wdn_ref[0])

    @pl.when(bvalid_ref[b] == 1)
    def _():
        xb = jnp.concatenate(_unpack_rows(xs_ref[...]), axis=1)
        gu = _dot(xb, wgu_bf[...]) + bgu_ref[0]
        gte = jnp.minimum(gu[:, :D_FF], SWIGLU_LIMIT)
        up = jnp.clip(gu[:, D_FF:], -SWIGLU_LIMIT, SWIGLU_LIMIT)
        act = (up + 1.0) * (gte * (1.0 / (1.0 + jnp.exp(-SWIGLU_ALPHA * gte))))
        y = _dot(_bf(act), wdn_bf[...]) + bdn_ref[0]
        ys_ref[...] = _pack_rows(_bf(y).astype(jnp.float32))

    @pl.when(bvalid_ref[b] == 0)
    def _():
        ys_ref[...] = jnp.zeros_like(ys_ref)


def _expert_ffn(blk_e, blk_first, blk_valid, xs, wgu, bgu, wdn, bdn):
    n_rows = xs.shape[0]
    tm = TM_EXP
    grid_spec = pltpu.PrefetchScalarGridSpec(
        num_scalar_prefetch=3,
        grid=(n_rows // tm,),
        in_specs=[pl.BlockSpec((tm, HALF), lambda b, e, f, v: (b, 0)),
                  pl.BlockSpec((1, D_MODEL, 2 * D_FF), lambda b, e, f, v: (e[b], 0, 0)),
                  pl.BlockSpec((1, 1, 2 * D_FF), lambda b, e, f, v: (e[b], 0, 0)),
                  pl.BlockSpec((1, D_FF, D_MODEL), lambda b, e, f, v: (e[b], 0, 0)),
                  pl.BlockSpec((1, 1, D_MODEL), lambda b, e, f, v: (e[b], 0, 0))],
        out_specs=pl.BlockSpec((tm, HALF), lambda b, e, f, v: (b, 0)),
        scratch_shapes=[pltpu.VMEM((D_MODEL, 2 * D_FF), jnp.bfloat16),
                        pltpu.VMEM((D_FF, D_MODEL), jnp.bfloat16)],
    )
    return pl.pallas_call(
        _expert_kernel,
        grid_spec=grid_spec,
        out_shape=jax.ShapeDtypeStruct((n_rows, HALF), jnp.uint32),
        compiler_params=_cparams(("arbitrary",)),
        name="expert_ffn",
    )(blk_e, blk_first, blk_valid, xs, wgu, bgu, wdn, bdn)


def _combine_kernel(final, pstart_ref, runs_ref, next_runs_ref, meta_ref, x_ref, ys_ref, gfin_ref, o_ref, loc, sem):
    i = pl.program_id(0)
    slot = i % 2

    def copy(buf, local, glob, size):
        return pltpu.make_async_copy(ys_ref.at[pl.ds(glob, size)], loc.at[buf, pl.ds(local, size)], sem.at[buf])

    @pl.when(i == 0)
    def _():
        loc[...] = jnp.zeros_like(loc)
        _tile_runs(runs_ref, pstart_ref, lambda l, g, s: copy(slot, l, g, s).start())

    @pl.when(i + 1 < pl.num_programs(0))
    def _():
        _tile_runs(next_runs_ref, pstart_ref, lambda l, g, s: copy(1 - slot, l, g, s).start())

    _wait_tile_runs(runs_ref, lambda size: copy(slot, 0, 0, size))

    meta = meta_ref[...]
    col = lax.broadcasted_iota(jnp.int32, (TM_TOK, LOCAL_ROWS), 1).astype(jnp.float32)
    wts = jnp.zeros((TM_TOK, LOCAL_ROWS), jnp.float32)
    for kk in range(TOP_K):
        wts = wts + jnp.where(col == meta[:, TOP_K + kk:TOP_K + kk + 1], meta[:, kk:kk + 1], 0.0)
    w_hi = _bf(wts)
    w_lo = _bf(wts - w_hi.astype(jnp.float32))
    w2 = jnp.concatenate([w_hi, w_lo], axis=0)
    y_lo, y_hi = _unpack_rows(loc[slot])
    r_lo, r_hi = _dot(w2, y_lo), _dot(w2, y_hi)
    out = x_ref[...] + jnp.concatenate([r_lo[:TM_TOK] + r_lo[TM_TOK:], r_hi[:TM_TOK] + r_hi[TM_TOK:]], axis=1)
    if final:
        out = _rms(out, gfin_ref[...])
    o_ref[...] = out


def _combine(pstart, runs, meta, x, ys, gfin, final):
    n = x.shape[0]
    tm = TM_TOK
    n_tiles = n // tm
    grid_spec = pltpu.PrefetchScalarGridSpec(
        num_scalar_prefetch=1,
        grid=(n_tiles,),
        in_specs=[pl.BlockSpec((8, LANES), lambda i, ps: (i, 0), memory_space=pltpu.SMEM),
                  pl.BlockSpec((8, LANES), lambda i, ps: (jnp.minimum(i + 1, n_tiles - 1), 0), memory_space=pltpu.SMEM),
                  pl.BlockSpec((tm, LANES), lambda i, ps: (i, 0)),
                  pl.BlockSpec((tm, D_MODEL), lambda i, ps: (i, 0)),
                  pl.BlockSpec(memory_space=pl.ANY),
                  pl.BlockSpec((1, D_MODEL), lambda i, ps: (0, 0))],
        out_specs=pl.BlockSpec((tm, D_MODEL), lambda i, ps: (i, 0)),
        scratch_shapes=[pltpu.VMEM((2, LOCAL_ROWS, HALF), jnp.uint32), pltpu.SemaphoreType.DMA((2,))],
    )
    return pl.pallas_call(
        functools.partial(_combine_kernel, final),
        grid_spec=grid_spec,
        out_shape=jax.ShapeDtypeStruct((n, D_MODEL), jnp.float32),
        compiler_params=_cparams(("arbitrary",)),
        name="combine_final" if final else "combine",
    )(pstart, runs, runs, meta, x, ys, gfin)


def _pad_heads(w, parts):
    rows = w.shape[0]
    per = w.shape[1] // MLA_HEADS
    w3 = w.reshape(rows, MLA_HEADS, per)
    cols = [jnp.zeros((rows, MLA_HEADS, b - a), w.dtype) if sign == 0 else sign * w3[:, :, a:b]
            for a, b, sign in parts]
    used = sum(b - a for a, b, _ in parts)
    cols.append(jnp.zeros((rows, MLA_HEADS, HEAD_PAD - used), w.dtype))
    return jnp.concatenate(cols, axis=2).reshape(rows, MLA_HEADS * HEAD_PAD)


def _layer_weights(w_in, w_uq, w_ukv, w_router, b_router, rel_bias):
    half = MLA_ROPE // 2
    cq, ckv, kpe, sb, ca = (w_in[:, 0:256], w_in[:, 256:384], w_in[:, 384:416],
                            w_in[:, 416:1184], w_in[:, 1184:1952])
    z = lambda width: jnp.zeros((D_MODEL, width), w_in.dtype)
    kpe_pad = jnp.concatenate([z(MLA_NOPE), kpe, z(HEAD_PAD - MLA_NOPE - MLA_ROPE)], axis=1)
    kpe_rot = jnp.concatenate([z(MLA_NOPE), -kpe[:, half:], kpe[:, :half],
                               z(HEAD_PAD - MLA_NOPE - MLA_ROPE)], axis=1)
    win2 = jnp.concatenate([cq, ckv, kpe_pad, kpe_rot, sb, ca], axis=1).astype(jnp.bfloat16)

    d = MLA_NOPE + MLA_ROPE
    wq_full = _pad_heads(w_uq, [(0, d, 1)])
    wq_rot = _pad_heads(w_uq, [(0, MLA_NOPE, 0), (MLA_NOPE + half, d, -1), (MLA_NOPE, MLA_NOPE + half, 1)])
    wq2 = jnp.concatenate([wq_full, wq_rot], axis=1).astype(jnp.bfloat16)

    wk = _pad_heads(w_ukv, [(0, MLA_NOPE, 1)])
    wv = w_ukv.reshape(MLA_KV_RANK, MLA_HEADS, MLA_NOPE + MLA_V)[:, :, MLA_NOPE:].reshape(MLA_KV_RANK, A_W)
    wkv2 = jnp.concatenate([wk, wv], axis=1).astype(jnp.bfloat16)

    wr = jnp.pad(w_router, ((0, 0), (0, LANES - N_EXPERTS)))
    wr_hi = wr.astype(jnp.bfloat16)
    wr_lo = (wr - wr_hi.astype(jnp.float32)).astype(jnp.bfloat16)
    wr2 = jnp.concatenate([wr_hi, wr_lo], axis=1)
    br = jnp.pad(b_router, (0, LANES - N_EXPERTS), constant_values=NEG).reshape(1, LANES)

    ext = jnp.concatenate([rel_bias, jnp.broadcast_to(rel_bias[:, -1:], (CA_HEADS, REL_TAB - 2 * REL_CLIP - 1))], axis=1)
    tab = ext[:, ::-1]
    return win2, wq2, wkv2, wr2, br, tab


def kernel(x, positions, attn_norm, w_in, q_norm, w_uq, kv_norm, w_ukv, rel_bias, mix_norm,
           w_o, ffn_norm, w_router, b_router, w_gate_up, b_gate_up, w_down, b_down, final_norm):
    batch, seq, _ = x.shape
    n = batch * seq
    depth = w_in.shape[0]
    xf = x.reshape(n, D_MODEL)

    cos_t, sin_t = _rope_tables(positions, n)

    n_rows = n * TOP_K + (n // TM_TOK) * N_EXPERTS * (RUN_ALIGN - 1) + N_EXPERTS * TM_EXP
    n_rows = -(-n_rows // TM_EXP) * TM_EXP
    n_blk = n_rows // TM_EXP
    row2 = lambda v: v.reshape(1, -1)
    wgu_all = w_gate_up.reshape(depth * N_EXPERTS, D_MODEL, 2 * D_FF)
    bgu_all = b_gate_up.reshape(depth * N_EXPERTS, 1, 2 * D_FF)
    wdn_all = w_down.reshape(depth * N_EXPERTS, D_FF, D_MODEL)
    bdn_all = b_down.reshape(depth * N_EXPERTS, 1, D_MODEL)

    for l in range(depth):
        win2, wq2, wkv2, wr2, br, tab = _layer_weights(
            w_in[l], w_uq[l], w_ukv[l], w_router[l], b_router[l], rel_bias[l])
        qt, k, vt, sbq, sbk, sbv, caq, cak, cav = _inproj(
            xf, row2(attn_norm[l]), win2, cos_t, sin_t, row2(q_norm[l]), wq2, row2(kv_norm[l]), wkv2)

        g = mix_norm[l]
        ma = _mla_attention(qt, k, vt, row2(g[:A_W]), batch, seq)
        mb = _sb_attention(sbq, sbk, sbv, row2(g[A_W:A_W + SB_W]), batch, seq)
        pad = lambda a: jnp.pad(a.reshape(batch, seq, CA_W), ((0, 0), (CA_PAD, 0), (0, 0))).reshape(-1, CA_W)
        mc = _ca_attention(caq, pad(cak), pad(cav), tab, row2(g[A_W + SB_W:]), batch, seq)

        xn, h, lpos, meta, runs, totals = _outproj_router(
            ma, mb, mc, w_o[l].astype(jnp.bfloat16), xf, row2(ffn_norm[l]), wr2, br)

        sizes = totals[0, :N_EXPERTS]
        padded = (sizes + TM_EXP - 1) // TM_EXP * TM_EXP
        p_ends = jnp.cumsum(padded)
        p_starts = (p_ends - padded).astype(jnp.int32)
        blk_start = jnp.arange(n_blk, dtype=jnp.int32) * TM_EXP
        blk_valid = (blk_start < p_ends[-1]).astype(jnp.int32)
        last_row = jnp.minimum(blk_start, p_ends[-1] - 1)
        blk_e = jnp.minimum(jnp.sum((last_row[:, None] >= p_ends[None, :]).astype(jnp.int32), axis=1),
                            N_EXPERTS - 1)
        blk_first = jnp.concatenate([jnp.ones((1,), jnp.int32), (blk_e[1:] != blk_e[:-1]).astype(jnp.int32)])

        xs = _dispatch(p_starts, sizes, padded.astype(jnp.int32), runs, lpos, h, n_rows)
        ys = _expert_ffn(blk_e + l * N_EXPERTS, blk_first, blk_valid, xs, wgu_all, bgu_all, wdn_all, bdn_all)
        xf = _combine(p_starts, runs, meta, xn, ys, row2(final_norm), final=(l == depth - 1))

    return xf.reshape(batch, seq, D_MODEL)
```

```python
import functools
import math

import jax
import jax.numpy as jnp
from jax import lax
from jax.experimental import pallas as pl
from jax.experimental.pallas import tpu as pltpu

D_MODEL = 1024
RMS_EPS = 1e-6
MLA_NOPE, MLA_ROPE, MLA_V, MLA_HEADS = 64, 32, 64, 8
MLA_Q_RANK, MLA_KV_RANK = 256, 128
ROPE_THETA = 10000.0
SB_DIM, SB_HEADS = 64, 4
CA_DIM, CA_HEADS = 64, 4
CHUNK = 64
CA_LEFT_CHUNKS = 8
REL_CLIP = 256
N_EXPERTS, TOP_K = 32, 4
D_FF = 1024
SWIGLU_LIMIT, SWIGLU_ALPHA = 7.0, 1.702

LANES = 128
HEAD_PAD = 128
A_W = MLA_HEADS * MLA_V
SB_W = SB_HEADS * SB_DIM
CA_W = CA_HEADS * CA_DIM
CA_PAD = CA_LEFT_CHUNKS * CHUNK
CA_WIN = CA_PAD + 2 * CHUNK
REL_TAB = 1024
HALF = D_MODEL // 2

TM_TOK = 256
TQ_MLA = 256
TQ_SB = 256
TK_SB = 128
TQ_CA = 2 * CHUNK
CA_BLOCKS = 4
ROUTER_TILES = 2
INPROJ_TILES = 4
TM_EXP = 512
RUN_ALIGN = 8
LOCAL_ROWS = 1280
assert LOCAL_ROWS >= TM_TOK * TOP_K + N_EXPERTS * (RUN_ALIGN - 1) and LOCAL_ROWS % LANES == 0
assert TM_EXP % TM_TOK == 0 and TQ_MLA == TM_TOK

_C_CQ, _C_CKV, _C_KPE, _C_KROT = 0, 256, 384, 512
_C_SB, _C_CA, D_IN2 = 640, 1408, 2176

NEG = -1e30
LOG2E = math.log2(math.e)
VMEM_LIMIT = 56 * 1024 * 1024


def _rms(v, g):
    return v * lax.rsqrt(jnp.mean(v * v, axis=-1, keepdims=True) + RMS_EPS) * g


def _dot(a, b):
    return jnp.dot(a, b, preferred_element_type=jnp.float32)


def _dot_nt(a, b):
    return lax.dot_general(a, b, (((1,), (1,)), ((), ())), preferred_element_type=jnp.float32)


def _bf(v):
    return v.astype(jnp.bfloat16)


def _pack_rows(v):
    bits = lax.bitcast_convert_type(v, jnp.uint32)
    return (bits[:, :HALF] >> 16) | (bits[:, HALF:] & jnp.uint32(0xFFFF0000))


def _unpack_rows(u):
    lo = lax.bitcast_convert_type(u << 16, jnp.float32)
    hi = lax.bitcast_convert_type(u & jnp.uint32(0xFFFF0000), jnp.float32)
    return _bf(lo), _bf(hi)


def _cparams(sem):
    return pltpu.CompilerParams(dimension_semantics=sem, vmem_limit_bytes=VMEM_LIMIT)


ROPE_PACK = LANES // MLA_ROPE


def _rope_kernel(pos_ref, inv_ref, cos_ref, sin_ref):
    ang = pos_ref[...].astype(jnp.float32) * inv_ref[...]
    cos_ref[...] = jnp.cos(ang)
    sin_ref[...] = jnp.sin(ang)


def _rope_tables(positions, n):
    inv = ROPE_THETA ** (-jnp.arange(0, MLA_ROPE, 2, dtype=jnp.float32) / MLA_ROPE)
    inv_row = jnp.tile(jnp.concatenate([inv, inv]), ROPE_PACK).reshape(1, LANES)
    rows = n // ROPE_PACK
    pos_packed = jnp.repeat(positions.reshape(rows, ROPE_PACK), MLA_ROPE, axis=1)
    tm = min(TM_TOK, rows)
    cos_p, sin_p = pl.pallas_call(
        _rope_kernel,
        grid=(rows // tm,),
        in_specs=[pl.BlockSpec((tm, LANES), lambda i: (i, 0)),
                  pl.BlockSpec((1, LANES), lambda i: (0, 0))],
        out_specs=[pl.BlockSpec((tm, LANES), lambda i: (i, 0))] * 2,
        out_shape=[jax.ShapeDtypeStruct((rows, LANES), jnp.float32)] * 2,
        compiler_params=_cparams(("parallel",)),
        name="rope_tables",
    )(pos_packed, inv_row)
    pad = HEAD_PAD - MLA_NOPE - MLA_ROPE
    ones = lambda width: jnp.ones((n, width), jnp.float32)
    zeros = lambda width: jnp.zeros((n, width), jnp.float32)
    cos_t = jnp.concatenate([ones(MLA_NOPE), cos_p.reshape(n, MLA_ROPE), ones(pad)], axis=1)
    sin_t = jnp.concatenate([zeros(MLA_NOPE), sin_p.reshape(n, MLA_ROPE), zeros(pad)], axis=1)
    return cos_t, sin_t


def _inproj_kernel(x_ref, g_ref, win_ref, cos_ref, sin_ref, qn_ref, wq_ref, kvn_ref, wkv_ref,
                   q_ref, k_ref, v_ref, sbq_ref, sbk_ref, sbv_ref, caq_ref, cak_ref, cav_ref):
    for j in range(INPROJ_TILES):
        _inproj_tile(j, x_ref, g_ref, win_ref, cos_ref, sin_ref, qn_ref, wq_ref, kvn_ref, wkv_ref,
                     q_ref, k_ref, v_ref, sbq_ref, sbk_ref, sbv_ref, caq_ref, cak_ref, cav_ref)


def _inproj_tile(j, x_ref, g_ref, win_ref, cos_ref, sin_ref, qn_ref, wq_ref, kvn_ref, wkv_ref,
                 q_ref, k_ref, v_ref, sbq_ref, sbk_ref, sbv_ref, caq_ref, cak_ref, cav_ref):
    rows = slice(j * TM_TOK, (j + 1) * TM_TOK)
    h = _rms(x_ref[rows, :], g_ref[...])
    proj = _dot(_bf(h), win_ref[...])
    cos128, sin128 = cos_ref[rows, :], sin_ref[rows, :]
    cos_h = jnp.concatenate([cos128] * MLA_HEADS, axis=1)
    sin_h = jnp.concatenate([sin128] * MLA_HEADS, axis=1)
    w = MLA_HEADS * HEAD_PAD

    cqn = _rms(proj[:, _C_CQ:_C_CKV], qn_ref[...])
    q2 = _dot(_bf(cqn), wq_ref[...])
    q = (q2[:, :w] * cos_h + q2[:, w:] * sin_h) * (LOG2E / math.sqrt(MLA_NOPE + MLA_ROPE))
    q_ref[j] = _bf(jnp.transpose(q))

    ckvn = _rms(proj[:, _C_CKV:_C_KPE], kvn_ref[...])
    kv2 = _dot(_bf(ckvn), wkv_ref[...])
    kpe = proj[:, _C_KPE:_C_KROT] * cos128 + proj[:, _C_KROT:_C_SB] * sin128
    k_ref[rows, :] = _bf(kv2[:, :w] + jnp.concatenate([kpe] * MLA_HEADS, axis=1))
    v_ref[j] = _bf(jnp.transpose(kv2[:, w:]))

    sb_scale = 1.0 / math.sqrt(SB_DIM)
    sbq_ref[rows, :] = _bf(proj[:, _C_SB:_C_SB + SB_W] * sb_scale)
    sbk_ref[rows, :] = _bf(proj[:, _C_SB + SB_W:_C_SB + 2 * SB_W])
    sbv_ref[rows, :] = _bf(proj[:, _C_SB + 2 * SB_W:_C_CA])
    ca_scale = LOG2E / math.sqrt(CA_DIM)
    caq_ref[rows, :] = _bf(proj[:, _C_CA:_C_CA + CA_W] * ca_scale)
    cak_ref[rows, :] = _bf(proj[:, _C_CA + CA_W:_C_CA + 2 * CA_W])
    cav_ref[rows, :] = _bf(proj[:, _C_CA + 2 * CA_W:D_IN2])


def _inproj(x, g, win2, cos_t, sin_t, qn, wq2, kvn, wkv2):
    n = x.shape[0]
    tm = TM_TOK * INPROJ_TILES
    row = lambda width: pl.BlockSpec((tm, width), lambda i: (i, 0))
    full = lambda a: pl.BlockSpec(a.shape, lambda i: (0,) * a.ndim)
    widths = [MLA_HEADS * HEAD_PAD] + [SB_W] * 3 + [CA_W] * 3
    tile_t = lambda rows: pl.BlockSpec((INPROJ_TILES, rows, TM_TOK), lambda i: (i, 0, 0))
    shape_t = lambda rows: jax.ShapeDtypeStruct((n // TM_TOK, rows, TM_TOK), jnp.bfloat16)
    return pl.pallas_call(
        _inproj_kernel,
        grid=(n // tm,),
        in_specs=[row(D_MODEL), full(g), full(win2), row(LANES), row(LANES),
                  full(qn), full(wq2), full(kvn), full(wkv2)],
        out_specs=[tile_t(MLA_HEADS * HEAD_PAD), row(widths[0]), tile_t(A_W)] + [row(wd) for wd in widths[1:]],
        out_shape=[shape_t(MLA_HEADS * HEAD_PAD), jax.ShapeDtypeStruct((n, widths[0]), jnp.bfloat16), shape_t(A_W)]
                  + [jax.ShapeDtypeStruct((n, wd), jnp.bfloat16) for wd in widths[1:]],
        compiler_params=_cparams(("parallel",)),
        name="inproj",
    )(x, g, win2, cos_t, sin_t, qn, wq2, kvn, wkv2)


def _mla_kernel(qt_ref, k_ref, vt_ref, g_ref, o_ref, m_ref, l_ref, acc_ref, s_ref):
    i = pl.program_id(1)
    tq = TQ_MLA
    n_pairs = MLA_HEADS // 2
    key_chunk = lax.broadcasted_iota(jnp.int32, (tq, tq), 0) // CHUNK
    qry_chunk = lax.broadcasted_iota(jnp.int32, (tq, tq), 1) // CHUNK
    diag_ok = key_chunk <= qry_chunk
    top = lax.broadcasted_iota(jnp.int32, (LANES, tq), 0) < MLA_V

    m_ref[...] = jnp.full(m_ref.shape, -jnp.inf, jnp.float32)
    l_ref[...] = jnp.zeros(l_ref.shape, jnp.float32)
    acc_ref[...] = jnp.zeros(acc_ref.shape, jnp.float32)

    def scores(kb, slot):
        start = pl.multiple_of(kb * tq, tq)
        for h in range(MLA_HEADS):
            qt = qt_ref[0, h * HEAD_PAD:(h + 1) * HEAD_PAD, :]
            kblk = k_ref[pl.ds(start, tq), h * HEAD_PAD:(h + 1) * HEAD_PAD]
            s_ref[slot, h] = _dot(kblk, qt)

    def absorb(kb, slot, mask):
        for pr in range(n_pairs):
            vt = vt_ref[kb, pr * LANES:(pr + 1) * LANES, :]
            zero = jnp.zeros_like(vt)
            vt_bd = jnp.concatenate([jnp.where(top, vt, zero), jnp.where(top, zero, vt)], axis=1)
            pts, alphas = [], []
            for h in (2 * pr, 2 * pr + 1):
                s = s_ref[slot, h]
                if mask is not None:
                    s = jnp.where(mask, s, -jnp.inf)
                m_old = m_ref[h]
                m_new = jnp.maximum(m_old, jnp.max(s, axis=0, keepdims=True))
                p = jnp.exp2(s - m_new)
                alpha = jnp.exp2(m_old - m_new)
                l_ref[h] = alpha * l_ref[h] + jnp.sum(p, axis=0, keepdims=True)
                m_ref[h] = m_new
                pts.append(_bf(p))
                alphas.append(alpha)
            alpha_pair = jnp.where(top, alphas[0], alphas[1])
            acc_ref[pr] = alpha_pair * acc_ref[pr] + _dot(vt_bd, jnp.concatenate(pts, axis=0))

    def body(j, carry):
        scores(2 * j + 1, 1)
        absorb(2 * j, 0, None)
        scores(2 * j + 2, 0)
        absorb(2 * j + 1, 1, None)
        return carry

    scores(0, 0)
    lax.fori_loop(0, i // 2, body, 0)

    @pl.when(i % 2 == 0)
    def _():
        absorb(i, 0, diag_ok)

    @pl.when(i % 2 == 1)
    def _():
        scores(i, 1)
        absorb(i - 1, 0, None)
        absorb(i, 1, diag_ok)

    outs = [jnp.transpose(acc_ref[pr] / jnp.where(top, l_ref[2 * pr], l_ref[2 * pr + 1])) for pr in range(n_pairs)]
    o = jnp.concatenate(outs, axis=1)
    o_ref[...] = _bf(_rms(o, g_ref[...]))


def _mla_attention(qt, k, vt, g, batch, seq):
    nq = seq // TQ_MLA
    w = MLA_HEADS * HEAD_PAD
    return pl.pallas_call(
        _mla_kernel,
        grid=(batch, nq),
        in_specs=[pl.BlockSpec((1, w, TQ_MLA), lambda b, i: (b * nq + i, 0, 0)),
                  pl.BlockSpec((seq, w), lambda b, i: (b, 0)),
                  pl.BlockSpec((nq, A_W, TQ_MLA), lambda b, i: (b, 0, 0)),
                  pl.BlockSpec((1, A_W), lambda b, i: (0, 0))],
        out_specs=pl.BlockSpec((TQ_MLA, A_W), lambda b, i: (b * nq + i, 0)),
        out_shape=jax.ShapeDtypeStruct((batch * seq, A_W), jnp.bfloat16),
        scratch_shapes=[pltpu.VMEM((MLA_HEADS, 1, TQ_MLA), jnp.float32),
                        pltpu.VMEM((MLA_HEADS, 1, TQ_MLA), jnp.float32),
                        pltpu.VMEM((MLA_HEADS // 2, LANES, TQ_MLA), jnp.float32),
                        pltpu.VMEM((2, MLA_HEADS, TQ_MLA, TQ_MLA), jnp.float32)],
        compiler_params=_cparams(("parallel", "parallel")),
        name="mla_attention",
    )(qt, k, vt, g)


def _sb_kernel(q_ref, k_ref, v_ref, g_ref, o_ref, run_ref, acc_ref, zl_ref, sums_ref):
    i = pl.program_id(1)
    tq, tk = TQ_SB, TK_SB
    n_pairs = SB_HEADS // 2
    r = lax.broadcasted_iota(jnp.int32, (2 * tq, tk), 0)
    c = lax.broadcasted_iota(jnp.int32, (2 * tq, tk), 1)
    q_off = jnp.where(r >= tq, r - tq, r)
    top_low = (r < tq) == (c < SB_DIM)
    r2 = lax.broadcasted_iota(jnp.int32, (tk, 2 * tk), 0)
    c2 = lax.broadcasted_iota(jnp.int32, (tk, 2 * tk), 1)
    sum_mat = jnp.where((c2 >= tk) | (r2 > c2), 1.0, 0.0).astype(jnp.bfloat16)

    run_ref[...] = jnp.zeros(run_ref.shape, jnp.float32)
    acc_ref[...] = jnp.zeros(acc_ref.shape, jnp.float32)

    def step(kb_hi, masked):
        masks = []
        for d in range(2):
            kb = kb_hi - d
            start = pl.multiple_of(kb * tk, tk)
            mask = (c + (kb * tk - i * tq)) < q_off if masked else None
            masks.append(mask)
            for pr in range(n_pairs):
                qp = q_ref[:, pr * LANES:(pr + 1) * LANES]
                q2 = jnp.concatenate([qp, qp], axis=0)
                qm = jnp.where(top_low, q2, jnp.zeros_like(q2))
                kblk = k_ref[pl.ds(start, tk), pr * LANES:(pr + 1) * LANES]
                z = _dot_nt(qm, kblk)
                log_keep = -(jnp.maximum(z, 0.0) + jnp.log(1.0 + jnp.exp(-jnp.abs(z))))
                if masked:
                    log_keep = jnp.where(mask, log_keep, 0.0)
                hi = _bf(log_keep)
                lo = _bf(log_keep - hi.astype(jnp.float32))
                sums_ref[2 * d + pr] = _dot(hi, sum_mat) + _dot(lo, sum_mat)
                zl_ref[2 * d + pr] = z + log_keep
        slowest = None
        for d in range(2):
            start = pl.multiple_of((kb_hi - d) * tk, tk)
            for pr in range(n_pairs):
                vblk = v_ref[pl.ds(start, tk), pr * LANES:(pr + 1) * LANES]
                run = run_ref[pr]
                a = jnp.exp(zl_ref[2 * d + pr] + run + sums_ref[2 * d + pr, :, 0:tk])
                if masked:
                    a = jnp.where(masks[d], a, 0.0)
                acc_ref[pr] = acc_ref[pr] + _dot(_bf(a), vblk)
                run = run + sums_ref[2 * d + pr, :, tk:2 * tk]
                run_ref[pr] = run
                if d == 1:
                    top = jnp.max(run)
                    slowest = top if slowest is None else jnp.maximum(slowest, top)
        return slowest

    underflow = -104.0
    assert tq == 2 * tk
    first = step(2 * i + 1, True)
    lax.while_loop(lambda cr: (cr[0] >= 0) & (cr[1] > underflow),
                   lambda cr: (cr[0] - 1, step(2 * cr[0] + 1, False)),
                   (i - 1, first))

    lane = lax.broadcasted_iota(jnp.int32, (tq, LANES), 1)
    outs = [jnp.where(lane < SB_DIM, acc_ref[pr, 0:tq, :], acc_ref[pr, tq:2 * tq, :]) for pr in range(n_pairs)]
    o = jnp.concatenate(outs, axis=1)
    o_ref[...] = _bf(_rms(o, g_ref[...]))


def _sb_attention(q, k, v, g, batch, seq):
    nq = seq // TQ_SB
    return pl.pallas_call(
        _sb_kernel,
        grid=(batch, nq),
        in_specs=[pl.BlockSpec((TQ_SB, SB_W), lambda b, i: (b * nq + i, 0)),
                  pl.BlockSpec((seq, SB_W), lambda b, i: (b, 0)),
                  pl.BlockSpec((seq, SB_W), lambda b, i: (b, 0)),
                  pl.BlockSpec((1, SB_W), lambda b, i: (0, 0))],
        out_specs=pl.BlockSpec((TQ_SB, SB_W), lambda b, i: (b * nq + i, 0)),
        out_shape=jax.ShapeDtypeStruct((batch * seq, SB_W), jnp.bfloat16),
        scratch_shapes=[pltpu.VMEM((SB_HEADS // 2, 2 * TQ_SB, TK_SB), jnp.float32),
                        pltpu.VMEM((SB_HEADS // 2, 2 * TQ_SB, LANES), jnp.float32),
                        pltpu.VMEM((SB_HEADS, 2 * TQ_SB, TK_SB), jnp.float32),
                        pltpu.VMEM((SB_HEADS, 2 * TQ_SB, 2 * TK_SB), jnp.float32)],
        compiler_params=_cparams(("parallel", "parallel")),
        name="sb_attention",
    )(q, k, v, g)


def _ca_kernel(q_ref, k_ref, v_ref, tab_ref, g_ref, o_ref, bias_ref):
    t = TQ_CA

    @pl.when((pl.program_id(0) == 0) & (pl.program_id(1) == 0))
    def _():
        r = lax.broadcasted_iota(jnp.int32, (t, CA_WIN), 0)
        c = lax.broadcasted_iota(jnp.int32, (t, CA_WIN), 1)
        lo = (r // CHUNK) * CHUNK
        band = (c >= lo) & (c < lo + CA_PAD + CHUNK)
        for h in range(CA_HEADS):
            tab = jnp.broadcast_to(tab_ref[h:h + 1, :], (t, REL_TAB))
            bias = pltpu.roll(tab, REL_TAB - (REL_CLIP - 1), 1, stride=1, stride_axis=0)[:, :CA_WIN]
            bias_ref[h] = jnp.where(band, bias * LOG2E, -jnp.inf)

    for j in range(CA_BLOCKS):
        _ca_block(pl.program_id(1) * CA_BLOCKS + j, slice(j * t, (j + 1) * t), q_ref, k_ref, v_ref, g_ref, o_ref, bias_ref)


def _ca_block(i, rows, q_ref, k_ref, v_ref, g_ref, o_ref, bias_ref):
    t = TQ_CA
    start = pl.multiple_of(i * t, t)
    r2 = lax.broadcasted_iota(jnp.int32, (2 * t, LANES), 0)
    c2 = lax.broadcasted_iota(jnp.int32, (2 * t, LANES), 1)
    top_low = (r2 < t) == (c2 < CA_DIM)
    in_seq = lax.broadcasted_iota(jnp.int32, (2 * t, CA_WIN), 1) + i * t >= CA_PAD
    lane = lax.broadcasted_iota(jnp.int32, (t, LANES), 1)

    outs = []
    for pair in range(CA_HEADS // 2):
        qp = q_ref[rows, pair * LANES:(pair + 1) * LANES]
        q2 = jnp.concatenate([qp, qp], axis=0)
        qm = jnp.where(top_low, q2, jnp.zeros_like(q2))
        kwin = k_ref[pl.ds(start, CA_WIN), pair * LANES:(pair + 1) * LANES]
        vwin = v_ref[pl.ds(start, CA_WIN), pair * LANES:(pair + 1) * LANES]
        bias = jnp.concatenate([bias_ref[2 * pair], bias_ref[2 * pair + 1]], axis=0)
        s = jnp.where(in_seq, _dot_nt(qm, kwin) + bias, -jnp.inf)
        m = jnp.max(s, axis=-1, keepdims=True)
        p = jnp.exp2(s - m)
        l = jnp.sum(p, axis=-1, keepdims=True)
        o2 = _dot(_bf(p), vwin) / l
        outs.append(jnp.where(lane < CA_DIM, o2[0:t], o2[t:2 * t]))
    o = jnp.concatenate(outs, axis=1)
    o_ref[rows, :] = _bf(_rms(o, g_ref[...]))


def _ca_attention(q, kpad, vpad, tab, g, batch, seq):
    tq = TQ_CA * CA_BLOCKS
    nq = seq // tq
    return pl.pallas_call(
        _ca_kernel,
        grid=(batch, nq),
        in_specs=[pl.BlockSpec((tq, CA_W), lambda b, i: (b * nq + i, 0)),
                  pl.BlockSpec((seq + CA_PAD, CA_W), lambda b, i: (b, 0)),
                  pl.BlockSpec((seq + CA_PAD, CA_W), lambda b, i: (b, 0)),
                  pl.BlockSpec((CA_HEADS, REL_TAB), lambda b, i: (0, 0)),
                  pl.BlockSpec((1, CA_W), lambda b, i: (0, 0))],
        out_specs=pl.BlockSpec((tq, CA_W), lambda b, i: (b * nq + i, 0)),
        out_shape=jax.ShapeDtypeStruct((batch * seq, CA_W), jnp.bfloat16),
        scratch_shapes=[pltpu.VMEM((CA_HEADS, TQ_CA, CA_WIN), jnp.float32)],
        compiler_params=_cparams(("arbitrary", "arbitrary")),
        name="ca_attention",
    )(q, kpad, vpad, tab, g)


def _outproj_router_kernel(ma_ref, mb_ref, mc_ref, wo_ref, x_ref, g_ref, wr2_ref, br_ref,
                           xn_ref, h_ref, lpos_ref, meta_ref, runs_ref, tot_ref, base_ref):
    @pl.when(pl.program_id(0) == 0)
    def _():
        base_ref[...] = jnp.zeros_like(base_ref)

    base = base_ref[...]
    for j in range(ROUTER_TILES):
        base = _route_tile(j, base, ma_ref, mb_ref, mc_ref, wo_ref, x_ref, g_ref, wr2_ref, br_ref,
                           xn_ref, h_ref, lpos_ref, meta_ref, runs_ref)
    base_ref[...] = base
    tot_ref[...] = base.astype(jnp.int32)


def _route_tile(j, base, ma_ref, mb_ref, mc_ref, wo_ref, x_ref, g_ref, wr2_ref, br_ref,
                xn_ref, h_ref, lpos_ref, meta_ref, runs_ref):
    tm = TM_TOK
    rows = slice(j * tm, (j + 1) * tm)
    attn = (_dot(ma_ref[rows, :], wo_ref[0:A_W, :]) + _dot(mb_ref[rows, :], wo_ref[A_W:A_W + SB_W, :])
            + _dot(mc_ref[rows, :], wo_ref[A_W + SB_W:, :]))
    xn = x_ref[rows, :] + attn
    xn_ref[rows, :] = xn
    h = _rms(xn, g_ref[...])

    h_hi = _bf(h)
    h_ref[rows, :] = h_hi
    h_lo = _bf(h - h_hi.astype(jnp.float32))
    both = _dot(h_hi, wr2_ref[...])
    logits = (both[:, :LANES] + both[:, LANES:] + _dot(h_lo, wr2_ref[:, 0:LANES])
              + br_ref[...])
    lane = lax.broadcasted_iota(jnp.int32, (tm, LANES), 1)
    lane_f = lane.astype(jnp.float32)

    work = logits
    vals, idxs, hots = [], [], []
    for _ in range(TOP_K):
        mx = jnp.max(work, axis=-1, keepdims=True)
        ix = jnp.min(jnp.where(work == mx, lane_f, float(LANES)), axis=-1, keepdims=True)
        hot = lane_f == ix
        work = jnp.where(hot, -jnp.inf, work)
        vals.append(mx)
        idxs.append(ix)
        hots.append(hot)
    exps = [jnp.exp(v - vals[0]) for v in vals]
    denom = exps[0] + exps[1] + exps[2] + exps[3]
    gates = [e / denom for e in exps]

    sel = jnp.zeros((tm, LANES), jnp.float32)
    for hot in hots:
        sel = sel + jnp.where(hot, 1.0, 0.0)
    r = lax.broadcasted_iota(jnp.int32, (tm, tm), 0)
    c = lax.broadcasted_iota(jnp.int32, (tm, tm), 1)
    before = jnp.where(c < r, 1.0, 0.0).astype(jnp.bfloat16)
    rank_in_tile = _dot(before, _bf(sel))

    cnt = jnp.sum(sel, axis=0, keepdims=True)
    cnt_al = jnp.ceil(cnt * (1.0 / RUN_ALIGN)) * RUN_ALIGN
    rl = lax.broadcasted_iota(jnp.int32, (LANES, LANES), 0)
    cl = lax.broadcasted_iota(jnp.int32, (LANES, LANES), 1)
    earlier = jnp.where(rl < cl, 1.0, 0.0).astype(jnp.bfloat16)
    loff = _dot(_bf(jnp.broadcast_to(cnt_al, (8, LANES))), earlier)[0:1, :]
    sub = lax.broadcasted_iota(jnp.int32, (8, LANES), 0)
    has_long = jnp.where(jnp.max(cnt_al, axis=-1, keepdims=True) >= LONG_RUN, 1.0, 0.0)
    runs = jnp.where(sub == 0, loff, jnp.where(sub == 1, base, jnp.where(sub == 2, cnt_al,
                     jnp.where(sub == 3, has_long, 0.0))))
    runs_ref[8 * j:8 * j + 8, :] = runs.astype(jnp.int32)

    lpos_dense = loff + rank_in_tile
    meta = jnp.zeros((tm, LANES), jnp.float32)
    for kk in range(TOP_K):
        lpos = jnp.sum(jnp.where(hots[kk], lpos_dense, 0.0), axis=-1, keepdims=True)
        meta = meta + jnp.where(lane == kk, gates[kk], 0.0) + jnp.where(lane == TOP_K + kk, lpos, 0.0)
    meta_ref[rows, :] = meta
    lpos_ref[:, rows] = jnp.transpose(meta)[TOP_K:TOP_K + 8, :].astype(jnp.int32)
    return base + cnt_al


def _outproj_router(ma, mb, mc, wo, x, g, wr2, br):
    n = x.shape[0]
    tm = TM_TOK * ROUTER_TILES
    row = lambda width: pl.BlockSpec((tm, width), lambda i: (i, 0))
    full = lambda a: pl.BlockSpec(a.shape, lambda i: (0,) * a.ndim)
    return pl.pallas_call(
        _outproj_router_kernel,
        grid=(n // tm,),
        in_specs=[row(A_W), row(SB_W), row(CA_W), full(wo), row(D_MODEL), full(g), full(wr2), full(br)],
        out_specs=[row(D_MODEL), row(D_MODEL), pl.BlockSpec((8, tm), lambda i: (0, i)),
                   row(LANES), pl.BlockSpec((8 * ROUTER_TILES, LANES), lambda i: (i, 0)),
                   pl.BlockSpec((1, LANES), lambda i: (0, 0))],
        out_shape=[jax.ShapeDtypeStruct((n, D_MODEL), jnp.float32),
                   jax.ShapeDtypeStruct((n, D_MODEL), jnp.bfloat16),
                   jax.ShapeDtypeStruct((8, n), jnp.int32),
                   jax.ShapeDtypeStruct((n, LANES), jnp.float32),
                   jax.ShapeDtypeStruct((8 * (n // TM_TOK), LANES), jnp.int32),
                   jax.ShapeDtypeStruct((1, LANES), jnp.int32)],
        scratch_shapes=[pltpu.VMEM((1, LANES), jnp.float32)],
        compiler_params=_cparams(("arbitrary",)),
        name="outproj_router",
    )(ma, mb, mc, wo, x, g, wr2, br)


_RUN_CHUNKS = tuple(TM_TOK >> s for s in range(TM_TOK.bit_length()) if (TM_TOK >> s) >= RUN_ALIGN)


LONG_RUN = 128


def _for_each_chunk(length, fn, sizes=_RUN_CHUNKS):
    for size in sizes:
        off = length & (~(2 * size - 1))

        @pl.when((length & size) != 0)
        def _(off=off, size=size):
            fn(off, size)


_WAIT_CHUNKS = tuple(1 << b for b in range((LOCAL_ROWS).bit_length() - 1, RUN_ALIGN.bit_length() - 2, -1))


def _wait_tile_runs(runs_ref, make_copy):
    total = runs_ref[0, N_EXPERTS - 1] + runs_ref[2, N_EXPERTS - 1]
    for size in _WAIT_CHUNKS:
        @pl.when((total & size) != 0)
        def _(size=size):
            make_copy(size).wait()


def _tile_runs(runs_ref, pstart_ref, fn):
    def experts(sizes, unroll):
        def body(e, _):
            local, glob, length = runs_ref[0, e], pstart_ref[e] + runs_ref[1, e], runs_ref[2, e]
            _for_each_chunk(length, lambda off, size: fn(pl.multiple_of(local + off, RUN_ALIGN),
                                                         pl.multiple_of(glob + off, RUN_ALIGN), size), sizes)
            return 0

        lax.fori_loop(0, N_EXPERTS, body, 0, unroll=unroll)

    experts(tuple(sz for sz in _RUN_CHUNKS if sz < LONG_RUN), 2)

    @pl.when(runs_ref[3, 0] != 0)
    def _():
        experts(tuple(sz for sz in _RUN_CHUNKS if sz >= LONG_RUN), 1)


def _dispatch_kernel(pstart_ref, tot_ref, pad_ref, runs_ref, prev_runs_ref, lpos_ref, h_ref, xs_ref,
                     loc, zbuf, sem, fill_sem):
    i = pl.program_id(0)
    slot = i % 2

    def fill(start_or_wait):
        def body(e, _):
            first = pstart_ref[e] + tot_ref[e]
            _for_each_chunk(pad_ref[e] - tot_ref[e], lambda off, size: start_or_wait(pltpu.make_async_copy(
                zbuf.at[pl.ds(0, size)], xs_ref.at[pl.ds(pl.multiple_of(first + off, RUN_ALIGN), size)], fill_sem)))
            return 0

        lax.fori_loop(0, N_EXPERTS, body, 0)

        used = pstart_ref[N_EXPERTS - 1] + pad_ref[N_EXPERTS - 1]

        def tail(b, _):
            start_or_wait(pltpu.make_async_copy(
                zbuf, xs_ref.at[pl.ds(pl.multiple_of(b * TM_TOK, TM_TOK), TM_TOK)], fill_sem))
            return 0

        lax.fori_loop(used // TM_TOK, xs_ref.shape[0] // TM_TOK, tail, 0)

    @pl.when(i == 0)
    def _():
        zbuf[...] = jnp.zeros_like(zbuf)
        fill(lambda c: c.start())
        fill(lambda c: c.wait())

    r = lax.broadcasted_iota(jnp.int32, (LOCAL_ROWS, TM_TOK), 0)
    hit = r == lpos_ref[0:1, :]
    for kk in range(1, TOP_K):
        hit = hit | (r == lpos_ref[kk:kk + 1, :])
    loc[slot] = _pack_rows(_dot(jnp.where(hit, 1.0, 0.0).astype(jnp.bfloat16), h_ref[...]))

    def copy(buf, local, glob, size):
        return pltpu.make_async_copy(loc.at[buf, pl.ds(local, size)], xs_ref.at[pl.ds(glob, size)], sem.at[buf])

    _tile_runs(runs_ref, pstart_ref, lambda l, g, s: copy(slot, l, g, s).start())

    @pl.when(i > 0)
    def _():
        _wait_tile_runs(prev_runs_ref, lambda size: copy(1 - slot, 0, 0, size))

    @pl.when(i == pl.num_programs(0) - 1)
    def _():
        _wait_tile_runs(runs_ref, lambda size: copy(slot, 0, 0, size))


def _dispatch(pstart, totals, padded, runs, lpos, h, n_rows):
    n = h.shape[0]
    tm = TM_TOK
    grid_spec = pltpu.PrefetchScalarGridSpec(
        num_scalar_prefetch=3,
        grid=(n // tm,),
        in_specs=[pl.BlockSpec((8, LANES), lambda i, *_: (i, 0), memory_space=pltpu.SMEM),
                  pl.BlockSpec((8, LANES), lambda i, *_: (jnp.maximum(i - 1, 0), 0), memory_space=pltpu.SMEM),
                  pl.BlockSpec((8, tm), lambda i, *_: (0, i)),
                  pl.BlockSpec((tm, D_MODEL), lambda i, *_: (i, 0))],
        out_specs=pl.BlockSpec(memory_space=pl.ANY),
        scratch_shapes=[pltpu.VMEM((2, LOCAL_ROWS, HALF), jnp.uint32),
                        pltpu.VMEM((TM_TOK, HALF), jnp.uint32),
                        pltpu.SemaphoreType.DMA((2,)),
                        pltpu.SemaphoreType.DMA],
    )
    return pl.pallas_call(
        _dispatch_kernel,
        grid_spec=grid_spec,
        out_shape=jax.ShapeDtypeStruct((n_rows, HALF), jnp.uint32),
        compiler_params=_cparams(("arbitrary",)),
        name="dispatch",
    )(pstart, totals, padded, runs, runs, lpos, h)


def _expert_kernel(be_ref, bfirst_ref, bvalid_ref, xs_ref, wgu_ref, bgu_ref, wdn_ref, bdn_ref,
                   ys_ref, wgu_bf, wdn_bf):
    b = pl.program_id(0)

    @pl.when(bfirst_ref[b] == 1)
    def _():
        wgu_bf[...] = _bf(wgu_ref[0])
        wdn_bf[...] = _bf(wdn_ref[0])

    @pl.when(bvalid_ref[b] == 1)
    def _():
        xb = jnp.concatenate(_unpack_rows(xs_ref[...]), axis=1)
        gu = _dot(xb, wgu_bf[...]) + bgu_ref[0]
        gte = jnp.minimum(gu[:, :D_FF], SWIGLU_LIMIT)
        up = jnp.clip(gu[:, D_FF:], -SWIGLU_LIMIT, SWIGLU_LIMIT)
        act = (up + 1.0) * (gte * (1.0 / (1.0 + jnp.exp(-SWIGLU_ALPHA * gte))))
        y = _dot(_bf(act), wdn_bf[...]) + bdn_ref[0]
        ys_ref[...] = _pack_rows(_bf(y).astype(jnp.float32))

    @pl.when(bvalid_ref[b] == 0)
    def _():
        ys_ref[...] = jnp.zeros_like(ys_ref)


def _expert_ffn(blk_e, blk_first, blk_valid, xs, wgu, bgu, wdn, bdn):
    n_rows = xs.shape[0]
    tm = TM_EXP
    grid_spec = pltpu.PrefetchScalarGridSpec(
        num_scalar_prefetch=3,
        grid=(n_rows // tm,),
        in_specs=[pl.BlockSpec((tm, HALF), lambda b, e, f, v: (b, 0)),
                  pl.BlockSpec((1, D_MODEL, 2 * D_FF), lambda b, e, f, v: (e[b], 0, 0)),
                  pl.BlockSpec((1, 1, 2 * D_FF), lambda b, e, f, v: (e[b], 0, 0)),
                  pl.BlockSpec((1, D_FF, D_MODEL), lambda b, e, f, v: (e[b], 0, 0)),
                  pl.BlockSpec((1, 1, D_MODEL), lambda b, e, f, v: (e[b], 0, 0))],
        out_specs=pl.BlockSpec((tm, HALF), lambda b, e, f, v: (b, 0)),
        scratch_shapes=[pltpu.VMEM((D_MODEL, 2 * D_FF), jnp.bfloat16),
                        pltpu.VMEM((D_FF, D_MODEL), jnp.bfloat16)],
    )
    return pl.pallas_call(
        _expert_kernel,
        grid_spec=grid_spec,
        out_shape=jax.ShapeDtypeStruct((n_rows, HALF), jnp.uint32),
        compiler_params=_cparams(("arbitrary",)),
        name="expert_ffn",
    )(blk_e, blk_first, blk_valid, xs, wgu, bgu, wdn, bdn)


def _combine_kernel(final, pstart_ref, runs_ref, next_runs_ref, meta_ref, x_ref, ys_ref, gfin_ref, o_ref, loc, sem):
    i = pl.program_id(0)
    slot = i % 2

    def copy(buf, local, glob, size):
        return pltpu.make_async_copy(ys_ref.at[pl.ds(glob, size)], loc.at[buf, pl.ds(local, size)], sem.at[buf])

    @pl.when(i == 0)
    def _():
        loc[...] = jnp.zeros_like(loc)
        _tile_runs(runs_ref, pstart_ref, lambda l, g, s: copy(slot, l, g, s).start())

    @pl.when(i + 1 < pl.num_programs(0))
    def _():
        _tile_runs(next_runs_ref, pstart_ref, lambda l, g, s: copy(1 - slot, l, g, s).start())

    _wait_tile_runs(runs_ref, lambda size: copy(slot, 0, 0, size))

    meta = meta_ref[...]
    col = lax.broadcasted_iota(jnp.int32, (TM_TOK, LOCAL_ROWS), 1).astype(jnp.float32)
    wts = jnp.zeros((TM_TOK, LOCAL_ROWS), jnp.float32)
    for kk in range(TOP_K):
        wts = wts + jnp.where(col == meta[:, TOP_K + kk:TOP_K + kk + 1], meta[:, kk:kk + 1], 0.0)
    w_hi = _bf(wts)
    w_lo = _bf(wts - w_hi.astype(jnp.float32))
    w2 = jnp.concatenate([w_hi, w_lo], axis=0)
    y_lo, y_hi = _unpack_rows(loc[slot])
    r_lo, r_hi = _dot(w2, y_lo), _dot(w2, y_hi)
    out = x_ref[...] + jnp.concatenate([r_lo[:TM_TOK] + r_lo[TM_TOK:], r_hi[:TM_TOK] + r_hi[TM_TOK:]], axis=1)
    if final:
        out = _rms(out, gfin_ref[...])
    o_ref[...] = out


def _combine(pstart, runs, meta, x, ys, gfin, final):
    n = x.shape[0]
    tm = TM_TOK
    n_tiles = n // tm
    grid_spec = pltpu.PrefetchScalarGridSpec(
        num_scalar_prefetch=1,
        grid=(n_tiles,),
        in_specs=[pl.BlockSpec((8, LANES), lambda i, ps: (i, 0), memory_space=pltpu.SMEM),
                  pl.BlockSpec((8, LANES), lambda i, ps: (jnp.minimum(i + 1, n_tiles - 1), 0), memory_space=pltpu.SMEM),
                  pl.BlockSpec((tm, LANES), lambda i, ps: (i, 0)),
                  pl.BlockSpec((tm, D_MODEL), lambda i, ps: (i, 0)),
                  pl.BlockSpec(memory_space=pl.ANY),
                  pl.BlockSpec((1, D_MODEL), lambda i, ps: (0, 0))],
        out_specs=pl.BlockSpec((tm, D_MODEL), lambda i, ps: (i, 0)),
        scratch_shapes=[pltpu.VMEM((2, LOCAL_ROWS, HALF), jnp.uint32), pltpu.SemaphoreType.DMA((2,))],
    )
    return pl.pallas_call(
        functools.partial(_combine_kernel, final),
        grid_spec=grid_spec,
        out_shape=jax.ShapeDtypeStruct((n, D_MODEL), jnp.float32),
        compiler_params=_cparams(("arbitrary",)),
        name="combine_final" if final else "combine",
    )(pstart, runs, runs, meta, x, ys, gfin)


def _pad_heads(w, parts):
    rows = w.shape[0]
    per = w.shape[1] // MLA_HEADS
    w3 = w.reshape(rows, MLA_HEADS, per)
    cols = [jnp.zeros((rows, MLA_HEADS, b - a), w.dtype) if sign == 0 else sign * w3[:, :, a:b]
            for a, b, sign in parts]
    used = sum(b - a for a, b, _ in parts)
    cols.append(jnp.zeros((rows, MLA_HEADS, HEAD_PAD - used), w.dtype))
    return jnp.concatenate(cols, axis=2).reshape(rows, MLA_HEADS * HEAD_PAD)


def _layer_weights(w_in, w_uq, w_ukv, w_router, b_router, rel_bias):
    half = MLA_ROPE // 2
    cq, ckv, kpe, sb, ca = (w_in[:, 0:256], w_in[:, 256:384], w_in[:, 384:416],
                            w_in[:, 416:1184], w_in[:, 1184:1952])
    z = lambda width: jnp.zeros((D_MODEL, width), w_in.dtype)
    kpe_pad = jnp.concatenate([z(MLA_NOPE), kpe, z(HEAD_PAD - MLA_NOPE - MLA_ROPE)], axis=1)
    kpe_rot = jnp.concatenate([z(MLA_NOPE), -kpe[:, half:], kpe[:, :half],
                               z(HEAD_PAD - MLA_NOPE - MLA_ROPE)], axis=1)
    win2 = jnp.concatenate([cq, ckv, kpe_pad, kpe_rot, sb, ca], axis=1).astype(jnp.bfloat16)

    d = MLA_NOPE + MLA_ROPE
    wq_full = _pad_heads(w_uq, [(0, d, 1)])
    wq_rot = _pad_heads(w_uq, [(0, MLA_NOPE, 0), (MLA_NOPE + half, d, -1), (MLA_NOPE, MLA_NOPE + half, 1)])
    wq2 = jnp.concatenate([wq_full, wq_rot], axis=1).astype(jnp.bfloat16)

    wk = _pad_heads(w_ukv, [(0, MLA_NOPE, 1)])
    wv = w_ukv.reshape(MLA_KV_RANK, MLA_HEADS, MLA_NOPE + MLA_V)[:, :, MLA_NOPE:].reshape(MLA_KV_RANK, A_W)
    wkv2 = jnp.concatenate([wk, wv], axis=1).astype(jnp.bfloat16)

    wr = jnp.pad(w_router, ((0, 0), (0, LANES - N_EXPERTS)))
    wr_hi = wr.astype(jnp.bfloat16)
    wr_lo = (wr - wr_hi.astype(jnp.float32)).astype(jnp.bfloat16)
    wr2 = jnp.concatenate([wr_hi, wr_lo], axis=1)
    br = jnp.pad(b_router, (0, LANES - N_EXPERTS), constant_values=NEG).reshape(1, LANES)

    ext = jnp.concatenate([rel_bias, jnp.broadcast_to(rel_bias[:, -1:], (CA_HEADS, REL_TAB - 2 * REL_CLIP - 1))], axis=1)
    tab = ext[:, ::-1]
    return win2, wq2, wkv2, wr2, br, tab


def kernel(x, positions, attn_norm, w_in, q_norm, w_uq, kv_norm, w_ukv, rel_bias, mix_norm,
           w_o, ffn_norm, w_router, b_router, w_gate_up, b_gate_up, w_down, b_down, final_norm):
    batch, seq, _ = x.shape
    n = batch * seq
    depth = w_in.shape[0]
    xf = x.reshape(n, D_MODEL)

    cos_t, sin_t = _rope_tables(positions, n)

    n_rows = n * TOP_K + (n // TM_TOK) * N_EXPERTS * (RUN_ALIGN - 1) + N_EXPERTS * TM_EXP
    n_rows = -(-n_rows // TM_EXP) * TM_EXP
    n_blk = n_rows // TM_EXP
    row2 = lambda v: v.reshape(1, -1)
    wgu_all = w_gate_up.reshape(depth * N_EXPERTS, D_MODEL, 2 * D_FF)
    bgu_all = b_gate_up.reshape(depth * N_EXPERTS, 1, 2 * D_FF)
    wdn_all = w_down.reshape(depth * N_EXPERTS, D_FF, D_MODEL)
    bdn_all = b_down.reshape(depth * N_EXPERTS, 1, D_MODEL)

    for l in range(depth):
        win2, wq2, wkv2, wr2, br, tab = _layer_weights(
            w_in[l], w_uq[l], w_ukv[l], w_router[l], b_router[l], rel_bias[l])
        qt, k, vt, sbq, sbk, sbv, caq, cak, cav = _inproj(
            xf, row2(attn_norm[l]), win2, cos_t, sin_t, row2(q_norm[l]), wq2, row2(kv_norm[l]), wkv2)

        g = mix_norm[l]
        ma = _mla_attention(qt, k, vt, row2(g[:A_W]), batch, seq)
        mb = _sb_attention(sbq, sbk, sbv, row2(g[A_W:A_W + SB_W]), batch, seq)
        pad = lambda a: jnp.pad(a.reshape(batch, seq, CA_W), ((0, 0), (CA_PAD, 0), (0, 0))).reshape(-1, CA_W)
        mc = _ca_attention(caq, pad(cak), pad(cav), tab, row2(g[A_W + SB_W:]), batch, seq)

        xn, h, lpos, meta, runs, totals = _outproj_router(
            ma, mb, mc, w_o[l].astype(jnp.bfloat16), xf, row2(ffn_norm[l]), wr2, br)

        sizes = totals[0, :N_EXPERTS]
        padded = (sizes + TM_EXP - 1) // TM_EXP * TM_EXP
        p_ends = jnp.cumsum(padded)
        p_starts = (p_ends - padded).astype(jnp.int32)
        blk_start = jnp.arange(n_blk, dtype=jnp.int32) * TM_EXP
        blk_valid = (blk_start < p_ends[-1]).astype(jnp.int32)
        last_row = jnp.minimum(blk_start, p_ends[-1] - 1)
        blk_e = jnp.minimum(jnp.sum((last_row[:, None] >= p_ends[None, :]).astype(jnp.int32), axis=1),
                            N_EXPERTS - 1)
        blk_first = jnp.concatenate([jnp.ones((1,), jnp.int32), (blk_e[1:] != blk_e[:-1]).astype(jnp.int32)])

        xs = _dispatch(p_starts, sizes, padded.astype(jnp.int32), runs, lpos, h, n_rows)
        ys = _expert_ffn(blk_e + l * N_EXPERTS, blk_first, blk_valid, xs, wgu_all, bgu_all, wdn_all, bdn_all)
        xf = _combine(p_starts, runs, meta, xn, ys, row2(final_norm), final=(l == depth - 1))

    return xf.reshape(batch, seq, D_MODEL)
```

```python
import functools
import math

import jax
import jax.numpy as jnp
from jax import lax
from jax.experimental import pallas as pl
from jax.experimental.pallas import tpu as pltpu

D_MODEL = 1024
RMS_EPS = 1e-6
MLA_NOPE, MLA_ROPE, MLA_V, MLA_HEADS = 64, 32, 64, 8
MLA_Q_RANK, MLA_KV_RANK = 256, 128
ROPE_THETA = 10000.0
SB_DIM, SB_HEADS = 64, 4
CA_DIM, CA_HEADS = 64, 4
CHUNK = 64
CA_LEFT_CHUNKS = 8
REL_CLIP = 256
N_EXPERTS, TOP_K = 32, 4
D_FF = 1024
SWIGLU_LIMIT, SWIGLU_ALPHA = 7.0, 1.702

LANES = 128
HEAD_PAD = 128
A_W = MLA_HEADS * MLA_V
SB_W = SB_HEADS * SB_DIM
CA_W = CA_HEADS * CA_DIM
CA_PAD = CA_LEFT_CHUNKS * CHUNK
CA_WIN = CA_PAD + 2 * CHUNK
REL_TAB = 1024
HALF = D_MODEL // 2

TM_TOK = 256
TQ_MLA = 256
TQ_SB = 256
TK_SB = 128
TQ_CA = 2 * CHUNK
CA_BLOCKS = 8
ROUTER_TILES = 4
INPROJ_TILES = 4
TM_EXP = 512
RUN_ALIGN = 8
LOCAL_ROWS = 1280
assert LOCAL_ROWS >= TM_TOK * TOP_K + N_EXPERTS * (RUN_ALIGN - 1) and LOCAL_ROWS % LANES == 0
assert TM_EXP % TM_TOK == 0 and TQ_MLA == TM_TOK

_C_CQ, _C_CKV, _C_KPE, _C_KROT = 0, 256, 384, 512
_C_SB, _C_CA, D_IN2 = 640, 1408, 2176

NEG = -1e30
LOG2E = math.log2(math.e)
VMEM_LIMIT = 56 * 1024 * 1024


def _rms(v, g):
    return v * lax.rsqrt(jnp.mean(v * v, axis=-1, keepdims=True) + RMS_EPS) * g


def _dot(a, b):
    return jnp.dot(a, b, preferred_element_type=jnp.float32)


def _dot_nt(a, b):
    return lax.dot_general(a, b, (((1,), (1,)), ((), ())), preferred_element_type=jnp.float32)


def _bf(v):
    return v.astype(jnp.bfloat16)


def _pack_rows(v):
    bits = lax.bitcast_convert_type(v, jnp.uint32)
    return (bits[:, :HALF] >> 16) | (bits[:, HALF:] & jnp.uint32(0xFFFF0000))


def _unpack_rows(u):
    lo = lax.bitcast_convert_type(u << 16, jnp.float32)
    hi = lax.bitcast_convert_type(u & jnp.uint32(0xFFFF0000), jnp.float32)
    return _bf(lo), _bf(hi)


def _cparams(sem):
    return pltpu.CompilerParams(dimension_semantics=sem, vmem_limit_bytes=VMEM_LIMIT)


ROPE_PACK = LANES // MLA_ROPE


def _rope_kernel(pos_ref, inv_ref, cos_ref, sin_ref):
    ang = pos_ref[...].astype(jnp.float32) * inv_ref[...]
    cos_ref[...] = jnp.cos(ang)
    sin_ref[...] = jnp.sin(ang)


def _rope_tables(positions, n):
    inv = ROPE_THETA ** (-jnp.arange(0, MLA_ROPE, 2, dtype=jnp.float32) / MLA_ROPE)
    inv_row = jnp.tile(jnp.concatenate([inv, inv]), ROPE_PACK).reshape(1, LANES)
    rows = n // ROPE_PACK
    pos_packed = jnp.repeat(positions.reshape(rows, ROPE_PACK), MLA_ROPE, axis=1)
    tm = min(TM_TOK, rows)
    cos_p, sin_p = pl.pallas_call(
        _rope_kernel,
        grid=(rows // tm,),
        in_specs=[pl.BlockSpec((tm, LANES), lambda i: (i, 0)),
                  pl.BlockSpec((1, LANES), lambda i: (0, 0))],
        out_specs=[pl.BlockSpec((tm, LANES), lambda i: (i, 0))] * 2,
        out_shape=[jax.ShapeDtypeStruct((rows, LANES), jnp.float32)] * 2,
        compiler_params=_cparams(("parallel",)),
        name="rope_tables",
    )(pos_packed, inv_row)
    pad = HEAD_PAD - MLA_NOPE - MLA_ROPE
    ones = lambda width: jnp.ones((n, width), jnp.float32)
    zeros = lambda width: jnp.zeros((n, width), jnp.float32)
    cos_t = jnp.concatenate([ones(MLA_NOPE), cos_p.reshape(n, MLA_ROPE), ones(pad)], axis=1)
    sin_t = jnp.concatenate([zeros(MLA_NOPE), sin_p.reshape(n, MLA_ROPE), zeros(pad)], axis=1)
    return cos_t, sin_t


def _inproj_kernel(x_ref, g_ref, win_ref, cos_ref, sin_ref, qn_ref, wq_ref, kvn_ref, wkv_ref,
                   q_ref, k_ref, v_ref, sbq_ref, sbk_ref, sbv_ref, caq_ref, cak_ref, cav_ref):
    for j in range(INPROJ_TILES):
        _inproj_tile(j, x_ref, g_ref, win_ref, cos_ref, sin_ref, qn_ref, wq_ref, kvn_ref, wkv_ref,
                     q_ref, k_ref, v_ref, sbq_ref, sbk_ref, sbv_ref, caq_ref, cak_ref, cav_ref)


def _inproj_tile(j, x_ref, g_ref, win_ref, cos_ref, sin_ref, qn_ref, wq_ref, kvn_ref, wkv_ref,
                 q_ref, k_ref, v_ref, sbq_ref, sbk_ref, sbv_ref, caq_ref, cak_ref, cav_ref):
    rows = slice(j * TM_TOK, (j + 1) * TM_TOK)
    h = _rms(x_ref[rows, :], g_ref[...])
    proj = _dot(_bf(h), win_ref[...])
    cos128, sin128 = cos_ref[rows, :], sin_ref[rows, :]
    cos_h = jnp.concatenate([cos128] * MLA_HEADS, axis=1)
    sin_h = jnp.concatenate([sin128] * MLA_HEADS, axis=1)
    w = MLA_HEADS * HEAD_PAD

    cqn = _rms(proj[:, _C_CQ:_C_CKV], qn_ref[...])
    q2 = _dot(_bf(cqn), wq_ref[...])
    q = (q2[:, :w] * cos_h + q2[:, w:] * sin_h) * (LOG2E / math.sqrt(MLA_NOPE + MLA_ROPE))
    q_ref[j] = _bf(jnp.transpose(q))

    ckvn = _rms(proj[:, _C_CKV:_C_KPE], kvn_ref[...])
    kv2 = _dot(_bf(ckvn), wkv_ref[...])
    kpe = proj[:, _C_KPE:_C_KROT] * cos128 + proj[:, _C_KROT:_C_SB] * sin128
    k_ref[rows, :] = _bf(kv2[:, :w] + jnp.concatenate([kpe] * MLA_HEADS, axis=1))
    v_ref[j] = _bf(jnp.transpose(kv2[:, w:]))

    sb_scale = 1.0 / math.sqrt(SB_DIM)
    sbq_ref[rows, :] = _bf(proj[:, _C_SB:_C_SB + SB_W] * sb_scale)
    sbk_ref[rows, :] = _bf(proj[:, _C_SB + SB_W:_C_SB + 2 * SB_W])
    sbv_ref[rows, :] = _bf(proj[:, _C_SB + 2 * SB_W:_C_CA])
    ca_scale = LOG2E / math.sqrt(CA_DIM)
    caq_ref[rows, :] = _bf(proj[:, _C_CA:_C_CA + CA_W] * ca_scale)
    cak_ref[rows, :] = _bf(proj[:, _C_CA + CA_W:_C_CA + 2 * CA_W])
    cav_ref[rows, :] = _bf(proj[:, _C_CA + 2 * CA_W:D_IN2])


def _inproj(x, g, win2, cos_t, sin_t, qn, wq2, kvn, wkv2):
    n = x.shape[0]
    tm = TM_TOK * INPROJ_TILES
    row = lambda width: pl.BlockSpec((tm, width), lambda i: (i, 0))
    full = lambda a: pl.BlockSpec(a.shape, lambda i: (0,) * a.ndim)
    widths = [MLA_HEADS * HEAD_PAD] + [SB_W] * 3 + [CA_W] * 3
    tile_t = lambda rows: pl.BlockSpec((INPROJ_TILES, rows, TM_TOK), lambda i: (i, 0, 0))
    shape_t = lambda rows: jax.ShapeDtypeStruct((n // TM_TOK, rows, TM_TOK), jnp.bfloat16)
    return pl.pallas_call(
        _inproj_kernel,
        grid=(n // tm,),
        in_specs=[row(D_MODEL), full(g), full(win2), row(LANES), row(LANES),
                  full(qn), full(wq2), full(kvn), full(wkv2)],
        out_specs=[tile_t(MLA_HEADS * HEAD_PAD), row(widths[0]), tile_t(A_W)] + [row(wd) for wd in widths[1:]],
        out_shape=[shape_t(MLA_HEADS * HEAD_PAD), jax.ShapeDtypeStruct((n, widths[0]), jnp.bfloat16), shape_t(A_W)]
                  + [jax.ShapeDtypeStruct((n, wd), jnp.bfloat16) for wd in widths[1:]],
        compiler_params=_cparams(("parallel",)),
        name="inproj",
    )(x, g, win2, cos_t, sin_t, qn, wq2, kvn, wkv2)


def _mla_kernel(qt_ref, k_ref, vt_ref, g_ref, o_ref, m_ref, l_ref, acc_ref, s_ref):
    i = pl.program_id(1)
    tq = TQ_MLA
    n_pairs = MLA_HEADS // 2
    key_chunk = lax.broadcasted_iota(jnp.int32, (tq, tq), 0) // CHUNK
    qry_chunk = lax.broadcasted_iota(jnp.int32, (tq, tq), 1) // CHUNK
    diag_ok = key_chunk <= qry_chunk
    top = lax.broadcasted_iota(jnp.int32, (LANES, tq), 0) < MLA_V

    m_ref[...] = jnp.full(m_ref.shape, -jnp.inf, jnp.float32)
    l_ref[...] = jnp.zeros(l_ref.shape, jnp.float32)
    acc_ref[...] = jnp.zeros(acc_ref.shape, jnp.float32)

    def scores(kb, slot):
        start = pl.multiple_of(kb * tq, tq)
        for h in range(MLA_HEADS):
            qt = qt_ref[0, h * HEAD_PAD:(h + 1) * HEAD_PAD, :]
            kblk = k_ref[pl.ds(start, tq), h * HEAD_PAD:(h + 1) * HEAD_PAD]
            s_ref[slot, h] = _dot(kblk, qt)

    def absorb(kb, slot, mask):
        for pr in range(n_pairs):
            vt = vt_ref[kb, pr * LANES:(pr + 1) * LANES, :]
            zero = jnp.zeros_like(vt)
            vt_bd = jnp.concatenate([jnp.where(top, vt, zero), jnp.where(top, zero, vt)], axis=1)
            pts, alphas = [], []
            for h in (2 * pr, 2 * pr + 1):
                s = s_ref[slot, h]
                if mask is not None:
                    s = jnp.where(mask, s, -jnp.inf)
                m_old = m_ref[h]
                m_new = jnp.maximum(m_old, jnp.max(s, axis=0, keepdims=True))
                p = jnp.exp2(s - m_new)
                alpha = jnp.exp2(m_old - m_new)
                l_ref[h] = alpha * l_ref[h] + jnp.sum(p, axis=0, keepdims=True)
                m_ref[h] = m_new
                pts.append(_bf(p))
                alphas.append(alpha)
            alpha_pair = jnp.where(top, alphas[0], alphas[1])
            acc_ref[pr] = alpha_pair * acc_ref[pr] + _dot(vt_bd, jnp.concatenate(pts, axis=0))

    def body(j, carry):
        scores(2 * j + 1, 1)
        absorb(2 * j, 0, None)
        scores(2 * j + 2, 0)
        absorb(2 * j + 1, 1, None)
        return carry

    scores(0, 0)
    lax.fori_loop(0, i // 2, body, 0)

    @pl.when(i % 2 == 0)
    def _():
        absorb(i, 0, diag_ok)

    @pl.when(i % 2 == 1)
    def _():
        scores(i, 1)
        absorb(i - 1, 0, None)
        absorb(i, 1, diag_ok)

    outs = [jnp.transpose(acc_ref[pr] / jnp.where(top, l_ref[2 * pr], l_ref[2 * pr + 1])) for pr in range(n_pairs)]
    o = jnp.concatenate(outs, axis=1)
    o_ref[...] = _bf(_rms(o, g_ref[...]))


def _mla_attention(qt, k, vt, g, batch, seq):
    nq = seq // TQ_MLA
    w = MLA_HEADS * HEAD_PAD
    return pl.pallas_call(
        _mla_kernel,
        grid=(batch, nq),
        in_specs=[pl.BlockSpec((1, w, TQ_MLA), lambda b, i: (b * nq + i, 0, 0)),
                  pl.BlockSpec((seq, w), lambda b, i: (b, 0)),
                  pl.BlockSpec((nq, A_W, TQ_MLA), lambda b, i: (b, 0, 0)),
                  pl.BlockSpec((1, A_W), lambda b, i: (0, 0))],
        out_specs=pl.BlockSpec((TQ_MLA, A_W), lambda b, i: (b * nq + i, 0)),
        out_shape=jax.ShapeDtypeStruct((batch * seq, A_W), jnp.bfloat16),
        scratch_shapes=[pltpu.VMEM((MLA_HEADS, 1, TQ_MLA), jnp.float32),
                        pltpu.VMEM((MLA_HEADS, 1, TQ_MLA), jnp.float32),
                        pltpu.VMEM((MLA_HEADS // 2, LANES, TQ_MLA), jnp.float32),
                        pltpu.VMEM((2, MLA_HEADS, TQ_MLA, TQ_MLA), jnp.float32)],
        compiler_params=_cparams(("parallel", "parallel")),
        name="mla_attention",
    )(qt, k, vt, g)


def _sb_kernel(q_ref, k_ref, v_ref, g_ref, o_ref, run_ref, acc_ref, zl_ref, sums_ref):
    i = pl.program_id(1)
    tq, tk = TQ_SB, TK_SB
    n_pairs = SB_HEADS // 2
    r = lax.broadcasted_iota(jnp.int32, (2 * tq, tk), 0)
    c = lax.broadcasted_iota(jnp.int32, (2 * tq, tk), 1)
    q_off = jnp.where(r >= tq, r - tq, r)
    top_low = (r < tq) == (c < SB_DIM)
    r2 = lax.broadcasted_iota(jnp.int32, (tk, 2 * tk), 0)
    c2 = lax.broadcasted_iota(jnp.int32, (tk, 2 * tk), 1)
    sum_mat = jnp.where((c2 >= tk) | (r2 > c2), 1.0, 0.0).astype(jnp.bfloat16)

    run_ref[...] = jnp.zeros(run_ref.shape, jnp.float32)
    acc_ref[...] = jnp.zeros(acc_ref.shape, jnp.float32)

    def step(kb_hi, masked):
        masks = []
        for d in range(2):
            kb = kb_hi - d
            start = pl.multiple_of(kb * tk, tk)
            mask = (c + (kb * tk - i * tq)) < q_off if masked else None
            masks.append(mask)
            for pr in range(n_pairs):
                qp = q_ref[:, pr * LANES:(pr + 1) * LANES]
                q2 = jnp.concatenate([qp, qp], axis=0)
                qm = jnp.where(top_low, q2, jnp.zeros_like(q2))
                kblk = k_ref[pl.ds(start, tk), pr * LANES:(pr + 1) * LANES]
                z = _dot_nt(qm, kblk)
                log_keep = -(jnp.maximum(z, 0.0) + jnp.log(1.0 + jnp.exp(-jnp.abs(z))))
                if masked:
                    log_keep = jnp.where(mask, log_keep, 0.0)
                hi = _bf(log_keep)
                lo = _bf(log_keep - hi.astype(jnp.float32))
                sums_ref[2 * d + pr] = _dot(hi, sum_mat) + _dot(lo, sum_mat)
                zl_ref[2 * d + pr] = z + log_keep
        slowest = None
        for d in range(2):
            start = pl.multiple_of((kb_hi - d) * tk, tk)
            for pr in range(n_pairs):
                vblk = v_ref[pl.ds(start, tk), pr * LANES:(pr + 1) * LANES]
                run = run_ref[pr]
                a = jnp.exp(zl_ref[2 * d + pr] + run + sums_ref[2 * d + pr, :, 0:tk])
                if masked:
                    a = jnp.where(masks[d], a, 0.0)
                acc_ref[pr] = acc_ref[pr] + _dot(_bf(a), vblk)
                run = run + sums_ref[2 * d + pr, :, tk:2 * tk]
                run_ref[pr] = run
                if d == 1:
                    top = jnp.max(run)
                    slowest = top if slowest is None else jnp.maximum(slowest, top)
        return slowest

    underflow = -104.0
    assert tq == 2 * tk
    first = step(2 * i + 1, True)
    lax.while_loop(lambda cr: (cr[0] >= 0) & (cr[1] > underflow),
                   lambda cr: (cr[0] - 1, step(2 * cr[0] + 1, False)),
                   (i - 1, first))

    lane = lax.broadcasted_iota(jnp.int32, (tq, LANES), 1)
    outs = [jnp.where(lane < SB_DIM, acc_ref[pr, 0:tq, :], acc_ref[pr, tq:2 * tq, :]) for pr in range(n_pairs)]
    o = jnp.concatenate(outs, axis=1)
    o_ref[...] = _bf(_rms(o, g_ref[...]))


def _sb_attention(q, k, v, g, batch, seq):
    nq = seq // TQ_SB
    return pl.pallas_call(
        _sb_kernel,
        grid=(batch, nq),
        in_specs=[pl.BlockSpec((TQ_SB, SB_W), lambda b, i: (b * nq + i, 0)),
                  pl.BlockSpec((seq, SB_W), lambda b, i: (b, 0)),
                  pl.BlockSpec((seq, SB_W), lambda b, i: (b, 0)),
                  pl.BlockSpec((1, SB_W), lambda b, i: (0, 0))],
        out_specs=pl.BlockSpec((TQ_SB, SB_W), lambda b, i: (b * nq + i, 0)),
        out_shape=jax.ShapeDtypeStruct((batch * seq, SB_W), jnp.bfloat16),
        scratch_shapes=[pltpu.VMEM((SB_HEADS // 2, 2 * TQ_SB, TK_SB), jnp.float32),
                        pltpu.VMEM((SB_HEADS // 2, 2 * TQ_SB, LANES), jnp.float32),
                        pltpu.VMEM((SB_HEADS, 2 * TQ_SB, TK_SB), jnp.float32),
                        pltpu.VMEM((SB_HEADS, 2 * TQ_SB, 2 * TK_SB), jnp.float32)],
        compiler_params=_cparams(("parallel", "parallel")),
        name="sb_attention",
    )(q, k, v, g)


def _ca_kernel(q_ref, k_ref, v_ref, tab_ref, g_ref, o_ref, bias_ref):
    t = TQ_CA

    @pl.when((pl.program_id(0) == 0) & (pl.program_id(1) == 0))
    def _():
        r = lax.broadcasted_iota(jnp.int32, (t, CA_WIN), 0)
        c = lax.broadcasted_iota(jnp.int32, (t, CA_WIN), 1)
        lo = (r // CHUNK) * CHUNK
        band = (c >= lo) & (c < lo + CA_PAD + CHUNK)
        for h in range(CA_HEADS):
            tab = jnp.broadcast_to(tab_ref[h:h + 1, :], (t, REL_TAB))
            bias = pltpu.roll(tab, REL_TAB - (REL_CLIP - 1), 1, stride=1, stride_axis=0)[:, :CA_WIN]
            bias_ref[h] = jnp.where(band, bias * LOG2E, -jnp.inf)

    for j in range(CA_BLOCKS):
        _ca_block(pl.program_id(1) * CA_BLOCKS + j, slice(j * t, (j + 1) * t), q_ref, k_ref, v_ref, g_ref, o_ref, bias_ref)


def _ca_block(i, rows, q_ref, k_ref, v_ref, g_ref, o_ref, bias_ref):
    t = TQ_CA
    start = pl.multiple_of(i * t, t)
    r2 = lax.broadcasted_iota(jnp.int32, (2 * t, LANES), 0)
    c2 = lax.broadcasted_iota(jnp.int32, (2 * t, LANES), 1)
    top_low = (r2 < t) == (c2 < CA_DIM)
    in_seq = lax.broadcasted_iota(jnp.int32, (2 * t, CA_WIN), 1) + i * t >= CA_PAD
    lane = lax.broadcasted_iota(jnp.int32, (t, LANES), 1)

    outs = []
    for pair in range(CA_HEADS // 2):
        qp = q_ref[rows, pair * LANES:(pair + 1) * LANES]
        q2 = jnp.concatenate([qp, qp], axis=0)
        qm = jnp.where(top_low, q2, jnp.zeros_like(q2))
        kwin = k_ref[pl.ds(start, CA_WIN), pair * LANES:(pair + 1) * LANES]
        vwin = v_ref[pl.ds(start, CA_WIN), pair * LANES:(pair + 1) * LANES]
        bias = jnp.concatenate([bias_ref[2 * pair], bias_ref[2 * pair + 1]], axis=0)
        s = jnp.where(in_seq, _dot_nt(qm, kwin) + bias, -jnp.inf)
        m = jnp.max(s, axis=-1, keepdims=True)
        p = jnp.exp2(s - m)
        l = jnp.sum(p, axis=-1, keepdims=True)
        o2 = _dot(_bf(p), vwin) / l
        outs.append(jnp.where(lane < CA_DIM, o2[0:t], o2[t:2 * t]))
    o = jnp.concatenate(outs, axis=1)
    o_ref[rows, :] = _bf(_rms(o, g_ref[...]))


def _ca_attention(q, kpad, vpad, tab, g, batch, seq):
    tq = TQ_CA * CA_BLOCKS
    nq = seq // tq
    return pl.pallas_call(
        _ca_kernel,
        grid=(batch, nq),
        in_specs=[pl.BlockSpec((tq, CA_W), lambda b, i: (b * nq + i, 0)),
                  pl.BlockSpec((seq + CA_PAD, CA_W), lambda b, i: (b, 0)),
                  pl.BlockSpec((seq + CA_PAD, CA_W), lambda b, i: (b, 0)),
                  pl.BlockSpec((CA_HEADS, REL_TAB), lambda b, i: (0, 0)),
                  pl.BlockSpec((1, CA_W), lambda b, i: (0, 0))],
        out_specs=pl.BlockSpec((tq, CA_W), lambda b, i: (b * nq + i, 0)),
        out_shape=jax.ShapeDtypeStruct((batch * seq, CA_W), jnp.bfloat16),
        scratch_shapes=[pltpu.VMEM((CA_HEADS, TQ_CA, CA_WIN), jnp.float32)],
        compiler_params=_cparams(("arbitrary", "arbitrary")),
        name="ca_attention",
    )(q, kpad, vpad, tab, g)


def _outproj_router_kernel(ma_ref, mb_ref, mc_ref, wo_ref, x_ref, g_ref, wr2_ref, br_ref,
                           xn_ref, h_ref, lpos_ref, meta_ref, runs_ref, tot_ref, base_ref):
    @pl.when(pl.program_id(0) == 0)
    def _():
        base_ref[...] = jnp.zeros_like(base_ref)

    base = base_ref[...]
    for j in range(ROUTER_TILES):
        base = _route_tile(j, base, ma_ref, mb_ref, mc_ref, wo_ref, x_ref, g_ref, wr2_ref, br_ref,
                           xn_ref, h_ref, lpos_ref, meta_ref, runs_ref)
    base_ref[...] = base
    tot_ref[...] = base.astype(jnp.int32)


def _route_tile(j, base, ma_ref, mb_ref, mc_ref, wo_ref, x_ref, g_ref, wr2_ref, br_ref,
                xn_ref, h_ref, lpos_ref, meta_ref, runs_ref):
    tm = TM_TOK
    rows = slice(j * tm, (j + 1) * tm)
    attn = (_dot(ma_ref[rows, :], wo_ref[0:A_W, :]) + _dot(mb_ref[rows, :], wo_ref[A_W:A_W + SB_W, :])
            + _dot(mc_ref[rows, :], wo_ref[A_W + SB_W:, :]))
    xn = x_ref[rows, :] + attn
    xn_ref[rows, :] = xn
    h = _rms(xn, g_ref[...])

    h_hi = _bf(h)
    h_ref[rows, :] = h_hi
    h_lo = _bf(h - h_hi.astype(jnp.float32))
    both = _dot(h_hi, wr2_ref[...])
    logits = (both[:, :LANES] + both[:, LANES:] + _dot(h_lo, wr2_ref[:, 0:LANES])
              + br_ref[...])
    lane = lax.broadcasted_iota(jnp.int32, (tm, LANES), 1)
    lane_f = lane.astype(jnp.float32)

    work = logits
    vals, idxs, hots = [], [], []
    for _ in range(TOP_K):
        mx = jnp.max(work, axis=-1, keepdims=True)
        ix = jnp.min(jnp.where(work == mx, lane_f, float(LANES)), axis=-1, keepdims=True)
        hot = lane_f == ix
        work = jnp.where(hot, -jnp.inf, work)
        vals.append(mx)
        idxs.append(ix)
        hots.append(hot)
    exps = [jnp.exp(v - vals[0]) for v in vals]
    denom = exps[0] + exps[1] + exps[2] + exps[3]
    gates = [e / denom for e in exps]

    sel = jnp.zeros((tm, LANES), jnp.float32)
    for hot in hots:
        sel = sel + jnp.where(hot, 1.0, 0.0)
    r = lax.broadcasted_iota(jnp.int32, (tm, tm), 0)
    c = lax.broadcasted_iota(jnp.int32, (tm, tm), 1)
    before = jnp.where(c < r, 1.0, 0.0).astype(jnp.bfloat16)
    rank_in_tile = _dot(before, _bf(sel))

    cnt = jnp.sum(sel, axis=0, keepdims=True)
    cnt_al = jnp.ceil(cnt * (1.0 / RUN_ALIGN)) * RUN_ALIGN
    rl = lax.broadcasted_iota(jnp.int32, (LANES, LANES), 0)
    cl = lax.broadcasted_iota(jnp.int32, (LANES, LANES), 1)
    earlier = jnp.where(rl < cl, 1.0, 0.0).astype(jnp.bfloat16)
    loff = _dot(_bf(jnp.broadcast_to(cnt_al, (8, LANES))), earlier)[0:1, :]
    sub = lax.broadcasted_iota(jnp.int32, (8, LANES), 0)
    has_long = jnp.where(jnp.max(cnt_al, axis=-1, keepdims=True) >= LONG_RUN, 1.0, 0.0)
    runs = jnp.where(sub == 0, loff, jnp.where(sub == 1, base, jnp.where(sub == 2, cnt_al,
                     jnp.where(sub == 3, has_long, 0.0))))
    runs_ref[8 * j:8 * j + 8, :] = runs.astype(jnp.int32)

    lpos_dense = loff + rank_in_tile
    meta = jnp.zeros((tm, LANES), jnp.float32)
    for kk in range(TOP_K):
        lpos = jnp.sum(jnp.where(hots[kk], lpos_dense, 0.0), axis=-1, keepdims=True)
        meta = meta + jnp.where(lane == kk, gates[kk], 0.0) + jnp.where(lane == TOP_K + kk, lpos, 0.0)
    meta_ref[rows, :] = meta
    lpos_ref[:, rows] = jnp.transpose(meta)[TOP_K:TOP_K + 8, :].astype(jnp.int32)
    return base + cnt_al


def _outproj_router(ma, mb, mc, wo, x, g, wr2, br):
    n = x.shape[0]
    tm = TM_TOK * ROUTER_TILES
    row = lambda width: pl.BlockSpec((tm, width), lambda i: (i, 0))
    full = lambda a: pl.BlockSpec(a.shape, lambda i: (0,) * a.ndim)
    return pl.pallas_call(
        _outproj_router_kernel,
        grid=(n // tm,),
        in_specs=[row(A_W), row(SB_W), row(CA_W), full(wo), row(D_MODEL), full(g), full(wr2), full(br)],
        out_specs=[row(D_MODEL), row(D_MODEL), pl.BlockSpec((8, tm), lambda i: (0, i)),
                   row(LANES), pl.BlockSpec((8 * ROUTER_TILES, LANES), lambda i: (i, 0)),
                   pl.BlockSpec((1, LANES), lambda i: (0, 0))],
        out_shape=[jax.ShapeDtypeStruct((n, D_MODEL), jnp.float32),
                   jax.ShapeDtypeStruct((n, D_MODEL), jnp.bfloat16),
                   jax.ShapeDtypeStruct((8, n), jnp.int32),
                   jax.ShapeDtypeStruct((n, LANES), jnp.float32),
                   jax.ShapeDtypeStruct((8 * (n // TM_TOK), LANES), jnp.int32),
                   jax.ShapeDtypeStruct((1, LANES), jnp.int32)],
        scratch_shapes=[pltpu.VMEM((1, LANES), jnp.float32)],
        compiler_params=_cparams(("arbitrary",)),
        name="outproj_router",
    )(ma, mb, mc, wo, x, g, wr2, br)


_RUN_CHUNKS = tuple(TM_TOK >> s for s in range(TM_TOK.bit_length()) if (TM_TOK >> s) >= RUN_ALIGN)


LONG_RUN = 128


def _for_each_chunk(length, fn, sizes=_RUN_CHUNKS):
    for size in sizes:
        off = length & (~(2 * size - 1))

        @pl.when((length & size) != 0)
        def _(off=off, size=size):
            fn(off, size)


_WAIT_CHUNKS = tuple(1 << b for b in range((LOCAL_ROWS).bit_length() - 1, RUN_ALIGN.bit_length() - 2, -1))


def _wait_tile_runs(runs_ref, make_copy):
    total = runs_ref[0, N_EXPERTS - 1] + runs_ref[2, N_EXPERTS - 1]
    for size in _WAIT_CHUNKS:
        @pl.when((total & size) != 0)
        def _(size=size):
            make_copy(size).wait()


def _tile_runs(runs_ref, pstart_ref, fn):
    def experts(sizes, unroll):
        def body(e, _):
            local, glob, length = runs_ref[0, e], pstart_ref[e] + runs_ref[1, e], runs_ref[2, e]
            _for_each_chunk(length, lambda off, size: fn(pl.multiple_of(local + off, RUN_ALIGN),
                                                         pl.multiple_of(glob + off, RUN_ALIGN), size), sizes)
            return 0

        lax.fori_loop(0, N_EXPERTS, body, 0, unroll=unroll)

    experts(tuple(sz for sz in _RUN_CHUNKS if sz < LONG_RUN), 2)

    @pl.when(runs_ref[3, 0] != 0)
    def _():
        experts(tuple(sz for sz in _RUN_CHUNKS if sz >= LONG_RUN), 1)


def _dispatch_kernel(pstart_ref, tot_ref, pad_ref, runs_ref, prev_runs_ref, lpos_ref, h_ref, xs_ref,
                     loc, zbuf, sem, fill_sem):
    i = pl.program_id(0)
    slot = i % 2

    def fill(start_or_wait):
        def body(e, _):
            first = pstart_ref[e] + tot_ref[e]
            _for_each_chunk(pad_ref[e] - tot_ref[e], lambda off, size: start_or_wait(pltpu.make_async_copy(
                zbuf.at[pl.ds(0, size)], xs_ref.at[pl.ds(pl.multiple_of(first + off, RUN_ALIGN), size)], fill_sem)))
            return 0

        lax.fori_loop(0, N_EXPERTS, body, 0)

        used = pstart_ref[N_EXPERTS - 1] + pad_ref[N_EXPERTS - 1]

        def tail(b, _):
            start_or_wait(pltpu.make_async_copy(
                zbuf, xs_ref.at[pl.ds(pl.multiple_of(b * TM_TOK, TM_TOK), TM_TOK)], fill_sem))
            return 0

        lax.fori_loop(used // TM_TOK, xs_ref.shape[0] // TM_TOK, tail, 0)

    @pl.when(i == 0)
    def _():
        zbuf[...] = jnp.zeros_like(zbuf)
        fill(lambda c: c.start())
        fill(lambda c: c.wait())

    r = lax.broadcasted_iota(jnp.int32, (LOCAL_ROWS, TM_TOK), 0)
    hit = r == lpos_ref[0:1, :]
    for kk in range(1, TOP_K):
        hit = hit | (r == lpos_ref[kk:kk + 1, :])
    loc[slot] = _pack_rows(_dot(jnp.where(hit, 1.0, 0.0).astype(jnp.bfloat16), h_ref[...]))

    def copy(buf, local, glob, size):
        return pltpu.make_async_copy(loc.at[buf, pl.ds(local, size)], xs_ref.at[pl.ds(glob, size)], sem.at[buf])

    _tile_runs(runs_ref, pstart_ref, lambda l, g, s: copy(slot, l, g, s).start())

    @pl.when(i > 0)
    def _():
        _wait_tile_runs(prev_runs_ref, lambda size: copy(1 - slot, 0, 0, size))

    @pl.when(i == pl.num_programs(0) - 1)
    def _():
        _wait_tile_runs(runs_ref, lambda size: copy(slot, 0, 0, size))


def _dispatch(pstart, totals, padded, runs, lpos, h, n_rows):
    n = h.shape[0]
    tm = TM_TOK
    grid_spec = pltpu.PrefetchScalarGridSpec(
        num_scalar_prefetch=3,
        grid=(n // tm,),
        in_specs=[pl.BlockSpec((8, LANES), lambda i, *_: (i, 0), memory_space=pltpu.SMEM),
                  pl.BlockSpec((8, LANES), lambda i, *_: (jnp.maximum(i - 1, 0), 0), memory_space=pltpu.SMEM),
                  pl.BlockSpec((8, tm), lambda i, *_: (0, i)),
                  pl.BlockSpec((tm, D_MODEL), lambda i, *_: (i, 0))],
        out_specs=pl.BlockSpec(memory_space=pl.ANY),
        scratch_shapes=[pltpu.VMEM((2, LOCAL_ROWS, HALF), jnp.uint32),
                        pltpu.VMEM((TM_TOK, HALF), jnp.uint32),
                        pltpu.SemaphoreType.DMA((2,)),
                        pltpu.SemaphoreType.DMA],
    )
    return pl.pallas_call(
        _dispatch_kernel,
        grid_spec=grid_spec,
        out_shape=jax.ShapeDtypeStruct((n_rows, HALF), jnp.uint32),
        compiler_params=_cparams(("arbitrary",)),
        name="dispatch",
    )(pstart, totals, padded, runs, runs, lpos, h)


def _expert_kernel(be_ref, bfirst_ref, bvalid_ref, xs_ref, wgu_ref, bgu_ref, wdn_ref, bdn_ref,
                   ys_ref, wgu_bf, wdn_bf):
    b = pl.program_id(0)

    @pl.when(bfirst_ref[b] == 1)
    def _():
        wgu_bf[...] = _bf(wgu_ref[0])
        wdn_bf[...] = _bf(wdn_ref[0])

    @pl.when(bvalid_ref[b] == 1)
    def _():
        xb = jnp.concatenate(_unpack_rows(xs_ref[...]), axis=1)
        gu = _dot(xb, wgu_bf[...]) + bgu_ref[0]
        gte = jnp.minimum(gu[:, :D_FF], SWIGLU_LIMIT)
        up = jnp.clip(gu[:, D_FF:], -SWIGLU_LIMIT, SWIGLU_LIMIT)
        act = (up + 1.0) * (gte * (1.0 / (1.0 + jnp.exp(-SWIGLU_ALPHA * gte))))
        y = _dot(_bf(act), wdn_bf[...]) + bdn_ref[0]
        ys_ref[...] = _pack_rows(_bf(y).astype(jnp.float32))

    @pl.when(bvalid_ref[b] == 0)
    def _():
        ys_ref[...] = jnp.zeros_like(ys_ref)


def _expert_ffn(blk_e, blk_first, blk_valid, xs, wgu, bgu, wdn, bdn):
    n_rows = xs.shape[0]
    tm = TM_EXP
    grid_spec = pltpu.PrefetchScalarGridSpec(
        num_scalar_prefetch=3,
        grid=(n_rows // tm,),
        in_specs=[pl.BlockSpec((tm, HALF), lambda b, e, f, v: (b, 0)),
                  pl.BlockSpec((1, D_MODEL, 2 * D_FF), lambda b, e, f, v: (e[b], 0, 0)),
                  pl.BlockSpec((1, 1, 2 * D_FF), lambda b, e, f, v: (e[b], 0, 0)),
                  pl.BlockSpec((1, D_FF, D_MODEL), lambda b, e, f, v: (e[b], 0, 0)),
                  pl.BlockSpec((1, 1, D_MODEL), lambda b, e, f, v: (e[b], 0, 0))],
        out_specs=pl.BlockSpec((tm, HALF), lambda b, e, f, v: (b, 0)),
        scratch_shapes=[pltpu.VMEM((D_MODEL, 2 * D_FF), jnp.bfloat16),
                        pltpu.VMEM((D_FF, D_MODEL), jnp.bfloat16)],
    )
    return pl.pallas_call(
        _expert_kernel,
        grid_spec=grid_spec,
        out_shape=jax.ShapeDtypeStruct((n_rows, HALF), jnp.uint32),
        compiler_params=_cparams(("arbitrary",)),
        name="expert_ffn",
    )(blk_e, blk_first, blk_valid, xs, wgu, bgu, wdn, bdn)


def _combine_kernel(final, pstart_ref, runs_ref, next_runs_ref, meta_ref, x_ref, ys_ref, gfin_ref, o_ref, loc, sem):
    i = pl.program_id(0)
    slot = i % 2

    def copy(buf, local, glob, size):
        return pltpu.make_async_copy(ys_ref.at[pl.ds(glob, size)], loc.at[buf, pl.ds(local, size)], sem.at[buf])

    @pl.when(i == 0)
    def _():
        loc[...] = jnp.zeros_like(loc)
        _tile_runs(runs_ref, pstart_ref, lambda l, g, s: copy(slot, l, g, s).start())

    @pl.when(i + 1 < pl.num_programs(0))
    def _():
        _tile_runs(next_runs_ref, pstart_ref, lambda l, g, s: copy(1 - slot, l, g, s).start())

    _wait_tile_runs(runs_ref, lambda size: copy(slot, 0, 0, size))

    meta = meta_ref[...]
    col = lax.broadcasted_iota(jnp.int32, (TM_TOK, LOCAL_ROWS), 1).astype(jnp.float32)
    wts = jnp.zeros((TM_TOK, LOCAL_ROWS), jnp.float32)
    for kk in range(TOP_K):
        wts = wts + jnp.where(col == meta[:, TOP_K + kk:TOP_K + kk + 1], meta[:, kk:kk + 1], 0.0)
    w_hi = _bf(wts)
    w_lo = _bf(wts - w_hi.astype(jnp.float32))
    w2 = jnp.concatenate([w_hi, w_lo], axis=0)
    y_lo, y_hi = _unpack_rows(loc[slot])
    r_lo, r_hi = _dot(w2, y_lo), _dot(w2, y_hi)
    out = x_ref[...] + jnp.concatenate([r_lo[:TM_TOK] + r_lo[TM_TOK:], r_hi[:TM_TOK] + r_hi[TM_TOK:]], axis=1)
    if final:
        out = _rms(out, gfin_ref[...])
    o_ref[...] = out


def _combine(pstart, runs, meta, x, ys, gfin, final):
    n = x.shape[0]
    tm = TM_TOK
    n_tiles = n // tm
    grid_spec = pltpu.PrefetchScalarGridSpec(
        num_scalar_prefetch=1,
        grid=(n_tiles,),
        in_specs=[pl.BlockSpec((8, LANES), lambda i, ps: (i, 0), memory_space=pltpu.SMEM),
                  pl.BlockSpec((8, LANES), lambda i, ps: (jnp.minimum(i + 1, n_tiles - 1), 0), memory_space=pltpu.SMEM),
                  pl.BlockSpec((tm, LANES), lambda i, ps: (i, 0)),
                  pl.BlockSpec((tm, D_MODEL), lambda i, ps: (i, 0)),
                  pl.BlockSpec(memory_space=pl.ANY),
                  pl.BlockSpec((1, D_MODEL), lambda i, ps: (0, 0))],
        out_specs=pl.BlockSpec((tm, D_MODEL), lambda i, ps: (i, 0)),
        scratch_shapes=[pltpu.VMEM((2, LOCAL_ROWS, HALF), jnp.uint32), pltpu.SemaphoreType.DMA((2,))],
    )
    return pl.pallas_call(
        functools.partial(_combine_kernel, final),
        grid_spec=grid_spec,
        out_shape=jax.ShapeDtypeStruct((n, D_MODEL), jnp.float32),
        compiler_params=_cparams(("arbitrary",)),
        name="combine_final" if final else "combine",
    )(pstart, runs, runs, meta, x, ys, gfin)


def _pad_heads(w, parts):
    rows = w.shape[0]
    per = w.shape[1] // MLA_HEADS
    w3 = w.reshape(rows, MLA_HEADS, per)
    cols = [jnp.zeros((rows, MLA_HEADS, b - a), w.dtype) if sign == 0 else sign * w3[:, :, a:b]
            for a, b, sign in parts]
    used = sum(b - a for a, b, _ in parts)
    cols.append(jnp.zeros((rows, MLA_HEADS, HEAD_PAD - used), w.dtype))
    return jnp.concatenate(cols, axis=2).reshape(rows, MLA_HEADS * HEAD_PAD)


def _layer_weights(w_in, w_uq, w_ukv, w_router, b_router, rel_bias):
    half = MLA_ROPE // 2
    cq, ckv, kpe, sb, ca = (w_in[:, 0:256], w_in[:, 256:384], w_in[:, 384:416],
                            w_in[:, 416:1184], w_in[:, 1184:1952])
    z = lambda width: jnp.zeros((D_MODEL, width), w_in.dtype)
    kpe_pad = jnp.concatenate([z(MLA_NOPE), kpe, z(HEAD_PAD - MLA_NOPE - MLA_ROPE)], axis=1)
    kpe_rot = jnp.concatenate([z(MLA_NOPE), -kpe[:, half:], kpe[:, :half],
                               z(HEAD_PAD - MLA_NOPE - MLA_ROPE)], axis=1)
    win2 = jnp.concatenate([cq, ckv, kpe_pad, kpe_rot, sb, ca], axis=1).astype(jnp.bfloat16)

    d = MLA_NOPE + MLA_ROPE
    wq_full = _pad_heads(w_uq, [(0, d, 1)])
    wq_rot = _pad_heads(w_uq, [(0, MLA_NOPE, 0), (MLA_NOPE + half, d, -1), (MLA_NOPE, MLA_NOPE + half, 1)])
    wq2 = jnp.concatenate([wq_full, wq_rot], axis=1).astype(jnp.bfloat16)

    wk = _pad_heads(w_ukv, [(0, MLA_NOPE, 1)])
    wv = w_ukv.reshape(MLA_KV_RANK, MLA_HEADS, MLA_NOPE + MLA_V)[:, :, MLA_NOPE:].reshape(MLA_KV_RANK, A_W)
    wkv2 = jnp.concatenate([wk, wv], axis=1).astype(jnp.bfloat16)

    wr = jnp.pad(w_router, ((0, 0), (0, LANES - N_EXPERTS)))
    wr_hi = wr.astype(jnp.bfloat16)
    wr_lo = (wr - wr_hi.astype(jnp.float32)).astype(jnp.bfloat16)
    wr2 = jnp.concatenate([wr_hi, wr_lo], axis=1)
    br = jnp.pad(b_router, (0, LANES - N_EXPERTS), constant_values=NEG).reshape(1, LANES)

    ext = jnp.concatenate([rel_bias, jnp.broadcast_to(rel_bias[:, -1:], (CA_HEADS, REL_TAB - 2 * REL_CLIP - 1))], axis=1)
    tab = ext[:, ::-1]
    return win2, wq2, wkv2, wr2, br, tab


def kernel(x, positions, attn_norm, w_in, q_norm, w_uq, kv_norm, w_ukv, rel_bias, mix_norm,
           w_o, ffn_norm, w_router, b_router, w_gate_up, b_gate_up, w_down, b_down, final_norm):
    batch, seq, _ = x.shape
    n = batch * seq
    depth = w_in.shape[0]
    xf = x.reshape(n, D_MODEL)

    cos_t, sin_t = _rope_tables(positions, n)

    n_rows = n * TOP_K + (n // TM_TOK) * N_EXPERTS * (RUN_ALIGN - 1) + N_EXPERTS * TM_EXP
    n_rows = -(-n_rows // TM_EXP) * TM_EXP
    n_blk = n_rows // TM_EXP
    row2 = lambda v: v.reshape(1, -1)
    wgu_all = w_gate_up.reshape(depth * N_EXPERTS, D_MODEL, 2 * D_FF)
    bgu_all = b_gate_up.reshape(depth * N_EXPERTS, 1, 2 * D_FF)
    wdn_all = w_down.reshape(depth * N_EXPERTS, D_FF, D_MODEL)
    bdn_all = b_down.reshape(depth * N_EXPERTS, 1, D_MODEL)

    for l in range(depth):
        win2, wq2, wkv2, wr2, br, tab = _layer_weights(
            w_in[l], w_uq[l], w_ukv[l], w_router[l], b_router[l], rel_bias[l])
        qt, k, vt, sbq, sbk, sbv, caq, cak, cav = _inproj(
            xf, row2(attn_norm[l]), win2, cos_t, sin_t, row2(q_norm[l]), wq2, row2(kv_norm[l]), wkv2)

        g = mix_norm[l]
        ma = _mla_attention(qt, k, vt, row2(g[:A_W]), batch, seq)
        mb = _sb_attention(sbq, sbk, sbv, row2(g[A_W:A_W + SB_W]), batch, seq)
        pad = lambda a: jnp.pad(a.reshape(batch, seq, CA_W), ((0, 0), (CA_PAD, 0), (0, 0))).reshape(-1, CA_W)
        mc = _ca_attention(caq, pad(cak), pad(cav), tab, row2(g[A_W + SB_W:]), batch, seq)

        xn, h, lpos, meta, runs, totals = _outproj_router(
            ma, mb, mc, w_o[l].astype(jnp.bfloat16), xf, row2(ffn_norm[l]), wr2, br)

        sizes = totals[0, :N_EXPERTS]
        padded = (sizes + TM_EXP - 1) // TM_EXP * TM_EXP
        p_ends = jnp.cumsum(padded)
        p_starts = (p_ends - padded).astype(jnp.int32)
        blk_start = jnp.arange(n_blk, dtype=jnp.int32) * TM_EXP
        blk_valid = (blk_start < p_ends[-1]).astype(jnp.int32)
        last_row = jnp.minimum(blk_start, p_ends[-1] - 1)
        blk_e = jnp.minimum(jnp.sum((last_row[:, None] >= p_ends[None, :]).astype(jnp.int32), axis=1),
                            N_EXPERTS - 1)
        blk_first = jnp.concatenate([jnp.ones((1,), jnp.int32), (blk_e[1:] != blk_e[:-1]).astype(jnp.int32)])

        xs = _dispatch(p_starts, sizes, padded.astype(jnp.int32), runs, lpos, h, n_rows)
        ys = _expert_ffn(blk_e + l * N_EXPERTS, blk_first, blk_valid, xs, wgu_all, bgu_all, wdn_all, bdn_all)
        xf = _combine(p_starts, runs, meta, xn, ys, row2(final_norm), final=(l == depth - 1))

    return xf.reshape(batch, seq, D_MODEL)
```

```python
import functools
import math

import jax
import jax.numpy as jnp
from jax import lax
from jax.experimental import pallas as pl
from jax.experimental.pallas import tpu as pltpu

D_MODEL = 1024
RMS_EPS = 1e-6
MLA_NOPE, MLA_ROPE, MLA_V, MLA_HEADS = 64, 32, 64, 8
MLA_Q_RANK, MLA_KV_RANK = 256, 128
ROPE_THETA = 10000.0
SB_DIM, SB_HEADS = 64, 4
CA_DIM, CA_HEADS = 64, 4
CHUNK = 64
CA_LEFT_CHUNKS = 8
REL_CLIP = 256
N_EXPERTS, TOP_K = 32, 4
D_FF = 1024
SWIGLU_LIMIT, SWIGLU_ALPHA = 7.0, 1.702

LANES = 128
HEAD_PAD = 128
A_W = MLA_HEADS * MLA_V
SB_W = SB_HEADS * SB_DIM
CA_W = CA_HEADS * CA_DIM
CA_PAD = CA_LEFT_CHUNKS * CHUNK
CA_WIN = CA_PAD + 2 * CHUNK
REL_TAB = 1024
HALF = D_MODEL // 2

TM_TOK = 256
TQ_MLA = 256
TQ_SB = 256
TK_SB = 128
TQ_CA = 2 * CHUNK
CA_BLOCKS = 4
ROUTER_TILES = 2
INPROJ_TILES = 4
TM_EXP = 512
RUN_ALIGN = 8
LOCAL_ROWS = 1280
assert LOCAL_ROWS >= TM_TOK * TOP_K + N_EXPERTS * (RUN_ALIGN - 1) and LOCAL_ROWS % LANES == 0
assert TM_EXP % TM_TOK == 0 and TQ_MLA == TM_TOK

_C_CQ, _C_CKV, _C_KPE, _C_KROT = 0, 256, 384, 512
_C_SB, _C_CA, D_IN2 = 640, 1408, 2176

NEG = -1e30
LOG2E = math.log2(math.e)
VMEM_LIMIT = 56 * 1024 * 1024


def _rms(v, g):
    return v * lax.rsqrt(jnp.mean(v * v, axis=-1, keepdims=True) + RMS_EPS) * g


def _dot(a, b):
    return jnp.dot(a, b, preferred_element_type=jnp.float32)


def _dot_nt(a, b):
    return lax.dot_general(a, b, (((1,), (1,)), ((), ())), preferred_element_type=jnp.float32)


def _bf(v):
    return v.astype(jnp.bfloat16)


def _pack_rows(v):
    bits = lax.bitcast_convert_type(v, jnp.uint32)
    return (bits[:, :HALF] >> 16) | (bits[:, HALF:] & jnp.uint32(0xFFFF0000))


def _unpack_rows(u):
    lo = lax.bitcast_convert_type(u << 16, jnp.float32)
    hi = lax.bitcast_convert_type(u & jnp.uint32(0xFFFF0000), jnp.float32)
    return _bf(lo), _bf(hi)


def _cparams(sem):
    return pltpu.CompilerParams(dimension_semantics=sem, vmem_limit_bytes=VMEM_LIMIT)


ROPE_PACK = LANES // MLA_ROPE


def _rope_kernel(pos_ref, inv_ref, cos_ref, sin_ref):
    ang = pos_ref[...].astype(jnp.float32) * inv_ref[...]
    cos_ref[...] = jnp.cos(ang)
    sin_ref[...] = jnp.sin(ang)


def _rope_tables(positions, n):
    inv = ROPE_THETA ** (-jnp.arange(0, MLA_ROPE, 2, dtype=jnp.float32) / MLA_ROPE)
    inv_row = jnp.tile(jnp.concatenate([inv, inv]), ROPE_PACK).reshape(1, LANES)
    rows = n // ROPE_PACK
    pos_packed = jnp.repeat(positions.reshape(rows, ROPE_PACK), MLA_ROPE, axis=1)
    tm = min(TM_TOK, rows)
    cos_p, sin_p = pl.pallas_call(
        _rope_kernel,
        grid=(rows // tm,),
        in_specs=[pl.BlockSpec((tm, LANES), lambda i: (i, 0)),
                  pl.BlockSpec((1, LANES), lambda i: (0, 0))],
        out_specs=[pl.BlockSpec((tm, LANES), lambda i: (i, 0))] * 2,
        out_shape=[jax.ShapeDtypeStruct((rows, LANES), jnp.float32)] * 2,
        compiler_params=_cparams(("parallel",)),
        name="rope_tables",
    )(pos_packed, inv_row)
    pad = HEAD_PAD - MLA_NOPE - MLA_ROPE
    ones = lambda width: jnp.ones((n, width), jnp.float32)
    zeros = lambda width: jnp.zeros((n, width), jnp.float32)
    cos_t = jnp.concatenate([ones(MLA_NOPE), cos_p.reshape(n, MLA_ROPE), ones(pad)], axis=1)
    sin_t = jnp.concatenate([zeros(MLA_NOPE), sin_p.reshape(n, MLA_ROPE), zeros(pad)], axis=1)
    return cos_t, sin_t


def _inproj_kernel(x_ref, g_ref, win_ref, cos_ref, sin_ref, qn_ref, wq_ref, kvn_ref, wkv_ref,
                   q_ref, k_ref, v_ref, sbq_ref, sbk_ref, sbv_ref, caq_ref, cak_ref, cav_ref):
    for j in range(INPROJ_TILES):
        _inproj_tile(j, x_ref, g_ref, win_ref, cos_ref, sin_ref, qn_ref, wq_ref, kvn_ref, wkv_ref,
                     q_ref, k_ref, v_ref, sbq_ref, sbk_ref, sbv_ref, caq_ref, cak_ref, cav_ref)


def _inproj_tile(j, x_ref, g_ref, win_ref, cos_ref, sin_ref, qn_ref, wq_ref, kvn_ref, wkv_ref,
                 q_ref, k_ref, v_ref, sbq_ref, sbk_ref, sbv_ref, caq_ref, cak_ref, cav_ref):
    rows = slice(j * TM_TOK, (j + 1) * TM_TOK)
    h = _rms(x_ref[rows, :], g_ref[...])
    proj = _dot(_bf(h), win_ref[...])
    cos128, sin128 = cos_ref[rows, :], sin_ref[rows, :]
    cos_h = jnp.concatenate([cos128] * MLA_HEADS, axis=1)
    sin_h = jnp.concatenate([sin128] * MLA_HEADS, axis=1)
    w = MLA_HEADS * HEAD_PAD

    cqn = _rms(proj[:, _C_CQ:_C_CKV], qn_ref[...])
    q2 = _dot(_bf(cqn), wq_ref[...])
    q = (q2[:, :w] * cos_h + q2[:, w:] * sin_h) * (LOG2E / math.sqrt(MLA_NOPE + MLA_ROPE))
    q_ref[j] = _bf(jnp.transpose(q))

    ckvn = _rms(proj[:, _C_CKV:_C_KPE], kvn_ref[...])
    kv2 = _dot(_bf(ckvn), wkv_ref[...])
    kpe = proj[:, _C_KPE:_C_KROT] * cos128 + proj[:, _C_KROT:_C_SB] * sin128
    k_ref[rows, :] = _bf(kv2[:, :w] + jnp.concatenate([kpe] * MLA_HEADS, axis=1))
    v_ref[j] = _bf(jnp.transpose(kv2[:, w:]))

    sb_scale = 1.0 / math.sqrt(SB_DIM)
    sbq_ref[rows, :] = _bf(proj[:, _C_SB:_C_SB + SB_W] * sb_scale)
    sbk_ref[rows, :] = _bf(proj[:, _C_SB + SB_W:_C_SB + 2 * SB_W])
    sbv_ref[rows, :] = _bf(proj[:, _C_SB + 2 * SB_W:_C_CA])
    ca_scale = LOG2E / math.sqrt(CA_DIM)
    caq_ref[rows, :] = _bf(proj[:, _C_CA:_C_CA + CA_W] * ca_scale)
    cak_ref[rows, :] = _bf(proj[:, _C_CA + CA_W:_C_CA + 2 * CA_W])
    cav_ref[rows, :] = _bf(proj[:, _C_CA + 2 * CA_W:D_IN2])


def _inproj(x, g, win2, cos_t, sin_t, qn, wq2, kvn, wkv2):
    n = x.shape[0]
    tm = TM_TOK * INPROJ_TILES
    row = lambda width: pl.BlockSpec((tm, width), lambda i: (i, 0))
    full = lambda a: pl.BlockSpec(a.shape, lambda i: (0,) * a.ndim)
    widths = [MLA_HEADS * HEAD_PAD] + [SB_W] * 3 + [CA_W] * 3
    tile_t = lambda rows: pl.BlockSpec((INPROJ_TILES, rows, TM_TOK), lambda i: (i, 0, 0))
    shape_t = lambda rows: jax.ShapeDtypeStruct((n // TM_TOK, rows, TM_TOK), jnp.bfloat16)
    return pl.pallas_call(
        _inproj_kernel,
        grid=(n // tm,),
        in_specs=[row(D_MODEL), full(g), full(win2), row(LANES), row(LANES),
                  full(qn), full(wq2), full(kvn), full(wkv2)],
        out_specs=[tile_t(MLA_HEADS * HEAD_PAD), row(widths[0]), tile_t(A_W)] + [row(wd) for wd in widths[1:]],
        out_shape=[shape_t(MLA_HEADS * HEAD_PAD), jax.ShapeDtypeStruct((n, widths[0]), jnp.bfloat16), shape_t(A_W)]
                  + [jax.ShapeDtypeStruct((n, wd), jnp.bfloat16) for wd in widths[1:]],
        compiler_params=_cparams(("parallel",)),
        name="inproj",
    )(x, g, win2, cos_t, sin_t, qn, wq2, kvn, wkv2)


def _mla_kernel(qt_ref, k_ref, vt_ref, g_ref, o_ref, m_ref, l_ref, acc_ref, s_ref):
    i = pl.program_id(1)
    tq = TQ_MLA
    n_pairs = MLA_HEADS // 2
    key_chunk = lax.broadcasted_iota(jnp.int32, (tq, tq), 0) // CHUNK
    qry_chunk = lax.broadcasted_iota(jnp.int32, (tq, tq), 1) // CHUNK
    diag_ok = key_chunk <= qry_chunk
    top = lax.broadcasted_iota(jnp.int32, (LANES, tq), 0) < MLA_V

    m_ref[...] = jnp.full(m_ref.shape, -jnp.inf, jnp.float32)
    l_ref[...] = jnp.zeros(l_ref.shape, jnp.float32)
    acc_ref[...] = jnp.zeros(acc_ref.shape, jnp.float32)

    def scores(kb, slot):
        start = pl.multiple_of(kb * tq, tq)
        for h in range(MLA_HEADS):
            qt = qt_ref[0, h * HEAD_PAD:(h + 1) * HEAD_PAD, :]
            kblk = k_ref[pl.ds(start, tq), h * HEAD_PAD:(h + 1) * HEAD_PAD]
            s_ref[slot, h] = _dot(kblk, qt)

    def absorb(kb, slot, mask):
        for pr in range(n_pairs):
            vt = vt_ref[kb, pr * LANES:(pr + 1) * LANES, :]
            zero = jnp.zeros_like(vt)
            vt_bd = jnp.concatenate([jnp.where(top, vt, zero), jnp.where(top, zero, vt)], axis=1)
            pts, alphas = [], []
            for h in (2 * pr, 2 * pr + 1):
                s = s_ref[slot, h]
                if mask is not None:
                    s = jnp.where(mask, s, -jnp.inf)
                m_old = m_ref[h]
                m_new = jnp.maximum(m_old, jnp.max(s, axis=0, keepdims=True))
                p = jnp.exp2(s - m_new)
                alpha = jnp.exp2(m_old - m_new)
                l_ref[h] = alpha * l_ref[h] + jnp.sum(p, axis=0, keepdims=True)
                m_ref[h] = m_new
                pts.append(_bf(p))
                alphas.append(alpha)
            alpha_pair = jnp.where(top, alphas[0], alphas[1])
            acc_ref[pr] = alpha_pair * acc_ref[pr] + _dot(vt_bd, jnp.concatenate(pts, axis=0))

    def body(j, carry):
        scores(2 * j + 1, 1)
        absorb(2 * j, 0, None)
        scores(2 * j + 2, 0)
        absorb(2 * j + 1, 1, None)
        return carry

    scores(0, 0)
    lax.fori_loop(0, i // 2, body, 0)

    @pl.when(i % 2 == 0)
    def _():
        absorb(i, 0, diag_ok)

    @pl.when(i % 2 == 1)
    def _():
        scores(i, 1)
        absorb(i - 1, 0, None)
        absorb(i, 1, diag_ok)

    outs = [jnp.transpose(acc_ref[pr] / jnp.where(top, l_ref[2 * pr], l_ref[2 * pr + 1])) for pr in range(n_pairs)]
    o = jnp.concatenate(outs, axis=1)
    o_ref[...] = _bf(_rms(o, g_ref[...]))


def _mla_attention(qt, k, vt, g, batch, seq):
    nq = seq // TQ_MLA
    w = MLA_HEADS * HEAD_PAD
    return pl.pallas_call(
        _mla_kernel,
        grid=(batch, nq),
        in_specs=[pl.BlockSpec((1, w, TQ_MLA), lambda b, i: (b * nq + i, 0, 0)),
                  pl.BlockSpec((seq, w), lambda b, i: (b, 0)),
                  pl.BlockSpec((nq, A_W, TQ_MLA), lambda b, i: (b, 0, 0)),
                  pl.BlockSpec((1, A_W), lambda b, i: (0, 0))],
        out_specs=pl.BlockSpec((TQ_MLA, A_W), lambda b, i: (b * nq + i, 0)),
        out_shape=jax.ShapeDtypeStruct((batch * seq, A_W), jnp.bfloat16),
        scratch_shapes=[pltpu.VMEM((MLA_HEADS, 1, TQ_MLA), jnp.float32),
                        pltpu.VMEM((MLA_HEADS, 1, TQ_MLA), jnp.float32),
                        pltpu.VMEM((MLA_HEADS // 2, LANES, TQ_MLA), jnp.float32),
                        pltpu.VMEM((2, MLA_HEADS, TQ_MLA, TQ_MLA), jnp.float32)],
        compiler_params=_cparams(("parallel", "parallel")),
        name="mla_attention",
    )(qt, k, vt, g)


def _sb_kernel(q_ref, k_ref, v_ref, g_ref, o_ref, run_ref, acc_ref, zl_ref, sums_ref):
    i = pl.program_id(1)
    tq, tk = TQ_SB, TK_SB
    n_pairs = SB_HEADS // 2
    r = lax.broadcasted_iota(jnp.int32, (2 * tq, tk), 0)
    c = lax.broadcasted_iota(jnp.int32, (2 * tq, tk), 1)
    q_off = jnp.where(r >= tq, r - tq, r)
    top_low = (r < tq) == (c < SB_DIM)
    r2 = lax.broadcasted_iota(jnp.int32, (tk, 2 * tk), 0)
    c2 = lax.broadcasted_iota(jnp.int32, (tk, 2 * tk), 1)
    sum_mat = jnp.where((c2 >= tk) | (r2 > c2), 1.0, 0.0).astype(jnp.bfloat16)

    run_ref[...] = jnp.zeros(run_ref.shape, jnp.float32)
    acc_ref[...] = jnp.zeros(acc_ref.shape, jnp.float32)

    hq = tq // 2

    def late(v):
        return jnp.concatenate([v[hq:tq], v[tq + hq:2 * tq]], axis=0)

    def step(kb_hi, masked):
        masks = []
        for d in range(2):
            kb = kb_hi - d
            half = masked and d == 0
            start = pl.multiple_of(kb * tk, tk)
            mask = (c + (kb * tk - i * tq)) < q_off if masked else None
            if half:
                mask = late(mask)
            masks.append(mask)
            for pr in range(n_pairs):
                qp = q_ref[:, pr * LANES:(pr + 1) * LANES]
                q2 = jnp.concatenate([qp, qp], axis=0)
                qm = jnp.where(top_low, q2, jnp.zeros_like(q2))
                if half:
                    qm = late(qm)
                kblk = k_ref[pl.ds(start, tk), pr * LANES:(pr + 1) * LANES]
                z = _dot_nt(qm, kblk)
                log_keep = -(jnp.maximum(z, 0.0) + jnp.log(1.0 + jnp.exp(-jnp.abs(z))))
                if masked:
                    log_keep = jnp.where(mask, log_keep, 0.0)
                hi = _bf(log_keep)
                lo = _bf(log_keep - hi.astype(jnp.float32))
                n = z.shape[0]
                sums_ref[2 * d + pr, 0:n] = _dot(hi, sum_mat) + _dot(lo, sum_mat)
                zl_ref[2 * d + pr, 0:n] = z + log_keep
        slowest = None
        for d in range(2):
            half = masked and d == 0
            start = pl.multiple_of((kb_hi - d) * tk, tk)
            for pr in range(n_pairs):
                vblk = v_ref[pl.ds(start, tk), pr * LANES:(pr + 1) * LANES]
                if half:
                    run = late(run_ref[pr])
                    a = jnp.exp(zl_ref[2 * d + pr, 0:tq] + run + sums_ref[2 * d + pr, 0:tq, 0:tk])
                    upd = _dot(_bf(jnp.where(masks[d], a, 0.0)), vblk)
                    run = run + sums_ref[2 * d + pr, 0:tq, tk:2 * tk]
                    for src, dst in ((slice(0, hq), slice(hq, tq)), (slice(hq, tq), slice(tq + hq, 2 * tq))):
                        acc_ref[pr, dst] = acc_ref[pr, dst] + upd[src]
                        run_ref[pr, dst] = run[src]
                    continue
                run = run_ref[pr]
                a = jnp.exp(zl_ref[2 * d + pr] + run + sums_ref[2 * d + pr, :, 0:tk])
                if masked:
                    a = jnp.where(masks[d], a, 0.0)
                acc_ref[pr] = acc_ref[pr] + _dot(_bf(a), vblk)
                run = run + sums_ref[2 * d + pr, :, tk:2 * tk]
                run_ref[pr] = run
                if d == 1:
                    top = jnp.max(run)
                    slowest = top if slowest is None else jnp.maximum(slowest, top)
        return slowest

    underflow = -104.0
    assert tq == 2 * tk
    first = step(2 * i + 1, True)
    lax.while_loop(lambda cr: (cr[0] >= 0) & (cr[1] > underflow),
                   lambda cr: (cr[0] - 1, step(2 * cr[0] + 1, False)),
                   (i - 1, first))

    lane = lax.broadcasted_iota(jnp.int32, (tq, LANES), 1)
    outs = [jnp.where(lane < SB_DIM, acc_ref[pr, 0:tq, :], acc_ref[pr, tq:2 * tq, :]) for pr in range(n_pairs)]
    o = jnp.concatenate(outs, axis=1)
    o_ref[...] = _bf(_rms(o, g_ref[...]))


def _sb_attention(q, k, v, g, batch, seq):
    nq = seq // TQ_SB
    return pl.pallas_call(
        _sb_kernel,
        grid=(batch, nq),
        in_specs=[pl.BlockSpec((TQ_SB, SB_W), lambda b, i: (b * nq + i, 0)),
                  pl.BlockSpec((seq, SB_W), lambda b, i: (b, 0)),
                  pl.BlockSpec((seq, SB_W), lambda b, i: (b, 0)),
                  pl.BlockSpec((1, SB_W), lambda b, i: (0, 0))],
        out_specs=pl.BlockSpec((TQ_SB, SB_W), lambda b, i: (b * nq + i, 0)),
        out_shape=jax.ShapeDtypeStruct((batch * seq, SB_W), jnp.bfloat16),
        scratch_shapes=[pltpu.VMEM((SB_HEADS // 2, 2 * TQ_SB, TK_SB), jnp.float32),
                        pltpu.VMEM((SB_HEADS // 2, 2 * TQ_SB, LANES), jnp.float32),
                        pltpu.VMEM((SB_HEADS, 2 * TQ_SB, TK_SB), jnp.float32),
                        pltpu.VMEM((SB_HEADS, 2 * TQ_SB, 2 * TK_SB), jnp.float32)],
        compiler_params=_cparams(("parallel", "parallel")),
        name="sb_attention",
    )(q, k, v, g)


def _ca_kernel(q_ref, k_ref, v_ref, tab_ref, g_ref, o_ref, bias_ref):
    t = TQ_CA

    @pl.when((pl.program_id(0) == 0) & (pl.program_id(1) == 0))
    def _():
        r = lax.broadcasted_iota(jnp.int32, (t, CA_WIN), 0)
        c = lax.broadcasted_iota(jnp.int32, (t, CA_WIN), 1)
        lo = (r // CHUNK) * CHUNK
        band = (c >= lo) & (c < lo + CA_PAD + CHUNK)
        for h in range(CA_HEADS):
            tab = jnp.broadcast_to(tab_ref[h:h + 1, :], (t, REL_TAB))
            bias = pltpu.roll(tab, REL_TAB - (REL_CLIP - 1), 1, stride=1, stride_axis=0)[:, :CA_WIN]
            bias_ref[h] = jnp.where(band, bias * LOG2E, -jnp.inf)

    for j in range(CA_BLOCKS):
        _ca_block(pl.program_id(1) * CA_BLOCKS + j, slice(j * t, (j + 1) * t), q_ref, k_ref, v_ref, g_ref, o_ref, bias_ref)


def _ca_block(i, rows, q_ref, k_ref, v_ref, g_ref, o_ref, bias_ref):
    t = TQ_CA
    start = pl.multiple_of(i * t, t)
    r2 = lax.broadcasted_iota(jnp.int32, (2 * t, LANES), 0)
    c2 = lax.broadcasted_iota(jnp.int32, (2 * t, LANES), 1)
    top_low = (r2 < t) == (c2 < CA_DIM)
    in_seq = lax.broadcasted_iota(jnp.int32, (2 * t, CA_WIN), 1) + i * t >= CA_PAD
    lane = lax.broadcasted_iota(jnp.int32, (t, LANES), 1)

    outs = []
    for pair in range(CA_HEADS // 2):
        qp = q_ref[rows, pair * LANES:(pair + 1) * LANES]
        q2 = jnp.concatenate([qp, qp], axis=0)
        qm = jnp.where(top_low, q2, jnp.zeros_like(q2))
        kwin = k_ref[pl.ds(start, CA_WIN), pair * LANES:(pair + 1) * LANES]
        vwin = v_ref[pl.ds(start, CA_WIN), pair * LANES:(pair + 1) * LANES]
        bias = jnp.concatenate([bias_ref[2 * pair], bias_ref[2 * pair + 1]], axis=0)
        s = jnp.where(in_seq, _dot_nt(qm, kwin) + bias, -jnp.inf)
        m = jnp.max(s, axis=-1, keepdims=True)
        p = jnp.exp2(s - m)
        l = jnp.sum(p, axis=-1, keepdims=True)
        o2 = _dot(_bf(p), vwin) / l
        outs.append(jnp.where(lane < CA_DIM, o2[0:t], o2[t:2 * t]))
    o = jnp.concatenate(outs, axis=1)
    o_ref[rows, :] = _bf(_rms(o, g_ref[...]))


def _ca_attention(q, kpad, vpad, tab, g, batch, seq):
    tq = TQ_CA * CA_BLOCKS
    nq = seq // tq
    return pl.pallas_call(
        _ca_kernel,
        grid=(batch, nq),
        in_specs=[pl.BlockSpec((tq, CA_W), lambda b, i: (b * nq + i, 0)),
                  pl.BlockSpec((seq + CA_PAD, CA_W), lambda b, i: (b, 0)),
                  pl.BlockSpec((seq + CA_PAD, CA_W), lambda b, i: (b, 0)),
                  pl.BlockSpec((CA_HEADS, REL_TAB), lambda b, i: (0, 0)),
                  pl.BlockSpec((1, CA_W), lambda b, i: (0, 0))],
        out_specs=pl.BlockSpec((tq, CA_W), lambda b, i: (b * nq + i, 0)),
        out_shape=jax.ShapeDtypeStruct((batch * seq, CA_W), jnp.bfloat16),
        scratch_shapes=[pltpu.VMEM((CA_HEADS, TQ_CA, CA_WIN), jnp.float32)],
        compiler_params=_cparams(("arbitrary", "arbitrary")),
        name="ca_attention",
    )(q, kpad, vpad, tab, g)


def _outproj_router_kernel(ma_ref, mb_ref, mc_ref, wo_ref, x_ref, g_ref, wr2_ref, br_ref,
                           xn_ref, h_ref, lpos_ref, meta_ref, runs_ref, tot_ref, base_ref):
    @pl.when(pl.program_id(0) == 0)
    def _():
        base_ref[...] = jnp.zeros_like(base_ref)

    base = base_ref[...]
    for j in range(ROUTER_TILES):
        base = _route_tile(j, base, ma_ref, mb_ref, mc_ref, wo_ref, x_ref, g_ref, wr2_ref, br_ref,
                           xn_ref, h_ref, lpos_ref, meta_ref, runs_ref)
    base_ref[...] = base
    tot_ref[...] = base.astype(jnp.int32)


def _route_tile(j, base, ma_ref, mb_ref, mc_ref, wo_ref, x_ref, g_ref, wr2_ref, br_ref,
                xn_ref, h_ref, lpos_ref, meta_ref, runs_ref):
    tm = TM_TOK
    rows = slice(j * tm, (j + 1) * tm)
    attn = (_dot(ma_ref[rows, :], wo_ref[0:A_W, :]) + _dot(mb_ref[rows, :], wo_ref[A_W:A_W + SB_W, :])
            + _dot(mc_ref[rows, :], wo_ref[A_W + SB_W:, :]))
    xn = x_ref[rows, :] + attn
    xn_ref[rows, :] = xn
    h = _rms(xn, g_ref[...])

    h_hi = _bf(h)
    h_ref[rows, :] = h_hi
    h_lo = _bf(h - h_hi.astype(jnp.float32))
    both = _dot(h_hi, wr2_ref[...])
    logits = (both[:, :LANES] + both[:, LANES:] + _dot(h_lo, wr2_ref[:, 0:LANES])
              + br_ref[...])
    lane = lax.broadcasted_iota(jnp.int32, (tm, LANES), 1)
    lane_f = lane.astype(jnp.float32)

    work = logits
    vals, idxs, hots = [], [], []
    for _ in range(TOP_K):
        mx = jnp.max(work, axis=-1, keepdims=True)
        ix = jnp.min(jnp.where(work == mx, lane_f, float(LANES)), axis=-1, keepdims=True)
        hot = lane_f == ix
        work = jnp.where(hot, -jnp.inf, work)
        vals.append(mx)
        idxs.append(ix)
        hots.append(hot)
    exps = [jnp.exp(v - vals[0]) for v in vals]
    denom = exps[0] + exps[1] + exps[2] + exps[3]
    gates = [e / denom for e in exps]

    sel = jnp.zeros((tm, LANES), jnp.float32)
    for hot in hots:
        sel = sel + jnp.where(hot, 1.0, 0.0)
    r = lax.broadcasted_iota(jnp.int32, (tm, tm), 0)
    c = lax.broadcasted_iota(jnp.int32, (tm, tm), 1)
    before = jnp.where(c < r, 1.0, 0.0).astype(jnp.bfloat16)
    rank_in_tile = _dot(before, _bf(sel))

    cnt = jnp.sum(sel, axis=0, keepdims=True)
    cnt_al = jnp.ceil(cnt * (1.0 / RUN_ALIGN)) * RUN_ALIGN
    rl = lax.broadcasted_iota(jnp.int32, (LANES, LANES), 0)
    cl = lax.broadcasted_iota(jnp.int32, (LANES, LANES), 1)
    earlier = jnp.where(rl < cl, 1.0, 0.0).astype(jnp.bfloat16)
    loff = _dot(_bf(jnp.broadcast_to(cnt_al, (8, LANES))), earlier)[0:1, :]
    sub = lax.broadcasted_iota(jnp.int32, (8, LANES), 0)
    has_long = jnp.where(jnp.max(cnt_al, axis=-1, keepdims=True) >= LONG_RUN, 1.0, 0.0)
    runs = jnp.where(sub == 0, loff, jnp.where(sub == 1, base, jnp.where(sub == 2, cnt_al,
                     jnp.where(sub == 3, has_long, 0.0))))
    runs_ref[8 * j:8 * j + 8, :] = runs.astype(jnp.int32)

    lpos_dense = loff + rank_in_tile
    meta = jnp.zeros((tm, LANES), jnp.float32)
    for kk in range(TOP_K):
        lpos = jnp.sum(jnp.where(hots[kk], lpos_dense, 0.0), axis=-1, keepdims=True)
        meta = meta + jnp.where(lane == kk, gates[kk], 0.0) + jnp.where(lane == TOP_K + kk, lpos, 0.0)
    meta_ref[rows, :] = meta
    lpos_ref[:, rows] = jnp.transpose(meta)[TOP_K:TOP_K + 8, :].astype(jnp.int32)
    return base + cnt_al


def _outproj_router(ma, mb, mc, wo, x, g, wr2, br):
    n = x.shape[0]
    tm = TM_TOK * ROUTER_TILES
    row = lambda width: pl.BlockSpec((tm, width), lambda i: (i, 0))
    full = lambda a: pl.BlockSpec(a.shape, lambda i: (0,) * a.ndim)
    return pl.pallas_call(
        _outproj_router_kernel,
        grid=(n // tm,),
        in_specs=[row(A_W), row(SB_W), row(CA_W), full(wo), row(D_MODEL), full(g), full(wr2), full(br)],
        out_specs=[row(D_MODEL), row(D_MODEL), pl.BlockSpec((8, tm), lambda i: (0, i)),
                   row(LANES), pl.BlockSpec((8 * ROUTER_TILES, LANES), lambda i: (i, 0)),
                   pl.BlockSpec((1, LANES), lambda i: (0, 0))],
        out_shape=[jax.ShapeDtypeStruct((n, D_MODEL), jnp.float32),
                   jax.ShapeDtypeStruct((n, D_MODEL), jnp.bfloat16),
                   jax.ShapeDtypeStruct((8, n), jnp.int32),
                   jax.ShapeDtypeStruct((n, LANES), jnp.float32),
                   jax.ShapeDtypeStruct((8 * (n // TM_TOK), LANES), jnp.int32),
                   jax.ShapeDtypeStruct((1, LANES), jnp.int32)],
        scratch_shapes=[pltpu.VMEM((1, LANES), jnp.float32)],
        compiler_params=_cparams(("arbitrary",)),
        name="outproj_router",
    )(ma, mb, mc, wo, x, g, wr2, br)


_RUN_CHUNKS = tuple(TM_TOK >> s for s in range(TM_TOK.bit_length()) if (TM_TOK >> s) >= RUN_ALIGN)


LONG_RUN = 128


def _for_each_chunk(length, fn, sizes=_RUN_CHUNKS):
    for size in sizes:
        off = length & (~(2 * size - 1))

        @pl.when((length & size) != 0)
        def _(off=off, size=size):
            fn(off, size)


_WAIT_CHUNKS = tuple(1 << b for b in range((LOCAL_ROWS).bit_length() - 1, RUN_ALIGN.bit_length() - 2, -1))


def _wait_tile_runs(runs_ref, make_copy):
    total = runs_ref[0, N_EXPERTS - 1] + runs_ref[2, N_EXPERTS - 1]
    for size in _WAIT_CHUNKS:
        @pl.when((total & size) != 0)
        def _(size=size):
            make_copy(size).wait()


def _tile_runs(runs_ref, pstart_ref, fn):
    def experts(sizes, unroll):
        def body(e, _):
            local, glob, length = runs_ref[0, e], pstart_ref[e] + runs_ref[1, e], runs_ref[2, e]
            _for_each_chunk(length, lambda off, size: fn(pl.multiple_of(local + off, RUN_ALIGN),
                                                         pl.multiple_of(glob + off, RUN_ALIGN), size), sizes)
            return 0

        lax.fori_loop(0, N_EXPERTS, body, 0, unroll=unroll)

    experts(tuple(sz for sz in _RUN_CHUNKS if sz < LONG_RUN), 4)

    @pl.when(runs_ref[3, 0] != 0)
    def _():
        experts(tuple(sz for sz in _RUN_CHUNKS if sz >= LONG_RUN), 1)


def _dispatch_kernel(pstart_ref, tot_ref, pad_ref, runs_ref, prev_runs_ref, lpos_ref, h_ref, xs_ref,
                     loc, zbuf, sem, fill_sem):
    i = pl.program_id(0)
    slot = i % 2

    def fill(start_or_wait):
        def body(e, _):
            first = pstart_ref[e] + tot_ref[e]
            _for_each_chunk(pad_ref[e] - tot_ref[e], lambda off, size: start_or_wait(pltpu.make_async_copy(
                zbuf.at[pl.ds(0, size)], xs_ref.at[pl.ds(pl.multiple_of(first + off, RUN_ALIGN), size)], fill_sem)))
            return 0

        lax.fori_loop(0, N_EXPERTS, body, 0)

        used = pstart_ref[N_EXPERTS - 1] + pad_ref[N_EXPERTS - 1]

        def tail(b, _):
            start_or_wait(pltpu.make_async_copy(
                zbuf, xs_ref.at[pl.ds(pl.multiple_of(b * TM_TOK, TM_TOK), TM_TOK)], fill_sem))
            return 0

        lax.fori_loop(used // TM_TOK, xs_ref.shape[0] // TM_TOK, tail, 0)

    @pl.when(i == 0)
    def _():
        zbuf[...] = jnp.zeros_like(zbuf)
        fill(lambda c: c.start())
        fill(lambda c: c.wait())

    r = lax.broadcasted_iota(jnp.int32, (LOCAL_ROWS, TM_TOK), 0)
    hit = r == lpos_ref[0:1, :]
    for kk in range(1, TOP_K):
        hit = hit | (r == lpos_ref[kk:kk + 1, :])
    loc[slot] = _pack_rows(_dot(jnp.where(hit, 1.0, 0.0).astype(jnp.bfloat16), h_ref[...]))

    def copy(buf, local, glob, size):
        return pltpu.make_async_copy(loc.at[buf, pl.ds(local, size)], xs_ref.at[pl.ds(glob, size)], sem.at[buf])

    _tile_runs(runs_ref, pstart_ref, lambda l, g, s: copy(slot, l, g, s).start())

    @pl.when(i > 0)
    def _():
        _wait_tile_runs(prev_runs_ref, lambda size: copy(1 - slot, 0, 0, size))

    @pl.when(i == pl.num_programs(0) - 1)
    def _():
        _wait_tile_runs(runs_ref, lambda size: copy(slot, 0, 0, size))


def _dispatch(pstart, totals, padded, runs, lpos, h, n_rows):
    n = h.shape[0]
    tm = TM_TOK
    grid_spec = pltpu.PrefetchScalarGridSpec(
        num_scalar_prefetch=3,
        grid=(n // tm,),
        in_specs=[pl.BlockSpec((8, LANES), lambda i, *_: (i, 0), memory_space=pltpu.SMEM),
                  pl.BlockSpec((8, LANES), lambda i, *_: (jnp.maximum(i - 1, 0), 0), memory_space=pltpu.SMEM),
                  pl.BlockSpec((8, tm), lambda i, *_: (0, i)),
                  pl.BlockSpec((tm, D_MODEL), lambda i, *_: (i, 0))],
        out_specs=pl.BlockSpec(memory_space=pl.ANY),
        scratch_shapes=[pltpu.VMEM((2, LOCAL_ROWS, HALF), jnp.uint32),
                        pltpu.VMEM((TM_TOK, HALF), jnp.uint32),
                        pltpu.SemaphoreType.DMA((2,)),
                        pltpu.SemaphoreType.DMA],
    )
    return pl.pallas_call(
        _dispatch_kernel,
        grid_spec=grid_spec,
        out_shape=jax.ShapeDtypeStruct((n_rows, HALF), jnp.uint32),
        compiler_params=_cparams(("arbitrary",)),
        name="dispatch",
    )(pstart, totals, padded, runs, runs, lpos, h)


def _expert_kernel(be_ref, bfirst_ref, bvalid_ref, xs_ref, wgu_ref, bgu_ref, wdn_ref, bdn_ref,
                   ys_ref, wgu_bf, wdn_bf):
    b = pl.program_id(0)

    @pl.when(bfirst_ref[b] == 1)
    def _():
        wgu_bf[...] = _bf(wgu_ref[0])
        wdn_bf[...] = _bf(wdn_ref[0])

    @pl.when(bvalid_ref[b] == 1)
    def _():
        xb = jnp.concatenate(_unpack_rows(xs_ref[...]), axis=1)
        gu = _dot(xb, wgu_bf[...]) + bgu_ref[0]
        gte = jnp.minimum(gu[:, :D_FF], SWIGLU_LIMIT)
        up = jnp.clip(gu[:, D_FF:], -SWIGLU_LIMIT, SWIGLU_LIMIT)
        act = (up + 1.0) * (gte * (1.0 / (1.0 + jnp.exp(-SWIGLU_ALPHA * gte))))
        y = _dot(_bf(act), wdn_bf[...]) + bdn_ref[0]
        ys_ref[...] = _pack_rows(_bf(y).astype(jnp.float32))

    @pl.when(bvalid_ref[b] == 0)
    def _():
        ys_ref[...] = jnp.zeros_like(ys_ref)


def _expert_ffn(blk_e, blk_first, blk_valid, xs, wgu, bgu, wdn, bdn):
    n_rows = xs.shape[0]
    tm = TM_EXP
    grid_spec = pltpu.PrefetchScalarGridSpec(
        num_scalar_prefetch=3,
        grid=(n_rows // tm,),
        in_specs=[pl.BlockSpec((tm, HALF), lambda b, e, f, v: (b, 0)),
                  pl.BlockSpec((1, D_MODEL, 2 * D_FF), lambda b, e, f, v: (e[b], 0, 0)),
                  pl.BlockSpec((1, 1, 2 * D_FF), lambda b, e, f, v: (e[b], 0, 0)),
                  pl.BlockSpec((1, D_FF, D_MODEL), lambda b, e, f, v: (e[b], 0, 0)),
                  pl.BlockSpec((1, 1, D_MODEL), lambda b, e, f, v: (e[b], 0, 0))],
        out_specs=pl.BlockSpec((tm, HALF), lambda b, e, f, v: (b, 0)),
        scratch_shapes=[pltpu.VMEM((D_MODEL, 2 * D_FF), jnp.bfloat16),
                        pltpu.VMEM((D_FF, D_MODEL), jnp.bfloat16)],
    )
    return pl.pallas_call(
        _expert_kernel,
        grid_spec=grid_spec,
        out_shape=jax.ShapeDtypeStruct((n_rows, HALF), jnp.uint32),
        compiler_params=_cparams(("arbitrary",)),
        name="expert_ffn",
    )(blk_e, blk_first, blk_valid, xs, wgu, bgu, wdn, bdn)


def _combine_kernel(final, pstart_ref, runs_ref, next_runs_ref, meta_ref, x_ref, ys_ref, gfin_ref, o_ref, loc, sem):
    i = pl.program_id(0)
    slot = i % 2

    def copy(buf, local, glob, size):
        return pltpu.make_async_copy(ys_ref.at[pl.ds(glob, size)], loc.at[buf, pl.ds(local, size)], sem.at[buf])

    @pl.when(i == 0)
    def _():
        loc[...] = jnp.zeros_like(loc)
        _tile_runs(runs_ref, pstart_ref, lambda l, g, s: copy(slot, l, g, s).start())

    @pl.when(i + 1 < pl.num_programs(0))
    def _():
        _tile_runs(next_runs_ref, pstart_ref, lambda l, g, s: copy(1 - slot, l, g, s).start())

    _wait_tile_runs(runs_ref, lambda size: copy(slot, 0, 0, size))

    meta = meta_ref[...]
    col = lax.broadcasted_iota(jnp.int32, (TM_TOK, LOCAL_ROWS), 1).astype(jnp.float32)
    wts = jnp.zeros((TM_TOK, LOCAL_ROWS), jnp.float32)
    for kk in range(TOP_K):
        wts = wts + jnp.where(col == meta[:, TOP_K + kk:TOP_K + kk + 1], meta[:, kk:kk + 1], 0.0)
    w_hi = _bf(wts)
    w_lo = _bf(wts - w_hi.astype(jnp.float32))
    w2 = jnp.concatenate([w_hi, w_lo], axis=0)
    y_lo, y_hi = _unpack_rows(loc[slot])
    r_lo, r_hi = _dot(w2, y_lo), _dot(w2, y_hi)
    out = x_ref[...] + jnp.concatenate([r_lo[:TM_TOK] + r_lo[TM_TOK:], r_hi[:TM_TOK] + r_hi[TM_TOK:]], axis=1)
    if final:
        out = _rms(out, gfin_ref[...])
    o_ref[...] = out


def _combine(pstart, runs, meta, x, ys, gfin, final):
    n = x.shape[0]
    tm = TM_TOK
    n_tiles = n // tm
    grid_spec = pltpu.PrefetchScalarGridSpec(
        num_scalar_prefetch=1,
        grid=(n_tiles,),
        in_specs=[pl.BlockSpec((8, LANES), lambda i, ps: (i, 0), memory_space=pltpu.SMEM),
                  pl.BlockSpec((8, LANES), lambda i, ps: (jnp.minimum(i + 1, n_tiles - 1), 0), memory_space=pltpu.SMEM),
                  pl.BlockSpec((tm, LANES), lambda i, ps: (i, 0)),
                  pl.BlockSpec((tm, D_MODEL), lambda i, ps: (i, 0)),
                  pl.BlockSpec(memory_space=pl.ANY),
                  pl.BlockSpec((1, D_MODEL), lambda i, ps: (0, 0))],
        out_specs=pl.BlockSpec((tm, D_MODEL), lambda i, ps: (i, 0)),
        scratch_shapes=[pltpu.VMEM((2, LOCAL_ROWS, HALF), jnp.uint32), pltpu.SemaphoreType.DMA((2,))],
    )
    return pl.pallas_call(
        functools.partial(_combine_kernel, final),
        grid_spec=grid_spec,
        out_shape=jax.ShapeDtypeStruct((n, D_MODEL), jnp.float32),
        compiler_params=_cparams(("arbitrary",)),
        name="combine_final" if final else "combine",
    )(pstart, runs, runs, meta, x, ys, gfin)


def _pad_heads(w, parts):
    rows = w.shape[0]
    per = w.shape[1] // MLA_HEADS
    w3 = w.reshape(rows, MLA_HEADS, per)
    cols = [jnp.zeros((rows, MLA_HEADS, b - a), w.dtype) if sign == 0 else sign * w3[:, :, a:b]
            for a, b, sign in parts]
    used = sum(b - a for a, b, _ in parts)
    cols.append(jnp.zeros((rows, MLA_HEADS, HEAD_PAD - used), w.dtype))
    return jnp.concatenate(cols, axis=2).reshape(rows, MLA_HEADS * HEAD_PAD)


def _layer_weights(w_in, w_uq, w_ukv, w_router, b_router, rel_bias):
    half = MLA_ROPE // 2
    cq, ckv, kpe, sb, ca = (w_in[:, 0:256], w_in[:, 256:384], w_in[:, 384:416],
                            w_in[:, 416:1184], w_in[:, 1184:1952])
    z = lambda width: jnp.zeros((D_MODEL, width), w_in.dtype)
    kpe_pad = jnp.concatenate([z(MLA_NOPE), kpe, z(HEAD_PAD - MLA_NOPE - MLA_ROPE)], axis=1)
    kpe_rot = jnp.concatenate([z(MLA_NOPE), -kpe[:, half:], kpe[:, :half],
                               z(HEAD_PAD - MLA_NOPE - MLA_ROPE)], axis=1)
    win2 = jnp.concatenate([cq, ckv, kpe_pad, kpe_rot, sb, ca], axis=1).astype(jnp.bfloat16)

    d = MLA_NOPE + MLA_ROPE
    wq_full = _pad_heads(w_uq, [(0, d, 1)])
    wq_rot = _pad_heads(w_uq, [(0, MLA_NOPE, 0), (MLA_NOPE + half, d, -1), (MLA_NOPE, MLA_NOPE + half, 1)])
    wq2 = jnp.concatenate([wq_full, wq_rot], axis=1).astype(jnp.bfloat16)

    wk = _pad_heads(w_ukv, [(0, MLA_NOPE, 1)])
    wv = w_ukv.reshape(MLA_KV_RANK, MLA_HEADS, MLA_NOPE + MLA_V)[:, :, MLA_NOPE:].reshape(MLA_KV_RANK, A_W)
    wkv2 = jnp.concatenate([wk, wv], axis=1).astype(jnp.bfloat16)

    wr = jnp.pad(w_router, ((0, 0), (0, LANES - N_EXPERTS)))
    wr_hi = wr.astype(jnp.bfloat16)
    wr_lo = (wr - wr_hi.astype(jnp.float32)).astype(jnp.bfloat16)
    wr2 = jnp.concatenate([wr_hi, wr_lo], axis=1)
    br = jnp.pad(b_router, (0, LANES - N_EXPERTS), constant_values=NEG).reshape(1, LANES)

    ext = jnp.concatenate([rel_bias, jnp.broadcast_to(rel_bias[:, -1:], (CA_HEADS, REL_TAB - 2 * REL_CLIP - 1))], axis=1)
    tab = ext[:, ::-1]
    return win2, wq2, wkv2, wr2, br, tab


def kernel(x, positions, attn_norm, w_in, q_norm, w_uq, kv_norm, w_ukv, rel_bias, mix_norm,
           w_o, ffn_norm, w_router, b_router, w_gate_up, b_gate_up, w_down, b_down, final_norm):
    batch, seq, _ = x.shape
    n = batch * seq
    depth = w_in.shape[0]
    xf = x.reshape(n, D_MODEL)

    cos_t, sin_t = _rope_tables(positions, n)

    n_rows = n * TOP_K + (n // TM_TOK) * N_EXPERTS * (RUN_ALIGN - 1) + N_EXPERTS * TM_EXP
    n_rows = -(-n_rows // TM_EXP) * TM_EXP
    n_blk = n_rows // TM_EXP
    row2 = lambda v: v.reshape(1, -1)
    wgu_all = w_gate_up.reshape(depth * N_EXPERTS, D_MODEL, 2 * D_FF)
    bgu_all = b_gate_up.reshape(depth * N_EXPERTS, 1, 2 * D_FF)
    wdn_all = w_down.reshape(depth * N_EXPERTS, D_FF, D_MODEL)
    bdn_all = b_down.reshape(depth * N_EXPERTS, 1, D_MODEL)

    for l in range(depth):
        win2, wq2, wkv2, wr2, br, tab = _layer_weights(
            w_in[l], w_uq[l], w_ukv[l], w_router[l], b_router[l], rel_bias[l])
        qt, k, vt, sbq, sbk, sbv, caq, cak, cav = _inproj(
            xf, row2(attn_norm[l]), win2, cos_t, sin_t, row2(q_norm[l]), wq2, row2(kv_norm[l]), wkv2)

        g = mix_norm[l]
        ma = _mla_attention(qt, k, vt, row2(g[:A_W]), batch, seq)
        mb = _sb_attention(sbq, sbk, sbv, row2(g[A_W:A_W + SB_W]), batch, seq)
        pad = lambda a: jnp.pad(a.reshape(batch, seq, CA_W), ((0, 0), (CA_PAD, 0), (0, 0))).reshape(-1, CA_W)
        mc = _ca_attention(caq, pad(cak), pad(cav), tab, row2(g[A_W + SB_W:]), batch, seq)

        xn, h, lpos, meta, runs, totals = _outproj_router(
            ma, mb, mc, w_o[l].astype(jnp.bfloat16), xf, row2(ffn_norm[l]), wr2, br)

        sizes = totals[0, :N_EXPERTS]
        padded = (sizes + TM_EXP - 1) // TM_EXP * TM_EXP
        p_ends = jnp.cumsum(padded)
        p_starts = (p_ends - padded).astype(jnp.int32)
        blk_start = jnp.arange(n_blk, dtype=jnp.int32) * TM_EXP
        blk_valid = (blk_start < p_ends[-1]).astype(jnp.int32)
        last_row = jnp.minimum(blk_start, p_ends[-1] - 1)
        blk_e = jnp.minimum(jnp.sum((last_row[:, None] >= p_ends[None, :]).astype(jnp.int32), axis=1),
                            N_EXPERTS - 1)
        blk_first = jnp.concatenate([jnp.ones((1,), jnp.int32), (blk_e[1:] != blk_e[:-1]).astype(jnp.int32)])

        xs = _dispatch(p_starts, sizes, padded.astype(jnp.int32), runs, lpos, h, n_rows)
        ys = _expert_ffn(blk_e + l * N_EXPERTS, blk_first, blk_valid, xs, wgu_all, bgu_all, wdn_all, bdn_all)
        xf = _combine(p_starts, runs, meta, xn, ys, row2(final_norm), final=(l == depth - 1))

    return xf.reshape(batch, seq, D_MODEL)
```

```python
import functools
import math

import jax
import jax.numpy as jnp
from jax import lax
from jax.experimental import pallas as pl
from jax.experimental.pallas import tpu as pltpu

D_MODEL = 1024
RMS_EPS = 1e-6
MLA_NOPE, MLA_ROPE, MLA_V, MLA_HEADS = 64, 32, 64, 8
MLA_Q_RANK, MLA_KV_RANK = 256, 128
ROPE_THETA = 10000.0
SB_DIM, SB_HEADS = 64, 4
CA_DIM, CA_HEADS = 64, 4
CHUNK = 64
CA_LEFT_CHUNKS = 8
REL_CLIP = 256
N_EXPERTS, TOP_K = 32, 4
D_FF = 1024
SWIGLU_LIMIT, SWIGLU_ALPHA = 7.0, 1.702

LANES = 128
HEAD_PAD = 128
A_W = MLA_HEADS * MLA_V
SB_W = SB_HEADS * SB_DIM
CA_W = CA_HEADS * CA_DIM
CA_PAD = CA_LEFT_CHUNKS * CHUNK
CA_WIN = CA_PAD + 2 * CHUNK
REL_TAB = 1024
HALF = D_MODEL // 2

TM_TOK = 256
TQ_MLA = 256
TQ_SB = 256
TK_SB = 128
TQ_CA = 2 * CHUNK
CA_BLOCKS = 8
ROUTER_TILES = 4
INPROJ_TILES = 4
TM_EXP = 512
RUN_ALIGN = 8
LOCAL_ROWS = 1280
assert LOCAL_ROWS >= TM_TOK * TOP_K + N_EXPERTS * (RUN_ALIGN - 1) and LOCAL_ROWS % LANES == 0
assert TM_EXP % TM_TOK == 0 and TQ_MLA == TM_TOK

_C_CQ, _C_CKV, _C_KPE, _C_KROT = 0, 256, 384, 512
_C_SB, _C_CA, D_IN2 = 640, 1408, 2176

NEG = -1e30
LOG2E = math.log2(math.e)
VMEM_LIMIT = 56 * 1024 * 1024


def _rms(v, g):
    return v * lax.rsqrt(jnp.mean(v * v, axis=-1, keepdims=True) + RMS_EPS) * g


def _dot(a, b):
    return jnp.dot(a, b, preferred_element_type=jnp.float32)


def _dot_nt(a, b):
    return lax.dot_general(a, b, (((1,), (1,)), ((), ())), preferred_element_type=jnp.float32)


def _bf(v):
    return v.astype(jnp.bfloat16)


def _pack_rows(v):
    bits = lax.bitcast_convert_type(v, jnp.uint32)
    return (bits[:, :HALF] >> 16) | (bits[:, HALF:] & jnp.uint32(0xFFFF0000))


def _unpack_rows(u):
    lo = lax.bitcast_convert_type(u << 16, jnp.float32)
    hi = lax.bitcast_convert_type(u & jnp.uint32(0xFFFF0000), jnp.float32)
    return _bf(lo), _bf(hi)


def _cparams(sem):
    return pltpu.CompilerParams(dimension_semantics=sem, vmem_limit_bytes=VMEM_LIMIT)


ROPE_PACK = LANES // MLA_ROPE


def _rope_kernel(pos_ref, inv_ref, cos_ref, sin_ref):
    ang = pos_ref[...].astype(jnp.float32) * inv_ref[...]
    cos_ref[...] = jnp.cos(ang)
    sin_ref[...] = jnp.sin(ang)


def _rope_tables(positions, n):
    inv = ROPE_THETA ** (-jnp.arange(0, MLA_ROPE, 2, dtype=jnp.float32) / MLA_ROPE)
    inv_row = jnp.tile(jnp.concatenate([inv, inv]), ROPE_PACK).reshape(1, LANES)
    rows = n // ROPE_PACK
    pos_packed = jnp.repeat(positions.reshape(rows, ROPE_PACK), MLA_ROPE, axis=1)
    tm = min(TM_TOK, rows)
    cos_p, sin_p = pl.pallas_call(
        _rope_kernel,
        grid=(rows // tm,),
        in_specs=[pl.BlockSpec((tm, LANES), lambda i: (i, 0)),
                  pl.BlockSpec((1, LANES), lambda i: (0, 0))],
        out_specs=[pl.BlockSpec((tm, LANES), lambda i: (i, 0))] * 2,
        out_shape=[jax.ShapeDtypeStruct((rows, LANES), jnp.float32)] * 2,
        compiler_params=_cparams(("parallel",)),
        name="rope_tables",
    )(pos_packed, inv_row)
    pad = HEAD_PAD - MLA_NOPE - MLA_ROPE
    ones = lambda width: jnp.ones((n, width), jnp.float32)
    zeros = lambda width: jnp.zeros((n, width), jnp.float32)
    cos_t = jnp.concatenate([ones(MLA_NOPE), cos_p.reshape(n, MLA_ROPE), ones(pad)], axis=1)
    sin_t = jnp.concatenate([zeros(MLA_NOPE), sin_p.reshape(n, MLA_ROPE), zeros(pad)], axis=1)
    return cos_t, sin_t


def _inproj_kernel(x_ref, g_ref, win_ref, cos_ref, sin_ref, qn_ref, wq_ref, kvn_ref, wkv_ref,
                   q_ref, k_ref, v_ref, sbq_ref, sbk_ref, sbv_ref, caq_ref, cak_ref, cav_ref):
    for j in range(INPROJ_TILES):
        _inproj_tile(j, x_ref, g_ref, win_ref, cos_ref, sin_ref, qn_ref, wq_ref, kvn_ref, wkv_ref,
                     q_ref, k_ref, v_ref, sbq_ref, sbk_ref, sbv_ref, caq_ref, cak_ref, cav_ref)


def _inproj_tile(j, x_ref, g_ref, win_ref, cos_ref, sin_ref, qn_ref, wq_ref, kvn_ref, wkv_ref,
                 q_ref, k_ref, v_ref, sbq_ref, sbk_ref, sbv_ref, caq_ref, cak_ref, cav_ref):
    rows = slice(j * TM_TOK, (j + 1) * TM_TOK)
    h = _rms(x_ref[rows, :], g_ref[...])
    proj = _dot(_bf(h), win_ref[...])
    cos128, sin128 = cos_ref[rows, :], sin_ref[rows, :]
    cos_h = jnp.concatenate([cos128] * MLA_HEADS, axis=1)
    sin_h = jnp.concatenate([sin128] * MLA_HEADS, axis=1)
    w = MLA_HEADS * HEAD_PAD

    cqn = _rms(proj[:, _C_CQ:_C_CKV], qn_ref[...])
    q2 = _dot(_bf(cqn), wq_ref[...])
    q = (q2[:, :w] * cos_h + q2[:, w:] * sin_h) * (LOG2E / math.sqrt(MLA_NOPE + MLA_ROPE))
    q_ref[j] = _bf(jnp.transpose(q))

    ckvn = _rms(proj[:, _C_CKV:_C_KPE], kvn_ref[...])
    kv2 = _dot(_bf(ckvn), wkv_ref[...])
    kpe = proj[:, _C_KPE:_C_KROT] * cos128 + proj[:, _C_KROT:_C_SB] * sin128
    k_ref[rows, :] = _bf(kv2[:, :w] + jnp.concatenate([kpe] * MLA_HEADS, axis=1))
    v_ref[j] = _bf(jnp.transpose(kv2[:, w:]))

    sb_scale = 1.0 / math.sqrt(SB_DIM)
    sbq_ref[rows, :] = _bf(proj[:, _C_SB:_C_SB + SB_W] * sb_scale)
    sbk_ref[rows, :] = _bf(proj[:, _C_SB + SB_W:_C_SB + 2 * SB_W])
    sbv_ref[rows, :] = _bf(proj[:, _C_SB + 2 * SB_W:_C_CA])
    ca_scale = LOG2E / math.sqrt(CA_DIM)
    caq_ref[rows, :] = _bf(proj[:, _C_CA:_C_CA + CA_W] * ca_scale)
    cak_ref[rows, :] = _bf(proj[:, _C_CA + CA_W:_C_CA + 2 * CA_W])
    cav_ref[rows, :] = _bf(proj[:, _C_CA + 2 * CA_W:D_IN2])


def _inproj(x, g, win2, cos_t, sin_t, qn, wq2, kvn, wkv2):
    n = x.shape[0]
    tm = TM_TOK * INPROJ_TILES
    row = lambda width: pl.BlockSpec((tm, width), lambda i: (i, 0))
    full = lambda a: pl.BlockSpec(a.shape, lambda i: (0,) * a.ndim)
    widths = [MLA_HEADS * HEAD_PAD] + [SB_W] * 3 + [CA_W] * 3
    tile_t = lambda rows: pl.BlockSpec((INPROJ_TILES, rows, TM_TOK), lambda i: (i, 0, 0))
    shape_t = lambda rows: jax.ShapeDtypeStruct((n // TM_TOK, rows, TM_TOK), jnp.bfloat16)
    return pl.pallas_call(
        _inproj_kernel,
        grid=(n // tm,),
        in_specs=[row(D_MODEL), full(g), full(win2), row(LANES), row(LANES),
                  full(qn), full(wq2), full(kvn), full(wkv2)],
        out_specs=[tile_t(MLA_HEADS * HEAD_PAD), row(widths[0]), tile_t(A_W)] + [row(wd) for wd in widths[1:]],
        out_shape=[shape_t(MLA_HEADS * HEAD_PAD), jax.ShapeDtypeStruct((n, widths[0]), jnp.bfloat16), shape_t(A_W)]
                  + [jax.ShapeDtypeStruct((n, wd), jnp.bfloat16) for wd in widths[1:]],
        compiler_params=_cparams(("parallel",)),
        name="inproj",
    )(x, g, win2, cos_t, sin_t, qn, wq2, kvn, wkv2)


def _mla_kernel(qt_ref, k_ref, vt_ref, g_ref, o_ref, m_ref, l_ref, acc_ref, s_ref):
    i = pl.program_id(1)
    tq = TQ_MLA
    n_pairs = MLA_HEADS // 2
    key_chunk = lax.broadcasted_iota(jnp.int32, (tq, tq), 0) // CHUNK
    qry_chunk = lax.broadcasted_iota(jnp.int32, (tq, tq), 1) // CHUNK
    diag_ok = key_chunk <= qry_chunk
    top = lax.broadcasted_iota(jnp.int32, (LANES, tq), 0) < MLA_V

    m_ref[...] = jnp.full(m_ref.shape, -jnp.inf, jnp.float32)
    l_ref[...] = jnp.zeros(l_ref.shape, jnp.float32)
    acc_ref[...] = jnp.zeros(acc_ref.shape, jnp.float32)

    def scores(kb, slot):
        start = pl.multiple_of(kb * tq, tq)
        for h in range(MLA_HEADS):
            qt = qt_ref[0, h * HEAD_PAD:(h + 1) * HEAD_PAD, :]
            kblk = k_ref[pl.ds(start, tq), h * HEAD_PAD:(h + 1) * HEAD_PAD]
            s_ref[slot, h] = _dot(kblk, qt)

    def absorb(kb, slot, mask):
        for pr in range(n_pairs):
            vt = vt_ref[kb, pr * LANES:(pr + 1) * LANES, :]
            zero = jnp.zeros_like(vt)
            vt_bd = jnp.concatenate([jnp.where(top, vt, zero), jnp.where(top, zero, vt)], axis=1)
            pts, alphas = [], []
            for h in (2 * pr, 2 * pr + 1):
                s = s_ref[slot, h]
                if mask is not None:
                    s = jnp.where(mask, s, -jnp.inf)
                m_old = m_ref[h]
                m_new = jnp.maximum(m_old, jnp.max(s, axis=0, keepdims=True))
                p = jnp.exp2(s - m_new)
                alpha = jnp.exp2(m_old - m_new)
                l_ref[h] = alpha * l_ref[h] + jnp.sum(p, axis=0, keepdims=True)
                m_ref[h] = m_new
                pts.append(_bf(p))
                alphas.append(alpha)
            alpha_pair = jnp.where(top, alphas[0], alphas[1])
            acc_ref[pr] = alpha_pair * acc_ref[pr] + _dot(vt_bd, jnp.concatenate(pts, axis=0))

    def body(j, carry):
        scores(2 * j + 1, 1)
        absorb(2 * j, 0, None)
        scores(2 * j + 2, 0)
        absorb(2 * j + 1, 1, None)
        return carry

    scores(0, 0)
    lax.fori_loop(0, i // 2, body, 0)

    @pl.when(i % 2 == 0)
    def _():
        absorb(i, 0, diag_ok)

    @pl.when(i % 2 == 1)
    def _():
        scores(i, 1)
        absorb(i - 1, 0, None)
        absorb(i, 1, diag_ok)

    outs = [jnp.transpose(acc_ref[pr] / jnp.where(top, l_ref[2 * pr], l_ref[2 * pr + 1])) for pr in range(n_pairs)]
    o = jnp.concatenate(outs, axis=1)
    o_ref[...] = _bf(_rms(o, g_ref[...]))


def _mla_attention(qt, k, vt, g, batch, seq):
    nq = seq // TQ_MLA
    w = MLA_HEADS * HEAD_PAD
    return pl.pallas_call(
        _mla_kernel,
        grid=(batch, nq),
        in_specs=[pl.BlockSpec((1, w, TQ_MLA), lambda b, i: (b * nq + i, 0, 0)),
                  pl.BlockSpec((seq, w), lambda b, i: (b, 0)),
                  pl.BlockSpec((nq, A_W, TQ_MLA), lambda b, i: (b, 0, 0)),
                  pl.BlockSpec((1, A_W), lambda b, i: (0, 0))],
        out_specs=pl.BlockSpec((TQ_MLA, A_W), lambda b, i: (b * nq + i, 0)),
        out_shape=jax.ShapeDtypeStruct((batch * seq, A_W), jnp.bfloat16),
        scratch_shapes=[pltpu.VMEM((MLA_HEADS, 1, TQ_MLA), jnp.float32),
                        pltpu.VMEM((MLA_HEADS, 1, TQ_MLA), jnp.float32),
                        pltpu.VMEM((MLA_HEADS // 2, LANES, TQ_MLA), jnp.float32),
                        pltpu.VMEM((2, MLA_HEADS, TQ_MLA, TQ_MLA), jnp.float32)],
        compiler_params=_cparams(("parallel", "parallel")),
        name="mla_attention",
    )(qt, k, vt, g)


def _sb_kernel(q_ref, k_ref, v_ref, g_ref, o_ref, run_ref, acc_ref, zl_ref, sums_ref):
    i = pl.program_id(1)
    tq, tk = TQ_SB, TK_SB
    n_pairs = SB_HEADS // 2
    r = lax.broadcasted_iota(jnp.int32, (2 * tq, tk), 0)
    c = lax.broadcasted_iota(jnp.int32, (2 * tq, tk), 1)
    q_off = jnp.where(r >= tq, r - tq, r)
    top_low = (r < tq) == (c < SB_DIM)
    r2 = lax.broadcasted_iota(jnp.int32, (tk, 2 * tk), 0)
    c2 = lax.broadcasted_iota(jnp.int32, (tk, 2 * tk), 1)
    sum_mat = jnp.where((c2 >= tk) | (r2 > c2), 1.0, 0.0).astype(jnp.bfloat16)

    run_ref[...] = jnp.zeros(run_ref.shape, jnp.float32)
    acc_ref[...] = jnp.zeros(acc_ref.shape, jnp.float32)

    hq = tq // 2

    def late(v):
        return jnp.concatenate([v[hq:tq], v[tq + hq:2 * tq]], axis=0)

    def step(kb_hi, masked):
        masks = []
        for d in range(2):
            kb = kb_hi - d
            half = masked and d == 0
            start = pl.multiple_of(kb * tk, tk)
            mask = (c + (kb * tk - i * tq)) < q_off if masked else None
            if half:
                mask = late(mask)
            masks.append(mask)
            for pr in range(n_pairs):
                qp = q_ref[:, pr * LANES:(pr + 1) * LANES]
                q2 = jnp.concatenate([qp, qp], axis=0)
                qm = jnp.where(top_low, q2, jnp.zeros_like(q2))
                if half:
                    qm = late(qm)
                kblk = k_ref[pl.ds(start, tk), pr * LANES:(pr + 1) * LANES]
                z = _dot_nt(qm, kblk)
                log_keep = -(jnp.maximum(z, 0.0) + jnp.log(1.0 + jnp.exp(-jnp.abs(z))))
                if masked:
                    log_keep = jnp.where(mask, log_keep, 0.0)
                hi = _bf(log_keep)
                lo = _bf(log_keep - hi.astype(jnp.float32))
                n = z.shape[0]
                sums_ref[2 * d + pr, 0:n] = _dot(hi, sum_mat) + _dot(lo, sum_mat)
                zl_ref[2 * d + pr, 0:n] = z + log_keep
        slowest = None
        for d in range(2):
            half = masked and d == 0
            start = pl.multiple_of((kb_hi - d) * tk, tk)
            for pr in range(n_pairs):
                vblk = v_ref[pl.ds(start, tk), pr * LANES:(pr + 1) * LANES]
                if half:
                    run = late(run_ref[pr])
                    a = jnp.exp(zl_ref[2 * d + pr, 0:tq] + run + sums_ref[2 * d + pr, 0:tq, 0:tk])
                    upd = _dot(_bf(jnp.where(masks[d], a, 0.0)), vblk)
                    run = run + sums_ref[2 * d + pr, 0:tq, tk:2 * tk]
                    for src, dst in ((slice(0, hq), slice(hq, tq)), (slice(hq, tq), slice(tq + hq, 2 * tq))):
                        acc_ref[pr, dst] = acc_ref[pr, dst] + upd[src]
                        run_ref[pr, dst] = run[src]
                    continue
                run = run_ref[pr]
                a = jnp.exp(zl_ref[2 * d + pr] + run + sums_ref[2 * d + pr, :, 0:tk])
                if masked:
                    a = jnp.where(masks[d], a, 0.0)
                acc_ref[pr] = acc_ref[pr] + _dot(_bf(a), vblk)
                run = run + sums_ref[2 * d + pr, :, tk:2 * tk]
                run_ref[pr] = run
                if d == 1:
                    top = jnp.max(run)
                    slowest = top if slowest is None else jnp.maximum(slowest, top)
        return slowest

    underflow = -104.0
    assert tq == 2 * tk
    first = step(2 * i + 1, True)
    lax.while_loop(lambda cr: (cr[0] >= 0) & (cr[1] > underflow),
                   lambda cr: (cr[0] - 1, step(2 * cr[0] + 1, False)),
                   (i - 1, first))

    lane = lax.broadcasted_iota(jnp.int32, (tq, LANES), 1)
    outs = [jnp.where(lane < SB_DIM, acc_ref[pr, 0:tq, :], acc_ref[pr, tq:2 * tq, :]) for pr in range(n_pairs)]
    o = jnp.concatenate(outs, axis=1)
    o_ref[...] = _bf(_rms(o, g_ref[...]))


def _sb_attention(q, k, v, g, batch, seq):
    nq = seq // TQ_SB
    return pl.pallas_call(
        _sb_kernel,
        grid=(batch, nq),
        in_specs=[pl.BlockSpec((TQ_SB, SB_W), lambda b, i: (b * nq + i, 0)),
                  pl.BlockSpec((seq, SB_W), lambda b, i: (b, 0)),
                  pl.BlockSpec((seq, SB_W), lambda b, i: (b, 0)),
                  pl.BlockSpec((1, SB_W), lambda b, i: (0, 0))],
        out_specs=pl.BlockSpec((TQ_SB, SB_W), lambda b, i: (b * nq + i, 0)),
        out_shape=jax.ShapeDtypeStruct((batch * seq, SB_W), jnp.bfloat16),
        scratch_shapes=[pltpu.VMEM((SB_HEADS // 2, 2 * TQ_SB, TK_SB), jnp.float32),
                        pltpu.VMEM((SB_HEADS // 2, 2 * TQ_SB, LANES), jnp.float32),
                        pltpu.VMEM((SB_HEADS, 2 * TQ_SB, TK_SB), jnp.float32),
                        pltpu.VMEM((SB_HEADS, 2 * TQ_SB, 2 * TK_SB), jnp.float32)],
        compiler_params=_cparams(("parallel", "parallel")),
        name="sb_attention",
    )(q, k, v, g)


def _ca_kernel(q_ref, k_ref, v_ref, tab_ref, g_ref, o_ref, bias_ref):
    t = TQ_CA

    @pl.when((pl.program_id(0) == 0) & (pl.program_id(1) == 0))
    def _():
        r = lax.broadcasted_iota(jnp.int32, (t, CA_WIN), 0)
        c = lax.broadcasted_iota(jnp.int32, (t, CA_WIN), 1)
        lo = (r // CHUNK) * CHUNK
        band = (c >= lo) & (c < lo + CA_PAD + CHUNK)
        for h in range(CA_HEADS):
            tab = jnp.broadcast_to(tab_ref[h:h + 1, :], (t, REL_TAB))
            bias = pltpu.roll(tab, REL_TAB - (REL_CLIP - 1), 1, stride=1, stride_axis=0)[:, :CA_WIN]
            bias_ref[h] = jnp.where(band, bias * LOG2E, -jnp.inf)

    for j in range(CA_BLOCKS):
        _ca_block(pl.program_id(1) * CA_BLOCKS + j, slice(j * t, (j + 1) * t), q_ref, k_ref, v_ref, g_ref, o_ref, bias_ref)


def _ca_block(i, rows, q_ref, k_ref, v_ref, g_ref, o_ref, bias_ref):
    t = TQ_CA
    start = pl.multiple_of(i * t, t)
    r2 = lax.broadcasted_iota(jnp.int32, (2 * t, LANES), 0)
    c2 = lax.broadcasted_iota(jnp.int32, (2 * t, LANES), 1)
    top_low = (r2 < t) == (c2 < CA_DIM)
    in_seq = lax.broadcasted_iota(jnp.int32, (2 * t, CA_WIN), 1) + i * t >= CA_PAD
    lane = lax.broadcasted_iota(jnp.int32, (t, LANES), 1)

    outs = []
    for pair in range(CA_HEADS // 2):
        qp = q_ref[rows, pair * LANES:(pair + 1) * LANES]
        q2 = jnp.concatenate([qp, qp], axis=0)
        qm = jnp.where(top_low, q2, jnp.zeros_like(q2))
        kwin = k_ref[pl.ds(start, CA_WIN), pair * LANES:(pair + 1) * LANES]
        vwin = v_ref[pl.ds(start, CA_WIN), pair * LANES:(pair + 1) * LANES]
        bias = jnp.concatenate([bias_ref[2 * pair], bias_ref[2 * pair + 1]], axis=0)
        s = jnp.where(in_seq, _dot_nt(qm, kwin) + bias, -jnp.inf)
        m = jnp.max(s, axis=-1, keepdims=True)
        p = jnp.exp2(s - m)
        l = jnp.sum(p, axis=-1, keepdims=True)
        o2 = _dot(_bf(p), vwin) / l
        outs.append(jnp.where(lane < CA_DIM, o2[0:t], o2[t:2 * t]))
    o = jnp.concatenate(outs, axis=1)
    o_ref[rows, :] = _bf(_rms(o, g_ref[...]))


def _ca_attention(q, kpad, vpad, tab, g, batch, seq):
    tq = TQ_CA * CA_BLOCKS
    nq = seq // tq
    return pl.pallas_call(
        _ca_kernel,
        grid=(batch, nq),
        in_specs=[pl.BlockSpec((tq, CA_W), lambda b, i: (b * nq + i, 0)),
                  pl.BlockSpec((seq + CA_PAD, CA_W), lambda b, i: (b, 0)),
                  pl.BlockSpec((seq + CA_PAD, CA_W), lambda b, i: (b, 0)),
                  pl.BlockSpec((CA_HEADS, REL_TAB), lambda b, i: (0, 0)),
                  pl.BlockSpec((1, CA_W), lambda b, i: (0, 0))],
        out_specs=pl.BlockSpec((tq, CA_W), lambda b, i: (b * nq + i, 0)),
        out_shape=jax.ShapeDtypeStruct((batch * seq, CA_W), jnp.bfloat16),
        scratch_shapes=[pltpu.VMEM((CA_HEADS, TQ_CA, CA_WIN), jnp.float32)],
        compiler_params=_cparams(("arbitrary", "arbitrary")),
        name="ca_attention",
    )(q, kpad, vpad, tab, g)


def _outproj_router_kernel(ma_ref, mb_ref, mc_ref, wo_ref, x_ref, g_ref, wr2_ref, br_ref,
                           xn_ref, h_ref, lpos_ref, meta_ref, runs_ref, tot_ref, base_ref):
    @pl.when(pl.program_id(0) == 0)
    def _():
        base_ref[...] = jnp.zeros_like(base_ref)

    base = base_ref[...]
    for j in range(ROUTER_TILES):
        base = _route_tile(j, base, ma_ref, mb_ref, mc_ref, wo_ref, x_ref, g_ref, wr2_ref, br_ref,
                           xn_ref, h_ref, lpos_ref, meta_ref, runs_ref)
    base_ref[...] = base
    tot_ref[...] = base.astype(jnp.int32)


def _route_tile(j, base, ma_ref, mb_ref, mc_ref, wo_ref, x_ref, g_ref, wr2_ref, br_ref,
                xn_ref, h_ref, lpos_ref, meta_ref, runs_ref):
    tm = TM_TOK
    rows = slice(j * tm, (j + 1) * tm)
    attn = (_dot(ma_ref[rows, :], wo_ref[0:A_W, :]) + _dot(mb_ref[rows, :], wo_ref[A_W:A_W + SB_W, :])
            + _dot(mc_ref[rows, :], wo_ref[A_W + SB_W:, :]))
    xn = x_ref[rows, :] + attn
    xn_ref[rows, :] = xn
    h = _rms(xn, g_ref[...])

    h_hi = _bf(h)
    h_ref[rows, :] = h_hi
    h_lo = _bf(h - h_hi.astype(jnp.float32))
    both = _dot(h_hi, wr2_ref[...])
    logits = (both[:, :LANES] + both[:, LANES:] + _dot(h_lo, wr2_ref[:, 0:LANES])
              + br_ref[...])
    lane = lax.broadcasted_iota(jnp.int32, (tm, LANES), 1)
    lane_f = lane.astype(jnp.float32)

    work = logits
    vals, idxs, hots = [], [], []
    for _ in range(TOP_K):
        mx = jnp.max(work, axis=-1, keepdims=True)
        ix = jnp.min(jnp.where(work == mx, lane_f, float(LANES)), axis=-1, keepdims=True)
        hot = lane_f == ix
        work = jnp.where(hot, -jnp.inf, work)
        vals.append(mx)
        idxs.append(ix)
        hots.append(hot)
    exps = [jnp.exp(v - vals[0]) for v in vals]
    denom = exps[0] + exps[1] + exps[2] + exps[3]
    gates = [e / denom for e in exps]

    sel = jnp.zeros((tm, LANES), jnp.float32)
    for hot in hots:
        sel = sel + jnp.where(hot, 1.0, 0.0)
    r = lax.broadcasted_iota(jnp.int32, (tm, tm), 0)
    c = lax.broadcasted_iota(jnp.int32, (tm, tm), 1)
    before = jnp.where(c < r, 1.0, 0.0).astype(jnp.bfloat16)
    rank_in_tile = _dot(before, _bf(sel))

    cnt = jnp.sum(sel, axis=0, keepdims=True)
    cnt_al = jnp.ceil(cnt * (1.0 / RUN_ALIGN)) * RUN_ALIGN
    rl = lax.broadcasted_iota(jnp.int32, (LANES, LANES), 0)
    cl = lax.broadcasted_iota(jnp.int32, (LANES, LANES), 1)
    earlier = jnp.where(rl < cl, 1.0, 0.0).astype(jnp.bfloat16)
    loff = _dot(_bf(jnp.broadcast_to(cnt_al, (8, LANES))), earlier)[0:1, :]
    sub = lax.broadcasted_iota(jnp.int32, (8, LANES), 0)
    has_long = jnp.where(jnp.max(cnt_al, axis=-1, keepdims=True) >= LONG_RUN, 1.0, 0.0)
    runs = jnp.where(sub == 0, loff, jnp.where(sub == 1, base, jnp.where(sub == 2, cnt_al,
                     jnp.where(sub == 3, has_long, 0.0))))
    runs_ref[8 * j:8 * j + 8, :] = runs.astype(jnp.int32)

    lpos_dense = loff + rank_in_tile
    meta = jnp.zeros((tm, LANES), jnp.float32)
    for kk in range(TOP_K):
        lpos = jnp.sum(jnp.where(hots[kk], lpos_dense, 0.0), axis=-1, keepdims=True)
        meta = meta + jnp.where(lane == kk, gates[kk], 0.0) + jnp.where(lane == TOP_K + kk, lpos, 0.0)
    meta_ref[rows, :] = meta
    lpos_ref[:, rows] = jnp.transpose(meta)[TOP_K:TOP_K + 8, :].astype(jnp.int32)
    return base + cnt_al


def _outproj_router(ma, mb, mc, wo, x, g, wr2, br):
    n = x.shape[0]
    tm = TM_TOK * ROUTER_TILES
    row = lambda width: pl.BlockSpec((tm, width), lambda i: (i, 0))
    full = lambda a: pl.BlockSpec(a.shape, lambda i: (0,) * a.ndim)
    return pl.pallas_call(
        _outproj_router_kernel,
        grid=(n // tm,),
        in_specs=[row(A_W), row(SB_W), row(CA_W), full(wo), row(D_MODEL), full(g), full(wr2), full(br)],
        out_specs=[row(D_MODEL), row(D_MODEL), pl.BlockSpec((8, tm), lambda i: (0, i)),
                   row(LANES), pl.BlockSpec((8 * ROUTER_TILES, LANES), lambda i: (i, 0)),
                   pl.BlockSpec((1, LANES), lambda i: (0, 0))],
        out_shape=[jax.ShapeDtypeStruct((n, D_MODEL), jnp.float32),
                   jax.ShapeDtypeStruct((n, D_MODEL), jnp.bfloat16),
                   jax.ShapeDtypeStruct((8, n), jnp.int32),
                   jax.ShapeDtypeStruct((n, LANES), jnp.float32),
                   jax.ShapeDtypeStruct((8 * (n // TM_TOK), LANES), jnp.int32),
                   jax.ShapeDtypeStruct((1, LANES), jnp.int32)],
        scratch_shapes=[pltpu.VMEM((1, LANES), jnp.float32)],
        compiler_params=_cparams(("arbitrary",)),
        name="outproj_router",
    )(ma, mb, mc, wo, x, g, wr2, br)


_RUN_CHUNKS = tuple(TM_TOK >> s for s in range(TM_TOK.bit_length()) if (TM_TOK >> s) >= RUN_ALIGN)


LONG_RUN = 128


def _for_each_chunk(length, fn, sizes=_RUN_CHUNKS):
    for size in sizes:
        off = length & (~(2 * size - 1))

        @pl.when((length & size) != 0)
        def _(off=off, size=size):
            fn(off, size)


_WAIT_CHUNKS = tuple(1 << b for b in range((LOCAL_ROWS).bit_length() - 1, RUN_ALIGN.bit_length() - 2, -1))


def _wait_tile_runs(runs_ref, make_copy):
    total = runs_ref[0, N_EXPERTS - 1] + runs_ref[2, N_EXPERTS - 1]
    for size in _WAIT_CHUNKS:
        @pl.when((total & size) != 0)
        def _(size=size):
            make_copy(size).wait()


def _tile_runs(runs_ref, pstart_ref, fn):
    def experts(sizes, unroll):
        def body(e, _):
            local, glob, length = runs_ref[0, e], pstart_ref[e] + runs_ref[1, e], runs_ref[2, e]
            _for_each_chunk(length, lambda off, size: fn(pl.multiple_of(local + off, RUN_ALIGN),
                                                         pl.multiple_of(glob + off, RUN_ALIGN), size), sizes)
            return 0

        lax.fori_loop(0, N_EXPERTS, body, 0, unroll=unroll)

    experts(tuple(sz for sz in _RUN_CHUNKS if sz < LONG_RUN), 4)

    @pl.when(runs_ref[3, 0] != 0)
    def _():
        experts(tuple(sz for sz in _RUN_CHUNKS if sz >= LONG_RUN), 1)


def _dispatch_kernel(pstart_ref, tot_ref, pad_ref, runs_ref, prev_runs_ref, lpos_ref, h_ref, xs_ref,
                     loc, zbuf, sem, fill_sem):
    i = pl.program_id(0)
    slot = i % 2

    def fill(start_or_wait):
        def body(e, _):
            first = pstart_ref[e] + tot_ref[e]
            _for_each_chunk(pad_ref[e] - tot_ref[e], lambda off, size: start_or_wait(pltpu.make_async_copy(
                zbuf.at[pl.ds(0, size)], xs_ref.at[pl.ds(pl.multiple_of(first + off, RUN_ALIGN), size)], fill_sem)))
            return 0

        lax.fori_loop(0, N_EXPERTS, body, 0)

        used = pstart_ref[N_EXPERTS - 1] + pad_ref[N_EXPERTS - 1]

        def tail(b, _):
            start_or_wait(pltpu.make_async_copy(
                zbuf, xs_ref.at[pl.ds(pl.multiple_of(b * TM_TOK, TM_TOK), TM_TOK)], fill_sem))
            return 0

        lax.fori_loop(used // TM_TOK, xs_ref.shape[0] // TM_TOK, tail, 0)

    @pl.when(i == 0)
    def _():
        zbuf[...] = jnp.zeros_like(zbuf)
        fill(lambda c: c.start())
        fill(lambda c: c.wait())

    r = lax.broadcasted_iota(jnp.int32, (LOCAL_ROWS, TM_TOK), 0)
    hit = r == lpos_ref[0:1, :]
    for kk in range(1, TOP_K):
        hit = hit | (r == lpos_ref[kk:kk + 1, :])
    loc[slot] = _pack_rows(_dot(jnp.where(hit, 1.0, 0.0).astype(jnp.bfloat16), h_ref[...]))

    def copy(buf, local, glob, size):
        return pltpu.make_async_copy(loc.at[buf, pl.ds(local, size)], xs_ref.at[pl.ds(glob, size)], sem.at[buf])

    _tile_runs(runs_ref, pstart_ref, lambda l, g, s: copy(slot, l, g, s).start())

    @pl.when(i > 0)
    def _():
        _wait_tile_runs(prev_runs_ref, lambda size: copy(1 - slot, 0, 0, size))

    @pl.when(i == pl.num_programs(0) - 1)
    def _():
        _wait_tile_runs(runs_ref, lambda size: copy(slot, 0, 0, size))


def _dispatch(pstart, totals, padded, runs, lpos, h, n_rows):
    n = h.shape[0]
    tm = TM_TOK
    grid_spec = pltpu.PrefetchScalarGridSpec(
        num_scalar_prefetch=3,
        grid=(n // tm,),
        in_specs=[pl.BlockSpec((8, LANES), lambda i, *_: (i, 0), memory_space=pltpu.SMEM),
                  pl.BlockSpec((8, LANES), lambda i, *_: (jnp.maximum(i - 1, 0), 0), memory_space=pltpu.SMEM),
                  pl.BlockSpec((8, tm), lambda i, *_: (0, i)),
                  pl.BlockSpec((tm, D_MODEL), lambda i, *_: (i, 0))],
        out_specs=pl.BlockSpec(memory_space=pl.ANY),
        scratch_shapes=[pltpu.VMEM((2, LOCAL_ROWS, HALF), jnp.uint32),
                        pltpu.VMEM((TM_TOK, HALF), jnp.uint32),
                        pltpu.SemaphoreType.DMA((2,)),
                        pltpu.SemaphoreType.DMA],
    )
    return pl.pallas_call(
        _dispatch_kernel,
        grid_spec=grid_spec,
        out_shape=jax.ShapeDtypeStruct((n_rows, HALF), jnp.uint32),
        compiler_params=_cparams(("arbitrary",)),
        name="dispatch",
    )(pstart, totals, padded, runs, runs, lpos, h)


def _expert_kernel(be_ref, bfirst_ref, bvalid_ref, xs_ref, wgu_ref, bgu_ref, wdn_ref, bdn_ref,
                   ys_ref, wgu_bf, wdn_bf):
    b = pl.program_id(0)

    @pl.when(bfirst_ref[b] == 1)
    def _():
        wgu_bf[...] = _bf(wgu_ref[0])
        wdn_bf[...] = _bf(wdn_ref[0])

    @pl.when(bvalid_ref[b] == 1)
    def _():
        xb = jnp.concatenate(_unpack_rows(xs_ref[...]), axis=1)
        gu = _dot(xb, wgu_bf[...]) + bgu_ref[0]
        gte = jnp.minimum(gu[:, :D_FF], SWIGLU_LIMIT)
        up = jnp.clip(gu[:, D_FF:], -SWIGLU_LIMIT, SWIGLU_LIMIT)
        act = (up + 1.0) * (gte * (1.0 / (1.0 + jnp.exp(-SWIGLU_ALPHA * gte))))
        y = _dot(_bf(act), wdn_bf[...]) + bdn_ref[0]
        ys_ref[...] = _pack_rows(_bf(y).astype(jnp.float32))

    @pl.when(bvalid_ref[b] == 0)
    def _():
        ys_ref[...] = jnp.zeros_like(ys_ref)


def _expert_ffn(blk_e, blk_first, blk_valid, xs, wgu, bgu, wdn, bdn):
    n_rows = xs.shape[0]
    tm = TM_EXP
    grid_spec = pltpu.PrefetchScalarGridSpec(
        num_scalar_prefetch=3,
        grid=(n_rows // tm,),
        in_specs=[pl.BlockSpec((tm, HALF), lambda b, e, f, v: (b, 0)),
                  pl.BlockSpec((1, D_MODEL, 2 * D_FF), lambda b, e, f, v: (e[b], 0, 0)),
                  pl.BlockSpec((1, 1, 2 * D_FF), lambda b, e, f, v: (e[b], 0, 0)),
                  pl.BlockSpec((1, D_FF, D_MODEL), lambda b, e, f, v: (e[b], 0, 0)),
                  pl.BlockSpec((1, 1, D_MODEL), lambda b, e, f, v: (e[b], 0, 0))],
        out_specs=pl.BlockSpec((tm, HALF), lambda b, e, f, v: (b, 0)),
        scratch_shapes=[pltpu.VMEM((D_MODEL, 2 * D_FF), jnp.bfloat16),
                        pltpu.VMEM((D_FF, D_MODEL), jnp.bfloat16)],
    )
    return pl.pallas_call(
        _expert_kernel,
        grid_spec=grid_spec,
        out_shape=jax.ShapeDtypeStruct((n_rows, HALF), jnp.uint32),
        compiler_params=_cparams(("arbitrary",)),
        name="expert_ffn",
    )(blk_e, blk_first, blk_valid, xs, wgu, bgu, wdn, bdn)


def _combine_kernel(final, pstart_ref, runs_ref, next_runs_ref, meta_ref, x_ref, ys_ref, gfin_ref, o_ref, loc, sem):
    i = pl.program_id(0)
    slot = i % 2

    def copy(buf, local, glob, size):
        return pltpu.make_async_copy(ys_ref.at[pl.ds(glob, size)], loc.at[buf, pl.ds(local, size)], sem.at[buf])

    @pl.when(i == 0)
    def _():
        loc[...] = jnp.zeros_like(loc)
        _tile_runs(runs_ref, pstart_ref, lambda l, g, s: copy(slot, l, g, s).start())

    @pl.when(i + 1 < pl.num_programs(0))
    def _():
        _tile_runs(next_runs_ref, pstart_ref, lambda l, g, s: copy(1 - slot, l, g, s).start())

    _wait_tile_runs(runs_ref, lambda size: copy(slot, 0, 0, size))

    meta = meta_ref[...]
    col = lax.broadcasted_iota(jnp.int32, (TM_TOK, LOCAL_ROWS), 1).astype(jnp.float32)
    wts = jnp.zeros((TM_TOK, LOCAL_ROWS), jnp.float32)
    for kk in range(TOP_K):
        wts = wts + jnp.where(col == meta[:, TOP_K + kk:TOP_K + kk + 1], meta[:, kk:kk + 1], 0.0)
    w_hi = _bf(wts)
    w_lo = _bf(wts - w_hi.astype(jnp.float32))
    w2 = jnp.concatenate([w_hi, w_lo], axis=0)
    y_lo, y_hi = _unpack_rows(loc[slot])
    r_lo, r_hi = _dot(w2, y_lo), _dot(w2, y_hi)
    out = x_ref[...] + jnp.concatenate([r_lo[:TM_TOK] + r_lo[TM_TOK:], r_hi[:TM_TOK] + r_hi[TM_TOK:]], axis=1)
    if final:
        out = _rms(out, gfin_ref[...])
    o_ref[...] = out


def _combine(pstart, runs, meta, x, ys, gfin, final):
    n = x.shape[0]
    tm = TM_TOK
    n_tiles = n // tm
    grid_spec = pltpu.PrefetchScalarGridSpec(
        num_scalar_prefetch=1,
        grid=(n_tiles,),
        in_specs=[pl.BlockSpec((8, LANES), lambda i, ps: (i, 0), memory_space=pltpu.SMEM),
                  pl.BlockSpec((8, LANES), lambda i, ps: (jnp.minimum(i + 1, n_tiles - 1), 0), memory_space=pltpu.SMEM),
                  pl.BlockSpec((tm, LANES), lambda i, ps: (i, 0)),
                  pl.BlockSpec((tm, D_MODEL), lambda i, ps: (i, 0)),
                  pl.BlockSpec(memory_space=pl.ANY),
                  pl.BlockSpec((1, D_MODEL), lambda i, ps: (0, 0))],
        out_specs=pl.BlockSpec((tm, D_MODEL), lambda i, ps: (i, 0)),
        scratch_shapes=[pltpu.VMEM((2, LOCAL_ROWS, HALF), jnp.uint32), pltpu.SemaphoreType.DMA((2,))],
    )
    return pl.pallas_call(
        functools.partial(_combine_kernel, final),
        grid_spec=grid_spec,
        out_shape=jax.ShapeDtypeStruct((n, D_MODEL), jnp.float32),
        compiler_params=_cparams(("arbitrary",)),
        name="combine_final" if final else "combine",
    )(pstart, runs, runs, meta, x, ys, gfin)


def _pad_heads(w, parts):
    rows = w.shape[0]
    per = w.shape[1] // MLA_HEADS
    w3 = w.reshape(rows, MLA_HEADS, per)
    cols = [jnp.zeros((rows, MLA_HEADS, b - a), w.dtype) if sign == 0 else sign * w3[:, :, a:b]
            for a, b, sign in parts]
    used = sum(b - a for a, b, _ in parts)
    cols.append(jnp.zeros((rows, MLA_HEADS, HEAD_PAD - used), w.dtype))
    return jnp.concatenate(cols, axis=2).reshape(rows, MLA_HEADS * HEAD_PAD)


def _layer_weights(w_in, w_uq, w_ukv, w_router, b_router, rel_bias):
    half = MLA_ROPE // 2
    cq, ckv, kpe, sb, ca = (w_in[:, 0:256], w_in[:, 256:384], w_in[:, 384:416],
                            w_in[:, 416:1184], w_in[:, 1184:1952])
    z = lambda width: jnp.zeros((D_MODEL, width), w_in.dtype)
    kpe_pad = jnp.concatenate([z(MLA_NOPE), kpe, z(HEAD_PAD - MLA_NOPE - MLA_ROPE)], axis=1)
    kpe_rot = jnp.concatenate([z(MLA_NOPE), -kpe[:, half:], kpe[:, :half],
                               z(HEAD_PAD - MLA_NOPE - MLA_ROPE)], axis=1)
    win2 = jnp.concatenate([cq, ckv, kpe_pad, kpe_rot, sb, ca], axis=1).astype(jnp.bfloat16)

    d = MLA_NOPE + MLA_ROPE
    wq_full = _pad_heads(w_uq, [(0, d, 1)])
    wq_rot = _pad_heads(w_uq, [(0, MLA_NOPE, 0), (MLA_NOPE + half, d, -1), (MLA_NOPE, MLA_NOPE + half, 1)])
    wq2 = jnp.concatenate([wq_full, wq_rot], axis=1).astype(jnp.bfloat16)

    wk = _pad_heads(w_ukv, [(0, MLA_NOPE, 1)])
    wv = w_ukv.reshape(MLA_KV_RANK, MLA_HEADS, MLA_NOPE + MLA_V)[:, :, MLA_NOPE:].reshape(MLA_KV_RANK, A_W)
    wkv2 = jnp.concatenate([wk, wv], axis=1).astype(jnp.bfloat16)

    wr = jnp.pad(w_router, ((0, 0), (0, LANES - N_EXPERTS)))
    wr_hi = wr.astype(jnp.bfloat16)
    wr_lo = (wr - wr_hi.astype(jnp.float32)).astype(jnp.bfloat16)
    wr2 = jnp.concatenate([wr_hi, wr_lo], axis=1)
    br = jnp.pad(b_router, (0, LANES - N_EXPERTS), constant_values=NEG).reshape(1, LANES)

    ext = jnp.concatenate([rel_bias, jnp.broadcast_to(rel_bias[:, -1:], (CA_HEADS, REL_TAB - 2 * REL_CLIP - 1))], axis=1)
    tab = ext[:, ::-1]
    return win2, wq2, wkv2, wr2, br, tab


def kernel(x, positions, attn_norm, w_in, q_norm, w_uq, kv_norm, w_ukv, rel_bias, mix_norm,
           w_o, ffn_norm, w_router, b_router, w_gate_up, b_gate_up, w_down, b_down, final_norm):
    batch, seq, _ = x.shape
    n = batch * seq
    depth = w_in.shape[0]
    xf = x.reshape(n, D_MODEL)

    cos_t, sin_t = _rope_tables(positions, n)

    n_rows = n * TOP_K + (n // TM_TOK) * N_EXPERTS * (RUN_ALIGN - 1) + N_EXPERTS * TM_EXP
    n_rows = -(-n_rows // TM_EXP) * TM_EXP
    n_blk = n_rows // TM_EXP
    row2 = lambda v: v.reshape(1, -1)
    wgu_all = w_gate_up.reshape(depth * N_EXPERTS, D_MODEL, 2 * D_FF)
    bgu_all = b_gate_up.reshape(depth * N_EXPERTS, 1, 2 * D_FF)
    wdn_all = w_down.reshape(depth * N_EXPERTS, D_FF, D_MODEL)
    bdn_all = b_down.reshape(depth * N_EXPERTS, 1, D_MODEL)

    for l in range(depth):
        win2, wq2, wkv2, wr2, br, tab = _layer_weights(
            w_in[l], w_uq[l], w_ukv[l], w_router[l], b_router[l], rel_bias[l])
        qt, k, vt, sbq, sbk, sbv, caq, cak, cav = _inproj(
            xf, row2(attn_norm[l]), win2, cos_t, sin_t, row2(q_norm[l]), wq2, row2(kv_norm[l]), wkv2)

        g = mix_norm[l]
        ma = _mla_attention(qt, k, vt, row2(g[:A_W]), batch, seq)
        mb = _sb_attention(sbq, sbk, sbv, row2(g[A_W:A_W + SB_W]), batch, seq)
        pad = lambda a: jnp.pad(a.reshape(batch, seq, CA_W), ((0, 0), (CA_PAD, 0), (0, 0))).reshape(-1, CA_W)
        mc = _ca_attention(caq, pad(cak), pad(cav), tab, row2(g[A_W + SB_W:]), batch, seq)

        xn, h, lpos, meta, runs, totals = _outproj_router(
            ma, mb, mc, w_o[l].astype(jnp.bfloat16), xf, row2(ffn_norm[l]), wr2, br)

        sizes = totals[0, :N_EXPERTS]
        padded = (sizes + TM_EXP - 1) // TM_EXP * TM_EXP
        p_ends = jnp.cumsum(padded)
        p_starts = (p_ends - padded).astype(jnp.int32)
        blk_start = jnp.arange(n_blk, dtype=jnp.int32) * TM_EXP
        blk_valid = (blk_start < p_ends[-1]).astype(jnp.int32)
        last_row = jnp.minimum(blk_start, p_ends[-1] - 1)
        blk_e = jnp.minimum(jnp.sum((last_row[:, None] >= p_ends[None, :]).astype(jnp.int32), axis=1),
                            N_EXPERTS - 1)
        blk_first = jnp.concatenate([jnp.ones((1,), jnp.int32), (blk_e[1:] != blk_e[:-1]).astype(jnp.int32)])

        xs = _dispatch(p_starts, sizes, padded.astype(jnp.int32), runs, lpos, h, n_rows)
        ys = _expert_ffn(blk_e + l * N_EXPERTS, blk_first, blk_valid, xs, wgu_all, bgu_all, wdn_all, bdn_all)
        xf = _combine(p_starts, runs, meta, xn, ys, row2(final_norm), final=(l == depth - 1))

    return xf.reshape(batch, seq, D_MODEL)
```

```python
import functools
import math

import jax
import jax.numpy as jnp
from jax import lax
from jax.experimental import pallas as pl
from jax.experimental.pallas import tpu as pltpu

D_MODEL = 1024
RMS_EPS = 1e-6
MLA_NOPE, MLA_ROPE, MLA_V, MLA_HEADS = 64, 32, 64, 8
MLA_Q_RANK, MLA_KV_RANK = 256, 128
ROPE_THETA = 10000.0
SB_DIM, SB_HEADS = 64, 4
CA_DIM, CA_HEADS = 64, 4
CHUNK = 64
CA_LEFT_CHUNKS = 8
REL_CLIP = 256
N_EXPERTS, TOP_K = 32, 4
D_FF = 1024
SWIGLU_LIMIT, SWIGLU_ALPHA = 7.0, 1.702

LANES = 128
HEAD_PAD = 128
A_W = MLA_HEADS * MLA_V
SB_W = SB_HEADS * SB_DIM
CA_W = CA_HEADS * CA_DIM
CA_PAD = CA_LEFT_CHUNKS * CHUNK
CA_WIN = CA_PAD + 2 * CHUNK
REL_TAB = 1024
HALF = D_MODEL // 2

TM_TOK = 256
TQ_MLA = 256
TQ_SB = 256
TK_SB = 128
TQ_CA = 2 * CHUNK
CA_BLOCKS = 4
ROUTER_TILES = 2
INPROJ_TILES = 4
TM_EXP = 512
RUN_ALIGN = 8
LOCAL_ROWS = 1280
assert LOCAL_ROWS >= TM_TOK * TOP_K + N_EXPERTS * (RUN_ALIGN - 1) and LOCAL_ROWS % LANES == 0
assert TM_EXP % TM_TOK == 0 and TQ_MLA == TM_TOK

_C_CQ, _C_CKV, _C_KPE, _C_KROT = 0, 256, 384, 512
_C_SB, _C_CA, D_IN2 = 640, 1408, 2176

NEG = -1e30
LOG2E = math.log2(math.e)
VMEM_LIMIT = 56 * 1024 * 1024


def _rms(v, g):
    return v * lax.rsqrt(jnp.mean(v * v, axis=-1, keepdims=True) + RMS_EPS) * g


def _dot(a, b):
    return jnp.dot(a, b, preferred_element_type=jnp.float32)


def _dot_nt(a, b):
    return lax.dot_general(a, b, (((1,), (1,)), ((), ())), preferred_element_type=jnp.float32)


def _bf(v):
    return v.astype(jnp.bfloat16)


def _pack_rows(v):
    bits = lax.bitcast_convert_type(v, jnp.uint32)
    return (bits[:, :HALF] >> 16) | (bits[:, HALF:] & jnp.uint32(0xFFFF0000))


def _unpack_rows(u):
    lo = lax.bitcast_convert_type(u << 16, jnp.float32)
    hi = lax.bitcast_convert_type(u & jnp.uint32(0xFFFF0000), jnp.float32)
    return _bf(lo), _bf(hi)


def _cparams(sem):
    return pltpu.CompilerParams(dimension_semantics=sem, vmem_limit_bytes=VMEM_LIMIT)


ROPE_PACK = LANES // MLA_ROPE


def _rope_kernel(pos_ref, inv_ref, cos_ref, sin_ref):
    ang = pos_ref[...].astype(jnp.float32) * inv_ref[...]
    cos_ref[...] = jnp.cos(ang)
    sin_ref[...] = jnp.sin(ang)


def _rope_tables(positions, n):
    inv = ROPE_THETA ** (-jnp.arange(0, MLA_ROPE, 2, dtype=jnp.float32) / MLA_ROPE)
    inv_row = jnp.tile(jnp.concatenate([inv, inv]), ROPE_PACK).reshape(1, LANES)
    rows = n // ROPE_PACK
    pos_packed = jnp.repeat(positions.reshape(rows, ROPE_PACK), MLA_ROPE, axis=1)
    tm = min(TM_TOK, rows)
    cos_p, sin_p = pl.pallas_call(
        _rope_kernel,
        grid=(rows // tm,),
        in_specs=[pl.BlockSpec((tm, LANES), lambda i: (i, 0)),
                  pl.BlockSpec((1, LANES), lambda i: (0, 0))],
        out_specs=[pl.BlockSpec((tm, LANES), lambda i: (i, 0))] * 2,
        out_shape=[jax.ShapeDtypeStruct((rows, LANES), jnp.float32)] * 2,
        compiler_params=_cparams(("parallel",)),
        name="rope_tables",
    )(pos_packed, inv_row)
    pad = HEAD_PAD - MLA_NOPE - MLA_ROPE
    ones = lambda width: jnp.ones((n, width), jnp.float32)
    zeros = lambda width: jnp.zeros((n, width), jnp.float32)
    cos_t = jnp.concatenate([ones(MLA_NOPE), cos_p.reshape(n, MLA_ROPE), ones(pad)], axis=1)
    sin_t = jnp.concatenate([zeros(MLA_NOPE), sin_p.reshape(n, MLA_ROPE), zeros(pad)], axis=1)
    return cos_t, sin_t


def _inproj_kernel(x_ref, g_ref, win_ref, cos_ref, sin_ref, qn_ref, wq_ref, kvn_ref, wkv_ref,
                   q_ref, k_ref, v_ref, sbq_ref, sbk_ref, sbv_ref, caq_ref, cak_ref, cav_ref):
    for j in range(INPROJ_TILES):
        _inproj_tile(j, x_ref, g_ref, win_ref, cos_ref, sin_ref, qn_ref, wq_ref, kvn_ref, wkv_ref,
                     q_ref, k_ref, v_ref, sbq_ref, sbk_ref, sbv_ref, caq_ref, cak_ref, cav_ref)


def _inproj_tile(j, x_ref, g_ref, win_ref, cos_ref, sin_ref, qn_ref, wq_ref, kvn_ref, wkv_ref,
                 q_ref, k_ref, v_ref, sbq_ref, sbk_ref, sbv_ref, caq_ref, cak_ref, cav_ref):
    rows = slice(j * TM_TOK, (j + 1) * TM_TOK)
    h = _rms(x_ref[rows, :], g_ref[...])
    proj = _dot(_bf(h), win_ref[...])
    cos128, sin128 = cos_ref[rows, :], sin_ref[rows, :]
    cos_h = jnp.concatenate([cos128] * MLA_HEADS, axis=1)
    sin_h = jnp.concatenate([sin128] * MLA_HEADS, axis=1)
    w = MLA_HEADS * HEAD_PAD

    cqn = _rms(proj[:, _C_CQ:_C_CKV], qn_ref[...])
    q2 = _dot(_bf(cqn), wq_ref[...])
    q = (q2[:, :w] * cos_h + q2[:, w:] * sin_h) * (LOG2E / math.sqrt(MLA_NOPE + MLA_ROPE))
    q_ref[j] = _bf(jnp.transpose(q))

    ckvn = _rms(proj[:, _C_CKV:_C_KPE], kvn_ref[...])
    kv2 = _dot(_bf(ckvn), wkv_ref[...])
    kpe = proj[:, _C_KPE:_C_KROT] * cos128 + proj[:, _C_KROT:_C_SB] * sin128
    k_ref[rows, :] = _bf(kv2[:, :w] + jnp.concatenate([kpe] * MLA_HEADS, axis=1))
    v_ref[j] = _bf(jnp.transpose(kv2[:, w:]))

    sb_scale = 1.0 / math.sqrt(SB_DIM)
    sbq_ref[rows, :] = _bf(proj[:, _C_SB:_C_SB + SB_W] * sb_scale)
    sbk_ref[rows, :] = _bf(proj[:, _C_SB + SB_W:_C_SB + 2 * SB_W])
    sbv_ref[rows, :] = _bf(proj[:, _C_SB + 2 * SB_W:_C_CA])
    ca_scale = LOG2E / math.sqrt(CA_DIM)
    caq_ref[rows, :] = _bf(proj[:, _C_CA:_C_CA + CA_W] * ca_scale)
    cak_ref[rows, :] = _bf(proj[:, _C_CA + CA_W:_C_CA + 2 * CA_W])
    cav_ref[rows, :] = _bf(proj[:, _C_CA + 2 * CA_W:D_IN2])


def _inproj(x, g, win2, cos_t, sin_t, qn, wq2, kvn, wkv2):
    n = x.shape[0]
    tm = TM_TOK * INPROJ_TILES
    row = lambda width: pl.BlockSpec((tm, width), lambda i: (i, 0))
    full = lambda a: pl.BlockSpec(a.shape, lambda i: (0,) * a.ndim)
    widths = [MLA_HEADS * HEAD_PAD] + [SB_W] * 3 + [CA_W] * 3
    tile_t = lambda rows: pl.BlockSpec((INPROJ_TILES, rows, TM_TOK), lambda i: (i, 0, 0))
    shape_t = lambda rows: jax.ShapeDtypeStruct((n // TM_TOK, rows, TM_TOK), jnp.bfloat16)
    return pl.pallas_call(
        _inproj_kernel,
        grid=(n // tm,),
        in_specs=[row(D_MODEL), full(g), full(win2), row(LANES), row(LANES),
                  full(qn), full(wq2), full(kvn), full(wkv2)],
        out_specs=[tile_t(MLA_HEADS * HEAD_PAD), row(widths[0]), tile_t(A_W)] + [row(wd) for wd in widths[1:]],
        out_shape=[shape_t(MLA_HEADS * HEAD_PAD), jax.ShapeDtypeStruct((n, widths[0]), jnp.bfloat16), shape_t(A_W)]
                  + [jax.ShapeDtypeStruct((n, wd), jnp.bfloat16) for wd in widths[1:]],
        compiler_params=_cparams(("parallel",)),
        name="inproj",
    )(x, g, win2, cos_t, sin_t, qn, wq2, kvn, wkv2)


def _mla_kernel(qt_ref, k_ref, vt_ref, g_ref, o_ref, m_ref, l_ref, acc_ref, s_ref):
    i = pl.program_id(1)
    tq = TQ_MLA
    n_pairs = MLA_HEADS // 2
    key_chunk = lax.broadcasted_iota(jnp.int32, (tq, tq), 0) // CHUNK
    qry_chunk = lax.broadcasted_iota(jnp.int32, (tq, tq), 1) // CHUNK
    diag_ok = key_chunk <= qry_chunk
    top = lax.broadcasted_iota(jnp.int32, (LANES, tq), 0) < MLA_V

    m_ref[...] = jnp.full(m_ref.shape, -jnp.inf, jnp.float32)
    l_ref[...] = jnp.zeros(l_ref.shape, jnp.float32)
    acc_ref[...] = jnp.zeros(acc_ref.shape, jnp.float32)

    def scores(kb, slot):
        start = pl.multiple_of(kb * tq, tq)
        for h in range(MLA_HEADS):
            qt = qt_ref[0, h * HEAD_PAD:(h + 1) * HEAD_PAD, :]
            kblk = k_ref[pl.ds(start, tq), h * HEAD_PAD:(h + 1) * HEAD_PAD]
            s_ref[slot, h] = _dot(kblk, qt)

    def absorb(kb, slot, mask):
        for pr in range(n_pairs):
            vt = vt_ref[kb, pr * LANES:(pr + 1) * LANES, :]
            zero = jnp.zeros_like(vt)
            vt_bd = jnp.concatenate([jnp.where(top, vt, zero), jnp.where(top, zero, vt)], axis=1)
            pts, alphas = [], []
            for h in (2 * pr, 2 * pr + 1):
                s = s_ref[slot, h]
                if mask is not None:
                    s = jnp.where(mask, s, -jnp.inf)
                m_old = m_ref[h]
                m_new = jnp.maximum(m_old, jnp.max(s, axis=0, keepdims=True))
                p = jnp.exp2(s - m_new)
                alpha = jnp.exp2(m_old - m_new)
                l_ref[h] = alpha * l_ref[h] + jnp.sum(p, axis=0, keepdims=True)
                m_ref[h] = m_new
                pts.append(_bf(p))
                alphas.append(alpha)
            alpha_pair = jnp.where(top, alphas[0], alphas[1])
            acc_ref[pr] = alpha_pair * acc_ref[pr] + _dot(vt_bd, jnp.concatenate(pts, axis=0))

    def body(j, carry):
        scores(2 * j + 1, 1)
        absorb(2 * j, 0, None)
        scores(2 * j + 2, 0)
        absorb(2 * j + 1, 1, None)
        return carry

    scores(0, 0)
    lax.fori_loop(0, i // 2, body, 0)

    @pl.when(i % 2 == 0)
    def _():
        absorb(i, 0, diag_ok)

    @pl.when(i % 2 == 1)
    def _():
        scores(i, 1)
        absorb(i - 1, 0, None)
        absorb(i, 1, diag_ok)

    outs = [jnp.transpose(acc_ref[pr] / jnp.where(top, l_ref[2 * pr], l_ref[2 * pr + 1])) for pr in range(n_pairs)]
    o = jnp.concatenate(outs, axis=1)
    o_ref[...] = _bf(_rms(o, g_ref[...]))


def _mla_attention(qt, k, vt, g, batch, seq):
    nq = seq // TQ_MLA
    w = MLA_HEADS * HEAD_PAD
    return pl.pallas_call(
        _mla_kernel,
        grid=(batch, nq),
        in_specs=[pl.BlockSpec((1, w, TQ_MLA), lambda b, i: (b * nq + i, 0, 0)),
                  pl.BlockSpec((seq, w), lambda b, i: (b, 0)),
                  pl.BlockSpec((nq, A_W, TQ_MLA), lambda b, i: (b, 0, 0)),
                  pl.BlockSpec((1, A_W), lambda b, i: (0, 0))],
        out_specs=pl.BlockSpec((TQ_MLA, A_W), lambda b, i: (b * nq + i, 0)),
        out_shape=jax.ShapeDtypeStruct((batch * seq, A_W), jnp.bfloat16),
        scratch_shapes=[pltpu.VMEM((MLA_HEADS, 1, TQ_MLA), jnp.float32),
                        pltpu.VMEM((MLA_HEADS, 1, TQ_MLA), jnp.float32),
                        pltpu.VMEM((MLA_HEADS // 2, LANES, TQ_MLA), jnp.float32),
                        pltpu.VMEM((2, MLA_HEADS, TQ_MLA, TQ_MLA), jnp.float32)],
        compiler_params=_cparams(("parallel", "parallel")),
        name="mla_attention",
    )(qt, k, vt, g)


def _sb_kernel(q_ref, k_ref, v_ref, g_ref, o_ref, run_ref, acc_ref, zl_ref, sums_ref):
    i = pl.program_id(1)
    tq, tk = TQ_SB, TK_SB
    n_pairs = SB_HEADS // 2
    r = lax.broadcasted_iota(jnp.int32, (2 * tq, tk), 0)
    c = lax.broadcasted_iota(jnp.int32, (2 * tq, tk), 1)
    q_off = jnp.where(r >= tq, r - tq, r)
    top_low = (r < tq) == (c < SB_DIM)
    r2 = lax.broadcasted_iota(jnp.int32, (tk, 2 * tk), 0)
    c2 = lax.broadcasted_iota(jnp.int32, (tk, 2 * tk), 1)
    sum_mat = jnp.where((c2 >= tk) | (r2 > c2), 1.0, 0.0).astype(jnp.bfloat16)

    run_ref[...] = jnp.zeros(run_ref.shape, jnp.float32)
    acc_ref[...] = jnp.zeros(acc_ref.shape, jnp.float32)

    hq = tq // 2

    def late(v):
        return jnp.concatenate([v[hq:tq], v[tq + hq:2 * tq]], axis=0)

    def step(kb_hi, masked):
        masks = []
        for d in range(2):
            kb = kb_hi - d
            half = masked and d == 0
            start = pl.multiple_of(kb * tk, tk)
            mask = (c + (kb * tk - i * tq)) < q_off if masked else None
            if half:
                mask = late(mask)
            masks.append(mask)
            for pr in range(n_pairs):
                qp = q_ref[:, pr * LANES:(pr + 1) * LANES]
                q2 = jnp.concatenate([qp, qp], axis=0)
                qm = jnp.where(top_low, q2, jnp.zeros_like(q2))
                if half:
                    qm = late(qm)
                kblk = k_ref[pl.ds(start, tk), pr * LANES:(pr + 1) * LANES]
                z = _dot_nt(qm, kblk)
                log_keep = -(jnp.maximum(z, 0.0) + jnp.log(1.0 + jnp.exp(-jnp.abs(z))))
                if masked:
                    log_keep = jnp.where(mask, log_keep, 0.0)
                hi = _bf(log_keep)
                lo = _bf(log_keep - hi.astype(jnp.float32))
                n = z.shape[0]
                sums_ref[2 * d + pr, 0:n] = _dot(hi, sum_mat) + _dot(lo, sum_mat)
                zl_ref[2 * d + pr, 0:n] = z + log_keep
        slowest = None
        for d in range(2):
            half = masked and d == 0
            start = pl.multiple_of((kb_hi - d) * tk, tk)
            for pr in range(n_pairs):
                vblk = v_ref[pl.ds(start, tk), pr * LANES:(pr + 1) * LANES]
                if half:
                    run = late(run_ref[pr])
                    a = jnp.exp(zl_ref[2 * d + pr, 0:tq] + run + sums_ref[2 * d + pr, 0:tq, 0:tk])
                    upd = _dot(_bf(jnp.where(masks[d], a, 0.0)), vblk)
                    run = run + sums_ref[2 * d + pr, 0:tq, tk:2 * tk]
                    for src, dst in ((slice(0, hq), slice(hq, tq)), (slice(hq, tq), slice(tq + hq, 2 * tq))):
                        acc_ref[pr, dst] = acc_ref[pr, dst] + upd[src]
                        run_ref[pr, dst] = run[src]
                    continue
                run = run_ref[pr]
                a = jnp.exp(zl_ref[2 * d + pr] + run + sums_ref[2 * d + pr, :, 0:tk])
                if masked:
                    a = jnp.where(masks[d], a, 0.0)
                acc_ref[pr] = acc_ref[pr] + _dot(_bf(a), vblk)
                run = run + sums_ref[2 * d + pr, :, tk:2 * tk]
                run_ref[pr] = run
                if d == 1:
                    top = jnp.max(run)
                    slowest = top if slowest is None else jnp.maximum(slowest, top)
        return slowest

    underflow = -104.0
    assert tq == 2 * tk
    first = step(2 * i + 1, True)
    lax.while_loop(lambda cr: (cr[0] >= 0) & (cr[1] > underflow),
                   lambda cr: (cr[0] - 1, step(2 * cr[0] + 1, False)),
                   (i - 1, first))

    lane = lax.broadcasted_iota(jnp.int32, (tq, LANES), 1)
    outs = [jnp.where(lane < SB_DIM, acc_ref[pr, 0:tq, :], acc_ref[pr, tq:2 * tq, :]) for pr in range(n_pairs)]
    o = jnp.concatenate(outs, axis=1)
    o_ref[...] = _bf(_rms(o, g_ref[...]))


def _sb_attention(q, k, v, g, batch, seq):
    nq = seq // TQ_SB
    return pl.pallas_call(
        _sb_kernel,
        grid=(batch, nq),
        in_specs=[pl.BlockSpec((TQ_SB, SB_W), lambda b, i: (b * nq + i, 0)),
                  pl.BlockSpec((seq, SB_W), lambda b, i: (b, 0)),
                  pl.BlockSpec((seq, SB_W), lambda b, i: (b, 0)),
                  pl.BlockSpec((1, SB_W), lambda b, i: (0, 0))],
        out_specs=pl.BlockSpec((TQ_SB, SB_W), lambda b, i: (b * nq + i, 0)),
        out_shape=jax.ShapeDtypeStruct((batch * seq, SB_W), jnp.bfloat16),
        scratch_shapes=[pltpu.VMEM((SB_HEADS // 2, 2 * TQ_SB, TK_SB), jnp.float32),
                        pltpu.VMEM((SB_HEADS // 2, 2 * TQ_SB, LANES), jnp.float32),
                        pltpu.VMEM((SB_HEADS, 2 * TQ_SB, TK_SB), jnp.float32),
                        pltpu.VMEM((SB_HEADS, 2 * TQ_SB, 2 * TK_SB), jnp.float32)],
        compiler_params=_cparams(("parallel", "parallel")),
        name="sb_attention",
    )(q, k, v, g)


def _ca_kernel(q_ref, k_ref, v_ref, tab_ref, g_ref, o_ref, bias_ref):
    t = TQ_CA

    @pl.when((pl.program_id(0) == 0) & (pl.program_id(1) == 0))
    def _():
        r = lax.broadcasted_iota(jnp.int32, (t, CA_WIN), 0)
        c = lax.broadcasted_iota(jnp.int32, (t, CA_WIN), 1)
        lo = (r // CHUNK) * CHUNK
        band = (c >= lo) & (c < lo + CA_PAD + CHUNK)
        for h in range(CA_HEADS):
            tab = jnp.broadcast_to(tab_ref[h:h + 1, :], (t, REL_TAB))
            bias = pltpu.roll(tab, REL_TAB - (REL_CLIP - 1), 1, stride=1, stride_axis=0)[:, :CA_WIN]
            bias_ref[h] = jnp.where(band, bias * LOG2E, -jnp.inf)

    for j in range(CA_BLOCKS):
        _ca_block(pl.program_id(1) * CA_BLOCKS + j, slice(j * t, (j + 1) * t), q_ref, k_ref, v_ref, g_ref, o_ref, bias_ref)


def _ca_block(i, rows, q_ref, k_ref, v_ref, g_ref, o_ref, bias_ref):
    t = TQ_CA
    start = pl.multiple_of(i * t, t)
    r2 = lax.broadcasted_iota(jnp.int32, (2 * t, LANES), 0)
    c2 = lax.broadcasted_iota(jnp.int32, (2 * t, LANES), 1)
    top_low = (r2 < t) == (c2 < CA_DIM)
    in_seq = lax.broadcasted_iota(jnp.int32, (2 * t, CA_WIN), 1) + i * t >= CA_PAD
    lane = lax.broadcasted_iota(jnp.int32, (t, LANES), 1)

    outs = []
    for pair in range(CA_HEADS // 2):
        qp = q_ref[rows, pair * LANES:(pair + 1) * LANES]
        q2 = jnp.concatenate([qp, qp], axis=0)
        qm = jnp.where(top_low, q2, jnp.zeros_like(q2))
        kwin = k_ref[pl.ds(start, CA_WIN), pair * LANES:(pair + 1) * LANES]
        vwin = v_ref[pl.ds(start, CA_WIN), pair * LANES:(pair + 1) * LANES]
        bias = jnp.concatenate([bias_ref[2 * pair], bias_ref[2 * pair + 1]], axis=0)
        s = jnp.where(in_seq, _dot_nt(qm, kwin) + bias, -jnp.inf)
        m = jnp.max(s, axis=-1, keepdims=True)
        p = jnp.exp2(s - m)
        l = jnp.sum(p, axis=-1, keepdims=True)
        o2 = _dot(_bf(p), vwin) / l
        outs.append(jnp.where(lane < CA_DIM, o2[0:t], o2[t:2 * t]))
    o = jnp.concatenate(outs, axis=1)
    o_ref[rows, :] = _bf(_rms(o, g_ref[...]))


def _ca_attention(q, kpad, vpad, tab, g, batch, seq):
    tq = TQ_CA * CA_BLOCKS
    nq = seq // tq
    return pl.pallas_call(
        _ca_kernel,
        grid=(batch, nq),
        in_specs=[pl.BlockSpec((tq, CA_W), lambda b, i: (b * nq + i, 0)),
                  pl.BlockSpec((seq + CA_PAD, CA_W), lambda b, i: (b, 0)),
                  pl.BlockSpec((seq + CA_PAD, CA_W), lambda b, i: (b, 0)),
                  pl.BlockSpec((CA_HEADS, REL_TAB), lambda b, i: (0, 0)),
                  pl.BlockSpec((1, CA_W), lambda b, i: (0, 0))],
        out_specs=pl.BlockSpec((tq, CA_W), lambda b, i: (b * nq + i, 0)),
        out_shape=jax.ShapeDtypeStruct((batch * seq, CA_W), jnp.bfloat16),
        scratch_shapes=[pltpu.VMEM((CA_HEADS, TQ_CA, CA_WIN), jnp.float32)],
        compiler_params=_cparams(("arbitrary", "arbitrary")),
        name="ca_attention",
    )(q, kpad, vpad, tab, g)


def _outproj_router_kernel(ma_ref, mb_ref, mc_ref, wo_ref, x_ref, g_ref, wr2_ref, br_ref,
                           xn_ref, h_ref, lpos_ref, meta_ref, runs_ref, tot_ref, base_ref):
    @pl.when(pl.program_id(0) == 0)
    def _():
        base_ref[...] = jnp.zeros_like(base_ref)

    base = base_ref[...]
    for j in range(ROUTER_TILES):
        base = _route_tile(j, base, ma_ref, mb_ref, mc_ref, wo_ref, x_ref, g_ref, wr2_ref, br_ref,
                           xn_ref, h_ref, lpos_ref, meta_ref, runs_ref)
    base_ref[...] = base
    tot_ref[...] = base.astype(jnp.int32)


def _route_tile(j, base, ma_ref, mb_ref, mc_ref, wo_ref, x_ref, g_ref, wr2_ref, br_ref,
                xn_ref, h_ref, lpos_ref, meta_ref, runs_ref):
    tm = TM_TOK
    rows = slice(j * tm, (j + 1) * tm)
    attn = (_dot(ma_ref[rows, :], wo_ref[0:A_W, :]) + _dot(mb_ref[rows, :], wo_ref[A_W:A_W + SB_W, :])
            + _dot(mc_ref[rows, :], wo_ref[A_W + SB_W:, :]))
    xn = x_ref[rows, :] + attn
    xn_ref[rows, :] = xn
    h = _rms(xn, g_ref[...])

    h_hi = _bf(h)
    h_ref[rows, :] = h_hi
    h_lo = _bf(h - h_hi.astype(jnp.float32))
    both = _dot(h_hi, wr2_ref[...])
    logits = (both[:, :LANES] + both[:, LANES:] + _dot(h_lo, wr2_ref[:, 0:LANES])
              + br_ref[...])
    lane = lax.broadcasted_iota(jnp.int32, (tm, LANES), 1)
    lane_f = lane.astype(jnp.float32)

    work = logits
    vals, idxs, hots = [], [], []
    for _ in range(TOP_K):
        mx = jnp.max(work, axis=-1, keepdims=True)
        ix = jnp.min(jnp.where(work == mx, lane_f, float(LANES)), axis=-1, keepdims=True)
        hot = lane_f == ix
        work = jnp.where(hot, -jnp.inf, work)
        vals.append(mx)
        idxs.append(ix)
        hots.append(hot)
    exps = [jnp.exp(v - vals[0]) for v in vals]
    denom = exps[0] + exps[1] + exps[2] + exps[3]
    gates = [e / denom for e in exps]

    sel = jnp.zeros((tm, LANES), jnp.float32)
    for hot in hots:
        sel = sel + jnp.where(hot, 1.0, 0.0)
    r = lax.broadcasted_iota(jnp.int32, (tm, tm), 0)
    c = lax.broadcasted_iota(jnp.int32, (tm, tm), 1)
    before = jnp.where(c < r, 1.0, 0.0).astype(jnp.bfloat16)
    rank_in_tile = _dot(before, _bf(sel))

    cnt = jnp.sum(sel, axis=0, keepdims=True)
    cnt_al = jnp.ceil(cnt * (1.0 / RUN_ALIGN)) * RUN_ALIGN
    rl = lax.broadcasted_iota(jnp.int32, (LANES, LANES), 0)
    cl = lax.broadcasted_iota(jnp.int32, (LANES, LANES), 1)
    earlier = jnp.where(rl < cl, 1.0, 0.0).astype(jnp.bfloat16)
    loff = _dot(_bf(jnp.broadcast_to(cnt_al, (8, LANES))), earlier)[0:1, :]
    sub = lax.broadcasted_iota(jnp.int32, (8, LANES), 0)
    has_long = jnp.where(jnp.max(cnt_al, axis=-1, keepdims=True) >= LONG_RUN, 1.0, 0.0)
    runs = jnp.where(sub == 0, loff, jnp.where(sub == 1, base, jnp.where(sub == 2, cnt_al,
                     jnp.where(sub == 3, has_long, 0.0))))
    runs_ref[8 * j:8 * j + 8, :] = runs.astype(jnp.int32)

    lpos_dense = loff + rank_in_tile
    meta = jnp.zeros((tm, LANES), jnp.float32)
    for kk in range(TOP_K):
        lpos = jnp.sum(jnp.where(hots[kk], lpos_dense, 0.0), axis=-1, keepdims=True)
        meta = meta + jnp.where(lane == kk, gates[kk], 0.0) + jnp.where(lane == TOP_K + kk, lpos, 0.0)
    meta_ref[rows, :] = meta
    lpos_ref[:, rows] = jnp.transpose(meta)[TOP_K:TOP_K + 8, :].astype(jnp.int32)
    return base + cnt_al


def _outproj_router(ma, mb, mc, wo, x, g, wr2, br):
    n = x.shape[0]
    tm = TM_TOK * ROUTER_TILES
    row = lambda width: pl.BlockSpec((tm, width), lambda i: (i, 0))
    full = lambda a: pl.BlockSpec(a.shape, lambda i: (0,) * a.ndim)
    return pl.pallas_call(
        _outproj_router_kernel,
        grid=(n // tm,),
        in_specs=[row(A_W), row(SB_W), row(CA_W), full(wo), row(D_MODEL), full(g), full(wr2), full(br)],
        out_specs=[row(D_MODEL), row(D_MODEL), pl.BlockSpec((8, tm), lambda i: (0, i)),
                   row(LANES), pl.BlockSpec((8 * ROUTER_TILES, LANES), lambda i: (i, 0)),
                   pl.BlockSpec((1, LANES), lambda i: (0, 0))],
        out_shape=[jax.ShapeDtypeStruct((n, D_MODEL), jnp.float32),
                   jax.ShapeDtypeStruct((n, D_MODEL), jnp.bfloat16),
                   jax.ShapeDtypeStruct((8, n), jnp.int32),
                   jax.ShapeDtypeStruct((n, LANES), jnp.float32),
                   jax.ShapeDtypeStruct((8 * (n // TM_TOK), LANES), jnp.int32),
                   jax.ShapeDtypeStruct((1, LANES), jnp.int32)],
        scratch_shapes=[pltpu.VMEM((1, LANES), jnp.float32)],
        compiler_params=_cparams(("arbitrary",)),
        name="outproj_router",
    )(ma, mb, mc, wo, x, g, wr2, br)


_RUN_CHUNKS = tuple(TM_TOK >> s for s in range(TM_TOK.bit_length()) if (TM_TOK >> s) >= RUN_ALIGN)


LONG_RUN = 128


def _for_each_chunk(length, fn, sizes=_RUN_CHUNKS):
    for size in sizes:
        off = length & (~(2 * size - 1))

        @pl.when((length & size) != 0)
        def _(off=off, size=size):
            fn(off, size)


_WAIT_CHUNKS = tuple(1 << b for b in range((LOCAL_ROWS).bit_length() - 1, RUN_ALIGN.bit_length() - 2, -1))


def _wait_tile_runs(runs_ref, make_copy):
    total = runs_ref[0, N_EXPERTS - 1] + runs_ref[2, N_EXPERTS - 1]
    for size in _WAIT_CHUNKS:
        @pl.when((total & size) != 0)
        def _(size=size):
            make_copy(size).wait()


def _dma_lane(size):
    return size.bit_length() % 2


def _tile_runs(runs_ref, pstart_ref, fn):
    def experts(sizes, unroll):
        def body(e, _):
            local, glob, length = runs_ref[0, e], pstart_ref[e] + runs_ref[1, e], runs_ref[2, e]
            _for_each_chunk(length, lambda off, size: fn(pl.multiple_of(local + off, RUN_ALIGN),
                                                         pl.multiple_of(glob + off, RUN_ALIGN), size), sizes)
            return 0

        lax.fori_loop(0, N_EXPERTS, body, 0, unroll=unroll)

    experts(tuple(sz for sz in _RUN_CHUNKS if sz < LONG_RUN), 4)

    @pl.when(runs_ref[3, 0] != 0)
    def _():
        experts(tuple(sz for sz in _RUN_CHUNKS if sz >= LONG_RUN), 1)


def _dispatch_kernel(pstart_ref, tot_ref, pad_ref, runs_ref, prev_runs_ref, lpos_ref, h_ref, xs_ref,
                     loc, zbuf, sem, fill_sem):
    i = pl.program_id(0)
    slot = i % 2

    def fill(start_or_wait):
        def body(e, _):
            first = pstart_ref[e] + tot_ref[e]
            _for_each_chunk(pad_ref[e] - tot_ref[e], lambda off, size: start_or_wait(pltpu.make_async_copy(
                zbuf.at[pl.ds(0, size)], xs_ref.at[pl.ds(pl.multiple_of(first + off, RUN_ALIGN), size)], fill_sem)))
            return 0

        lax.fori_loop(0, N_EXPERTS, body, 0)

        used = pstart_ref[N_EXPERTS - 1] + pad_ref[N_EXPERTS - 1]

        def tail(b, _):
            start_or_wait(pltpu.make_async_copy(
                zbuf, xs_ref.at[pl.ds(pl.multiple_of(b * TM_TOK, TM_TOK), TM_TOK)], fill_sem))
            return 0

        lax.fori_loop(used // TM_TOK, xs_ref.shape[0] // TM_TOK, tail, 0)

    @pl.when(i == 0)
    def _():
        zbuf[...] = jnp.zeros_like(zbuf)
        fill(lambda c: c.start())
        fill(lambda c: c.wait())

    r = lax.broadcasted_iota(jnp.int32, (LOCAL_ROWS, TM_TOK), 0)
    hit = r == lpos_ref[0:1, :]
    for kk in range(1, TOP_K):
        hit = hit | (r == lpos_ref[kk:kk + 1, :])
    loc[slot] = _pack_rows(_dot(jnp.where(hit, 1.0, 0.0).astype(jnp.bfloat16), h_ref[...]))

    def copy(buf, local, glob, size):
        return pltpu.make_async_copy(loc.at[buf, pl.ds(local, size)], xs_ref.at[pl.ds(glob, size)], sem.at[buf])

    _tile_runs(runs_ref, pstart_ref, lambda l, g, s: copy(slot, l, g, s).start(priority=_dma_lane(s)))

    @pl.when(i > 0)
    def _():
        _wait_tile_runs(prev_runs_ref, lambda size: copy(1 - slot, 0, 0, size))

    @pl.when(i == pl.num_programs(0) - 1)
    def _():
        _wait_tile_runs(runs_ref, lambda size: copy(slot, 0, 0, size))


def _dispatch(pstart, totals, padded, runs, lpos, h, n_rows):
    n = h.shape[0]
    tm = TM_TOK
    grid_spec = pltpu.PrefetchScalarGridSpec(
        num_scalar_prefetch=3,
        grid=(n // tm,),
        in_specs=[pl.BlockSpec((8, LANES), lambda i, *_: (i, 0), memory_space=pltpu.SMEM),
                  pl.BlockSpec((8, LANES), lambda i, *_: (jnp.maximum(i - 1, 0), 0), memory_space=pltpu.SMEM),
                  pl.BlockSpec((8, tm), lambda i, *_: (0, i)),
                  pl.BlockSpec((tm, D_MODEL), lambda i, *_: (i, 0))],
        out_specs=pl.BlockSpec(memory_space=pl.ANY),
        scratch_shapes=[pltpu.VMEM((2, LOCAL_ROWS, HALF), jnp.uint32),
                        pltpu.VMEM((TM_TOK, HALF), jnp.uint32),
                        pltpu.SemaphoreType.DMA((2,)),
                        pltpu.SemaphoreType.DMA],
    )
    return pl.pallas_call(
        _dispatch_kernel,
        grid_spec=grid_spec,
        out_shape=jax.ShapeDtypeStruct((n_rows, HALF), jnp.uint32),
        compiler_params=_cparams(("arbitrary",)),
        name="dispatch",
    )(pstart, totals, padded, runs, runs, lpos, h)


def _expert_kernel(be_ref, bfirst_ref, bvalid_ref, xs_ref, wgu_ref, bgu_ref, wdn_ref, bdn_ref,
                   ys_ref, wgu_bf, wdn_bf):
    b = pl.program_id(0)

    @pl.when(bfirst_ref[b] == 1)
    def _():
        wgu_bf[...] = _bf(wgu_ref[0])
        wdn_bf[...] = _bf(wdn_ref[0])

    @pl.when(bvalid_ref[b] == 1)
    def _():
        xb = jnp.concatenate(_unpack_rows(xs_ref[...]), axis=1)
        gu = _dot(xb, wgu_bf[...]) + bgu_ref[0]
        gte = jnp.minimum(gu[:, :D_FF], SWIGLU_LIMIT)
        up = jnp.clip(gu[:, D_FF:], -SWIGLU_LIMIT, SWIGLU_LIMIT)
        act = (up + 1.0) * (gte * (1.0 / (1.0 + jnp.exp(-SWIGLU_ALPHA * gte))))
        y = _dot(_bf(act), wdn_bf[...]) + bdn_ref[0]
        ys_ref[...] = _pack_rows(_bf(y).astype(jnp.float32))

    @pl.when(bvalid_ref[b] == 0)
    def _():
        ys_ref[...] = jnp.zeros_like(ys_ref)


def _expert_ffn(blk_e, blk_first, blk_valid, xs, wgu, bgu, wdn, bdn):
    n_rows = xs.shape[0]
    tm = TM_EXP
    grid_spec = pltpu.PrefetchScalarGridSpec(
        num_scalar_prefetch=3,
        grid=(n_rows // tm,),
        in_specs=[pl.BlockSpec((tm, HALF), lambda b, e, f, v: (b, 0)),
                  pl.BlockSpec((1, D_MODEL, 2 * D_FF), lambda b, e, f, v: (e[b], 0, 0)),
                  pl.BlockSpec((1, 1, 2 * D_FF), lambda b, e, f, v: (e[b], 0, 0)),
                  pl.BlockSpec((1, D_FF, D_MODEL), lambda b, e, f, v: (e[b], 0, 0)),
                  pl.BlockSpec((1, 1, D_MODEL), lambda b, e, f, v: (e[b], 0, 0))],
        out_specs=pl.BlockSpec((tm, HALF), lambda b, e, f, v: (b, 0)),
        scratch_shapes=[pltpu.VMEM((D_MODEL, 2 * D_FF), jnp.bfloat16),
                        pltpu.VMEM((D_FF, D_MODEL), jnp.bfloat16)],
    )
    return pl.pallas_call(
        _expert_kernel,
        grid_spec=grid_spec,
        out_shape=jax.ShapeDtypeStruct((n_rows, HALF), jnp.uint32),
        compiler_params=_cparams(("arbitrary",)),
        name="expert_ffn",
    )(blk_e, blk_first, blk_valid, xs, wgu, bgu, wdn, bdn)


def _combine_kernel(final, pstart_ref, runs_ref, next_runs_ref, meta_ref, x_ref, ys_ref, gfin_ref, o_ref, loc, sem):
    i = pl.program_id(0)
    slot = i % 2

    def copy(buf, local, glob, size):
        return pltpu.make_async_copy(ys_ref.at[pl.ds(glob, size)], loc.at[buf, pl.ds(local, size)], sem.at[buf])

    @pl.when(i == 0)
    def _():
        loc[...] = jnp.zeros_like(loc)
        _tile_runs(runs_ref, pstart_ref, lambda l, g, s: copy(slot, l, g, s).start(priority=_dma_lane(s)))

    @pl.when(i + 1 < pl.num_programs(0))
    def _():
        _tile_runs(next_runs_ref, pstart_ref, lambda l, g, s: copy(1 - slot, l, g, s).start(priority=_dma_lane(s)))

    _wait_tile_runs(runs_ref, lambda size: copy(slot, 0, 0, size))

    meta = meta_ref[...]
    col = lax.broadcasted_iota(jnp.int32, (TM_TOK, LOCAL_ROWS), 1).astype(jnp.float32)
    wts = jnp.zeros((TM_TOK, LOCAL_ROWS), jnp.float32)
    for kk in range(TOP_K):
        wts = wts + jnp.where(col == meta[:, TOP_K + kk:TOP_K + kk + 1], meta[:, kk:kk + 1], 0.0)
    w_hi = _bf(wts)
    w_lo = _bf(wts - w_hi.astype(jnp.float32))
    w2 = jnp.concatenate([w_hi, w_lo], axis=0)
    y_lo, y_hi = _unpack_rows(loc[slot])
    r_lo, r_hi = _dot(w2, y_lo), _dot(w2, y_hi)
    out = x_ref[...] + jnp.concatenate([r_lo[:TM_TOK] + r_lo[TM_TOK:], r_hi[:TM_TOK] + r_hi[TM_TOK:]], axis=1)
    if final:
        out = _rms(out, gfin_ref[...])
    o_ref[...] = out


def _combine(pstart, runs, meta, x, ys, gfin, final):
    n = x.shape[0]
    tm = TM_TOK
    n_tiles = n // tm
    grid_spec = pltpu.PrefetchScalarGridSpec(
        num_scalar_prefetch=1,
        grid=(n_tiles,),
        in_specs=[pl.BlockSpec((8, LANES), lambda i, ps: (i, 0), memory_space=pltpu.SMEM),
                  pl.BlockSpec((8, LANES), lambda i, ps: (jnp.minimum(i + 1, n_tiles - 1), 0), memory_space=pltpu.SMEM),
                  pl.BlockSpec((tm, LANES), lambda i, ps: (i, 0)),
                  pl.BlockSpec((tm, D_MODEL), lambda i, ps: (i, 0)),
                  pl.BlockSpec(memory_space=pl.ANY),
                  pl.BlockSpec((1, D_MODEL), lambda i, ps: (0, 0))],
        out_specs=pl.BlockSpec((tm, D_MODEL), lambda i, ps: (i, 0)),
        scratch_shapes=[pltpu.VMEM((2, LOCAL_ROWS, HALF), jnp.uint32), pltpu.SemaphoreType.DMA((2,))],
    )
    return pl.pallas_call(
        functools.partial(_combine_kernel, final),
        grid_spec=grid_spec,
        out_shape=jax.ShapeDtypeStruct((n, D_MODEL), jnp.float32),
        compiler_params=_cparams(("arbitrary",)),
        name="combine_final" if final else "combine",
    )(pstart, runs, runs, meta, x, ys, gfin)


def _pad_heads(w, parts):
    rows = w.shape[0]
    per = w.shape[1] // MLA_HEADS
    w3 = w.reshape(rows, MLA_HEADS, per)
    cols = [jnp.zeros((rows, MLA_HEADS, b - a), w.dtype) if sign == 0 else sign * w3[:, :, a:b]
            for a, b, sign in parts]
    used = sum(b - a for a, b, _ in parts)
    cols.append(jnp.zeros((rows, MLA_HEADS, HEAD_PAD - used), w.dtype))
    return jnp.concatenate(cols, axis=2).reshape(rows, MLA_HEADS * HEAD_PAD)


def _layer_weights(w_in, w_uq, w_ukv, w_router, b_router, rel_bias):
    half = MLA_ROPE // 2
    cq, ckv, kpe, sb, ca = (w_in[:, 0:256], w_in[:, 256:384], w_in[:, 384:416],
                            w_in[:, 416:1184], w_in[:, 1184:1952])
    z = lambda width: jnp.zeros((D_MODEL, width), w_in.dtype)
    kpe_pad = jnp.concatenate([z(MLA_NOPE), kpe, z(HEAD_PAD - MLA_NOPE - MLA_ROPE)], axis=1)
    kpe_rot = jnp.concatenate([z(MLA_NOPE), -kpe[:, half:], kpe[:, :half],
                               z(HEAD_PAD - MLA_NOPE - MLA_ROPE)], axis=1)
    win2 = jnp.concatenate([cq, ckv, kpe_pad, kpe_rot, sb, ca], axis=1).astype(jnp.bfloat16)

    d = MLA_NOPE + MLA_ROPE
    wq_full = _pad_heads(w_uq, [(0, d, 1)])
    wq_rot = _pad_heads(w_uq, [(0, MLA_NOPE, 0), (MLA_NOPE + half, d, -1), (MLA_NOPE, MLA_NOPE + half, 1)])
    wq2 = jnp.concatenate([wq_full, wq_rot], axis=1).astype(jnp.bfloat16)

    wk = _pad_heads(w_ukv, [(0, MLA_NOPE, 1)])
    wv = w_ukv.reshape(MLA_KV_RANK, MLA_HEADS, MLA_NOPE + MLA_V)[:, :, MLA_NOPE:].reshape(MLA_KV_RANK, A_W)
    wkv2 = jnp.concatenate([wk, wv], axis=1).astype(jnp.bfloat16)

    wr = jnp.pad(w_router, ((0, 0), (0, LANES - N_EXPERTS)))
    wr_hi = wr.astype(jnp.bfloat16)
    wr_lo = (wr - wr_hi.astype(jnp.float32)).astype(jnp.bfloat16)
    wr2 = jnp.concatenate([wr_hi, wr_lo], axis=1)
    br = jnp.pad(b_router, (0, LANES - N_EXPERTS), constant_values=NEG).reshape(1, LANES)

    ext = jnp.concatenate([rel_bias, jnp.broadcast_to(rel_bias[:, -1:], (CA_HEADS, REL_TAB - 2 * REL_CLIP - 1))], axis=1)
    tab = ext[:, ::-1]
    return win2, wq2, wkv2, wr2, br, tab


def kernel(x, positions, attn_norm, w_in, q_norm, w_uq, kv_norm, w_ukv, rel_bias, mix_norm,
           w_o, ffn_norm, w_router, b_router, w_gate_up, b_gate_up, w_down, b_down, final_norm):
    batch, seq, _ = x.shape
    n = batch * seq
    depth = w_in.shape[0]
    xf = x.reshape(n, D_MODEL)

    cos_t, sin_t = _rope_tables(positions, n)

    n_rows = n * TOP_K + (n // TM_TOK) * N_EXPERTS * (RUN_ALIGN - 1) + N_EXPERTS * TM_EXP
    n_rows = -(-n_rows // TM_EXP) * TM_EXP
    n_blk = n_rows // TM_EXP
    row2 = lambda v: v.reshape(1, -1)
    wgu_all = w_gate_up.reshape(depth * N_EXPERTS, D_MODEL, 2 * D_FF)
    bgu_all = b_gate_up.reshape(depth * N_EXPERTS, 1, 2 * D_FF)
    wdn_all = w_down.reshape(depth * N_EXPERTS, D_FF, D_MODEL)
    bdn_all = b_down.reshape(depth * N_EXPERTS, 1, D_MODEL)

    for l in range(depth):
        win2, wq2, wkv2, wr2, br, tab = _layer_weights(
            w_in[l], w_uq[l], w_ukv[l], w_router[l], b_router[l], rel_bias[l])
        qt, k, vt, sbq, sbk, sbv, caq, cak, cav = _inproj(
            xf, row2(attn_norm[l]), win2, cos_t, sin_t, row2(q_norm[l]), wq2, row2(kv_norm[l]), wkv2)

        g = mix_norm[l]
        ma = _mla_attention(qt, k, vt, row2(g[:A_W]), batch, seq)
        mb = _sb_attention(sbq, sbk, sbv, row2(g[A_W:A_W + SB_W]), batch, seq)
        pad = lambda a: jnp.pad(a.reshape(batch, seq, CA_W), ((0, 0), (CA_PAD, 0), (0, 0))).reshape(-1, CA_W)
        mc = _ca_attention(caq, pad(cak), pad(cav), tab, row2(g[A_W + SB_W:]), batch, seq)

        xn, h, lpos, meta, runs, totals = _outproj_router(
            ma, mb, mc, w_o[l].astype(jnp.bfloat16), xf, row2(ffn_norm[l]), wr2, br)

        sizes = totals[0, :N_EXPERTS]
        padded = (sizes + TM_EXP - 1) // TM_EXP * TM_EXP
        p_ends = jnp.cumsum(padded)
        p_starts = (p_ends - padded).astype(jnp.int32)
        blk_start = jnp.arange(n_blk, dtype=jnp.int32) * TM_EXP
        blk_valid = (blk_start < p_ends[-1]).astype(jnp.int32)
        last_row = jnp.minimum(blk_start, p_ends[-1] - 1)
        blk_e = jnp.minimum(jnp.sum((last_row[:, None] >= p_ends[None, :]).astype(jnp.int32), axis=1),
                            N_EXPERTS - 1)
        blk_first = jnp.concatenate([jnp.ones((1,), jnp.int32), (blk_e[1:] != blk_e[:-1]).astype(jnp.int32)])

        xs = _dispatch(p_starts, sizes, padded.astype(jnp.int32), runs, lpos, h, n_rows)
        ys = _expert_ffn(blk_e + l * N_EXPERTS, blk_first, blk_valid, xs, wgu_all, bgu_all, wdn_all, bdn_all)
        xf = _combine(p_starts, runs, meta, xn, ys, row2(final_norm), final=(l == depth - 1))

    return xf.reshape(batch, seq, D_MODEL)
```
